```python
import jax, jax.numpy as jnp
from jax import lax

D_MODEL = 2048
BATCH = 4
SEQ = 2048
DEPTH = 2
DEC_BATCH = 128
DEC_SEQ = 4
PAST_LEN = 16384
PAGE_SIZE = 128

RET_HEADS = 8
RET_DK = D_MODEL // 16
RET_DV = D_MODEL // 8
RET_CHUNK = 128
ROPE_BASE = 10000.0
QK_W = RET_HEADS * RET_DK
V_W = RET_HEADS * RET_DV
POOL_WINDOWS = (2, 4, 8, 16)
POOL_GROUPS = 4
D_POOL = D_MODEL // 2
POOL_GW = D_POOL // POOL_GROUPS
POOL_BUF = max(POOL_WINDOWS) - 1
N_IN = 2 * QK_W + 2 * V_W + D_POOL + 2 * D_MODEL
D_FF = 256 * ((8 * D_MODEL // 3 + 255) // 256)
N_EXPERTS = 8
TOP_K = 2
D_EXPERT = D_FF // 2
N_DENSE = (DEPTH + 1) // 2
N_MOE = DEPTH // 2
EPS = 1e-6

kernel_name = "retnet_pool_gated_hybrid_step"


def rmsnorm(x, g):
    xf = x.astype(jnp.float32)
    y = xf * lax.rsqrt(jnp.mean(xf * xf, axis=-1, keepdims=True) + EPS) * g.astype(jnp.float32)
    return y.astype(x.dtype)


def rope(x, pos):
    half = x.shape[-1] // 2
    inv = ROPE_BASE ** (-jnp.arange(half, dtype=jnp.float32) / half)
    ang = pos[:, None] * inv[None, :]
    cos = jnp.cos(ang)[None, :, None, :]
    sin = jnp.sin(ang)[None, :, None, :]
    x1, x2 = x[..., :half], x[..., half:]
    return jnp.concatenate([x1 * cos - x2 * sin, x1 * sin + x2 * cos], axis=-1)


def retention(q, k, v, s0):
    b, L, h, dk = q.shape
    dv = v.shape[-1]
    C = RET_CHUNK if L % RET_CHUNK == 0 else L
    n = L // C
    log_gamma = jnp.log1p(-jnp.exp2(-5.0 - jnp.arange(RET_HEADS, dtype=jnp.float32)))
    idx = jnp.arange(C, dtype=jnp.float32)
    rel = idx[:, None] - idx[None, :]
    dmask = jnp.where(rel[None] >= 0, jnp.exp(log_gamma[:, None, None] * jnp.maximum(rel, 0.0)[None]), 0.0)
    q_dec = jnp.exp(log_gamma[None, :] * (idx + 1.0)[:, None])[None, :, :, None]
    k_dec = jnp.exp(log_gamma[None, :] * (C - 1.0 - idx)[:, None])[None, :, :, None]
    chunk_dec = jnp.exp(log_gamma * C)[None, :, None, None]

    def to_chunks(t):
        return t.reshape(b, n, C, h, t.shape[-1]).transpose(1, 0, 2, 3, 4)

    def step(S, xs):
        qc, kc, vc = xs
        scores = jnp.einsum('bihd,bjhd->bhij', qc, kc) * dmask[None]
        inner = jnp.einsum('bhij,bjhe->bihe', scores, vc)
        cross = jnp.einsum('bihd,bhde->bihe', qc, S) * q_dec
        S_new = chunk_dec * S + jnp.einsum('bjhd,bjhe->bhde', kc * k_dec, vc)
        return S_new, inner + cross

    s_fin, o = lax.scan(step, s0, (to_chunks(q), to_chunks(k), to_chunks(v)))
    o = o.transpose(1, 0, 2, 3, 4).reshape(b, L, h, dv)
    return o, s_fin


def head_norm(o, g):
    mu = jnp.mean(o, axis=-1, keepdims=True)
    var = jnp.mean(jnp.square(o - mu), axis=-1, keepdims=True)
    y = (o - mu) * lax.rsqrt(var + EPS)
    return y.reshape(o.shape[0], o.shape[1], -1) * g.astype(jnp.float32)


def multiscale_pool(u, buf, n_past, w_map, p_scale):
    b, L, _ = u.shape
    uf = jnp.concatenate([buf.astype(u.dtype), u], axis=1)
    ufl = uf.astype(jnp.float32)
    cs = jnp.concatenate([jnp.zeros((b, 1, D_POOL), jnp.float32), jnp.cumsum(ufl, axis=1)], axis=1)
    t = jnp.arange(L, dtype=jnp.float32)
    outs = []
    for gi, w in enumerate(POOL_WINDOWS):
        lo_c, hi_c = gi * POOL_GW, (gi + 1) * POOL_GW
        hi = cs[:, POOL_BUF + 1:POOL_BUF + 1 + L, lo_c:hi_c]
        lo = cs[:, POOL_BUF + 1 - w:POOL_BUF + 1 - w + L, lo_c:hi_c]
        cnt = jnp.minimum(float(w), t + 1.0 + float(n_past))
        mean = (hi - lo) / cnt[None, :, None]
        outs.append(mean - ufl[:, POOL_BUF:, lo_c:hi_c])
    pooled = jnp.stack(outs, axis=2).astype(u.dtype)
    mixed = jnp.einsum('blgc,gcd->blgd', pooled, w_map).reshape(b, L, D_POOL) * p_scale
    return mixed, uf[:, -POOL_BUF:]


def mixer(h, s0, buf, pos, n_past, w_in, g_ret, w_ret_out, w_pool_map, pool_scale, w_pool_out, w_o):
    b, L, _ = h.shape
    proj = h @ w_in
    o1 = QK_W
    o2 = o1 + QK_W
    o3 = o2 + V_W
    o4 = o3 + V_W
    o5 = o4 + D_POOL
    q = rope(proj[..., :o1].reshape(b, L, RET_HEADS, RET_DK).astype(jnp.float32), pos)
    k = rope(proj[..., o1:o2].reshape(b, L, RET_HEADS, RET_DK).astype(jnp.float32), pos) * (RET_DK ** -0.5)
    v = proj[..., o2:o3].reshape(b, L, RET_HEADS, RET_DV).astype(jnp.float32)
    o, s_new = retention(q, k, v, s0.astype(jnp.float32))
    ret = (jax.nn.silu(proj[..., o3:o4].astype(jnp.float32)) * head_norm(o, g_ret)).astype(h.dtype) @ w_ret_out
    pooled, buf_new = multiscale_pool(proj[..., o4:o5], buf, n_past, w_pool_map, pool_scale)
    pool = pooled @ w_pool_out
    gates = jax.nn.sigmoid(proj[..., o5:].astype(jnp.float32)).astype(h.dtype)
    merged = gates[..., :D_MODEL] * ret + gates[..., D_MODEL:] * pool
    return merged @ w_o, s_new, buf_new


def swiglu(h, w1, w3, w2):
    return (jax.nn.silu(h @ w1) * (h @ w3)) @ w2


def moe(h, w_router, b_router, w1, w3, w2):
    logits = (h @ w_router).astype(jnp.float32) + b_router.astype(jnp.float32)
    top_l, top_i = lax.top_k(logits, TOP_K)
    top_w = jax.nn.softmax(top_l, axis=-1)
    combine = jnp.sum(jax.nn.one_hot(top_i, N_EXPERTS, dtype=jnp.float32) * top_w[..., None], axis=-2)
    combine = combine.astype(h.dtype)
    out = jnp.zeros_like(h)
    for e in range(N_EXPERTS):
        out = out + combine[..., e:e + 1] * swiglu(h, w1[e], w3[e], w2[e])
    return out


def decoder(x, c, ret_states, pool_bufs, n_past, p):
    L = x.shape[1]
    pos = jnp.arange(L, dtype=jnp.float32) + float(n_past)
    cs = jax.nn.silu(c.astype(jnp.float32)).astype(x.dtype)
    new_s, new_b = [], []
    for i in range(DEPTH):
        mod = cs @ p['w_ada'][i] + p['b_ada'][i]
        sh_m, sc_m, gt_m, sh_f, sc_f, gt_f = [m[:, None, :] for m in jnp.split(mod, 6, axis=-1)]
        h = rmsnorm(x, p['g_mix'][i]) * (1.0 + sc_m) + sh_m
        mix, s_i, b_i = mixer(h, ret_states[i], pool_bufs[i], pos, n_past,
                              p['w_in'][i], p['g_ret'][i], p['w_ret_out'][i], p['w_pool_map'][i],
                              p['pool_scale'][i], p['w_pool_out'][i], p['w_o'][i])
        x = x + (1.0 + gt_m) * mix
        h = rmsnorm(x, p['g_ffn'][i]) * (1.0 + sc_f) + sh_f
        if i % 2 == 0:
            j = i // 2
            f = swiglu(h, p['w1_dense'][j], p['w3_dense'][j], p['w2_dense'][j])
        else:
            j = i // 2
            f = moe(h, p['w_router'][j], p['b_router'][j], p['w1_moe'][j], p['w3_moe'][j], p['w2_moe'][j])
        x = x + (1.0 + gt_f) * f
        new_s.append(s_i.astype(x.dtype))
        new_b.append(b_i.astype(x.dtype))
    return rmsnorm(x, p['g_final']), jnp.stack(new_s), jnp.stack(new_b)


def setup_inputs(seed: int = 0) -> dict:
    key = jax.random.key(seed)
    ks = jax.random.split(key, 26)
    D = D_MODEL

    def nrm(k, shape, scale):
        return jax.random.normal(k, shape, jnp.float32) * scale

    return {
        'x_prompt': nrm(ks[0], (BATCH, SEQ, D), 1.0),
        'x_sample': nrm(ks[1], (DEC_BATCH, DEC_SEQ, D), 1.0),
        'c_prompt': nrm(ks[2], (BATCH, D), 1.0),
        'c_sample': nrm(ks[3], (DEC_BATCH, D), 1.0),
        'state_ret': nrm(ks[4], (DEPTH, DEC_BATCH, RET_HEADS, RET_DK, RET_DV), 0.5),
        'state_pool': nrm(ks[5], (DEPTH, DEC_BATCH, POOL_BUF, D_POOL), 1.0),
        'w_ada': nrm(ks[6], (DEPTH, D, 6 * D), 0.5 * D ** -0.5),
        'b_ada': nrm(ks[7], (DEPTH, 6 * D), 0.01),
        'g_mix': 1.0 + nrm(ks[8], (DEPTH, D), 0.1),
        'w_in': nrm(ks[9], (DEPTH, D, N_IN), D ** -0.5),
        'g_ret': 1.0 + nrm(ks[10], (DEPTH, V_W), 0.1),
        'w_ret_out': nrm(ks[11], (DEPTH, V_W, D), V_W ** -0.5),
        'w_pool_map': nrm(ks[12], (DEPTH, POOL_GROUPS, POOL_GW, POOL_GW), POOL_GW ** -0.5),
        'pool_scale': 1.0 + nrm(ks[13], (DEPTH, D_POOL), 0.1),
        'w_pool_out': nrm(ks[14], (DEPTH, D_POOL, D), D_POOL ** -0.5),
        'w_o': nrm(ks[15], (DEPTH, D, D), D ** -0.5),
        'g_ffn': 1.0 + nrm(ks[16], (DEPTH, D), 0.1),
        'w1_dense': nrm(ks[17], (N_DENSE, D, D_FF), D ** -0.5),
        'w3_dense': nrm(ks[18], (N_DENSE, D, D_FF), D ** -0.5),
        'w2_dense': nrm(ks[19], (N_DENSE, D_FF, D), D_FF ** -0.5),
        'w_router': nrm(ks[20], (N_MOE, D, N_EXPERTS), D ** -0.5),
        'b_router': nrm(ks[21], (N_MOE, N_EXPERTS), 0.01),
        'w1_moe': nrm(ks[22], (N_MOE, N_EXPERTS, D, D_EXPERT), D ** -0.5),
        'w3_moe': nrm(ks[23], (N_MOE, N_EXPERTS, D, D_EXPERT), D ** -0.5),
        'w2_moe': nrm(ks[24], (N_MOE, N_EXPERTS, D_EXPERT, D), D_EXPERT ** -0.5),
        'g_final': 1.0 + nrm(ks[25], (D,), 0.1),
    }


def reference(x_prompt, x_sample, c_prompt, c_sample, state_ret, state_pool,
              w_ada, b_ada, g_mix, w_in, g_ret, w_ret_out, w_pool_map, pool_scale, w_pool_out, w_o,
              g_ffn, w1_dense, w3_dense, w2_dense, w_router, b_router, w1_moe, w3_moe, w2_moe, g_final):
    p = dict(w_ada=w_ada, b_ada=b_ada, g_mix=g_mix, w_in=w_in, g_ret=g_ret, w_ret_out=w_ret_out,
             w_pool_map=w_pool_map, pool_scale=pool_scale, w_pool_out=w_pool_out, w_o=w_o,
             g_ffn=g_ffn, w1_dense=w1_dense, w3_dense=w3_dense, w2_dense=w2_dense,
             w_router=w_router, b_router=b_router, w1_moe=w1_moe, w3_moe=w3_moe, w2_moe=w2_moe,
             g_final=g_final)
    b_p = x_prompt.shape[0]
    s0_prompt = jnp.zeros((DEPTH, b_p, RET_HEADS, RET_DK, RET_DV), jnp.float32)
    buf0_prompt = jnp.zeros((DEPTH, b_p, POOL_BUF, D_POOL), x_prompt.dtype)
    y_prompt, s_prompt, buf_prompt = decoder(x_prompt, c_prompt, s0_prompt, buf0_prompt, 0, p)
    y_sample, s_sample, buf_sample = decoder(x_sample, c_sample, state_ret, state_pool, PAST_LEN, p)
    return (y_prompt, y_sample, s_prompt, buf_prompt, s_sample, buf_sample)
```

```python
import functools

import numpy as np
import jax
import jax.numpy as jnp
from jax import lax
from jax.experimental import pallas as pl
from jax.experimental.pallas import tpu as pltpu

F32 = jnp.float32
BF16 = jnp.bfloat16

D_MODEL = 2048
DEPTH = 2
PAST_LEN = 16384
RET_HEADS = 8
RET_DK = 128
RET_DV = 256
RET_CHUNK = 128
ROPE_BASE = 10000.0
QK_W = RET_HEADS * RET_DK
V_W = RET_HEADS * RET_DV
POOL_WINDOWS = (2, 4, 8, 16)
D_POOL = 1024
POOL_GW = 256
POOL_BUF = 15
POOL_HALO = 16
N_IN = 2 * QK_W + 2 * V_W + D_POOL + 2 * D_MODEL
N_EXPERTS = 8
EPS = 1e-6
N_MOD = 6

O_Q, O_K, O_V, O_G, O_U = 0, QK_W, 2 * QK_W, 2 * QK_W + V_W, 2 * QK_W + 2 * V_W
O_M1 = O_U + D_POOL
O_M2 = O_M1 + D_MODEL

VMEM_LIMIT_BYTES = 56 * 1024 * 1024
TN = 512
TF = 256
TM_PROMPT = 1024
TM_SAMPLE = 512
TM_FFN = 512
TL_POOL = 512


def _params(*sem):
    return pltpu.CompilerParams(dimension_semantics=sem, vmem_limit_bytes=VMEM_LIMIT_BYTES)


class _Tokens:
    def __init__(self, n_batch, seq, n_past, tm, per_row_mod):
        self.n_batch, self.seq, self.n_past = n_batch, seq, n_past
        self.rows = n_batch * seq
        self.tm = min(tm, self.rows)
        self.per_row_mod = per_row_mod
        if not per_row_mod:
            self.tm = min(self.tm, seq)
            assert seq % self.tm == 0
        assert self.rows % self.tm == 0
        self.n_tiles = self.rows // self.tm

    def prep_mod(self, mod):
        if self.per_row_mod:
            return jnp.repeat(mod, self.seq, axis=0).reshape(self.n_tiles, self.tm, N_MOD * D_MODEL)
        return mod.reshape(self.n_batch, 1, N_MOD * D_MODEL)

    def mod_spec(self, chunk, tn=D_MODEL, col_axis=False):
        per_chunk = D_MODEL // tn
        if self.per_row_mod:
            shape = (1, self.tm, tn)
            if col_axis:
                return pl.BlockSpec(shape, lambda m, n, *_: (m, 0, chunk * per_chunk + n))
            return pl.BlockSpec(shape, lambda m, *_: (m, 0, chunk * per_chunk))
        tiles_per_batch = self.seq // self.tm
        shape = (1, 1, tn)
        if col_axis:
            return pl.BlockSpec(shape, lambda m, n, *_: (m // tiles_per_batch, 0, chunk * per_chunk + n))
        return pl.BlockSpec(shape, lambda m, *_: (m // tiles_per_batch, 0, chunk * per_chunk))

    def rope_tables(self):
        half = RET_DK // 2
        pos = jnp.arange(self.seq, dtype=F32) + float(self.n_past)
        inv = ROPE_BASE ** (-jnp.arange(half, dtype=F32) / half)
        ang = pos[:, None] * inv[None, :]
        cos, sin = jnp.cos(ang), jnp.sin(ang)
        cos_full = jnp.concatenate([cos, cos], axis=1)
        sin_signed = jnp.concatenate([-sin, sin], axis=1)
        if self.per_row_mod:
            cos_full = jnp.tile(cos_full, (self.n_batch, 1))
            sin_signed = jnp.tile(sin_signed, (self.n_batch, 1))
        return cos_full, sin_signed

    def rope_spec(self):
        n_blocks = (self.rows if self.per_row_mod else self.seq) // self.tm
        return pl.BlockSpec((self.tm, RET_DK), lambda m, *_: (m % n_blocks, 0))


def _retention_tables(chunk):
    log_gamma = jnp.log1p(-jnp.exp2(-5.0 - jnp.arange(RET_HEADS, dtype=F32)))
    idx = jnp.arange(chunk, dtype=F32)
    rel = idx[:, None] - idx[None, :]
    dmask = jnp.where(rel[None] >= 0, jnp.exp(log_gamma[:, None, None] * jnp.maximum(rel, 0.0)[None]), 0.0)
    q_dec = jnp.exp(log_gamma[:, None] * (idx + 1.0)[None, :])
    k_dec = jnp.exp(log_gamma[:, None] * (chunk - 1.0 - idx)[None, :])
    chunk_dec = jnp.exp(log_gamma * chunk)
    q_dec = jnp.broadcast_to(q_dec[:, :, None], (RET_HEADS, chunk, RET_DV))
    k_dec = jnp.broadcast_to(k_dec[:, :, None], (RET_HEADS, chunk, RET_DK))
    chunk_dec = jnp.broadcast_to(chunk_dec[:, None, None], (RET_HEADS, 1, RET_DV))
    return dmask, q_dec, k_dec, chunk_dec


def _rmsnorm_mod(x, g, scale, shift):
    y = x * lax.rsqrt(jnp.mean(x * x, axis=-1, keepdims=True) + EPS) * g
    return y * (1.0 + scale) + shift


def _silu(x):
    return x * jax.nn.sigmoid(x)


def _ada_kernel(c_ref, w_ref, b_ref, o_ref):
    cs = _silu(c_ref[...]).astype(BF16)
    o_ref[0] = jnp.dot(cs, w_ref[0].astype(BF16), preferred_element_type=F32) + b_ref[0]


def _ada(c_all, w_ada, b_ada):
    nb = c_all.shape[0]
    tn = 1024
    width = N_MOD * D_MODEL
    return pl.pallas_call(
        _ada_kernel,
        grid=(DEPTH, width // tn),
        in_specs=[pl.BlockSpec((nb, D_MODEL), lambda l, n: (0, 0)),
                  pl.BlockSpec((1, D_MODEL, tn), lambda l, n: (l, 0, n)),
                  pl.BlockSpec((1, 1, tn), lambda l, n: (l, 0, n))],
        out_specs=pl.BlockSpec((1, nb, tn), lambda l, n: (l, 0, n)),
        out_shape=jax.ShapeDtypeStruct((DEPTH, nb, width), F32),
        compiler_params=_params("arbitrary", "arbitrary"),
        name="ada",
    )(c_all, w_ada, b_ada.reshape(DEPTH, 1, width))


def _inproj_kernel(x_ref, sc_ref, sh_ref, g_ref, w_ref, cos_ref, sin_ref, o_ref, u_ref, h_scr):
    n = pl.program_id(1)

    @pl.when(n == 0)
    def _():
        h_scr[...] = _rmsnorm_mod(x_ref[...], g_ref[0], sc_ref[0], sh_ref[0]).astype(BF16)

    acc = jnp.dot(h_scr[...], w_ref[0].astype(BF16), preferred_element_type=F32)

    def rope(scale):
        for j in range(TN // RET_DK):
            xh = acc[:, j * RET_DK:(j + 1) * RET_DK]
            r = xh * cos_ref[...] + pltpu.roll(xh, RET_DK // 2, 1) * sin_ref[...]
            if scale != 1.0:
                r = r * scale
            o_ref[:, j * RET_DK:(j + 1) * RET_DK] = r.astype(o_ref.dtype)

    @pl.when(n < O_K // TN)
    def _():
        rope(1.0)

    @pl.when(jnp.logical_and(n >= O_K // TN, n < O_V // TN))
    def _():
        rope(RET_DK ** -0.5)

    @pl.when(jnp.logical_and(n >= O_V // TN, n < O_G // TN))
    def _():
        o_ref[...] = acc.astype(o_ref.dtype)

    @pl.when(jnp.logical_and(n >= O_G // TN, n < O_U // TN))
    def _():
        o_ref[...] = _silu(acc).astype(o_ref.dtype)

    @pl.when(jnp.logical_and(n >= O_U // TN, n < O_M1 // TN))
    def _():
        o_ref[...] = acc.astype(o_ref.dtype)
        u_ref[...] = acc

    @pl.when(n >= O_M1 // TN)
    def _():
        o_ref[...] = jax.nn.sigmoid(acc).astype(o_ref.dtype)


def _inproj(tok, layer, x, mod3, g_mix, w_in, rope_tabs, out_dtype):
    tm = tok.tm
    n_u = D_POOL // TN
    proj, u = pl.pallas_call(
        _inproj_kernel,
        grid=(tok.n_tiles, N_IN // TN),
        in_specs=[pl.BlockSpec((tm, D_MODEL), lambda m, n: (m, 0)),
                  tok.mod_spec(1), tok.mod_spec(0),
                  pl.BlockSpec((1, 1, D_MODEL), lambda m, n: (layer, 0, 0)),
                  pl.BlockSpec((1, D_MODEL, TN), lambda m, n: (layer, 0, n)),
                  tok.rope_spec(), tok.rope_spec()],
        out_specs=[pl.BlockSpec((tm, TN), lambda m, n: (m, n)),
                   pl.BlockSpec((tm, TN), lambda m, n: (m, jnp.clip(n - O_U // TN, 0, n_u - 1)))],
        out_shape=[jax.ShapeDtypeStruct((tok.rows, N_IN), out_dtype),
                   jax.ShapeDtypeStruct((tok.rows, D_POOL), F32)],
        scratch_shapes=[pltpu.VMEM((tm, D_MODEL), BF16)],
        compiler_params=_params("arbitrary", "arbitrary"),
        name="inproj",
    )(x, mod3, mod3, g_mix, w_in, *rope_tabs)
    return proj, u


def _head_norm_gate(o, gret, gate):
    mu = jnp.mean(o, axis=-1, keepdims=True)
    d = o - mu
    var = jnp.mean(d * d, axis=-1, keepdims=True)
    return gate.astype(F32) * (d * lax.rsqrt(var + EPS) * gret)


def _ret_prompt_kernel(q_ref, k_ref, v_ref, g_ref, gret_ref, dmask_ref, qdec_ref, kdec_ref, cdec_ref,
                       *rest):
    y_ref, s_ref = rest[-2:]

    @pl.when(pl.program_id(1) == 0)
    def _():
        s_ref[...] = jnp.zeros_like(s_ref)

    for h in range(RET_HEADS):
        ks, vs = slice(h * RET_DK, (h + 1) * RET_DK), slice(h * RET_DV, (h + 1) * RET_DV)
        q, k, v = q_ref[:, ks], k_ref[:, ks], v_ref[:, vs]
        state = s_ref[0, 0, h]
        scores = lax.dot_general(q, k, (((1,), (1,)), ((), ())), preferred_element_type=F32) * dmask_ref[h]
        inner = jnp.dot(scores.astype(BF16), v, preferred_element_type=F32)
        cross = jnp.dot(q, state.astype(BF16), preferred_element_type=F32) * qdec_ref[h]
        k_decayed = (k.astype(F32) * kdec_ref[h]).astype(BF16)
        s_ref[0, 0, h] = cdec_ref[h] * state + lax.dot_general(
            k_decayed, v, (((0,), (0,)), ((), ())), preferred_element_type=F32)
        y_ref[:, vs] = _head_norm_gate(inner + cross, gret_ref[0, :, vs], g_ref[:, vs]).astype(y_ref.dtype)


def _table_specs(tables):
    return [pl.BlockSpec(t.shape, lambda *_: (0, 0, 0)) for t in tables]


def _ret_prompt(tok, layer, proj, g_ret, s_all):
    n_b, seq = tok.n_batch, tok.seq
    c = RET_CHUNK
    assert seq % c == 0
    nc = seq // c
    tables = _retention_tables(c)
    in_specs = [pl.BlockSpec((c, QK_W), lambda b, i: (b * nc + i, O_Q // QK_W)),
                pl.BlockSpec((c, QK_W), lambda b, i: (b * nc + i, O_K // QK_W)),
                pl.BlockSpec((c, V_W), lambda b, i: (b * nc + i, O_V // V_W)),
                pl.BlockSpec((c, V_W), lambda b, i: (b * nc + i, O_G // V_W)),
                pl.BlockSpec((1, 1, V_W), lambda b, i: (layer, 0, 0))] + _table_specs(tables)
    args = [proj, proj, proj, proj, g_ret, *tables]
    aliases = {}
    if s_all is not None:
        aliases = {len(args): 1}
        in_specs.append(pl.BlockSpec(memory_space=pl.ANY))
        args.append(s_all)
    y, s_all = pl.pallas_call(
        _ret_prompt_kernel,
        grid=(n_b, nc),
        in_specs=in_specs,
        out_specs=[pl.BlockSpec((c, V_W), lambda b, i: (b * nc + i, 0)),
                   pl.BlockSpec((1, 1, RET_HEADS, RET_DK, RET_DV), lambda b, i: (layer, b, 0, 0, 0))],
        out_shape=[jax.ShapeDtypeStruct((tok.rows, V_W), BF16),
                   jax.ShapeDtypeStruct((DEPTH, n_b, RET_HEADS, RET_DK, RET_DV), F32)],
        input_output_aliases=aliases,
        compiler_params=_params("arbitrary", "arbitrary"),
        name="ret_prompt",
    )(*args)
    return y, s_all


def _ret_sample_kernel(q_ref, k_ref, v_ref, g_ref, gret_ref, s0_ref, dmask_ref, qdec_ref, kdec_ref, cdec_ref,
                       *rest, bb, seq):
    y_ref, s_ref = rest[-2:]
    for b in range(bb):
        for h in range(RET_HEADS):
            ks, vs = slice(h * RET_DK, (h + 1) * RET_DK), slice(h * RET_DV, (h + 1) * RET_DV)
            q, k, v = q_ref[b, :, ks], k_ref[b, :, ks], v_ref[b, :, vs]
            state = s0_ref[0, b, h]
            inner = jnp.zeros((seq, RET_DV), F32)
            for j in range(seq):
                s_j = jnp.sum(q * k[j:j + 1, :], axis=-1, keepdims=True) * dmask_ref[h, :, j:j + 1]
                inner = inner + s_j * v[j:j + 1, :]
            cross = jnp.dot(q.astype(BF16), state.astype(BF16), preferred_element_type=F32) * qdec_ref[h]
            k_decayed = (k * kdec_ref[h]).astype(BF16)
            s_ref[0, b, h] = cdec_ref[h] * state + lax.dot_general(
                k_decayed, v.astype(BF16), (((0,), (0,)), ((), ())), preferred_element_type=F32)
            y_ref[b, :, vs] = _head_norm_gate(inner + cross, gret_ref[0, :, vs], g_ref[b, :, vs])


def _ret_sample(tok, layer, proj, g_ret, state_ret, s_all):
    n_b, seq = tok.n_batch, tok.seq
    bb = 4
    assert n_b % bb == 0
    tables = _retention_tables(seq)
    proj3 = proj.reshape(n_b, seq, N_IN)
    state_spec = pl.BlockSpec((1, bb, RET_HEADS, RET_DK, RET_DV), lambda i: (layer, i, 0, 0, 0))
    in_specs = [pl.BlockSpec((bb, seq, QK_W), lambda i: (i, 0, O_Q // QK_W)),
                pl.BlockSpec((bb, seq, QK_W), lambda i: (i, 0, O_K // QK_W)),
                pl.BlockSpec((bb, seq, V_W), lambda i: (i, 0, O_V // V_W)),
                pl.BlockSpec((bb, seq, V_W), lambda i: (i, 0, O_G // V_W)),
                pl.BlockSpec((1, 1, V_W), lambda i: (layer, 0, 0)),
                state_spec] + _table_specs(tables)
    args = [proj3, proj3, proj3, proj3, g_ret, state_ret, *tables]
    aliases = {}
    if s_all is not None:
        aliases = {len(args): 1}
        in_specs.append(pl.BlockSpec(memory_space=pl.ANY))
        args.append(s_all)
    y, s_all = pl.pallas_call(
        functools.partial(_ret_sample_kernel, bb=bb, seq=seq),
        grid=(n_b // bb,),
        in_specs=in_specs,
        out_specs=[pl.BlockSpec((bb, seq, V_W), lambda i: (i, 0, 0)), state_spec],
        out_shape=[jax.ShapeDtypeStruct((n_b, seq, V_W), F32),
                   jax.ShapeDtypeStruct(state_ret.shape, F32)],
        input_output_aliases=aliases,
        compiler_params=_params("arbitrary"),
        name="ret_sample",
    )(*args)
    return y.reshape(tok.rows, V_W), s_all


def _pool_map(pooled, wmap_ref, scale_ref, gi):
    cols = slice(gi * POOL_GW, (gi + 1) * POOL_GW)
    mixed = jnp.dot(pooled.astype(BF16), wmap_ref[0, gi].astype(BF16), preferred_element_type=F32)
    return mixed * scale_ref[0, :, cols]


def _pool_prompt_kernel(u_ref, halo_ref, wmap_ref, scale_ref, o_ref, uf_scr, *, tl):
    first = pl.program_id(1) == 0
    uf_scr[0:POOL_HALO, :] = jnp.where(first, 0.0, halo_ref[...])
    uf_scr[POOL_HALO:, :] = u_ref[...]
    row = lax.broadcasted_iota(jnp.int32, (tl, POOL_GW), 0) + pl.program_id(1) * tl
    for gi, w in enumerate(POOL_WINDOWS):
        cols = slice(gi * POOL_GW, (gi + 1) * POOL_GW)
        u = uf_scr[POOL_HALO:, cols]
        acc = u
        for j in range(1, w):
            acc = acc + uf_scr[POOL_HALO - j:POOL_HALO - j + tl, cols]
        cnt = jnp.minimum(w, row + 1).astype(F32)
        o_ref[:, cols] = _pool_map(acc / cnt - u, wmap_ref, scale_ref, gi).astype(o_ref.dtype)


def _pool_prompt(tok, layer, u, w_pool_map, pool_scale):
    n_b, seq = tok.n_batch, tok.seq
    tl = min(TL_POOL, seq)
    assert seq % tl == 0 and tl % POOL_HALO == 0
    nl = seq // tl
    per = tl // POOL_HALO
    return pl.pallas_call(
        functools.partial(_pool_prompt_kernel, tl=tl),
        grid=(n_b, nl),
        in_specs=[pl.BlockSpec((tl, D_POOL), lambda b, l: (b * nl + l, 0)),
                  pl.BlockSpec((POOL_HALO, D_POOL), lambda b, l: (jnp.maximum((b * nl + l) * per - 1, 0), 0)),
                  pl.BlockSpec((1,) + w_pool_map.shape[1:], lambda b, l: (layer, 0, 0, 0)),
                  pl.BlockSpec((1, 1, D_POOL), lambda b, l: (layer, 0, 0))],
        out_specs=pl.BlockSpec((tl, D_POOL), lambda b, l: (b * nl + l, 0)),
        out_shape=jax.ShapeDtypeStruct((tok.rows, D_POOL), BF16),
        scratch_shapes=[pltpu.VMEM((POOL_HALO + tl, D_POOL), F32)],
        compiler_params=_params("arbitrary", "arbitrary"),
        name="pool_prompt",
    )(u, u, w_pool_map, pool_scale)


def _pool_sample_kernel(buf_ref, u_ref, wmap_ref, scale_ref, o_ref, *, seq, n_b):
    def row(j, cols):
        return buf_ref[j, :, cols] if j < POOL_BUF else u_ref[j - POOL_BUF, :, cols]

    for gi, w in enumerate(POOL_WINDOWS):
        cols = slice(gi * POOL_GW, (gi + 1) * POOL_GW)
        for t in range(seq):
            acc = row(POOL_BUF + t, cols)
            for j in range(1, w):
                acc = acc + row(POOL_BUF + t - j, cols)
            pooled = acc / float(w) - row(POOL_BUF + t, cols)
            o_ref[t, :, cols] = _pool_map(pooled, wmap_ref, scale_ref, gi)


def _pool_sample(tok, layer, u, buf_tm, w_pool_map, pool_scale):
    n_b, seq = tok.n_batch, tok.seq
    assert tok.n_past >= max(POOL_WINDOWS)
    u_tm = u.reshape(n_b, seq, D_POOL).transpose(1, 0, 2)
    mixed_tm = pl.pallas_call(
        functools.partial(_pool_sample_kernel, seq=seq, n_b=n_b),
        grid=(1,),
        in_specs=[pl.BlockSpec(buf_tm.shape, lambda i: (0, 0, 0)),
                  pl.BlockSpec(u_tm.shape, lambda i: (0, 0, 0)),
                  pl.BlockSpec((1,) + w_pool_map.shape[1:], lambda i: (layer, 0, 0, 0)),
                  pl.BlockSpec((1, 1, D_POOL), lambda i: (layer, 0, 0))],
        out_specs=pl.BlockSpec(u_tm.shape, lambda i: (0, 0, 0)),
        out_shape=jax.ShapeDtypeStruct(u_tm.shape, F32),
        compiler_params=_params("arbitrary"),
        name="pool_sample",
    )(buf_tm, u_tm, w_pool_map, pool_scale)
    return mixed_tm.transpose(1, 0, 2).reshape(tok.rows, D_POOL)


def _merge_kernel(yret_ref, mixed_ref, wr_ref, wp_ref, g1_ref, g2_ref, o_ref):
    ret = jnp.dot(yret_ref[...].astype(BF16), wr_ref[0].astype(BF16), preferred_element_type=F32)
    pool = jnp.dot(mixed_ref[...].astype(BF16), wp_ref[0].astype(BF16), preferred_element_type=F32)
    o_ref[...] = (g1_ref[...].astype(F32) * ret + g2_ref[...].astype(F32) * pool).astype(o_ref.dtype)


def _merge(tok, layer, yret, mixed, proj, w_ret_out, w_pool_out):
    tm = tok.tm
    return pl.pallas_call(
        _merge_kernel,
        grid=(tok.n_tiles, D_MODEL // TN),
        in_specs=[pl.BlockSpec((tm, V_W), lambda m, n: (m, 0)),
                  pl.BlockSpec((tm, D_POOL), lambda m, n: (m, 0)),
                  pl.BlockSpec((1, V_W, TN), lambda m, n: (layer, 0, n)),
                  pl.BlockSpec((1, D_POOL, TN), lambda m, n: (layer, 0, n)),
                  pl.BlockSpec((tm, TN), lambda m, n: (m, O_M1 // TN + n)),
                  pl.BlockSpec((tm, TN), lambda m, n: (m, O_M2 // TN + n))],
        out_specs=pl.BlockSpec((tm, TN), lambda m, n: (m, n)),
        out_shape=jax.ShapeDtypeStruct((tok.rows, D_MODEL), BF16),
        compiler_params=_params("arbitrary", "arbitrary"),
        name="merge",
    )(yret, mixed, w_ret_out, w_pool_out, proj, proj)


def _wo_kernel(merged_ref, w_ref, x_ref, gt_ref, o_ref):
    mix = jnp.dot(merged_ref[...], w_ref[0].astype(BF16), preferred_element_type=F32)
    o_ref[...] = x_ref[...] + (1.0 + gt_ref[0]) * mix


def _wo(tok, layer, merged, x, mod3, w_o):
    tm = tok.tm
    return pl.pallas_call(
        _wo_kernel,
        grid=(tok.n_tiles, D_MODEL // TN),
        in_specs=[pl.BlockSpec((tm, D_MODEL), lambda m, n: (m, 0)),
                  pl.BlockSpec((1, D_MODEL, TN), lambda m, n: (layer, 0, n)),
                  pl.BlockSpec((tm, TN), lambda m, n: (m, n)),
                  tok.mod_spec(2, tn=TN, col_axis=True)],
        out_specs=pl.BlockSpec((tm, TN), lambda m, n: (m, n)),
        out_shape=jax.ShapeDtypeStruct((tok.rows, D_MODEL), F32),
        compiler_params=_params("arbitrary", "arbitrary"),
        name="wo",
    )(merged, w_o, x, mod3)


def _ffn_kernel(*refs, n_f, with_combine, with_final):
    x_ref, sc_ref, sh_ref, gt_ref, g_ref, w1_ref, w3_ref, w2_ref = refs[:8]
    rest = list(refs[8:])
    comb_ref = rest.pop(0) if with_combine else None
    gfin_ref = rest.pop(0) if with_final else None
    o_ref, h_scr = rest
    e, f = pl.program_id(1), pl.program_id(2)

    @pl.when(jnp.logical_and(e == 0, f == 0))
    def _():
        h_scr[...] = _rmsnorm_mod(x_ref[...], g_ref[0], sc_ref[0], sh_ref[0]).astype(BF16)
        o_ref[...] = jnp.zeros_like(o_ref)

    h = h_scr[...]
    a = jnp.dot(h, w1_ref[0, 0].astype(BF16), preferred_element_type=F32)
    b = jnp.dot(h, w3_ref[0, 0].astype(BF16), preferred_element_type=F32)
    part = jnp.dot((_silu(a) * b).astype(BF16), w2_ref[0, 0].astype(BF16), preferred_element_type=F32)
    if with_combine:
        part = comb_ref[0] * part
    o_ref[...] += part

    @pl.when(jnp.logical_and(e == pl.num_programs(1) - 1, f == n_f - 1))
    def _():
        y = x_ref[...] + (1.0 + gt_ref[0]) * o_ref[...]
        if with_final:
            y = y * lax.rsqrt(jnp.mean(y * y, axis=-1, keepdims=True) + EPS) * gfin_ref[...]
        o_ref[...] = y


def _ffn(tok, layer, x, mod, g_ffn, w1, w3, w2, combine=None, g_final=None):
    j = layer // 2
    n_e, d_ff = w1.shape[1], w1.shape[3]
    tok_f = _Tokens(tok.n_batch, tok.seq, tok.n_past, TM_FFN, tok.per_row_mod)
    tm, n_tiles = tok_f.tm, tok_f.n_tiles
    mod3 = tok_f.prep_mod(mod)
    assert d_ff % TF == 0
    n_f = d_ff // TF
    in_specs = [pl.BlockSpec((tm, D_MODEL), lambda m, e, f: (m, 0)),
                tok_f.mod_spec(4), tok_f.mod_spec(3), tok_f.mod_spec(5),
                pl.BlockSpec((1, 1, D_MODEL), lambda m, e, f: (layer, 0, 0)),
                pl.BlockSpec((1, 1, D_MODEL, TF), lambda m, e, f: (j, e, 0, f)),
                pl.BlockSpec((1, 1, D_MODEL, TF), lambda m, e, f: (j, e, 0, f)),
                pl.BlockSpec((1, 1, TF, D_MODEL), lambda m, e, f: (j, e, f, 0))]
    args = [x, mod3, mod3, mod3, g_ffn, w1, w3, w2]
    if combine is not None:
        in_specs.append(pl.BlockSpec((1, tm, 1), lambda m, e, f: (e, m, 0)))
        args.append(combine)
    if g_final is not None:
        in_specs.append(pl.BlockSpec((1, D_MODEL), lambda m, e, f: (0, 0)))
        args.append(g_final)
    return pl.pallas_call(
        functools.partial(_ffn_kernel, n_f=n_f, with_combine=combine is not None, with_final=g_final is not None),
        grid=(n_tiles, n_e, n_f),
        in_specs=in_specs,
        out_specs=pl.BlockSpec((tm, D_MODEL), lambda m, e, f: (m, 0)),
        out_shape=jax.ShapeDtypeStruct((tok.rows, D_MODEL), F32),
        scratch_shapes=[pltpu.VMEM((tm, D_MODEL), BF16)],
        compiler_params=_params("arbitrary", "arbitrary", "arbitrary"),
        name="ffn",
    )(*args)


def _router_kernel(x_ref, sc_ref, sh_ref, g_ref, w_ref, b_ref, o_ref):
    h = _rmsnorm_mod(x_ref[...], g_ref[0], sc_ref[0], sh_ref[0]).astype(BF16)
    logits = jnp.dot(h, w_ref[0].astype(BF16), preferred_element_type=F32) + b_ref[0]
    idx =lax.broadcasted_iota(jnp.int32, logits.shape, 1)
    top1 = jnp.max(logits, axis=-1, keepdims=True)
    i1 = jnp.min(jnp.where(logits == top1, idx, N_EXPERTS), axis=-1, keepdims=True)
    rest = jnp.where(idx == i1, -jnp.inf, logits)
    top2 = jnp.max(rest, axis=-1, keepdims=True)
    i2 = jnp.min(jnp.where(rest == top2, idx, N_EXPERTS), axis=-1, keepdims=True)
    e2 = jnp.exp(top2 - top1)
    denom = 1.0 + e2
    o_ref[...] = jnp.where(idx == i1, 1.0 / denom, 0.0) + jnp.where(idx == i2, e2 / denom, 0.0)


def _router(tok, layer, x, mod3, g_ffn, w_router, b_router):
    j = layer // 2
    tm = tok.tm
    return pl.pallas_call(
        _router_kernel,
        grid=(tok.n_tiles,),
        in_specs=[pl.BlockSpec((tm, D_MODEL), lambda m: (m, 0)),
                  tok.mod_spec(4), tok.mod_spec(3),
                  pl.BlockSpec((1, 1, D_MODEL), lambda m: (layer, 0, 0)),
                  pl.BlockSpec((1, D_MODEL, N_EXPERTS), lambda m: (j, 0, 0)),
                  pl.BlockSpec((1, 1, N_EXPERTS), lambda m: (j, 0, 0))],
        out_specs=pl.BlockSpec((tm, N_EXPERTS), lambda m: (m, 0)),
        out_shape=jax.ShapeDtypeStruct((tok.rows, N_EXPERTS), F32),
        compiler_params=_params("arbitrary"),
        name="router",
    )(x, mod3, mod3, g_ffn, w_router, b_router)


def _decoder(tok, x, mod_layers, ret_state, pool_state, p):
    rope_tabs = tok.rope_tables()
    is_prompt = ret_state is None
    act_dtype = BF16 if is_prompt else F32
    u_layers = []
    s_all = None
    for i in range(DEPTH):
        mod3 = tok.prep_mod(mod_layers[i])
        proj, u = _inproj(tok, i, x, mod3, p['g_mix'], p['w_in'], rope_tabs, act_dtype)
        u_layers.append(u)
        if is_prompt:
            yret, s_all = _ret_prompt(tok, i, proj, p['g_ret'], s_all)
            mixed = _pool_prompt(tok, i, u, p['w_pool_map'], p['pool_scale'])
        else:
            yret, s_all = _ret_sample(tok, i, proj, p['g_ret'], ret_state, s_all)
            mixed = _pool_sample(tok, i, u, pool_state[i].transpose(1, 0, 2), p['w_pool_map'], p['pool_scale'])
        merged = _merge(tok, i, yret, mixed, proj, p['w_ret_out'], p['w_pool_out'])
        x = _wo(tok, i, merged, x, mod3, p['w_o'])
        g_final = p['g_final'] if i == DEPTH - 1 else None
        if i % 2 == 0:
            x = _ffn(tok, i, x, mod_layers[i], p['g_ffn'], p['w1_dense'][:, None], p['w3_dense'][:, None],
                     p['w2_dense'][:, None], g_final=g_final)
        else:
            comb = _router(tok, i, x, mod3, p['g_ffn'], p['w_router'], p['b_router'])
            comb = comb.T.reshape(N_EXPERTS, tok.rows, 1)
            x = _ffn(tok, i, x, mod_layers[i], p['g_ffn'], p['w1_moe'], p['w3_moe'], p['w2_moe'],
                     combine=comb, g_final=g_final)
    return x, s_all, u_layers


def kernel(x_prompt, x_sample, c_prompt, c_sample, state_ret, state_pool, w_ada, b_ada, g_mix, w_in, g_ret,
           w_ret_out, w_pool_map, pool_scale, w_pool_out, w_o, g_ffn, w1_dense, w3_dense, w2_dense, w_router,
           b_router, w1_moe, w3_moe, w2_moe, g_final):
    row3 = lambda a: a.reshape(a.shape[0], 1, a.shape[1])
    p = dict(g_mix=row3(g_mix), w_in=w_in, g_ret=row3(g_ret), w_ret_out=w_ret_out, w_pool_map=w_pool_map,
             pool_scale=row3(pool_scale), w_pool_out=w_pool_out, w_o=w_o, g_ffn=row3(g_ffn),
             w1_dense=w1_dense, w3_dense=w3_dense, w2_dense=w2_dense, w_router=w_router, b_router=row3(b_router),
             w1_moe=w1_moe, w3_moe=w3_moe, w2_moe=w2_moe, g_final=g_final.reshape(1, D_MODEL))
    n_bp, seq_p, _ = x_prompt.shape
    n_bs, seq_s, _ = x_sample.shape

    mod = _ada(jnp.concatenate([c_prompt, c_sample], axis=0), w_ada, b_ada)

    tok_p = _Tokens(n_bp, seq_p, 0, TM_PROMPT, per_row_mod=False)
    tok_s = _Tokens(n_bs, seq_s, PAST_LEN, TM_SAMPLE, per_row_mod=True)

    y_p, s_prompt, u_p = _decoder(tok_p, x_prompt.reshape(tok_p.rows, D_MODEL), [mod[i, :n_bp] for i in range(DEPTH)],
                                  None, None, p)
    y_s, s_sample, u_s = _decoder(tok_s, x_sample.reshape(tok_s.rows, D_MODEL), [mod[i, n_bp:] for i in range(DEPTH)],
                                  state_ret, state_pool, p)

    buf_prompt = jnp.stack([u.reshape(n_bp, seq_p, D_POOL)[:, seq_p - POOL_BUF:] for u in u_p])
    buf_sample = jnp.stack([
        jnp.concatenate([state_pool[i], u_s[i].reshape(n_bs, seq_s, D_POOL)], axis=1)[:, -POOL_BUF:]
        for i in range(DEPTH)])
    return (y_p.reshape(x_prompt.shape), y_s.reshape(x_sample.shape), s_prompt, buf_prompt, s_sample, buf_sample)
```

```python
import functools

import numpy as np
import jax
import jax.numpy as jnp
from jax import lax
from jax.experimental import pallas as pl
from jax.experimental.pallas import tpu as pltpu

F32 = jnp.float32
BF16 = jnp.bfloat16

D_MODEL = 2048
DEPTH = 2
PAST_LEN = 16384
RET_HEADS = 8
RET_DK = 128
RET_DV = 256
RET_CHUNK = 128
ROPE_BASE = 10000.0
QK_W = RET_HEADS * RET_DK
V_W = RET_HEADS * RET_DV
POOL_WINDOWS = (2, 4, 8, 16)
D_POOL = 1024
POOL_GW = 256
POOL_BUF = 15
POOL_HALO = 16
N_IN = 2 * QK_W + 2 * V_W + D_POOL + 2 * D_MODEL
N_EXPERTS = 8
EPS = 1e-6
N_MOD = 6

O_Q, O_K, O_V, O_G, O_U = 0, QK_W, 2 * QK_W, 2 * QK_W + V_W, 2 * QK_W + 2 * V_W
O_M1 = O_U + D_POOL
O_M2 = O_M1 + D_MODEL

VMEM_LIMIT_BYTES = 56 * 1024 * 1024
TN = 512
TF = 256
TM_PROMPT = 1024
TM_SAMPLE = 512
TM_FFN = 512
TL_POOL = 512
TM_MOE = 512
TM_COMBINE = 256


def _params(*sem):
    return pltpu.CompilerParams(dimension_semantics=sem, vmem_limit_bytes=VMEM_LIMIT_BYTES)


class _Tokens:
    def __init__(self, n_batch, seq, n_past, tm, per_row_mod):
        self.n_batch, self.seq, self.n_past = n_batch, seq, n_past
        self.rows = n_batch * seq
        self.tm = min(tm, self.rows)
        self.per_row_mod = per_row_mod
        if not per_row_mod:
            self.tm = min(self.tm, seq)
            assert seq % self.tm == 0
        assert self.rows % self.tm == 0
        self.n_tiles = self.rows // self.tm

    def prep_mod(self, mod):
        if self.per_row_mod:
            return jnp.repeat(mod, self.seq, axis=0).reshape(self.n_tiles, self.tm, N_MOD * D_MODEL)
        return mod.reshape(self.n_batch, 1, N_MOD * D_MODEL)

    def mod_spec(self, chunk, tn=D_MODEL, col_axis=False):
        per_chunk = D_MODEL // tn
        if self.per_row_mod:
            shape = (1, self.tm, tn)
            if col_axis:
                return pl.BlockSpec(shape, lambda m, n, *_: (m, 0, chunk * per_chunk + n))
            return pl.BlockSpec(shape, lambda m, *_: (m, 0, chunk * per_chunk))
        tiles_per_batch = self.seq // self.tm
        shape = (1, 1, tn)
        if col_axis:
            return pl.BlockSpec(shape, lambda m, n, *_: (m // tiles_per_batch, 0, chunk * per_chunk + n))
        return pl.BlockSpec(shape, lambda m, *_: (m // tiles_per_batch, 0, chunk * per_chunk))

    def rope_tables(self):
        half = RET_DK // 2
        pos = jnp.arange(self.seq, dtype=F32) + float(self.n_past)
        inv = ROPE_BASE ** (-jnp.arange(half, dtype=F32) / half)
        ang = pos[:, None] * inv[None, :]
        cos, sin = jnp.cos(ang), jnp.sin(ang)
        cos_full = jnp.concatenate([cos, cos], axis=1)
        sin_signed = jnp.concatenate([-sin, sin], axis=1)
        if self.per_row_mod:
            cos_full = jnp.tile(cos_full, (self.n_batch, 1))
            sin_signed = jnp.tile(sin_signed, (self.n_batch, 1))
        return cos_full, sin_signed

    def rope_spec(self):
        n_blocks = (self.rows if self.per_row_mod else self.seq) // self.tm
        return pl.BlockSpec((self.tm, RET_DK), lambda m, *_: (m % n_blocks, 0))


def _retention_tables(chunk):
    log_gamma = jnp.log1p(-jnp.exp2(-5.0 - jnp.arange(RET_HEADS, dtype=F32)))
    idx = jnp.arange(chunk, dtype=F32)
    rel = idx[:, None] - idx[None, :]
    dmask = jnp.where(rel[None] >= 0, jnp.exp(log_gamma[:, None, None] * jnp.maximum(rel, 0.0)[None]), 0.0)
    q_dec = jnp.exp(log_gamma[:, None] * (idx + 1.0)[None, :])
    k_dec = jnp.exp(log_gamma[:, None] * (chunk - 1.0 - idx)[None, :])
    chunk_dec = jnp.exp(log_gamma * chunk)
    q_dec = jnp.broadcast_to(q_dec[:, :, None], (RET_HEADS, chunk, RET_DV))
    k_dec = jnp.broadcast_to(k_dec[:, :, None], (RET_HEADS, chunk, RET_DK))
    chunk_dec = jnp.broadcast_to(chunk_dec[:, None, None], (RET_HEADS, 1, RET_DV))
    return dmask, q_dec, k_dec, chunk_dec


def _rmsnorm_mod(x, g, scale, shift):
    y = x * lax.rsqrt(jnp.mean(x * x, axis=-1, keepdims=True) + EPS) * g
    return y * (1.0 + scale) + shift


def _silu(x):
    return x * jax.nn.sigmoid(x)


def _ada_kernel(c_ref, w_ref, b_ref, o_ref):
    cs = _silu(c_ref[...]).astype(BF16)
    o_ref[0] = jnp.dot(cs, w_ref[0].astype(BF16), preferred_element_type=F32) + b_ref[0]


def _ada(c_all, w_ada, b_ada):
    nb = c_all.shape[0]
    tn = 1024
    width = N_MOD * D_MODEL
    return pl.pallas_call(
        _ada_kernel,
        grid=(DEPTH, width // tn),
        in_specs=[pl.BlockSpec((nb, D_MODEL), lambda l, n: (0, 0)),
                  pl.BlockSpec((1, D_MODEL, tn), lambda l, n: (l, 0, n)),
                  pl.BlockSpec((1, 1, tn), lambda l, n: (l, 0, n))],
        out_specs=pl.BlockSpec((1, nb, tn), lambda l, n: (l, 0, n)),
        out_shape=jax.ShapeDtypeStruct((DEPTH, nb, width), F32),
        compiler_params=_params("arbitrary", "arbitrary"),
        name="ada",
    )(c_all, w_ada, b_ada.reshape(DEPTH, 1, width))


def _inproj_kernel(x_ref, sc_ref, sh_ref, g_ref, w_ref, cos_ref, sin_ref, o_ref, u_ref, h_scr):
    n = pl.program_id(1)

    @pl.when(n == 0)
    def _():
        h_scr[...] = _rmsnorm_mod(x_ref[...], g_ref[0], sc_ref[0], sh_ref[0]).astype(BF16)

    acc = jnp.dot(h_scr[...], w_ref[0].astype(BF16), preferred_element_type=F32)

    def rope(scale):
        for j in range(TN // RET_DK):
            xh = acc[:, j * RET_DK:(j + 1) * RET_DK]
            r = xh * cos_ref[...] + pltpu.roll(xh, RET_DK // 2, 1) * sin_ref[...]
            if scale != 1.0:
                r = r * scale
            o_ref[:, j * RET_DK:(j + 1) * RET_DK] = r.astype(o_ref.dtype)

    @pl.when(n < O_K // TN)
    def _():
        rope(1.0)

    @pl.when(jnp.logical_and(n >= O_K // TN, n < O_V // TN))
    def _():
        rope(RET_DK ** -0.5)

    @pl.when(jnp.logical_and(n >= O_V // TN, n < O_G // TN))
    def _():
        o_ref[...] = acc.astype(o_ref.dtype)

    @pl.when(jnp.logical_and(n >= O_G // TN, n < O_U // TN))
    def _():
        o_ref[...] = _silu(acc).astype(o_ref.dtype)

    @pl.when(jnp.logical_and(n >= O_U // TN, n < O_M1 // TN))
    def _():
        o_ref[...] = acc.astype(o_ref.dtype)
        u_ref[...] = acc

    @pl.when(n >= O_M1 // TN)
    def _():
        o_ref[...] = jax.nn.sigmoid(acc).astype(o_ref.dtype)


def _inproj(tok, layer, x, mod3, g_mix, w_in, rope_tabs, out_dtype):
    tm = tok.tm
    n_u = D_POOL // TN
    proj, u = pl.pallas_call(
        _inproj_kernel,
        grid=(tok.n_tiles, N_IN // TN),
        in_specs=[pl.BlockSpec((tm, D_MODEL), lambda m, n: (m, 0)),
                  tok.mod_spec(1), tok.mod_spec(0),
                  pl.BlockSpec((1, 1, D_MODEL), lambda m, n: (layer, 0, 0)),
                  pl.BlockSpec((1, D_MODEL, TN), lambda m, n: (layer, 0, n)),
                  tok.rope_spec(), tok.rope_spec()],
        out_specs=[pl.BlockSpec((tm, TN), lambda m, n: (m, n)),
                   pl.BlockSpec((tm, TN), lambda m, n: (m, jnp.clip(n - O_U // TN, 0, n_u - 1)))],
        out_shape=[jax.ShapeDtypeStruct((tok.rows, N_IN), out_dtype),
                   jax.ShapeDtypeStruct((tok.rows, D_POOL), F32)],
        scratch_shapes=[pltpu.VMEM((tm, D_MODEL), BF16)],
        compiler_params=_params("arbitrary", "arbitrary"),
        name="inproj",
    )(x, mod3, mod3, g_mix, w_in, *rope_tabs)
    return proj, u


def _head_norm_gate(o, gret, gate):
    mu = jnp.mean(o, axis=-1, keepdims=True)
    d = o - mu
    var = jnp.mean(d * d, axis=-1, keepdims=True)
    return gate.astype(F32) * (d * lax.rsqrt(var + EPS) * gret)


def _ret_prompt_kernel(q_ref, k_ref, v_ref, g_ref, gret_ref, dmask_ref, qdec_ref, kdec_ref, cdec_ref,
                       *rest):
    y_ref, s_ref = rest[-2:]

    @pl.when(pl.program_id(1) == 0)
    def _():
        s_ref[...] = jnp.zeros_like(s_ref)

    for h in range(RET_HEADS):
        ks, vs = slice(h * RET_DK, (h + 1) * RET_DK), slice(h * RET_DV, (h + 1) * RET_DV)
        q, k, v = q_ref[:, ks], k_ref[:, ks], v_ref[:, vs]
        state = s_ref[0, 0, h]
        scores = lax.dot_general(q, k, (((1,), (1,)), ((), ())), preferred_element_type=F32) * dmask_ref[h]
        inner = jnp.dot(scores.astype(BF16), v, preferred_element_type=F32)
        cross = jnp.dot(q, state.astype(BF16), preferred_element_type=F32) * qdec_ref[h]
        k_decayed = (k.astype(F32) * kdec_ref[h]).astype(BF16)
        s_ref[0, 0, h] = cdec_ref[h] * state + lax.dot_general(
            k_decayed, v, (((0,), (0,)), ((), ())), preferred_element_type=F32)
        y_ref[:, vs] = _head_norm_gate(inner + cross, gret_ref[0, :, vs], g_ref[:, vs]).astype(y_ref.dtype)


def _table_specs(tables):
    return [pl.BlockSpec(t.shape, lambda *_: (0, 0, 0)) for t in tables]


def _ret_prompt(tok, layer, proj, g_ret, s_all):
    n_b, seq = tok.n_batch, tok.seq
    c = RET_CHUNK
    assert seq % c == 0
    nc = seq // c
    tables = _retention_tables(c)
    in_specs = [pl.BlockSpec((c, QK_W), lambda b, i: (b * nc + i, O_Q // QK_W)),
                pl.BlockSpec((c, QK_W), lambda b, i: (b * nc + i, O_K // QK_W)),
                pl.BlockSpec((c, V_W), lambda b, i: (b * nc + i, O_V // V_W)),
                pl.BlockSpec((c, V_W), lambda b, i: (b * nc + i, O_G // V_W)),
                pl.BlockSpec((1, 1, V_W), lambda b, i: (layer, 0, 0))] + _table_specs(tables)
    args = [proj, proj, proj, proj, g_ret, *tables]
    aliases = {}
    if s_all is not None:
        aliases = {len(args): 1}
        in_specs.append(pl.BlockSpec(memory_space=pl.ANY))
        args.append(s_all)
    y, s_all = pl.pallas_call(
        _ret_prompt_kernel,
        grid=(n_b, nc),
        in_specs=in_specs,
        out_specs=[pl.BlockSpec((c, V_W), lambda b, i: (b * nc + i, 0)),
                   pl.BlockSpec((1, 1, RET_HEADS, RET_DK, RET_DV), lambda b, i: (layer, b, 0, 0, 0))],
        out_shape=[jax.ShapeDtypeStruct((tok.rows, V_W), BF16),
                   jax.ShapeDtypeStruct((DEPTH, n_b, RET_HEADS, RET_DK, RET_DV), F32)],
        input_output_aliases=aliases,
        compiler_params=_params("arbitrary", "arbitrary"),
        name="ret_prompt",
    )(*args)
    return y, s_all


def _ret_sample_kernel(q_ref, k_ref, v_ref, g_ref, gret_ref, s0_ref, dmask_ref, qdec_ref, kdec_ref, cdec_ref,
                       *rest, bb, seq):
    y_ref, s_ref = rest[-2:]
    for b in range(bb):
        for h in range(RET_HEADS):
            ks, vs = slice(h * RET_DK, (h + 1) * RET_DK), slice(h * RET_DV, (h + 1) * RET_DV)
            q, k, v = q_ref[b, :, ks], k_ref[b, :, ks], v_ref[b, :, vs]
            state = s0_ref[0, b, h]
            inner = jnp.zeros((seq, RET_DV), F32)
            for j in range(seq):
                s_j = jnp.sum(q * k[j:j + 1, :], axis=-1, keepdims=True) * dmask_ref[h, :, j:j + 1]
                inner = inner + s_j * v[j:j + 1, :]
            cross = jnp.dot(q.astype(BF16), state.astype(BF16), preferred_element_type=F32) * qdec_ref[h]
            k_decayed = (k * kdec_ref[h]).astype(BF16)
            s_ref[0, b, h] = cdec_ref[h] * state + lax.dot_general(
                k_decayed, v.astype(BF16), (((0,), (0,)), ((), ())), preferred_element_type=F32)
            y_ref[b, :, vs] = _head_norm_gate(inner + cross, gret_ref[0, :, vs], g_ref[b, :, vs])


def _ret_sample(tok, layer, proj, g_ret, state_ret, s_all):
    n_b, seq = tok.n_batch, tok.seq
    bb = 4
    assert n_b % bb == 0
    tables = _retention_tables(seq)
    proj3 = proj.reshape(n_b, seq, N_IN)
    state_spec = pl.BlockSpec((1, bb, RET_HEADS, RET_DK, RET_DV), lambda i: (layer, i, 0, 0, 0))
    in_specs = [pl.BlockSpec((bb, seq, QK_W), lambda i: (i, 0, O_Q // QK_W)),
                pl.BlockSpec((bb, seq, QK_W), lambda i: (i, 0, O_K // QK_W)),
                pl.BlockSpec((bb, seq, V_W), lambda i: (i, 0, O_V // V_W)),
                pl.BlockSpec((bb, seq, V_W), lambda i: (i, 0, O_G // V_W)),
                pl.BlockSpec((1, 1, V_W), lambda i: (layer, 0, 0)),
                state_spec] + _table_specs(tables)
    args = [proj3, proj3, proj3, proj3, g_ret, state_ret, *tables]
    aliases = {}
    if s_all is not None:
        aliases = {len(args): 1}
        in_specs.append(pl.BlockSpec(memory_space=pl.ANY))
        args.append(s_all)
    y, s_all = pl.pallas_call(
        functools.partial(_ret_sample_kernel, bb=bb, seq=seq),
        grid=(n_b // bb,),
        in_specs=in_specs,
        out_specs=[pl.BlockSpec((bb, seq, V_W), lambda i: (i, 0, 0)), state_spec],
        out_shape=[jax.ShapeDtypeStruct((n_b, seq, V_W), F32),
                   jax.ShapeDtypeStruct(state_ret.shape, F32)],
        input_output_aliases=aliases,
        compiler_params=_params("arbitrary"),
        name="ret_sample",
    )(*args)
    return y.reshape(tok.rows, V_W), s_all


def _pool_map(pooled, wmap_ref, scale_ref, gi):
    cols = slice(gi * POOL_GW, (gi + 1) * POOL_GW)
    mixed = jnp.dot(pooled.astype(BF16), wmap_ref[0, gi].astype(BF16), preferred_element_type=F32)
    return mixed * scale_ref[0, :, cols]


def _pool_prompt_kernel(u_ref, halo_ref, wmap_ref, scale_ref, o_ref, uf_scr, *, tl):
    first = pl.program_id(1) == 0
    uf_scr[0:POOL_HALO, :] = jnp.where(first, 0.0, halo_ref[...])
    uf_scr[POOL_HALO:, :] = u_ref[...]
    row = lax.broadcasted_iota(jnp.int32, (tl, POOL_GW), 0) + pl.program_id(1) * tl
    for gi, w in enumerate(POOL_WINDOWS):
        cols = slice(gi * POOL_GW, (gi + 1) * POOL_GW)
        u = uf_scr[POOL_HALO:, cols]
        acc = u
        for j in range(1, w):
            acc = acc + uf_scr[POOL_HALO - j:POOL_HALO - j + tl, cols]
        cnt = jnp.minimum(w, row + 1).astype(F32)
        o_ref[:, cols] = _pool_map(acc / cnt - u, wmap_ref, scale_ref, gi).astype(o_ref.dtype)


def _pool_prompt(tok, layer, u, w_pool_map, pool_scale):
    n_b, seq = tok.n_batch, tok.seq
    tl = min(TL_POOL, seq)
    assert seq % tl == 0 and tl % POOL_HALO == 0
    nl = seq // tl
    per = tl // POOL_HALO
    return pl.pallas_call(
        functools.partial(_pool_prompt_kernel, tl=tl),
        grid=(n_b, nl),
        in_specs=[pl.BlockSpec((tl, D_POOL), lambda b, l: (b * nl + l, 0)),
                  pl.BlockSpec((POOL_HALO, D_POOL), lambda b, l: (jnp.maximum((b * nl + l) * per - 1, 0), 0)),
                  pl.BlockSpec((1,) + w_pool_map.shape[1:], lambda b, l: (layer, 0, 0, 0)),
                  pl.BlockSpec((1, 1, D_POOL), lambda b, l: (layer, 0, 0))],
        out_specs=pl.BlockSpec((tl, D_POOL), lambda b, l: (b * nl + l, 0)),
        out_shape=jax.ShapeDtypeStruct((tok.rows, D_POOL), BF16),
        scratch_shapes=[pltpu.VMEM((POOL_HALO + tl, D_POOL), F32)],
        compiler_params=_params("arbitrary", "arbitrary"),
        name="pool_prompt",
    )(u, u, w_pool_map, pool_scale)


def _pool_sample_kernel(buf_ref, u_ref, wmap_ref, scale_ref, o_ref, *, seq, n_b):
    def row(j, cols):
        return buf_ref[j, :, cols] if j < POOL_BUF else u_ref[j - POOL_BUF, :, cols]

    for gi, w in enumerate(POOL_WINDOWS):
        cols = slice(gi * POOL_GW, (gi + 1) * POOL_GW)
        for t in range(seq):
            acc = row(POOL_BUF + t, cols)
            for j in range(1, w):
                acc = acc + row(POOL_BUF + t - j, cols)
            pooled = acc / float(w) - row(POOL_BUF + t, cols)
            o_ref[t, :, cols] = _pool_map(pooled, wmap_ref, scale_ref, gi)


def _pool_sample(tok, layer, u, buf_tm, w_pool_map, pool_scale):
    n_b, seq = tok.n_batch, tok.seq
    assert tok.n_past >= max(POOL_WINDOWS)
    u_tm = u.reshape(n_b, seq, D_POOL).transpose(1, 0, 2)
    mixed_tm = pl.pallas_call(
        functools.partial(_pool_sample_kernel, seq=seq, n_b=n_b),
        grid=(1,),
        in_specs=[pl.BlockSpec(buf_tm.shape, lambda i: (0, 0, 0)),
                  pl.BlockSpec(u_tm.shape, lambda i: (0, 0, 0)),
                  pl.BlockSpec((1,) + w_pool_map.shape[1:], lambda i: (layer, 0, 0, 0)),
                  pl.BlockSpec((1, 1, D_POOL), lambda i: (layer, 0, 0))],
        out_specs=pl.BlockSpec(u_tm.shape, lambda i: (0, 0, 0)),
        out_shape=jax.ShapeDtypeStruct(u_tm.shape, F32),
        compiler_params=_params("arbitrary"),
        name="pool_sample",
    )(buf_tm, u_tm, w_pool_map, pool_scale)
    return mixed_tm.transpose(1, 0, 2).reshape(tok.rows, D_POOL)


def _merge_kernel(yret_ref, mixed_ref, wr_ref, wp_ref, g1_ref, g2_ref, o_ref):
    ret = jnp.dot(yret_ref[...].astype(BF16), wr_ref[0].astype(BF16), preferred_element_type=F32)
    pool = jnp.dot(mixed_ref[...].astype(BF16), wp_ref[0].astype(BF16), preferred_element_type=F32)
    o_ref[...] = (g1_ref[...].astype(F32) * ret + g2_ref[...].astype(F32) * pool).astype(o_ref.dtype)


def _merge(tok, layer, yret, mixed, proj, w_ret_out, w_pool_out):
    tm = tok.tm
    return pl.pallas_call(
        _merge_kernel,
        grid=(tok.n_tiles, D_MODEL // TN),
        in_specs=[pl.BlockSpec((tm, V_W), lambda m, n: (m, 0)),
                  pl.BlockSpec((tm, D_POOL), lambda m, n: (m, 0)),
                  pl.BlockSpec((1, V_W, TN), lambda m, n: (layer, 0, n)),
                  pl.BlockSpec((1, D_POOL, TN), lambda m, n: (layer, 0, n)),
                  pl.BlockSpec((tm, TN), lambda m, n: (m, O_M1 // TN + n)),
                  pl.BlockSpec((tm, TN), lambda m, n: (m, O_M2 // TN + n))],
        out_specs=pl.BlockSpec((tm, TN), lambda m, n: (m, n)),
        out_shape=jax.ShapeDtypeStruct((tok.rows, D_MODEL), BF16),
        compiler_params=_params("arbitrary", "arbitrary"),
        name="merge",
    )(yret, mixed, w_ret_out, w_pool_out, proj, proj)


def _wo_kernel(merged_ref, w_ref, x_ref, gt_ref, o_ref):
    mix = jnp.dot(merged_ref[...], w_ref[0].astype(BF16), preferred_element_type=F32)
    o_ref[...] = x_ref[...] + (1.0 + gt_ref[0]) * mix


def _wo(tok, layer, merged, x, mod3, w_o):
    tm = tok.tm
    return pl.pallas_call(
        _wo_kernel,
        grid=(tok.n_tiles, D_MODEL // TN),
        in_specs=[pl.BlockSpec((tm, D_MODEL), lambda m, n: (m, 0)),
                  pl.BlockSpec((1, D_MODEL, TN), lambda m, n: (layer, 0, n)),
                  pl.BlockSpec((tm, TN), lambda m, n: (m, n)),
                  tok.mod_spec(2, tn=TN, col_axis=True)],
        out_specs=pl.BlockSpec((tm, TN), lambda m, n: (m, n)),
        out_shape=jax.ShapeDtypeStruct((tok.rows, D_MODEL), F32),
        compiler_params=_params("arbitrary", "arbitrary"),
        name="wo",
    )(merged, w_o, x, mod3)


def _ffn_kernel(*refs, n_f, with_combine, with_final):
    x_ref, sc_ref, sh_ref, gt_ref, g_ref, w1_ref, w3_ref, w2_ref = refs[:8]
    rest = list(refs[8:])
    comb_ref = rest.pop(0) if with_combine else None
    gfin_ref = rest.pop(0) if with_final else None
    o_ref, h_scr = rest
    e, f = pl.program_id(1), pl.program_id(2)

    @pl.when(jnp.logical_and(e == 0, f == 0))
    def _():
        h_scr[...] = _rmsnorm_mod(x_ref[...], g_ref[0], sc_ref[0], sh_ref[0]).astype(BF16)
        o_ref[...] = jnp.zeros_like(o_ref)

    h = h_scr[...]
    a = jnp.dot(h, w1_ref[0, 0].astype(BF16), preferred_element_type=F32)
    b = jnp.dot(h, w3_ref[0, 0].astype(BF16), preferred_element_type=F32)
    part = jnp.dot((_silu(a) * b).astype(BF16), w2_ref[0, 0].astype(BF16), preferred_element_type=F32)
    if with_combine:
        part = comb_ref[0] * part
    o_ref[...] += part

    @pl.when(jnp.logical_and(e == pl.num_programs(1) - 1, f == n_f - 1))
    def _():
        y = x_ref[...] + (1.0 + gt_ref[0]) * o_ref[...]
        if with_final:
            y = y * lax.rsqrt(jnp.mean(y * y, axis=-1, keepdims=True) + EPS) * gfin_ref[...]
        o_ref[...] = y


def _ffn(tok, layer, x, mod, g_ffn, w1, w3, w2, combine=None, g_final=None):
    j = layer // 2
    n_e, d_ff = w1.shape[1], w1.shape[3]
    tok_f = _Tokens(tok.n_batch, tok.seq, tok.n_past, TM_FFN, tok.per_row_mod)
    tm, n_tiles = tok_f.tm, tok_f.n_tiles
    mod3 = tok_f.prep_mod(mod)
    assert d_ff % TF == 0
    n_f = d_ff // TF
    in_specs = [pl.BlockSpec((tm, D_MODEL), lambda m, e, f: (m, 0)),
                tok_f.mod_spec(4), tok_f.mod_spec(3), tok_f.mod_spec(5),
                pl.BlockSpec((1, 1, D_MODEL), lambda m, e, f: (layer, 0, 0)),
                pl.BlockSpec((1, 1, D_MODEL, TF), lambda m, e, f: (j, e, 0, f)),
                pl.BlockSpec((1, 1, D_MODEL, TF), lambda m, e, f: (j, e, 0, f)),
                pl.BlockSpec((1, 1, TF, D_MODEL), lambda m, e, f: (j, e, f, 0))]
    args = [x, mod3, mod3, mod3, g_ffn, w1, w3, w2]
    if combine is not None:
        in_specs.append(pl.BlockSpec((1, tm, 1), lambda m, e, f: (e, m, 0)))
        args.append(combine)
    if g_final is not None:
        in_specs.append(pl.BlockSpec((1, D_MODEL), lambda m, e, f: (0, 0)))
        args.append(g_final)
    return pl.pallas_call(
        functools.partial(_ffn_kernel, n_f=n_f, with_combine=combine is not None, with_final=g_final is not None),
        grid=(n_tiles, n_e, n_f),
        in_specs=in_specs,
        out_specs=pl.BlockSpec((tm, D_MODEL), lambda m, e, f: (m, 0)),
        out_shape=jax.ShapeDtypeStruct((tok.rows, D_MODEL), F32),
        scratch_shapes=[pltpu.VMEM((tm, D_MODEL), BF16)],
        compiler_params=_params("arbitrary", "arbitrary", "arbitrary"),
        name="ffn",
    )(*args)


def _router_kernel(x_ref, sc_ref, sh_ref, g_ref, w_ref, b_ref, o_ref, esel_ref, wsel_ref, h_ref):
    h = _rmsnorm_mod(x_ref[...], g_ref[0], sc_ref[0], sh_ref[0])
    logits = jnp.dot(h.astype(BF16), w_ref[0].astype(BF16), preferred_element_type=F32) + b_ref[0]
    idx = lax.broadcasted_iota(jnp.int32, logits.shape, 1)
    top1 = jnp.max(logits, axis=-1, keepdims=True)
    i1 = jnp.min(jnp.where(logits == top1, idx, N_EXPERTS), axis=-1, keepdims=True)
    rest = jnp.where(idx == i1, -jnp.inf, logits)
    top2 = jnp.max(rest, axis=-1, keepdims=True)
    i2 = jnp.min(jnp.where(rest == top2, idx, N_EXPERTS), axis=-1, keepdims=True)
    e2 = jnp.exp(top2 - top1)
    denom = 1.0 + e2
    w1, w2 = 1.0 / denom, e2 / denom
    o_ref[...] = jnp.where(idx == i1, w1, 0.0) + jnp.where(idx == i2, w2, 0.0)
    esel_ref[...] = jnp.where(idx == 0, i1, i2)
    wsel_ref[...] = jnp.where(idx == 0, w1, w2)
    h_ref[...] = h


def _router(tok, layer, x, mod3, g_ffn, w_router, b_router):
    j = layer // 2
    tm = tok.tm
    small = pl.BlockSpec((tm, N_EXPERTS), lambda m: (m, 0))
    return pl.pallas_call(
        _router_kernel,
        grid=(tok.n_tiles,),
        in_specs=[pl.BlockSpec((tm, D_MODEL), lambda m: (m, 0)),
                  tok.mod_spec(4), tok.mod_spec(3),
                  pl.BlockSpec((1, 1, D_MODEL), lambda m: (layer, 0, 0)),
                  pl.BlockSpec((1, D_MODEL, N_EXPERTS), lambda m: (j, 0, 0)),
                  pl.BlockSpec((1, 1, N_EXPERTS), lambda m: (j, 0, 0))],
        out_specs=[small, small, small, pl.BlockSpec((tm, D_MODEL), lambda m: (m, 0))],
        out_shape=[jax.ShapeDtypeStruct((tok.rows, N_EXPERTS), F32),
                   jax.ShapeDtypeStruct((tok.rows, N_EXPERTS), jnp.int32),
                   jax.ShapeDtypeStruct((tok.rows, N_EXPERTS), F32),
                   jax.ShapeDtypeStruct((tok.rows, D_MODEL), F32)],
        compiler_params=_params("arbitrary"),
        name="router",
    )(x, mod3, mod3, g_ffn, w_router, b_router)


def _route_plan(choice, tm):
    rows = choice.shape[0]
    n_pairs = 2 * rows
    n_tiles = n_pairs // tm + N_EXPERTS
    e_flat = choice.T.reshape(n_pairs)
    onehot = (e_flat[:, None] == jnp.arange(N_EXPERTS, dtype=jnp.int32)[None, :]).astype(jnp.int32)
    csum = jnp.cumsum(onehot, axis=0)
    rank = jnp.sum(csum * onehot, axis=1) - 1
    tiles_per = (csum[-1] + tm - 1) // tm
    tile_end = jnp.cumsum(tiles_per)
    pos = (tile_end - tiles_per)[e_flat] * tm + rank
    token = jnp.arange(n_pairs, dtype=jnp.int32) % rows
    src_row = jnp.zeros((n_tiles * tm,), jnp.int32).at[pos].set(token)
    tile_id = jnp.arange(n_tiles, dtype=jnp.int32)
    n_used = tile_end[-1]
    tile_expert = jnp.minimum(jnp.sum(tile_id[:, None] >= tile_end[None, :], axis=1), N_EXPERTS - 1).astype(jnp.int32)
    tile_valid = (tile_id < n_used).astype(jnp.int32)
    last_expert = tile_expert[jnp.maximum(n_used - 1, 0)]
    tile_expert = jnp.where(tile_valid == 1, tile_expert, last_expert)
    return tile_expert, tile_valid, src_row, pos.astype(jnp.int32)


def _row_copy(src_hbm, row, dst, r, sem):
    return pltpu.make_async_copy(src_hbm.at[pl.ds(row, 1)], dst.at[pl.ds(r, 1)], sem)


def _moe_kernel(te_ref, tv_ref, src_ref, h_hbm, w1_ref, w3_ref, w2_ref, o_ref, rows_scr, h_scr, sem, *, tm):
    del te_ref
    m, f = pl.program_id(0), pl.program_id(1)
    valid = tv_ref[m] == 1

    @pl.when(f == 0)
    def _():
        o_ref[...] = jnp.zeros_like(o_ref)

    @pl.when(jnp.logical_and(valid, f == 0))
    def _():
        def start(r, c):
            _row_copy(h_hbm, src_ref[m * tm + r], rows_scr, r, sem).start()
            return c

        def wait(r, c):
            _row_copy(h_hbm, src_ref[m * tm + r], rows_scr, r, sem).wait()
            return c

        lax.fori_loop(0, tm, start, 0, unroll=8)
        lax.fori_loop(0, tm, wait, 0, unroll=8)
        h_scr[...] = rows_scr[...].astype(BF16)

    @pl.when(valid)
    def _():
        h = h_scr[...]
        a = jnp.dot(h, w1_ref[0, 0].astype(BF16), preferred_element_type=F32)
        b = jnp.dot(h, w3_ref[0, 0].astype(BF16), preferred_element_type=F32)
        o_ref[...] += jnp.dot((_silu(a) * b).astype(BF16), w2_ref[0, 0].astype(BF16), preferred_element_type=F32)


def _moe_sparse(layer, h, plan, w1, w3, w2, tm):
    j = layer // 2
    tile_expert, tile_valid, src_row, _ = plan
    n_tiles = tile_expert.shape[0]
    d_ff = w1.shape[3]
    assert d_ff % TF == 0
    n_f = d_ff // TF

    def f_sel(m, f, tv):
        return jnp.where(tv[m] == 1, f, n_f - 1)

    return pl.pallas_call(
        functools.partial(_moe_kernel, tm=tm),
        grid_spec=pltpu.PrefetchScalarGridSpec(
            num_scalar_prefetch=3,
            grid=(n_tiles, n_f),
            in_specs=[pl.BlockSpec(memory_space=pl.ANY),
                      pl.BlockSpec((1, 1, D_MODEL, TF), lambda m, f, te, tv, sr: (j, te[m], 0, f_sel(m, f, tv))),
                      pl.BlockSpec((1, 1, D_MODEL, TF), lambda m, f, te, tv, sr: (j, te[m], 0, f_sel(m, f, tv))),
                      pl.BlockSpec((1, 1, TF, D_MODEL), lambda m, f, te, tv, sr: (j, te[m], f_sel(m, f, tv), 0))],
            out_specs=pl.BlockSpec((tm, D_MODEL), lambda m, f, te, tv, sr: (m, 0)),
            scratch_shapes=[pltpu.VMEM((tm, D_MODEL), F32), pltpu.VMEM((tm, D_MODEL), BF16),
                            pltpu.SemaphoreType.DMA(())]),
        out_shape=jax.ShapeDtypeStruct((n_tiles * tm, D_MODEL), F32),
        compiler_params=_params("arbitrary", "arbitrary"),
        name="moe",
    )(tile_expert, tile_valid, src_row, h, w1, w3, w2)


def _combine_kernel(pos_ref, y_hbm, x_ref, gt_ref, w_ref, *rest, tm, rows, with_final):
    gfin_ref = rest[0] if with_final else None
    o_ref, y_scr, sems = rest[-3:]
    i, n = pl.program_id(0), pl.num_programs(0)

    def for_each_row(tile, slot, fn):
        def body(r, c):
            for k in range(2):
                fn(_row_copy(y_hbm, pos_ref[k * rows + tile * tm + r], y_scr.at[slot, k], r, sems.at[slot]))
            return c
        lax.fori_loop(0, tm, body, 0, unroll=8)

    @pl.when(i == 0)
    def _():
        for_each_row(0, 0, lambda cp: cp.start())

    @pl.when(i + 1 < n)
    def _():
        for_each_row(i + 1, (i + 1) % 2, lambda cp: cp.start())

    slot = i % 2
    for_each_row(i, slot, lambda cp: cp.wait())
    w = w_ref[...]
    f = w[:, 0:1] * y_scr[slot, 0] + w[:, 1:2] * y_scr[slot, 1]
    y = x_ref[...] + (1.0 + gt_ref[0]) * f
    if with_final:
        y = y * lax.rsqrt(jnp.mean(y * y, axis=-1, keepdims=True) + EPS) * gfin_ref[...]
    o_ref[...] = y


def _combine(tok, y_sorted, pos, x, mod, wsel, g_final):
    tok_c = _Tokens(tok.n_batch, tok.seq, tok.n_past, TM_COMBINE, tok.per_row_mod)
    tm = tok_c.tm
    mod3 = tok_c.prep_mod(mod)
    in_specs = [pl.BlockSpec(memory_space=pl.ANY),
                pl.BlockSpec((tm, D_MODEL), lambda m, pos: (m, 0)),
                tok_c.mod_spec(5),
                pl.BlockSpec((tm, N_EXPERTS), lambda m, pos: (m, 0))]
    args = [pos, y_sorted, x, mod3, wsel]
    if g_final is not None:
        in_specs.append(pl.BlockSpec((1, D_MODEL), lambda m, pos: (0, 0)))
        args.append(g_final)
    return pl.pallas_call(
        functools.partial(_combine_kernel, tm=tm, rows=tok.rows, with_final=g_final is not None),
        grid_spec=pltpu.PrefetchScalarGridSpec(
            num_scalar_prefetch=1,
            grid=(tok_c.n_tiles,),
            in_specs=in_specs,
            out_specs=pl.BlockSpec((tm, D_MODEL), lambda m, pos: (m, 0)),
            scratch_shapes=[pltpu.VMEM((2, 2, tm, D_MODEL), F32), pltpu.SemaphoreType.DMA((2,))]),
        out_shape=jax.ShapeDtypeStruct((tok.rows, D_MODEL), F32),
        compiler_params=_params("arbitrary"),
        name="combine",
    )(*args)


def _decoder(tok, x, mod_layers, ret_state, pool_state, p):
    rope_tabs = tok.rope_tables()
    is_prompt = ret_state is None
    act_dtype = BF16 if is_prompt else F32
    u_layers = []
    s_all = None
    for i in range(DEPTH):
        mod3 = tok.prep_mod(mod_layers[i])
        proj, u = _inproj(tok, i, x, mod3, p['g_mix'], p['w_in'], rope_tabs, act_dtype)
        u_layers.append(u)
        if is_prompt:
            yret, s_all = _ret_prompt(tok, i, proj, p['g_ret'], s_all)
            mixed = _pool_prompt(tok, i, u, p['w_pool_map'], p['pool_scale'])
        else:
            yret, s_all = _ret_sample(tok, i, proj, p['g_ret'], ret_state, s_all)
            mixed = _pool_sample(tok, i, u, pool_state[i].transpose(1, 0, 2), p['w_pool_map'], p['pool_scale'])
        merged = _merge(tok, i, yret, mixed, proj, p['w_ret_out'], p['w_pool_out'])
        x = _wo(tok, i, merged, x, mod3, p['w_o'])
        g_final = p['g_final'] if i == DEPTH - 1 else None
        if i % 2 == 0:
            x = _ffn(tok, i, x, mod_layers[i], p['g_ffn'], p['w1_dense'][:, None], p['w3_dense'][:, None],
                     p['w2_dense'][:, None], g_final=g_final)
        else:
            comb, esel, wsel, h = _router(tok, i, x, mod3, p['g_ffn'], p['w_router'], p['b_router'])
            if is_prompt:
                plan = _route_plan(esel[:, :2], TM_MOE)
                y_sorted = _moe_sparse(i, h, plan, p['w1_moe'], p['w3_moe'], p['w2_moe'], TM_MOE)
                x = _combine(tok, y_sorted, plan[3], x, mod_layers[i], wsel, g_final)
            else:
                comb = comb.T.reshape(N_EXPERTS, tok.rows, 1)
                x = _ffn(tok, i, x, mod_layers[i], p['g_ffn'], p['w1_moe'], p['w3_moe'], p['w2_moe'],
                         combine=comb, g_final=g_final)
    return x, s_all, u_layers


def kernel(x_prompt, x_sample, c_prompt, c_sample, state_ret, state_pool, w_ada, b_ada, g_mix, w_in, g_ret,
           w_ret_out, w_pool_map, pool_scale, w_pool_out, w_o, g_ffn, w1_dense, w3_dense, w2_dense, w_router,
           b_router, w1_moe, w3_moe, w2_moe, g_final):
    row3 = lambda a: a.reshape(a.shape[0], 1, a.shape[1])
    p = dict(g_mix=row3(g_mix), w_in=w_in, g_ret=row3(g_ret), w_ret_out=w_ret_out, w_pool_map=w_pool_map,
             pool_scale=row3(pool_scale), w_pool_out=w_pool_out, w_o=w_o, g_ffn=row3(g_ffn),
             w1_dense=w1_dense, w3_dense=w3_dense, w2_dense=w2_dense, w_router=w_router, b_router=row3(b_router),
             w1_moe=w1_moe, w3_moe=w3_moe, w2_moe=w2_moe, g_final=g_final.reshape(1, D_MODEL))
    n_bp, seq_p, _ = x_prompt.shape
    n_bs, seq_s, _ = x_sample.shape

    mod = _ada(jnp.concatenate([c_prompt, c_sample], axis=0), w_ada, b_ada)

    tok_p = _Tokens(n_bp, seq_p, 0, TM_PROMPT, per_row_mod=False)
    tok_s = _Tokens(n_bs, seq_s, PAST_LEN, TM_SAMPLE, per_row_mod=True)

    y_p, s_prompt, u_p = _decoder(tok_p, x_prompt.reshape(tok_p.rows, D_MODEL), [mod[i, :n_bp] for i in range(DEPTH)],
                                  None, None, p)
    y_s, s_sample, u_s = _decoder(tok_s, x_sample.reshape(tok_s.rows, D_MODEL), [mod[i, n_bp:] for i in range(DEPTH)],
                                  state_ret, state_pool, p)

    buf_prompt = jnp.stack([u.reshape(n_bp, seq_p, D_POOL)[:, seq_p - POOL_BUF:] for u in u_p])
    buf_sample = jnp.stack([
        jnp.concatenate([state_pool[i], u_s[i].reshape(n_bs, seq_s, D_POOL)], axis=1)[:, -POOL_BUF:]
        for i in range(DEPTH)])
    return (y_p.reshape(x_prompt.shape), y_s.reshape(x_sample.shape), s_prompt, buf_prompt, s_sample, buf_sample)
```

```python
import functools

import numpy as np
import jax
import jax.numpy as jnp
from jax import lax
from jax.experimental import pallas as pl
from jax.experimental.pallas import tpu as pltpu

F32 = jnp.float32
BF16 = jnp.bfloat16

D_MODEL = 2048
DEPTH = 2
PAST_LEN = 16384
RET_HEADS = 8
RET_DK = 128
RET_DV = 256
RET_CHUNK = 128
ROPE_BASE = 10000.0
QK_W = RET_HEADS * RET_DK
V_W = RET_HEADS * RET_DV
POOL_WINDOWS = (2, 4, 8, 16)
D_POOL = 1024
POOL_GW = 256
POOL_BUF = 15
POOL_HALO = 16
N_IN = 2 * QK_W + 2 * V_W + D_POOL + 2 * D_MODEL
N_EXPERTS = 8
EPS = 1e-6
N_MOD = 6

O_Q, O_K, O_V, O_G, O_U = 0, QK_W, 2 * QK_W, 2 * QK_W + V_W, 2 * QK_W + 2 * V_W
O_M1 = O_U + D_POOL
O_M2 = O_M1 + D_MODEL

VMEM_LIMIT_BYTES = 56 * 1024 * 1024
TN = 512
TF = 256
TM_PROMPT = 1024
TM_SAMPLE = 512
TM_FFN = 1024
TL_POOL = 512
TM_MOE = 1024
SUB_MOE = 256
TM_COMBINE = 256


def _params(*sem):
    return pltpu.CompilerParams(dimension_semantics=sem, vmem_limit_bytes=VMEM_LIMIT_BYTES)


class _Tokens:
    def __init__(self, n_batch, seq, n_past, tm, per_row_mod):
        self.n_batch, self.seq, self.n_past = n_batch, seq, n_past
        self.rows = n_batch * seq
        self.tm = min(tm, self.rows)
        self.per_row_mod = per_row_mod
        if not per_row_mod:
            self.tm = min(self.tm, seq)
            assert seq % self.tm == 0
        assert self.rows % self.tm == 0
        self.n_tiles = self.rows // self.tm

    def prep_mod(self, mod):
        if self.per_row_mod:
            return jnp.repeat(mod, self.seq, axis=0).reshape(self.n_tiles, self.tm, N_MOD * D_MODEL)
        return mod.reshape(self.n_batch, 1, N_MOD * D_MODEL)

    def mod_spec(self, chunk, tn=D_MODEL, col_axis=False):
        per_chunk = D_MODEL // tn
        if self.per_row_mod:
            shape = (1, self.tm, tn)
            if col_axis:
                return pl.BlockSpec(shape, lambda m, n, *_: (m, 0, chunk * per_chunk + n))
            return pl.BlockSpec(shape, lambda m, *_: (m, 0, chunk * per_chunk))
        tiles_per_batch = self.seq // self.tm
        shape = (1, 1, tn)
        if col_axis:
            return pl.BlockSpec(shape, lambda m, n, *_: (m // tiles_per_batch, 0, chunk * per_chunk + n))
        return pl.BlockSpec(shape, lambda m, *_: (m // tiles_per_batch, 0, chunk * per_chunk))

    def rope_tables(self):
        half = RET_DK // 2
        pos = jnp.arange(self.seq, dtype=F32) + float(self.n_past)
        inv = ROPE_BASE ** (-jnp.arange(half, dtype=F32) / half)
        ang = pos[:, None] * inv[None, :]
        cos, sin = jnp.cos(ang), jnp.sin(ang)
        cos_full = jnp.concatenate([cos, cos], axis=1)
        sin_signed = jnp.concatenate([-sin, sin], axis=1)
        if self.per_row_mod:
            cos_full = jnp.tile(cos_full, (self.n_batch, 1))
            sin_signed = jnp.tile(sin_signed, (self.n_batch, 1))
        return cos_full, sin_signed

    def rope_spec(self):
        n_blocks = (self.rows if self.per_row_mod else self.seq) // self.tm
        return pl.BlockSpec((self.tm, RET_DK), lambda m, *_: (m % n_blocks, 0))


def _retention_tables(chunk):
    log_gamma = jnp.log1p(-jnp.exp2(-5.0 - jnp.arange(RET_HEADS, dtype=F32)))
    idx = jnp.arange(chunk, dtype=F32)
    rel = idx[:, None] - idx[None, :]
    dmask = jnp.where(rel[None] >= 0, jnp.exp(log_gamma[:, None, None] * jnp.maximum(rel, 0.0)[None]), 0.0)
    q_dec = jnp.exp(log_gamma[:, None] * (idx + 1.0)[None, :])
    k_dec = jnp.exp(log_gamma[:, None] * (chunk - 1.0 - idx)[None, :])
    chunk_dec = jnp.exp(log_gamma * chunk)
    q_dec = jnp.broadcast_to(q_dec[:, :, None], (RET_HEADS, chunk, RET_DV))
    k_dec = jnp.broadcast_to(k_dec[:, :, None], (RET_HEADS, chunk, RET_DK))
    chunk_dec = jnp.broadcast_to(chunk_dec[:, None, None], (RET_HEADS, 1, RET_DV))
    return dmask, q_dec, k_dec, chunk_dec


def _rmsnorm_mod(x, g, scale, shift):
    y = x * lax.rsqrt(jnp.mean(x * x, axis=-1, keepdims=True) + EPS) * g
    return y * (1.0 + scale) + shift


def _silu(x):
    return x * jax.nn.sigmoid(x)


def _ada_kernel(c_ref, w_ref, b_ref, o_ref):
    cs = _silu(c_ref[...]).astype(BF16)
    o_ref[0] = jnp.dot(cs, w_ref[0].astype(BF16), preferred_element_type=F32) + b_ref[0]


def _ada(c_all, w_ada, b_ada):
    nb = c_all.shape[0]
    tn = 1024
    width = N_MOD * D_MODEL
    return pl.pallas_call(
        _ada_kernel,
        grid=(DEPTH, width // tn),
        in_specs=[pl.BlockSpec((nb, D_MODEL), lambda l, n: (0, 0)),
                  pl.BlockSpec((1, D_MODEL, tn), lambda l, n: (l, 0, n)),
                  pl.BlockSpec((1, 1, tn), lambda l, n: (l, 0, n))],
        out_specs=pl.BlockSpec((1, nb, tn), lambda l, n: (l, 0, n)),
        out_shape=jax.ShapeDtypeStruct((DEPTH, nb, width), F32),
        compiler_params=_params("arbitrary", "arbitrary"),
        name="ada",
    )(c_all, w_ada, b_ada.reshape(DEPTH, 1, width))


def _inproj_kernel(x_ref, sc_ref, sh_ref, g_ref, w_ref, cos_ref, sin_ref, o_ref, u_ref, h_scr):
    n = pl.program_id(1)

    @pl.when(n == 0)
    def _():
        h_scr[...] = _rmsnorm_mod(x_ref[...], g_ref[0], sc_ref[0], sh_ref[0]).astype(BF16)

    acc = jnp.dot(h_scr[...], w_ref[0].astype(BF16), preferred_element_type=F32)

    def rope(scale):
        for j in range(TN // RET_DK):
            xh = acc[:, j * RET_DK:(j + 1) * RET_DK]
            r = xh * cos_ref[...] + pltpu.roll(xh, RET_DK // 2, 1) * sin_ref[...]
            if scale != 1.0:
                r = r * scale
            o_ref[:, j * RET_DK:(j + 1) * RET_DK] = r.astype(o_ref.dtype)

    @pl.when(n < O_K // TN)
    def _():
        rope(1.0)

    @pl.when(jnp.logical_and(n >= O_K // TN, n < O_V // TN))
    def _():
        rope(RET_DK ** -0.5)

    @pl.when(jnp.logical_and(n >= O_V // TN, n < O_G // TN))
    def _():
        o_ref[...] = acc.astype(o_ref.dtype)

    @pl.when(jnp.logical_and(n >= O_G // TN, n < O_U // TN))
    def _():
        o_ref[...] = _silu(acc).astype(o_ref.dtype)

    @pl.when(jnp.logical_and(n >= O_U // TN, n < O_M1 // TN))
    def _():
        o_ref[...] = acc.astype(o_ref.dtype)
        u_ref[...] = acc

    @pl.when(n >= O_M1 // TN)
    def _():
        o_ref[...] = jax.nn.sigmoid(acc).astype(o_ref.dtype)


def _inproj(tok, layer, x, mod3, g_mix, w_in, rope_tabs, out_dtype):
    tm = tok.tm
    n_u = D_POOL // TN
    proj, u = pl.pallas_call(
        _inproj_kernel,
        grid=(tok.n_tiles, N_IN // TN),
        in_specs=[pl.BlockSpec((tm, D_MODEL), lambda m, n: (m, 0)),
                  tok.mod_spec(1), tok.mod_spec(0),
                  pl.BlockSpec((1, 1, D_MODEL), lambda m, n: (layer, 0, 0)),
                  pl.BlockSpec((1, D_MODEL, TN), lambda m, n: (layer, 0, n)),
                  tok.rope_spec(), tok.rope_spec()],
        out_specs=[pl.BlockSpec((tm, TN), lambda m, n: (m, n)),
                   pl.BlockSpec((tm, TN), lambda m, n: (m, jnp.clip(n - O_U // TN, 0, n_u - 1)))],
        out_shape=[jax.ShapeDtypeStruct((tok.rows, N_IN), out_dtype),
                   jax.ShapeDtypeStruct((tok.rows, D_POOL), F32)],
        scratch_shapes=[pltpu.VMEM((tm, D_MODEL), BF16)],
        compiler_params=_params("arbitrary", "arbitrary"),
        name="inproj",
    )(x, mod3, mod3, g_mix, w_in, *rope_tabs)
    return proj, u


def _head_norm_gate(o, gret, gate):
    mu = jnp.mean(o, axis=-1, keepdims=True)
    d = o - mu
    var = jnp.mean(d * d, axis=-1, keepdims=True)
    return gate.astype(F32) * (d * lax.rsqrt(var + EPS) * gret)


def _ret_prompt_kernel(q_ref, k_ref, v_ref, g_ref, gret_ref, dmask_ref, qdec_ref, kdec_ref, cdec_ref,
                       *rest):
    y_ref, s_ref = rest[-2:]

    @pl.when(pl.program_id(1) == 0)
    def _():
        s_ref[...] = jnp.zeros_like(s_ref)

    for h in range(RET_HEADS):
        ks, vs = slice(h * RET_DK, (h + 1) * RET_DK), slice(h * RET_DV, (h + 1) * RET_DV)
        q, k, v = q_ref[:, ks], k_ref[:, ks], v_ref[:, vs]
        state = s_ref[0, 0, h]
        scores = lax.dot_general(q, k, (((1,), (1,)), ((), ())), preferred_element_type=F32) * dmask_ref[h]
        inner = jnp.dot(scores.astype(BF16), v, preferred_element_type=F32)
        cross = jnp.dot(q, state.astype(BF16), preferred_element_type=F32) * qdec_ref[h]
        k_decayed = (k.astype(F32) * kdec_ref[h]).astype(BF16)
        s_ref[0, 0, h] = cdec_ref[h] * state + lax.dot_general(
            k_decayed, v, (((0,), (0,)), ((), ())), preferred_element_type=F32)
        y_ref[:, vs] = _head_norm_gate(inner + cross, gret_ref[0, :, vs], g_ref[:, vs]).astype(y_ref.dtype)


def _table_specs(tables):
    return [pl.BlockSpec(t.shape, lambda *_: (0, 0, 0)) for t in tables]


def _ret_prompt(tok, layer, proj, g_ret, s_all):
    n_b, seq = tok.n_batch, tok.seq
    c = RET_CHUNK
    assert seq % c == 0
    nc = seq // c
    tables = _retention_tables(c)
    in_specs = [pl.BlockSpec((c, QK_W), lambda b, i: (b * nc + i, O_Q // QK_W)),
                pl.BlockSpec((c, QK_W), lambda b, i: (b * nc + i, O_K // QK_W)),
                pl.BlockSpec((c, V_W), lambda b, i: (b * nc + i, O_V // V_W)),
                pl.BlockSpec((c, V_W), lambda b, i: (b * nc + i, O_G // V_W)),
                pl.BlockSpec((1, 1, V_W), lambda b, i: (layer, 0, 0))] + _table_specs(tables)
    args = [proj, proj, proj, proj, g_ret, *tables]
    aliases = {}
    if s_all is not None:
        aliases = {len(args): 1}
        in_specs.append(pl.BlockSpec(memory_space=pl.ANY))
        args.append(s_all)
    y, s_all = pl.pallas_call(
        _ret_prompt_kernel,
        grid=(n_b, nc),
        in_specs=in_specs,
        out_specs=[pl.BlockSpec((c, V_W), lambda b, i: (b * nc + i, 0)),
                   pl.BlockSpec((1, 1, RET_HEADS, RET_DK, RET_DV), lambda b, i: (layer, b, 0, 0, 0))],
        out_shape=[jax.ShapeDtypeStruct((tok.rows, V_W), BF16),
                   jax.ShapeDtypeStruct((DEPTH, n_b, RET_HEADS, RET_DK, RET_DV), F32)],
        input_output_aliases=aliases,
        compiler_params=_params("arbitrary", "arbitrary"),
        name="ret_prompt",
    )(*args)
    return y, s_all


def _ret_sample_kernel(q_ref, k_ref, v_ref, g_ref, gret_ref, s0_ref, dmask_ref, qdec_ref, kdec_ref, cdec_ref,
                       *rest, bb, seq):
    y_ref, s_ref = rest[-2:]
    for b in range(bb):
        for h in range(RET_HEADS):
            ks, vs = slice(h * RET_DK, (h + 1) * RET_DK), slice(h * RET_DV, (h + 1) * RET_DV)
            q, k, v = q_ref[b, :, ks], k_ref[b, :, ks], v_ref[b, :, vs]
            state = s0_ref[0, b, h]
            inner = jnp.zeros((seq, RET_DV), F32)
            for j in range(seq):
                s_j = jnp.sum(q * k[j:j + 1, :], axis=-1, keepdims=True) * dmask_ref[h, :, j:j + 1]
                inner = inner + s_j * v[j:j + 1, :]
            cross = jnp.dot(q.astype(BF16), state.astype(BF16), preferred_element_type=F32) * qdec_ref[h]
            k_decayed = (k * kdec_ref[h]).astype(BF16)
            s_ref[0, b, h] = cdec_ref[h] * state + lax.dot_general(
                k_decayed, v.astype(BF16), (((0,), (0,)), ((), ())), preferred_element_type=F32)
            y_ref[b, :, vs] = _head_norm_gate(inner + cross, gret_ref[0, :, vs], g_ref[b, :, vs])


def _ret_sample(tok, layer, proj, g_ret, state_ret, s_all):
    n_b, seq = tok.n_batch, tok.seq
    bb = 4
    assert n_b % bb == 0
    tables = _retention_tables(seq)
    proj3 = proj.reshape(n_b, seq, N_IN)
    state_spec = pl.BlockSpec((1, bb, RET_HEADS, RET_DK, RET_DV), lambda i: (layer, i, 0, 0, 0))
    in_specs = [pl.BlockSpec((bb, seq, QK_W), lambda i: (i, 0, O_Q // QK_W)),
                pl.BlockSpec((bb, seq, QK_W), lambda i: (i, 0, O_K // QK_W)),
                pl.BlockSpec((bb, seq, V_W), lambda i: (i, 0, O_V // V_W)),
                pl.BlockSpec((bb, seq, V_W), lambda i: (i, 0, O_G // V_W)),
                pl.BlockSpec((1, 1, V_W), lambda i: (layer, 0, 0)),
                state_spec] + _table_specs(tables)
    args = [proj3, proj3, proj3, proj3, g_ret, state_ret, *tables]
    aliases = {}
    if s_all is not None:
        aliases = {len(args): 1}
        in_specs.append(pl.BlockSpec(memory_space=pl.ANY))
        args.append(s_all)
    y, s_all = pl.pallas_call(
        functools.partial(_ret_sample_kernel, bb=bb, seq=seq),
        grid=(n_b // bb,),
        in_specs=in_specs,
        out_specs=[pl.BlockSpec((bb, seq, V_W), lambda i: (i, 0, 0)), state_spec],
        out_shape=[jax.ShapeDtypeStruct((n_b, seq, V_W), F32),
                   jax.ShapeDtypeStruct(state_ret.shape, F32)],
        input_output_aliases=aliases,
        compiler_params=_params("arbitrary"),
        name="ret_sample",
    )(*args)
    return y.reshape(tok.rows, V_W), s_all


def _pool_map(pooled, wmap_ref, scale_ref, gi):
    cols = slice(gi * POOL_GW, (gi + 1) * POOL_GW)
    mixed = jnp.dot(pooled.astype(BF16), wmap_ref[0, gi].astype(BF16), preferred_element_type=F32)
    return mixed * scale_ref[0, :, cols]


def _pool_prompt_kernel(u_ref, halo_ref, wmap_ref, scale_ref, o_ref, uf_scr, *, tl):
    first = pl.program_id(1) == 0
    uf_scr[0:POOL_HALO, :] = jnp.where(first, 0.0, halo_ref[...])
    uf_scr[POOL_HALO:, :] = u_ref[...]
    row = lax.broadcasted_iota(jnp.int32, (tl, POOL_GW), 0) + pl.program_id(1) * tl
    for gi, w in enumerate(POOL_WINDOWS):
        cols = slice(gi * POOL_GW, (gi + 1) * POOL_GW)
        u = uf_scr[POOL_HALO:, cols]
        acc = u
        for j in range(1, w):
            acc = acc + uf_scr[POOL_HALO - j:POOL_HALO - j + tl, cols]
        cnt = jnp.minimum(w, row + 1).astype(F32)
        o_ref[:, cols] = _pool_map(acc / cnt - u, wmap_ref, scale_ref, gi).astype(o_ref.dtype)


def _pool_prompt(tok, layer, u, w_pool_map, pool_scale):
    n_b, seq = tok.n_batch, tok.seq
    tl = min(TL_POOL, seq)
    assert seq % tl == 0 and tl % POOL_HALO == 0
    nl = seq // tl
    per = tl // POOL_HALO
    return pl.pallas_call(
        functools.partial(_pool_prompt_kernel, tl=tl),
        grid=(n_b, nl),
        in_specs=[pl.BlockSpec((tl, D_POOL), lambda b, l: (b * nl + l, 0)),
                  pl.BlockSpec((POOL_HALO, D_POOL), lambda b, l: (jnp.maximum((b * nl + l) * per - 1, 0), 0)),
                  pl.BlockSpec((1,) + w_pool_map.shape[1:], lambda b, l: (layer, 0, 0, 0)),
                  pl.BlockSpec((1, 1, D_POOL), lambda b, l: (layer, 0, 0))],
        out_specs=pl.BlockSpec((tl, D_POOL), lambda b, l: (b * nl + l, 0)),
        out_shape=jax.ShapeDtypeStruct((tok.rows, D_POOL), BF16),
        scratch_shapes=[pltpu.VMEM((POOL_HALO + tl, D_POOL), F32)],
        compiler_params=_params("arbitrary", "arbitrary"),
        name="pool_prompt",
    )(u, u, w_pool_map, pool_scale)


def _pool_sample_kernel(buf_ref, u_ref, wmap_ref, scale_ref, o_ref, *, seq, n_b):
    def row(j, cols):
        return buf_ref[j, :, cols] if j < POOL_BUF else u_ref[j - POOL_BUF, :, cols]

    for gi, w in enumerate(POOL_WINDOWS):
        cols = slice(gi * POOL_GW, (gi + 1) * POOL_GW)
        for t in range(seq):
            acc = row(POOL_BUF + t, cols)
            for j in range(1, w):
                acc = acc + row(POOL_BUF + t - j, cols)
            pooled = acc / float(w) - row(POOL_BUF + t, cols)
            o_ref[t, :, cols] = _pool_map(pooled, wmap_ref, scale_ref, gi)


def _pool_sample(tok, layer, u, buf_tm, w_pool_map, pool_scale):
    n_b, seq = tok.n_batch, tok.seq
    assert tok.n_past >= max(POOL_WINDOWS)
    u_tm = u.reshape(n_b, seq, D_POOL).transpose(1, 0, 2)
    mixed_tm = pl.pallas_call(
        functools.partial(_pool_sample_kernel, seq=seq, n_b=n_b),
        grid=(1,),
        in_specs=[pl.BlockSpec(buf_tm.shape, lambda i: (0, 0, 0)),
                  pl.BlockSpec(u_tm.shape, lambda i: (0, 0, 0)),
                  pl.BlockSpec((1,) + w_pool_map.shape[1:], lambda i: (layer, 0, 0, 0)),
                  pl.BlockSpec((1, 1, D_POOL), lambda i: (layer, 0, 0))],
        out_specs=pl.BlockSpec(u_tm.shape, lambda i: (0, 0, 0)),
        out_shape=jax.ShapeDtypeStruct(u_tm.shape, F32),
        compiler_params=_params("arbitrary"),
        name="pool_sample",
    )(buf_tm, u_tm, w_pool_map, pool_scale)
    return mixed_tm.transpose(1, 0, 2).reshape(tok.rows, D_POOL)


def _merge_kernel(yret_ref, mixed_ref, wr_ref, wp_ref, g1_ref, g2_ref, o_ref):
    ret = jnp.dot(yret_ref[...].astype(BF16), wr_ref[0].astype(BF16), preferred_element_type=F32)
    pool = jnp.dot(mixed_ref[...].astype(BF16), wp_ref[0].astype(BF16), preferred_element_type=F32)
    o_ref[...] = (g1_ref[...].astype(F32) * ret + g2_ref[...].astype(F32) * pool).astype(o_ref.dtype)


def _merge(tok, layer, yret, mixed, proj, w_ret_out, w_pool_out):
    tm = tok.tm
    return pl.pallas_call(
        _merge_kernel,
        grid=(tok.n_tiles, D_MODEL // TN),
        in_specs=[pl.BlockSpec((tm, V_W), lambda m, n: (m, 0)),
                  pl.BlockSpec((tm, D_POOL), lambda m, n: (m, 0)),
                  pl.BlockSpec((1, V_W, TN), lambda m, n: (layer, 0, n)),
                  pl.BlockSpec((1, D_POOL, TN), lambda m, n: (layer, 0, n)),
                  pl.BlockSpec((tm, TN), lambda m, n: (m, O_M1 // TN + n)),
                  pl.BlockSpec((tm, TN), lambda m, n: (m, O_M2 // TN + n))],
        out_specs=pl.BlockSpec((tm, TN), lambda m, n: (m, n)),
        out_shape=jax.ShapeDtypeStruct((tok.rows, D_MODEL), BF16),
        compiler_params=_params("arbitrary", "arbitrary"),
        name="merge",
    )(yret, mixed, w_ret_out, w_pool_out, proj, proj)


def _wo_kernel(merged_ref, w_ref, x_ref, gt_ref, o_ref):
    mix = jnp.dot(merged_ref[...], w_ref[0].astype(BF16), preferred_element_type=F32)
    o_ref[...] = x_ref[...] + (1.0 + gt_ref[0]) * mix


def _wo(tok, layer, merged, x, mod3, w_o):
    tm = tok.tm
    return pl.pallas_call(
        _wo_kernel,
        grid=(tok.n_tiles, D_MODEL // TN),
        in_specs=[pl.BlockSpec((tm, D_MODEL), lambda m, n: (m, 0)),
                  pl.BlockSpec((1, D_MODEL, TN), lambda m, n: (layer, 0, n)),
                  pl.BlockSpec((tm, TN), lambda m, n: (m, n)),
                  tok.mod_spec(2, tn=TN, col_axis=True)],
        out_specs=pl.BlockSpec((tm, TN), lambda m, n: (m, n)),
        out_shape=jax.ShapeDtypeStruct((tok.rows, D_MODEL), F32),
        compiler_params=_params("arbitrary", "arbitrary"),
        name="wo",
    )(merged, w_o, x, mod3)


def _mat(w_ref):
    return w_ref[(0,) * (len(w_ref.shape) - 2)]


def _ffn_kernel(*refs, n_f, with_combine, with_final):
    x_ref, sc_ref, sh_ref, gt_ref, g_ref, w1_ref, w3_ref, w2_ref = refs[:8]
    rest = list(refs[8:])
    comb_ref = rest.pop(0) if with_combine else None
    gfin_ref = rest.pop(0) if with_final else None
    o_ref, h_scr = rest
    e, f = pl.program_id(1), pl.program_id(2)

    @pl.when(jnp.logical_and(e == 0, f == 0))
    def _():
        h_scr[...] = _rmsnorm_mod(x_ref[...], g_ref[0], sc_ref[0], sh_ref[0]).astype(BF16)
        o_ref[...] = jnp.zeros_like(o_ref)

    h = h_scr[...]
    a = jnp.dot(h, _mat(w1_ref).astype(BF16), preferred_element_type=F32)
    b = jnp.dot(h, _mat(w3_ref).astype(BF16), preferred_element_type=F32)
    part = jnp.dot((_silu(a) * b).astype(BF16), _mat(w2_ref).astype(BF16), preferred_element_type=F32)
    if with_combine:
        part = comb_ref[0] * part
    o_ref[...] += part

    @pl.when(jnp.logical_and(e == pl.num_programs(1) - 1, f == n_f - 1))
    def _():
        y = x_ref[...] + (1.0 + gt_ref[0]) * o_ref[...]
        if with_final:
            y = y * lax.rsqrt(jnp.mean(y * y, axis=-1, keepdims=True) + EPS) * gfin_ref[...]
        o_ref[...] = y


def _ffn(tok, layer, x, mod, g_ffn, w1, w3, w2, combine=None, g_final=None):
    j = layer // 2
    n_e = w1.shape[1] if w1.ndim == 4 else 1
    d_ff = w1.shape[-1]
    tok_f = _Tokens(tok.n_batch, tok.seq, tok.n_past, TM_FFN, tok.per_row_mod)
    tm, n_tiles = tok_f.tm, tok_f.n_tiles
    mod3 = tok_f.prep_mod(mod)
    assert d_ff % TF == 0
    n_f = d_ff // TF
    once = pl.Buffered(1)
    if w1.ndim == 4:
        w_specs = [pl.BlockSpec((1, 1, D_MODEL, TF), lambda m, e, f: (j, e, 0, f)),
                   pl.BlockSpec((1, 1, D_MODEL, TF), lambda m, e, f: (j, e, 0, f)),
                   pl.BlockSpec((1, 1, TF, D_MODEL), lambda m, e, f: (j, e, f, 0))]
    else:
        w_specs = [pl.BlockSpec((1, D_MODEL, TF), lambda m, e, f: (j, 0, f)),
                   pl.BlockSpec((1, D_MODEL, TF), lambda m, e, f: (j, 0, f)),
                   pl.BlockSpec((1, TF, D_MODEL), lambda m, e, f: (j, f, 0))]
    in_specs = [pl.BlockSpec((tm, D_MODEL), lambda m, e, f: (m, 0), pipeline_mode=once),
                tok_f.mod_spec(4), tok_f.mod_spec(3), tok_f.mod_spec(5),
                pl.BlockSpec((1, 1, D_MODEL), lambda m, e, f: (layer, 0, 0))] + w_specs
    args = [x, mod3, mod3, mod3, g_ffn, w1, w3, w2]
    if combine is not None:
        in_specs.append(pl.BlockSpec((1, tm, 1), lambda m, e, f: (e, m, 0)))
        args.append(combine)
    if g_final is not None:
        in_specs.append(pl.BlockSpec((1, D_MODEL), lambda m, e, f: (0, 0)))
        args.append(g_final)
    return pl.pallas_call(
        functools.partial(_ffn_kernel, n_f=n_f, with_combine=combine is not None, with_final=g_final is not None),
        grid=(n_tiles, n_e, n_f),
        in_specs=in_specs,
        out_specs=pl.BlockSpec((tm, D_MODEL), lambda m, e, f: (m, 0), pipeline_mode=once),
        out_shape=jax.ShapeDtypeStruct((tok.rows, D_MODEL), F32),
        scratch_shapes=[pltpu.VMEM((tm, D_MODEL), BF16)],
        compiler_params=_params("arbitrary", "arbitrary", "arbitrary"),
        name="ffn",
    )(*args)


def _router_kernel(x_ref, sc_ref, sh_ref, g_ref, w_ref, b_ref, o_ref, esel_ref, wsel_ref, h_ref):
    h = _rmsnorm_mod(x_ref[...], g_ref[0], sc_ref[0], sh_ref[0])
    logits = jnp.dot(h.astype(BF16), w_ref[0].astype(BF16), preferred_element_type=F32) + b_ref[0]
    idx = lax.broadcasted_iota(jnp.int32, logits.shape, 1)
    top1 = jnp.max(logits, axis=-1, keepdims=True)
    i1 = jnp.min(jnp.where(logits == top1, idx, N_EXPERTS), axis=-1, keepdims=True)
    rest = jnp.where(idx == i1, -jnp.inf, logits)
    top2 = jnp.max(rest, axis=-1, keepdims=True)
    i2 = jnp.min(jnp.where(rest == top2, idx, N_EXPERTS), axis=-1, keepdims=True)
    e2 = jnp.exp(top2 - top1)
    denom = 1.0 + e2
    w1, w2 = 1.0 / denom, e2 / denom
    o_ref[...] = jnp.where(idx == i1, w1, 0.0) + jnp.where(idx == i2, w2, 0.0)
    esel_ref[...] = jnp.where(idx == 0, i1, i2)
    wsel_ref[...] = jnp.where(idx == 0, w1, w2)
    h_ref[...] = h


def _router(tok, layer, x, mod3, g_ffn, w_router, b_router):
    j = layer // 2
    tm = tok.tm
    small = pl.BlockSpec((tm, N_EXPERTS), lambda m: (m, 0))
    return pl.pallas_call(
        _router_kernel,
        grid=(tok.n_tiles,),
        in_specs=[pl.BlockSpec((tm, D_MODEL), lambda m: (m, 0)),
                  tok.mod_spec(4), tok.mod_spec(3),
                  pl.BlockSpec((1, 1, D_MODEL), lambda m: (layer, 0, 0)),
                  pl.BlockSpec((1, D_MODEL, N_EXPERTS), lambda m: (j, 0, 0)),
                  pl.BlockSpec((1, 1, N_EXPERTS), lambda m: (j, 0, 0))],
        out_specs=[small, small, small, pl.BlockSpec((tm, D_MODEL), lambda m: (m, 0))],
        out_shape=[jax.ShapeDtypeStruct((tok.rows, N_EXPERTS), F32),
                   jax.ShapeDtypeStruct((tok.rows, N_EXPERTS), jnp.int32),
                   jax.ShapeDtypeStruct((tok.rows, N_EXPERTS), F32),
                   jax.ShapeDtypeStruct((tok.rows, D_MODEL), F32)],
        compiler_params=_params("arbitrary"),
        name="router",
    )(x, mod3, mod3, g_ffn, w_router, b_router)


def _route_plan(choice, tm):
    rows = choice.shape[0]
    n_pairs = 2 * rows
    n_tiles = n_pairs // tm + N_EXPERTS
    e_flat = choice.T.reshape(n_pairs)
    onehot = (e_flat[:, None] == jnp.arange(N_EXPERTS, dtype=jnp.int32)[None, :]).astype(jnp.int32)
    csum = jnp.cumsum(onehot, axis=0)
    rank = jnp.sum(csum * onehot, axis=1) - 1
    count = csum[-1]
    tiles_per = (count + tm - 1) // tm
    tile_end = jnp.cumsum(tiles_per)
    tile_start = tile_end - tiles_per
    pos = tile_start[e_flat] * tm + rank
    token = jnp.arange(n_pairs, dtype=jnp.int32) % rows
    src_row = jnp.zeros((n_tiles * tm,), jnp.int32).at[pos].set(token)
    tile_id = jnp.arange(n_tiles, dtype=jnp.int32)
    n_used = tile_end[-1]
    tile_expert = jnp.minimum(jnp.sum(tile_id[:, None] >= tile_end[None, :], axis=1), N_EXPERTS - 1).astype(jnp.int32)
    tile_rows = jnp.clip(count[tile_expert] - (tile_id - tile_start[tile_expert]) * tm, 0, tm)
    tile_rows = jnp.where(tile_id < n_used, tile_rows, 0).astype(jnp.int32)
    last_expert = tile_expert[jnp.maximum(n_used - 1, 0)]
    tile_expert = jnp.where(tile_rows > 0, tile_expert, last_expert)
    return tile_expert, tile_rows, src_row, pos.astype(jnp.int32)


def _row_copy(src_hbm, row, dst, r, sem):
    return pltpu.make_async_copy(src_hbm.at[pl.ds(row, 1)], dst.at[pl.ds(r, 1)], sem)


def _moe_kernel(te_ref, nr_ref, src_ref, h_hbm, w1_ref, w3_ref, w2_ref, o_ref, rows_scr, h_scr, sem, *, tm, sub):
    del te_ref
    m, f = pl.program_id(0), pl.program_id(1)
    group = 8

    def for_each_row(tile, fn):
        def body(i, c):
            for r8 in range(group):
                r = i * group + r8
                fn(_row_copy(h_hbm, src_ref[tile * tm + r], rows_scr, r, sem))
            return c
        lax.fori_loop(0, (nr_ref[tile] + group - 1) // group, body, 0)

    @pl.when(f == 0)
    def _():
        o_ref[...] = jnp.zeros_like(o_ref)

        @pl.when(m == 0)
        def _():
            rows_scr[...] = jnp.zeros_like(rows_scr)
            for_each_row(0, lambda cp: cp.start())

        for_each_row(m, lambda cp: cp.wait())
        h_scr[...] = rows_scr[...].astype(BF16)

        @pl.when(m + 1 < pl.num_programs(0))
        def _():
            for_each_row(m + 1, lambda cp: cp.start())

    w1, w3, w2 = _mat(w1_ref).astype(BF16), _mat(w3_ref).astype(BF16), _mat(w2_ref).astype(BF16)

    def sub_block(s, c):
        rows = pl.ds(pl.multiple_of(s * sub, sub), sub)
        h = h_scr[rows, :]
        a = jnp.dot(h, w1, preferred_element_type=F32)
        b = jnp.dot(h, w3, preferred_element_type=F32)
        o_ref[rows, :] += jnp.dot((_silu(a) * b).astype(BF16), w2, preferred_element_type=F32)
        return c

    lax.fori_loop(0, (nr_ref[m] + sub - 1) // sub, sub_block, 0)


def _moe_sparse(layer, h, plan, w1, w3, w2, tm):
    j = layer // 2
    tile_expert, tile_rows, src_row, _ = plan
    n_tiles = tile_expert.shape[0]
    d_ff = w1.shape[3]
    assert d_ff % TF == 0 and tm % SUB_MOE == 0
    n_f = d_ff // TF

    def f_sel(m, f, nr):
        return jnp.where(nr[m] > 0, f, n_f - 1)

    return pl.pallas_call(
        functools.partial(_moe_kernel, tm=tm, sub=SUB_MOE),
        grid_spec=pltpu.PrefetchScalarGridSpec(
            num_scalar_prefetch=3,
            grid=(n_tiles, n_f),
            in_specs=[pl.BlockSpec(memory_space=pl.ANY),
                      pl.BlockSpec((1, 1, D_MODEL, TF), lambda m, f, te, nr, sr: (j, te[m], 0, f_sel(m, f, nr))),
                      pl.BlockSpec((1, 1, D_MODEL, TF), lambda m, f, te, nr, sr: (j, te[m], 0, f_sel(m, f, nr))),
                      pl.BlockSpec((1, 1, TF, D_MODEL), lambda m, f, te, nr, sr: (j, te[m], f_sel(m, f, nr), 0))],
            out_specs=pl.BlockSpec((tm, D_MODEL), lambda m, f, te, nr, sr: (m, 0)),
            scratch_shapes=[pltpu.VMEM((tm, D_MODEL), F32), pltpu.VMEM((tm, D_MODEL), BF16),
                            pltpu.SemaphoreType.DMA(())]),
        out_shape=jax.ShapeDtypeStruct((n_tiles * tm, D_MODEL), F32),
        compiler_params=_params("arbitrary", "arbitrary"),
        name="moe",
    )(tile_expert, tile_rows, src_row, h, w1, w3, w2)


def _combine_kernel(pos_ref, y_hbm, x_ref, gt_ref, w_ref, *rest, tm, rows, with_final):
    gfin_ref = rest[0] if with_final else None
    o_ref, y_scr, sems = rest[-3:]
    i, n = pl.program_id(0), pl.num_programs(0)

    def for_each_row(tile, slot, fn):
        def body(r, c):
            for k in range(2):
                fn(_row_copy(y_hbm, pos_ref[k * rows + tile * tm + r], y_scr.at[slot, k], r, sems.at[slot]))
            return c
        lax.fori_loop(0, tm, body, 0, unroll=8)

    @pl.when(i == 0)
    def _():
        for_each_row(0, 0, lambda cp: cp.start())

    @pl.when(i + 1 < n)
    def _():
        for_each_row(i + 1, (i + 1) % 2, lambda cp: cp.start())

    slot = i % 2
    for_each_row(i, slot, lambda cp: cp.wait())
    w = w_ref[...]
    f = w[:, 0:1] * y_scr[slot, 0] + w[:, 1:2] * y_scr[slot, 1]
    y = x_ref[...] + (1.0 + gt_ref[0]) * f
    if with_final:
        y = y * lax.rsqrt(jnp.mean(y * y, axis=-1, keepdims=True) + EPS) * gfin_ref[...]
    o_ref[...] = y


def _combine(tok, y_sorted, pos, x, mod, wsel, g_final):
    tok_c = _Tokens(tok.n_batch, tok.seq, tok.n_past, TM_COMBINE, tok.per_row_mod)
    tm = tok_c.tm
    mod3 = tok_c.prep_mod(mod)
    in_specs = [pl.BlockSpec(memory_space=pl.ANY),
                pl.BlockSpec((tm, D_MODEL), lambda m, pos: (m, 0)),
                tok_c.mod_spec(5),
                pl.BlockSpec((tm, N_EXPERTS), lambda m, pos: (m, 0))]
    args = [pos, y_sorted, x, mod3, wsel]
    if g_final is not None:
        in_specs.append(pl.BlockSpec((1, D_MODEL), lambda m, pos: (0, 0)))
        args.append(g_final)
    return pl.pallas_call(
        functools.partial(_combine_kernel, tm=tm, rows=tok.rows, with_final=g_final is not None),
        grid_spec=pltpu.PrefetchScalarGridSpec(
            num_scalar_prefetch=1,
            grid=(tok_c.n_tiles,),
            in_specs=in_specs,
            out_specs=pl.BlockSpec((tm, D_MODEL), lambda m, pos: (m, 0)),
            scratch_shapes=[pltpu.VMEM((2, 2, tm, D_MODEL), F32), pltpu.SemaphoreType.DMA((2,))]),
        out_shape=jax.ShapeDtypeStruct((tok.rows, D_MODEL), F32),
        compiler_params=_params("arbitrary"),
        name="combine",
    )(*args)


def _decoder(tok, x, mod_layers, ret_state, pool_state, p):
    rope_tabs = tok.rope_tables()
    is_prompt = ret_state is None
    act_dtype = BF16 if is_prompt else F32
    u_layers = []
    s_all = None
    for i in range(DEPTH):
        mod3 = tok.prep_mod(mod_layers[i])
        proj, u = _inproj(tok, i, x, mod3, p['g_mix'], p['w_in'], rope_tabs, act_dtype)
        u_layers.append(u)
        if is_prompt:
            yret, s_all = _ret_prompt(tok, i, proj, p['g_ret'], s_all)
            mixed = _pool_prompt(tok, i, u, p['w_pool_map'], p['pool_scale'])
        else:
            yret, s_all = _ret_sample(tok, i, proj, p['g_ret'], ret_state, s_all)
            mixed = _pool_sample(tok, i, u, pool_state[i].transpose(1, 0, 2), p['w_pool_map'], p['pool_scale'])
        merged = _merge(tok, i, yret, mixed, proj, p['w_ret_out'], p['w_pool_out'])
        x = _wo(tok, i, merged, x, mod3, p['w_o'])
        g_final = p['g_final'] if i == DEPTH - 1 else None
        if i % 2 == 0:
            x = _ffn(tok, i, x, mod_layers[i], p['g_ffn'], p['w1_dense'], p['w3_dense'], p['w2_dense'],
                     g_final=g_final)
        else:
            comb, esel, wsel, h = _router(tok, i, x, mod3, p['g_ffn'], p['w_router'], p['b_router'])
            if is_prompt:
                plan = _route_plan(esel[:, :2], TM_MOE)
                y_sorted = _moe_sparse(i, h, plan, p['w1_moe'], p['w3_moe'], p['w2_moe'], TM_MOE)
                x = _combine(tok, y_sorted, plan[3], x, mod_layers[i], wsel, g_final)
            else:
                comb = comb.T.reshape(N_EXPERTS, tok.rows, 1)
                x = _ffn(tok, i, x, mod_layers[i], p['g_ffn'], p['w1_moe'], p['w3_moe'], p['w2_moe'],
                         combine=comb, g_final=g_final)
    return x, s_all, u_layers


def kernel(x_prompt, x_sample, c_prompt, c_sample, state_ret, state_pool, w_ada, b_ada, g_mix, w_in, g_ret,
           w_ret_out, w_pool_map, pool_scale, w_pool_out, w_o, g_ffn, w1_dense, w3_dense, w2_dense, w_router,
           b_router, w1_moe, w3_moe, w2_moe, g_final):
    row3 = lambda a: a.reshape(a.shape[0], 1, a.shape[1])
    p = dict(g_mix=row3(g_mix), w_in=w_in, g_ret=row3(g_ret), w_ret_out=w_ret_out, w_pool_map=w_pool_map,
             pool_scale=row3(pool_scale), w_pool_out=w_pool_out, w_o=w_o, g_ffn=row3(g_ffn),
             w1_dense=w1_dense, w3_dense=w3_dense, w2_dense=w2_dense, w_router=w_router, b_router=row3(b_router),
             w1_moe=w1_moe, w3_moe=w3_moe, w2_moe=w2_moe, g_final=g_final.reshape(1, D_MODEL))
    n_bp, seq_p, _ = x_prompt.shape
    n_bs, seq_s, _ = x_sample.shape

    mod = _ada(jnp.concatenate([c_prompt, c_sample], axis=0), w_ada, b_ada)

    tok_p = _Tokens(n_bp, seq_p, 0, TM_PROMPT, per_row_mod=False)
    tok_s = _Tokens(n_bs, seq_s, PAST_LEN, TM_SAMPLE, per_row_mod=True)

    y_p, s_prompt, u_p = _decoder(tok_p, x_prompt.reshape(tok_p.rows, D_MODEL), [mod[i, :n_bp] for i in range(DEPTH)],
                                  None, None, p)
    y_s, s_sample, u_s = _decoder(tok_s, x_sample.reshape(tok_s.rows, D_MODEL), [mod[i, n_bp:] for i in range(DEPTH)],
                                  state_ret, state_pool, p)

    buf_prompt = jnp.stack([u.reshape(n_bp, seq_p, D_POOL)[:, seq_p - POOL_BUF:] for u in u_p])
    buf_sample = jnp.stack([
        jnp.concatenate([state_pool[i], u_s[i].reshape(n_bs, seq_s, D_POOL)], axis=1)[:, -POOL_BUF:]
        for i in range(DEPTH)])
    return (y_p.reshape(x_prompt.shape), y_s.reshape(x_sample.shape), s_prompt, buf_prompt, s_sample, buf_sample)
```

```python
import functools

import numpy as np
import jax
import jax.numpy as jnp
from jax import lax
from jax.experimental import pallas as pl
from jax.experimental.pallas import tpu as pltpu

F32 = jnp.float32
BF16 = jnp.bfloat16

D_MODEL = 2048
DEPTH = 2
PAST_LEN = 16384
RET_HEADS = 8
RET_DK = 128
RET_DV = 256
RET_CHUNK = 128
ROPE_BASE = 10000.0
QK_W = RET_HEADS * RET_DK
V_W = RET_HEADS * RET_DV
POOL_WINDOWS = (2, 4, 8, 16)
D_POOL = 1024
POOL_GW = 256
POOL_BUF = 15
POOL_HALO = 16
N_IN = 2 * QK_W + 2 * V_W + D_POOL + 2 * D_MODEL
N_EXPERTS = 8
EPS = 1e-6
N_MOD = 6

O_Q, O_K, O_V, O_G, O_U = 0, QK_W, 2 * QK_W, 2 * QK_W + V_W, 2 * QK_W + 2 * V_W
O_M1 = O_U + D_POOL
O_M2 = O_M1 + D_MODEL

VMEM_LIMIT_BYTES = 56 * 1024 * 1024
TN = 512
TF = 256
TM_PROMPT = 1024
TM_SAMPLE = 512
TM_FFN = 1024
TL_POOL = 512
TM_MOE = 1024
SUB_MOE = 256
TM_COMBINE = 256


def _params(*sem):
    return pltpu.CompilerParams(dimension_semantics=sem, vmem_limit_bytes=VMEM_LIMIT_BYTES)


class _Tokens:
    def __init__(self, n_batch, seq, n_past, tm, per_row_mod):
        self.n_batch, self.seq, self.n_past = n_batch, seq, n_past
        self.rows = n_batch * seq
        self.tm = min(tm, self.rows)
        self.per_row_mod = per_row_mod
        if not per_row_mod:
            self.tm = min(self.tm, seq)
            assert seq % self.tm == 0
        assert self.rows % self.tm == 0
        self.n_tiles = self.rows // self.tm

    def prep_mod(self, mod):
        if self.per_row_mod:
            return jnp.repeat(mod, self.seq, axis=0).reshape(self.n_tiles, self.tm, N_MOD * D_MODEL)
        return mod.reshape(self.n_batch, 1, N_MOD * D_MODEL)

    def mod_spec(self, chunk, tn=D_MODEL, col_axis=False):
        per_chunk = D_MODEL // tn
        if self.per_row_mod:
            shape = (1, self.tm, tn)
            if col_axis:
                return pl.BlockSpec(shape, lambda m, n, *_: (m, 0, chunk * per_chunk + n))
            return pl.BlockSpec(shape, lambda m, *_: (m, 0, chunk * per_chunk))
        tiles_per_batch = self.seq // self.tm
        shape = (1, 1, tn)
        if col_axis:
            return pl.BlockSpec(shape, lambda m, n, *_: (m // tiles_per_batch, 0, chunk * per_chunk + n))
        return pl.BlockSpec(shape, lambda m, *_: (m // tiles_per_batch, 0, chunk * per_chunk))

    def rope_tables(self):
        half = RET_DK // 2
        pos = jnp.arange(self.seq, dtype=F32) + float(self.n_past)
        inv = ROPE_BASE ** (-jnp.arange(half, dtype=F32) / half)
        ang = pos[:, None] * inv[None, :]
        cos, sin = jnp.cos(ang), jnp.sin(ang)
        cos_full = jnp.concatenate([cos, cos], axis=1)
        sin_signed = jnp.concatenate([-sin, sin], axis=1)
        if self.per_row_mod:
            cos_full = jnp.tile(cos_full, (self.n_batch, 1))
            sin_signed = jnp.tile(sin_signed, (self.n_batch, 1))
        return cos_full, sin_signed

    def rope_spec(self):
        n_blocks = (self.rows if self.per_row_mod else self.seq) // self.tm
        return pl.BlockSpec((self.tm, RET_DK), lambda m, *_: (m % n_blocks, 0))


def _retention_tables(chunk):
    log_gamma = jnp.log1p(-jnp.exp2(-5.0 - jnp.arange(RET_HEADS, dtype=F32)))
    idx = jnp.arange(chunk, dtype=F32)
    rel = idx[:, None] - idx[None, :]
    dmask = jnp.where(rel[None] >= 0, jnp.exp(log_gamma[:, None, None] * jnp.maximum(rel, 0.0)[None]), 0.0)
    q_dec = jnp.exp(log_gamma[:, None] * (idx + 1.0)[None, :])
    k_dec = jnp.exp(log_gamma[:, None] * (chunk - 1.0 - idx)[None, :])
    chunk_dec = jnp.exp(log_gamma * chunk)
    q_dec = jnp.broadcast_to(q_dec[:, :, None], (RET_HEADS, chunk, RET_DV))
    k_dec = jnp.broadcast_to(k_dec[:, :, None], (RET_HEADS, chunk, RET_DK))
    chunk_dec = jnp.broadcast_to(chunk_dec[:, None, None], (RET_HEADS, 1, RET_DV))
    return dmask, q_dec, k_dec, chunk_dec


def _rmsnorm_mod(x, g, scale, shift):
    y = x * lax.rsqrt(jnp.mean(x * x, axis=-1, keepdims=True) + EPS) * g
    return y * (1.0 + scale) + shift


def _silu(x):
    return x * jax.nn.sigmoid(x)


def _ada_kernel(c_ref, w_ref, b_ref, o_ref):
    cs = _silu(c_ref[...]).astype(BF16)
    o_ref[0] = jnp.dot(cs, w_ref[0].astype(BF16), preferred_element_type=F32) + b_ref[0]


def _ada(c_all, w_ada, b_ada):
    nb = c_all.shape[0]
    tn = 1024
    width = N_MOD * D_MODEL
    return pl.pallas_call(
        _ada_kernel,
        grid=(DEPTH, width // tn),
        in_specs=[pl.BlockSpec((nb, D_MODEL), lambda l, n: (0, 0)),
                  pl.BlockSpec((1, D_MODEL, tn), lambda l, n: (l, 0, n)),
                  pl.BlockSpec((1, 1, tn), lambda l, n: (l, 0, n))],
        out_specs=pl.BlockSpec((1, nb, tn), lambda l, n: (l, 0, n)),
        out_shape=jax.ShapeDtypeStruct((DEPTH, nb, width), F32),
        compiler_params=_params("arbitrary", "arbitrary"),
        name="ada",
    )(c_all, w_ada, b_ada.reshape(DEPTH, 1, width))


def _inproj_kernel(x_ref, sc_ref, sh_ref, g_ref, w_ref, cos_ref, sin_ref, o_ref, u_ref, h_scr):
    n = pl.program_id(1)

    @pl.when(n == 0)
    def _():
        h_scr[...] = _rmsnorm_mod(x_ref[...], g_ref[0], sc_ref[0], sh_ref[0]).astype(BF16)

    def project():
        return jnp.dot(h_scr[...], w_ref[0].astype(BF16), preferred_element_type=F32)

    def rope(scale):
        acc = project()
        for j in range(TN // RET_DK):
            xh = acc[:, j * RET_DK:(j + 1) * RET_DK]
            r = xh * cos_ref[...] + pltpu.roll(xh, RET_DK // 2, 1) * sin_ref[...]
            if scale != 1.0:
                r = r * scale
            o_ref[:, j * RET_DK:(j + 1) * RET_DK] = r.astype(o_ref.dtype)

    @pl.when(n < O_K // TN)
    def _():
        rope(1.0)

    @pl.when(jnp.logical_and(n >= O_K // TN, n < O_V // TN))
    def _():
        rope(RET_DK ** -0.5)

    @pl.when(jnp.logical_and(n >= O_V // TN, n < O_G // TN))
    def _():
        o_ref[...] = project().astype(o_ref.dtype)

    @pl.when(jnp.logical_and(n >= O_G // TN, n < O_U // TN))
    def _():
        o_ref[...] = _silu(project()).astype(o_ref.dtype)

    @pl.when(jnp.logical_and(n >= O_U // TN, n < O_M1 // TN))
    def _():
        acc = project()
        o_ref[...] = acc.astype(o_ref.dtype)
        u_ref[...] = acc

    @pl.when(n >= O_M1 // TN)
    def _():
        o_ref[...] = jax.nn.sigmoid(project()).astype(o_ref.dtype)


def _inproj(tok, layer, x, mod3, g_mix, w_in, rope_tabs, out_dtype):
    tm = tok.tm
    n_u = D_POOL // TN
    proj, u = pl.pallas_call(
        _inproj_kernel,
        grid=(tok.n_tiles, N_IN // TN),
        in_specs=[pl.BlockSpec((tm, D_MODEL), lambda m, n: (m, 0)),
                  tok.mod_spec(1), tok.mod_spec(0),
                  pl.BlockSpec((1, 1, D_MODEL), lambda m, n: (layer, 0, 0)),
                  pl.BlockSpec((1, D_MODEL, TN), lambda m, n: (layer, 0, n)),
                  tok.rope_spec(), tok.rope_spec()],
        out_specs=[pl.BlockSpec((tm, TN), lambda m, n: (m, n)),
                   pl.BlockSpec((tm, TN), lambda m, n: (m, jnp.clip(n - O_U // TN, 0, n_u - 1)))],
        out_shape=[jax.ShapeDtypeStruct((tok.rows, N_IN), out_dtype),
                   jax.ShapeDtypeStruct((tok.rows, D_POOL), F32)],
        scratch_shapes=[pltpu.VMEM((tm, D_MODEL), BF16)],
        compiler_params=_params("arbitrary", "arbitrary"),
        name="inproj",
    )(x, mod3, mod3, g_mix, w_in, *rope_tabs)
    return proj, u


def _head_norm_gate(o, gret, gate):
    mu = jnp.mean(o, axis=-1, keepdims=True)
    d = o - mu
    var = jnp.mean(d * d, axis=-1, keepdims=True)
    return gate.astype(F32) * (d * lax.rsqrt(var + EPS) * gret)


def _ret_prompt_kernel(q_ref, k_ref, v_ref, g_ref, gret_ref, dmask_ref, qdec_ref, kdec_ref, cdec_ref,
                       *rest):
    y_ref, s_ref = rest[-2:]

    @pl.when(pl.program_id(1) == 0)
    def _():
        s_ref[...] = jnp.zeros_like(s_ref)

    for h in range(RET_HEADS):
        ks, vs = slice(h * RET_DK, (h + 1) * RET_DK), slice(h * RET_DV, (h + 1) * RET_DV)
        q, k, v = q_ref[:, ks], k_ref[:, ks], v_ref[:, vs]
        state = s_ref[0, 0, h]
        scores = lax.dot_general(q, k, (((1,), (1,)), ((), ())), preferred_element_type=F32) * dmask_ref[h]
        inner = jnp.dot(scores.astype(BF16), v, preferred_element_type=F32)
        cross = jnp.dot(q, state.astype(BF16), preferred_element_type=F32) * qdec_ref[h]
        k_decayed = (k.astype(F32) * kdec_ref[h]).astype(BF16)
        s_ref[0, 0, h] = cdec_ref[h] * state + lax.dot_general(
            k_decayed, v, (((0,), (0,)), ((), ())), preferred_element_type=F32)
        y_ref[:, vs] = _head_norm_gate(inner + cross, gret_ref[0, :, vs], g_ref[:, vs]).astype(y_ref.dtype)


def _table_specs(tables):
    return [pl.BlockSpec(t.shape, lambda *_: (0, 0, 0)) for t in tables]


def _ret_prompt(tok, layer, proj, g_ret, s_all):
    n_b, seq = tok.n_batch, tok.seq
    c = RET_CHUNK
    assert seq % c == 0
    nc = seq // c
    tables = _retention_tables(c)
    in_specs = [pl.BlockSpec((c, QK_W), lambda b, i: (b * nc + i, O_Q // QK_W)),
                pl.BlockSpec((c, QK_W), lambda b, i: (b * nc + i, O_K // QK_W)),
                pl.BlockSpec((c, V_W), lambda b, i: (b * nc + i, O_V // V_W)),
                pl.BlockSpec((c, V_W), lambda b, i: (b * nc + i, O_G // V_W)),
                pl.BlockSpec((1, 1, V_W), lambda b, i: (layer, 0, 0))] + _table_specs(tables)
    args = [proj, proj, proj, proj, g_ret, *tables]
    aliases = {}
    if s_all is not None:
        aliases = {len(args): 1}
        in_specs.append(pl.BlockSpec(memory_space=pl.ANY))
        args.append(s_all)
    y, s_all = pl.pallas_call(
        _ret_prompt_kernel,
        grid=(n_b, nc),
        in_specs=in_specs,
        out_specs=[pl.BlockSpec((c, V_W), lambda b, i: (b * nc + i, 0)),
                   pl.BlockSpec((1, 1, RET_HEADS, RET_DK, RET_DV), lambda b, i: (layer, b, 0, 0, 0))],
        out_shape=[jax.ShapeDtypeStruct((tok.rows, V_W), BF16),
                   jax.ShapeDtypeStruct((DEPTH, n_b, RET_HEADS, RET_DK, RET_DV), F32)],
        input_output_aliases=aliases,
        compiler_params=_params("arbitrary", "arbitrary"),
        name="ret_prompt",
    )(*args)
    return y, s_all


def _ret_sample_kernel(q_ref, k_ref, v_ref, g_ref, kt_ref, gret_ref, s0_ref, dm_ref, qdec_ref, kdect_ref, cdec_ref,
                       *rest, bb, seq):
    y_ref, s_ref = rest[-2:]
    hs = RET_HEADS * seq
    reps = lambda a: jnp.concatenate([a] * RET_HEADS, axis=0)
    mask_q = (lax.broadcasted_iota(jnp.int32, (hs, QK_W), 0) // seq
              == lax.broadcasted_iota(jnp.int32, (hs, QK_W), 1) // RET_DK)
    mask_v = (lax.broadcasted_iota(jnp.int32, (hs, V_W), 0) // seq
              == lax.broadcasted_iota(jnp.int32, (hs, V_W), 1) // RET_DV)
    nt = (((1,), (1,)), ((), ()))
    for b in range(bb):
        q, k, v = q_ref[b], k_ref[b], v_ref[b]
        q_blk = jnp.where(mask_q, reps(q), 0.0).astype(BF16)
        scores = lax.dot_general(q_blk, reps(k).astype(BF16), nt, preferred_element_type=F32) * dm_ref[...]
        v_stack = jnp.concatenate([v[:, h * RET_DV:(h + 1) * RET_DV] for h in range(RET_HEADS)], axis=0)
        inner = jnp.dot(scores.astype(BF16), v_stack.astype(BF16), preferred_element_type=F32)
        state = s0_ref[0, b]
        cross = jnp.dot(q_blk, state.reshape(QK_W, RET_DV).astype(BF16), preferred_element_type=F32)
        o = inner + cross * qdec_ref[...]
        mu = jnp.mean(o, axis=-1, keepdims=True)
        d = o - mu
        normed = d * lax.rsqrt(jnp.mean(d * d, axis=-1, keepdims=True) + EPS)
        k_dec_t = (kt_ref[b] * kdect_ref[...]).astype(BF16)
        v_blk = jnp.where(mask_v, reps(v), 0.0).astype(BF16)
        upd = jnp.dot(k_dec_t, v_blk, preferred_element_type=F32)
        for h in range(RET_HEADS):
            vs = slice(h * RET_DV, (h + 1) * RET_DV)
            y_ref[b, :, vs] = g_ref[b, :, vs] * (normed[h * seq:(h + 1) * seq] * gret_ref[0, :, vs])
            s_ref[0, b, h] = cdec_ref[h] * state[h] + upd[:, vs]


def _ret_sample(tok, layer, proj, g_ret, state_ret, s_all):
    n_b, seq = tok.n_batch, tok.seq
    bb = 4
    assert n_b % bb == 0
    hs = RET_HEADS * seq
    dmask, q_dec, k_dec, chunk_dec = _retention_tables(seq)
    dm_blk = (dmask[:, :, None, :] * jnp.eye(RET_HEADS, dtype=F32)[:, None, :, None]).reshape(hs, hs)
    tables = [dm_blk, q_dec.reshape(hs, RET_DV), k_dec[:, :, 0].reshape(1, hs), chunk_dec]
    proj3 = proj.reshape(n_b, seq, N_IN)
    k_t = proj3[:, :, O_K:O_K + QK_W].reshape(n_b, seq, RET_HEADS, RET_DK).transpose(0, 3, 2, 1).reshape(n_b, RET_DK, hs)
    state_spec = pl.BlockSpec((1, bb, RET_HEADS, RET_DK, RET_DV), lambda i: (layer, i, 0, 0, 0))
    in_specs = [pl.BlockSpec((bb, seq, QK_W), lambda i: (i, 0, O_Q // QK_W)),
                pl.BlockSpec((bb, seq, QK_W), lambda i: (i, 0, O_K // QK_W)),
                pl.BlockSpec((bb, seq, V_W), lambda i: (i, 0, O_V // V_W)),
                pl.BlockSpec((bb, seq, V_W), lambda i: (i, 0, O_G // V_W)),
                pl.BlockSpec((bb, RET_DK, hs), lambda i: (i, 0, 0)),
                pl.BlockSpec((1, 1, V_W), lambda i: (layer, 0, 0)),
                state_spec] + [pl.BlockSpec(t.shape, lambda i, nd=t.ndim: (0,) * nd) for t in tables]
    args = [proj3, proj3, proj3, proj3, k_t, g_ret, state_ret, *tables]
    aliases = {}
    if s_all is not None:
        aliases = {len(args): 1}
        in_specs.append(pl.BlockSpec(memory_space=pl.ANY))
        args.append(s_all)
    y, s_all = pl.pallas_call(
        functools.partial(_ret_sample_kernel, bb=bb, seq=seq),
        grid=(n_b // bb,),
        in_specs=in_specs,
        out_specs=[pl.BlockSpec((bb, seq, V_W), lambda i: (i, 0, 0)), state_spec],
        out_shape=[jax.ShapeDtypeStruct((n_b, seq, V_W), F32),
                   jax.ShapeDtypeStruct(state_ret.shape, F32)],
        input_output_aliases=aliases,
        compiler_params=_params("arbitrary"),
        name="ret_sample",
    )(*args)
    return y.reshape(tok.rows, V_W), s_all


def _pool_map(pooled, wmap_ref, scale_ref, gi):
    cols = slice(gi * POOL_GW, (gi + 1) * POOL_GW)
    mixed = jnp.dot(pooled.astype(BF16), wmap_ref[0, gi].astype(BF16), preferred_element_type=F32)
    return mixed * scale_ref[0, :, cols]


def _pool_prompt_kernel(u_ref, halo_ref, wmap_ref, scale_ref, o_ref, uf_scr, *, tl):
    first = pl.program_id(1) == 0
    uf_scr[0:POOL_HALO, :] = jnp.where(first, 0.0, halo_ref[...])
    uf_scr[POOL_HALO:, :] = u_ref[...]
    row = lax.broadcasted_iota(jnp.int32, (tl, POOL_GW), 0) + pl.program_id(1) * tl
    for gi, w in enumerate(POOL_WINDOWS):
        cols = slice(gi * POOL_GW, (gi + 1) * POOL_GW)
        u = uf_scr[POOL_HALO:, cols]
        acc = u
        for j in range(1, w):
            acc = acc + uf_scr[POOL_HALO - j:POOL_HALO - j + tl, cols]
        cnt = jnp.minimum(w, row + 1).astype(F32)
        o_ref[:, cols] = _pool_map(acc / cnt - u, wmap_ref, scale_ref, gi).astype(o_ref.dtype)


def _pool_prompt(tok, layer, u, w_pool_map, pool_scale):
    n_b, seq = tok.n_batch, tok.seq
    tl = min(TL_POOL, seq)
    assert seq % tl == 0 and tl % POOL_HALO == 0
    nl = seq // tl
    per = tl // POOL_HALO
    return pl.pallas_call(
        functools.partial(_pool_prompt_kernel, tl=tl),
        grid=(n_b, nl),
        in_specs=[pl.BlockSpec((tl, D_POOL), lambda b, l: (b * nl + l, 0)),
                  pl.BlockSpec((POOL_HALO, D_POOL), lambda b, l: (jnp.maximum((b * nl + l) * per - 1, 0), 0)),
                  pl.BlockSpec((1,) + w_pool_map.shape[1:], lambda b, l: (layer, 0, 0, 0)),
                  pl.BlockSpec((1, 1, D_POOL), lambda b, l: (layer, 0, 0))],
        out_specs=pl.BlockSpec((tl, D_POOL), lambda b, l: (b * nl + l, 0)),
        out_shape=jax.ShapeDtypeStruct((tok.rows, D_POOL), BF16),
        scratch_shapes=[pltpu.VMEM((POOL_HALO + tl, D_POOL), F32)],
        compiler_params=_params("arbitrary", "arbitrary"),
        name="pool_prompt",
    )(u, u, w_pool_map, pool_scale)


def _pool_sample_kernel(buf_ref, u_ref, wmap_ref, scale_ref, o_ref, *, seq, n_b):
    def row(j, cols):
        return buf_ref[j, :, cols] if j < POOL_BUF else u_ref[j - POOL_BUF, :, cols]

    for gi, w in enumerate(POOL_WINDOWS):
        cols = slice(gi * POOL_GW, (gi + 1) * POOL_GW)
        for t in range(seq):
            acc = row(POOL_BUF + t, cols)
            for j in range(1, w):
                acc = acc + row(POOL_BUF + t - j, cols)
            pooled = acc / float(w) - row(POOL_BUF + t, cols)
            o_ref[t, :, cols] = _pool_map(pooled, wmap_ref, scale_ref, gi)


def _pool_sample(tok, layer, u, buf_tm, w_pool_map, pool_scale):
    n_b, seq = tok.n_batch, tok.seq
    assert tok.n_past >= max(POOL_WINDOWS)
    u_tm = u.reshape(n_b, seq, D_POOL).transpose(1, 0, 2)
    mixed_tm = pl.pallas_call(
        functools.partial(_pool_sample_kernel, seq=seq, n_b=n_b),
        grid=(1,),
        in_specs=[pl.BlockSpec(buf_tm.shape, lambda i: (0, 0, 0)),
                  pl.BlockSpec(u_tm.shape, lambda i: (0, 0, 0)),
                  pl.BlockSpec((1,) + w_pool_map.shape[1:], lambda i: (layer, 0, 0, 0)),
                  pl.BlockSpec((1, 1, D_POOL), lambda i: (layer, 0, 0))],
        out_specs=pl.BlockSpec(u_tm.shape, lambda i: (0, 0, 0)),
        out_shape=jax.ShapeDtypeStruct(u_tm.shape, F32),
        compiler_params=_params("arbitrary"),
        name="pool_sample",
    )(buf_tm, u_tm, w_pool_map, pool_scale)
    return mixed_tm.transpose(1, 0, 2).reshape(tok.rows, D_POOL)


def _merge_kernel(yret_ref, mixed_ref, wr_ref, wp_ref, g1_ref, g2_ref, o_ref):
    ret = jnp.dot(yret_ref[...].astype(BF16), wr_ref[0].astype(BF16), preferred_element_type=F32)
    pool = jnp.dot(mixed_ref[...].astype(BF16), wp_ref[0].astype(BF16), preferred_element_type=F32)
    o_ref[...] = (g1_ref[...].astype(F32) * ret + g2_ref[...].astype(F32) * pool).astype(o_ref.dtype)


def _merge(tok, layer, yret, mixed, proj, w_ret_out, w_pool_out):
    tm = tok.tm
    return pl.pallas_call(
        _merge_kernel,
        grid=(tok.n_tiles, D_MODEL // TN),
        in_specs=[pl.BlockSpec((tm, V_W), lambda m, n: (m, 0)),
                  pl.BlockSpec((tm, D_POOL), lambda m, n: (m, 0)),
                  pl.BlockSpec((1, V_W, TN), lambda m, n: (layer, 0, n)),
                  pl.BlockSpec((1, D_POOL, TN), lambda m, n: (layer, 0, n)),
                  pl.BlockSpec((tm, TN), lambda m, n: (m, O_M1 // TN + n)),
                  pl.BlockSpec((tm, TN), lambda m, n: (m, O_M2 // TN + n))],
        out_specs=pl.BlockSpec((tm, TN), lambda m, n: (m, n)),
        out_shape=jax.ShapeDtypeStruct((tok.rows, D_MODEL), BF16),
        compiler_params=_params("arbitrary", "arbitrary"),
        name="merge",
    )(yret, mixed, w_ret_out, w_pool_out, proj, proj)


def _wo_kernel(merged_ref, w_ref, x_ref, gt_ref, o_ref):
    mix = jnp.dot(merged_ref[...], w_ref[0].astype(BF16), preferred_element_type=F32)
    o_ref[...] = x_ref[...] + (1.0 + gt_ref[0]) * mix


def _wo(tok, layer, merged, x, mod3, w_o):
    tm = tok.tm
    return pl.pallas_call(
        _wo_kernel,
        grid=(tok.n_tiles, D_MODEL // TN),
        in_specs=[pl.BlockSpec((tm, D_MODEL), lambda m, n: (m, 0)),
                  pl.BlockSpec((1, D_MODEL, TN), lambda m, n: (layer, 0, n)),
                  pl.BlockSpec((tm, TN), lambda m, n: (m, n)),
                  tok.mod_spec(2, tn=TN, col_axis=True)],
        out_specs=pl.BlockSpec((tm, TN), lambda m, n: (m, n)),
        out_shape=jax.ShapeDtypeStruct((tok.rows, D_MODEL), F32),
        compiler_params=_params("arbitrary", "arbitrary"),
        name="wo",
    )(merged, w_o, x, mod3)


def _mat(w_ref):
    return w_ref[(0,) * (len(w_ref.shape) - 2)]


def _ffn_kernel(*refs, n_f, with_combine, with_final):
    x_ref, sc_ref, sh_ref, gt_ref, g_ref, w1_ref, w3_ref, w2_ref = refs[:8]
    rest = list(refs[8:])
    comb_ref = rest.pop(0) if with_combine else None
    gfin_ref = rest.pop(0) if with_final else None
    o_ref, h_scr = rest
    e, f = pl.program_id(1), pl.program_id(2)

    @pl.when(jnp.logical_and(e == 0, f == 0))
    def _():
        h_scr[...] = _rmsnorm_mod(x_ref[...], g_ref[0], sc_ref[0], sh_ref[0]).astype(BF16)
        o_ref[...] = jnp.zeros_like(o_ref)

    h = h_scr[...]
    a = jnp.dot(h, _mat(w1_ref).astype(BF16), preferred_element_type=F32)
    b = jnp.dot(h, _mat(w3_ref).astype(BF16), preferred_element_type=F32)
    part = jnp.dot((_silu(a) * b).astype(BF16), _mat(w2_ref).astype(BF16), preferred_element_type=F32)
    if with_combine:
        part = comb_ref[0] * part
    o_ref[...] += part

    @pl.when(jnp.logical_and(e == pl.num_programs(1) - 1, f == n_f - 1))
    def _():
        y = x_ref[...] + (1.0 + gt_ref[0]) * o_ref[...]
        if with_final:
            y = y * lax.rsqrt(jnp.mean(y * y, axis=-1, keepdims=True) + EPS) * gfin_ref[...]
        o_ref[...] = y


def _ffn(tok, layer, x, mod, g_ffn, w1, w3, w2, combine=None, g_final=None):
    j = layer // 2
    n_e = w1.shape[1] if w1.ndim == 4 else 1
    d_ff = w1.shape[-1]
    tok_f = _Tokens(tok.n_batch, tok.seq, tok.n_past, TM_FFN, tok.per_row_mod)
    tm, n_tiles = tok_f.tm, tok_f.n_tiles
    mod3 = tok_f.prep_mod(mod)
    assert d_ff % TF == 0
    n_f = d_ff // TF
    once = pl.Buffered(1)
    if w1.ndim == 4:
        w_specs = [pl.BlockSpec((1, 1, D_MODEL, TF), lambda m, e, f: (j, e, 0, f)),
                   pl.BlockSpec((1, 1, D_MODEL, TF), lambda m, e, f: (j, e, 0, f)),
                   pl.BlockSpec((1, 1, TF, D_MODEL), lambda m, e, f: (j, e, f, 0))]
    else:
        w_specs = [pl.BlockSpec((1, D_MODEL, TF), lambda m, e, f: (j, 0, f)),
                   pl.BlockSpec((1, D_MODEL, TF), lambda m, e, f: (j, 0, f)),
                   pl.BlockSpec((1, TF, D_MODEL), lambda m, e, f: (j, f, 0))]
    in_specs = [pl.BlockSpec((tm, D_MODEL), lambda m, e, f: (m, 0), pipeline_mode=once),
                tok_f.mod_spec(4), tok_f.mod_spec(3), tok_f.mod_spec(5),
                pl.BlockSpec((1, 1, D_MODEL), lambda m, e, f: (layer, 0, 0))] + w_specs
    args = [x, mod3, mod3, mod3, g_ffn, w1, w3, w2]
    if combine is not None:
        in_specs.append(pl.BlockSpec((1, tm, 1), lambda m, e, f: (e, m, 0)))
        args.append(combine)
    if g_final is not None:
        in_specs.append(pl.BlockSpec((1, D_MODEL), lambda m, e, f: (0, 0)))
        args.append(g_final)
    return pl.pallas_call(
        functools.partial(_ffn_kernel, n_f=n_f, with_combine=combine is not None, with_final=g_final is not None),
        grid=(n_tiles, n_e, n_f),
        in_specs=in_specs,
        out_specs=pl.BlockSpec((tm, D_MODEL), lambda m, e, f: (m, 0), pipeline_mode=once),
        out_shape=jax.ShapeDtypeStruct((tok.rows, D_MODEL), F32),
        scratch_shapes=[pltpu.VMEM((tm, D_MODEL), BF16)],
        compiler_params=_params("arbitrary", "arbitrary", "arbitrary"),
        name="ffn",
    )(*args)


def _router_kernel(x_ref, sc_ref, sh_ref, g_ref, w_ref, b_ref, o_ref, esel_ref, wsel_ref, h_ref):
    h = _rmsnorm_mod(x_ref[...], g_ref[0], sc_ref[0], sh_ref[0])
    logits = jnp.dot(h.astype(BF16), w_ref[0].astype(BF16), preferred_element_type=F32) + b_ref[0]
    idx = lax.broadcasted_iota(jnp.int32, logits.shape, 1)
    top1 = jnp.max(logits, axis=-1, keepdims=True)
    i1 = jnp.min(jnp.where(logits == top1, idx, N_EXPERTS), axis=-1, keepdims=True)
    rest = jnp.where(idx == i1, -jnp.inf, logits)
    top2 = jnp.max(rest, axis=-1, keepdims=True)
    i2 = jnp.min(jnp.where(rest == top2, idx, N_EXPERTS), axis=-1, keepdims=True)
    e2 = jnp.exp(top2 - top1)
    denom = 1.0 + e2
    w1, w2 = 1.0 / denom, e2 / denom
    o_ref[...] = jnp.where(idx == i1, w1, 0.0) + jnp.where(idx == i2, w2, 0.0)
    esel_ref[...] = jnp.where(idx == 0, i1, i2)
    wsel_ref[...] = jnp.where(idx == 0, w1, w2)
    h_ref[...] = h


def _router(tok, layer, x, mod3, g_ffn, w_router, b_router):
    j = layer // 2
    tm = tok.tm
    small = pl.BlockSpec((tm, N_EXPERTS), lambda m: (m, 0))
    return pl.pallas_call(
        _router_kernel,
        grid=(tok.n_tiles,),
        in_specs=[pl.BlockSpec((tm, D_MODEL), lambda m: (m, 0)),
                  tok.mod_spec(4), tok.mod_spec(3),
                  pl.BlockSpec((1, 1, D_MODEL), lambda m: (layer, 0, 0)),
                  pl.BlockSpec((1, D_MODEL, N_EXPERTS), lambda m: (j, 0, 0)),
                  pl.BlockSpec((1, 1, N_EXPERTS), lambda m: (j, 0, 0))],
        out_specs=[small, small, small, pl.BlockSpec((tm, D_MODEL), lambda m: (m, 0))],
        out_shape=[jax.ShapeDtypeStruct((tok.rows, N_EXPERTS), F32),
                   jax.ShapeDtypeStruct((tok.rows, N_EXPERTS), jnp.int32),
                   jax.ShapeDtypeStruct((tok.rows, N_EXPERTS), F32),
                   jax.ShapeDtypeStruct((tok.rows, D_MODEL), F32)],
        compiler_params=_params("arbitrary"),
        name="router",
    )(x, mod3, mod3, g_ffn, w_router, b_router)


def _route_plan(choice, tm):
    rows = choice.shape[0]
    n_pairs = 2 * rows
    n_tiles = n_pairs // tm + N_EXPERTS
    e_flat = choice.T.reshape(n_pairs)
    onehot = (e_flat[:, None] == jnp.arange(N_EXPERTS, dtype=jnp.int32)[None, :]).astype(jnp.int32)
    csum = jnp.cumsum(onehot, axis=0)
    rank = jnp.sum(csum * onehot, axis=1) - 1
    count = csum[-1]
    tiles_per = (count + tm - 1) // tm
    rows_per = jnp.maximum((((count + jnp.maximum(tiles_per, 1) - 1) // jnp.maximum(tiles_per, 1)) + 7) // 8 * 8, 8)
    tile_end = jnp.cumsum(tiles_per)
    tile_start = tile_end - tiles_per
    pos = (tile_start[e_flat] + rank // rows_per[e_flat]) * tm + rank % rows_per[e_flat]
    token = jnp.arange(n_pairs, dtype=jnp.int32) % rows
    src_row = jnp.zeros((n_tiles * tm,), jnp.int32).at[pos].set(token)
    tile_id = jnp.arange(n_tiles, dtype=jnp.int32)
    n_used = tile_end[-1]
    tile_expert = jnp.minimum(jnp.sum(tile_id[:, None] >= tile_end[None, :], axis=1), N_EXPERTS - 1).astype(jnp.int32)
    tile_rows = jnp.clip(count[tile_expert] - (tile_id - tile_start[tile_expert]) * rows_per[tile_expert],
                         0, rows_per[tile_expert])
    tile_rows = jnp.where(tile_id < n_used, tile_rows, 0).astype(jnp.int32)
    last_expert = tile_expert[jnp.maximum(n_used - 1, 0)]
    tile_expert = jnp.where(tile_rows > 0, tile_expert, last_expert)
    return tile_expert, tile_rows, src_row, pos.astype(jnp.int32)


def _row_copy(src_hbm, row, dst, r, sem):
    return pltpu.make_async_copy(src_hbm.at[pl.ds(row, 1)], dst.at[pl.ds(r, 1)], sem)


def _moe_kernel(te_ref, nr_ref, src_ref, h_hbm, w1_ref, w3_ref, w2_ref, o_ref, rows_scr, h_scr, sem, *, tm, sub):
    del te_ref
    m, f = pl.program_id(0), pl.program_id(1)
    group = 8

    def for_each_row(tile, fn):
        def body(i, c):
            for r8 in range(group):
                r = i * group + r8
                fn(_row_copy(h_hbm, src_ref[tile * tm + r], rows_scr, r, sem))
            return c
        lax.fori_loop(0, (nr_ref[tile] + group - 1) // group, body, 0)

    @pl.when(f == 0)
    def _():
        o_ref[...] = jnp.zeros_like(o_ref)

        @pl.when(m == 0)
        def _():
            rows_scr[...] = jnp.zeros_like(rows_scr)
            for_each_row(0, lambda cp: cp.start())

        for_each_row(m, lambda cp: cp.wait())
        h_scr[...] = rows_scr[...].astype(BF16)

        @pl.when(m + 1 < pl.num_programs(0))
        def _():
            for_each_row(m + 1, lambda cp: cp.start())

    n_sub = (nr_ref[m] + sub - 1) // sub
    for n in range(1, tm // sub + 1):
        @pl.when(n_sub == n)
        def _(n=n):
            h = h_scr[0:n * sub, :]
            a = jnp.dot(h, _mat(w1_ref).astype(BF16), preferred_element_type=F32)
            b = jnp.dot(h, _mat(w3_ref).astype(BF16), preferred_element_type=F32)
            o_ref[0:n * sub, :] += jnp.dot((_silu(a) * b).astype(BF16), _mat(w2_ref).astype(BF16),
                                           preferred_element_type=F32)


def _moe_sparse(layer, h, plan, w1, w3, w2, tm):
    j = layer // 2
    tile_expert, tile_rows, src_row, _ = plan
    n_tiles = tile_expert.shape[0]
    d_ff = w1.shape[3]
    assert d_ff % TF == 0 and tm % SUB_MOE == 0
    n_f = d_ff // TF

    def f_sel(m, f, nr):
        return jnp.where(nr[m] > 0, f, n_f - 1)

    return pl.pallas_call(
        functools.partial(_moe_kernel, tm=tm, sub=SUB_MOE),
        grid_spec=pltpu.PrefetchScalarGridSpec(
            num_scalar_prefetch=3,
            grid=(n_tiles, n_f),
            in_specs=[pl.BlockSpec(memory_space=pl.ANY),
                      pl.BlockSpec((1, 1, D_MODEL, TF), lambda m, f, te, nr, sr: (j, te[m], 0, f_sel(m, f, nr))),
                      pl.BlockSpec((1, 1, D_MODEL, TF), lambda m, f, te, nr, sr: (j, te[m], 0, f_sel(m, f, nr))),
                      pl.BlockSpec((1, 1, TF, D_MODEL), lambda m, f, te, nr, sr: (j, te[m], f_sel(m, f, nr), 0))],
            out_specs=pl.BlockSpec((tm, D_MODEL), lambda m, f, te, nr, sr: (m, 0)),
            scratch_shapes=[pltpu.VMEM((tm, D_MODEL), F32), pltpu.VMEM((tm, D_MODEL), BF16),
                            pltpu.SemaphoreType.DMA(())]),
        out_shape=jax.ShapeDtypeStruct((n_tiles * tm, D_MODEL), F32),
        compiler_params=_params("arbitrary", "arbitrary"),
        name="moe",
    )(tile_expert, tile_rows, src_row, h, w1, w3, w2)


def _combine_kernel(pos_ref, y_hbm, x_ref, gt_ref, w_ref, *rest, tm, rows, with_final):
    gfin_ref = rest[0] if with_final else None
    o_ref, y_scr, sems = rest[-3:]
    i, n = pl.program_id(0), pl.num_programs(0)

    def for_each_row(tile, slot, fn):
        def body(r, c):
            for k in range(2):
                fn(_row_copy(y_hbm, pos_ref[k * rows + tile * tm + r], y_scr.at[slot, k], r, sems.at[slot]))
            return c
        lax.fori_loop(0, tm, body, 0, unroll=8)

    @pl.when(i == 0)
    def _():
        for_each_row(0, 0, lambda cp: cp.start())

    @pl.when(i + 1 < n)
    def _():
        for_each_row(i + 1, (i + 1) % 2, lambda cp: cp.start())

    slot = i % 2
    for_each_row(i, slot, lambda cp: cp.wait())
    w = w_ref[...]
    f = w[:, 0:1] * y_scr[slot, 0] + w[:, 1:2] * y_scr[slot, 1]
    y = x_ref[...] + (1.0 + gt_ref[0]) * f
    if with_final:
        y = y * lax.rsqrt(jnp.mean(y * y, axis=-1, keepdims=True) + EPS) * gfin_ref[...]
    o_ref[...] = y


def _combine(tok, y_sorted, pos, x, mod, wsel, g_final):
    tok_c = _Tokens(tok.n_batch, tok.seq, tok.n_past, TM_COMBINE, tok.per_row_mod)
    tm = tok_c.tm
    mod3 = tok_c.prep_mod(mod)
    in_specs = [pl.BlockSpec(memory_space=pl.ANY),
                pl.BlockSpec((tm, D_MODEL), lambda m, pos: (m, 0)),
                tok_c.mod_spec(5),
                pl.BlockSpec((tm, N_EXPERTS), lambda m, pos: (m, 0))]
    args = [pos, y_sorted, x, mod3, wsel]
    if g_final is not None:
        in_specs.append(pl.BlockSpec((1, D_MODEL), lambda m, pos: (0, 0)))
        args.append(g_final)
    return pl.pallas_call(
        functools.partial(_combine_kernel, tm=tm, rows=tok.rows, with_final=g_final is not None),
        grid_spec=pltpu.PrefetchScalarGridSpec(
            num_scalar_prefetch=1,
            grid=(tok_c.n_tiles,),
            in_specs=in_specs,
            out_specs=pl.BlockSpec((tm, D_MODEL), lambda m, pos: (m, 0)),
            scratch_shapes=[pltpu.VMEM((2, 2, tm, D_MODEL), F32), pltpu.SemaphoreType.DMA((2,))]),
        out_shape=jax.ShapeDtypeStruct((tok.rows, D_MODEL), F32),
        compiler_params=_params("arbitrary"),
        name="combine",
    )(*args)


def _decoder(tok, x, mod_layers, ret_state, pool_state, p):
    rope_tabs = tok.rope_tables()
    is_prompt = ret_state is None
    act_dtype = BF16 if is_prompt else F32
    u_layers = []
    s_all = None
    for i in range(DEPTH):
        mod3 = tok.prep_mod(mod_layers[i])
        proj, u = _inproj(tok, i, x, mod3, p['g_mix'], p['w_in'], rope_tabs, act_dtype)
        u_layers.append(u)
        if is_prompt:
            yret, s_all = _ret_prompt(tok, i, proj, p['g_ret'], s_all)
            mixed = _pool_prompt(tok, i, u, p['w_pool_map'], p['pool_scale'])
        else:
            yret, s_all = _ret_sample(tok, i, proj, p['g_ret'], ret_state, s_all)
            mixed = _pool_sample(tok, i, u, pool_state[i].transpose(1, 0, 2), p['w_pool_map'], p['pool_scale'])
        merged = _merge(tok, i, yret, mixed, proj, p['w_ret_out'], p['w_pool_out'])
        x = _wo(tok, i, merged, x, mod3, p['w_o'])
        g_final = p['g_final'] if i == DEPTH - 1 else None
        if i % 2 == 0:
            x = _ffn(tok, i, x, mod_layers[i], p['g_ffn'], p['w1_dense'], p['w3_dense'], p['w2_dense'],
                     g_final=g_final)
        else:
            comb, esel, wsel, h = _router(tok, i, x, mod3, p['g_ffn'], p['w_router'], p['b_router'])
            if is_prompt:
                plan = _route_plan(esel[:, :2], TM_MOE)
                y_sorted = _moe_sparse(i, h, plan, p['w1_moe'], p['w3_moe'], p['w2_moe'], TM_MOE)
                x = _combine(tok, y_sorted, plan[3], x, mod_layers[i], wsel, g_final)
            else:
                comb = comb.T.reshape(N_EXPERTS, tok.rows, 1)
                x = _ffn(tok, i, x, mod_layers[i], p['g_ffn'], p['w1_moe'], p['w3_moe'], p['w2_moe'],
                         combine=comb, g_final=g_final)
    return x, s_all, u_layers


def kernel(x_prompt, x_sample, c_prompt, c_sample, state_ret, state_pool, w_ada, b_ada, g_mix, w_in, g_ret,
           w_ret_out, w_pool_map, pool_scale, w_pool_out, w_o, g_ffn, w1_dense, w3_dense, w2_dense, w_router,
           b_router, w1_moe, w3_moe, w2_moe, g_final):
    row3 = lambda a: a.reshape(a.shape[0], 1, a.shape[1])
    p = dict(g_mix=row3(g_mix), w_in=w_in, g_ret=row3(g_ret), w_ret_out=w_ret_out, w_pool_map=w_pool_map,
             pool_scale=row3(pool_scale), w_pool_out=w_pool_out, w_o=w_o, g_ffn=row3(g_ffn),
             w1_dense=w1_dense, w3_dense=w3_dense, w2_dense=w2_dense, w_router=w_router, b_router=row3(b_router),
             w1_moe=w1_moe, w3_moe=w3_moe, w2_moe=w2_moe, g_final=g_final.reshape(1, D_MODEL))
    n_bp, seq_p, _ = x_prompt.shape
    n_bs, seq_s, _ = x_sample.shape

    mod = _ada(jnp.concatenate([c_prompt, c_sample], axis=0), w_ada, b_ada)

    tok_p = _Tokens(n_bp, seq_p, 0, TM_PROMPT, per_row_mod=False)
    tok_s = _Tokens(n_bs, seq_s, PAST_LEN, TM_SAMPLE, per_row_mod=True)

    y_p, s_prompt, u_p = _decoder(tok_p, x_prompt.reshape(tok_p.rows, D_MODEL), [mod[i, :n_bp] for i in range(DEPTH)],
                                  None, None, p)
    y_s, s_sample, u_s = _decoder(tok_s, x_sample.reshape(tok_s.rows, D_MODEL), [mod[i, n_bp:] for i in range(DEPTH)],
                                  state_ret, state_pool, p)

    buf_prompt = jnp.stack([u.reshape(n_bp, seq_p, D_POOL)[:, seq_p - POOL_BUF:] for u in u_p])
    buf_sample = jnp.stack([
        jnp.concatenate([state_pool[i], u_s[i].reshape(n_bs, seq_s, D_POOL)], axis=1)[:, -POOL_BUF:]
        for i in range(DEPTH)])
    return (y_p.reshape(x_prompt.shape), y_s.reshape(x_sample.shape), s_prompt, buf_prompt, s_sample, buf_sample)
```

```python
import functools

import numpy as np
import jax
import jax.numpy as jnp
from jax import lax
from jax.experimental import pallas as pl
from jax.experimental.pallas import tpu as pltpu

F32 = jnp.float32
BF16 = jnp.bfloat16

D_MODEL = 2048
DEPTH = 2
PAST_LEN = 16384
RET_HEADS = 8
RET_DK = 128
RET_DV = 256
RET_CHUNK = 128
ROPE_BASE = 10000.0
QK_W = RET_HEADS * RET_DK
V_W = RET_HEADS * RET_DV
POOL_WINDOWS = (2, 4, 8, 16)
D_POOL = 1024
POOL_GW = 256
POOL_BUF = 15
POOL_HALO = 16
N_IN = 2 * QK_W + 2 * V_W + D_POOL + 2 * D_MODEL
N_EXPERTS = 8
EPS = 1e-6
N_MOD = 6

O_Q, O_K, O_V, O_G, O_U = 0, QK_W, 2 * QK_W, 2 * QK_W + V_W, 2 * QK_W + 2 * V_W
O_M1 = O_U + D_POOL
O_M2 = O_M1 + D_MODEL

VMEM_LIMIT_BYTES = 56 * 1024 * 1024
TN = 512
TF = 256
TM_PROMPT = 1024
TM_SAMPLE = 512
TM_FFN = 1024
TL_POOL = 512
TM_MOE = 1024
SUB_MOE = 256
TM_COMBINE = 256


def _params(*sem):
    return pltpu.CompilerParams(dimension_semantics=sem, vmem_limit_bytes=VMEM_LIMIT_BYTES)


class _Tokens:
    def __init__(self, n_batch, seq, n_past, tm, per_row_mod, mod=None):
        self.n_batch, self.seq, self.n_past = n_batch, seq, n_past
        self.rows = n_batch * seq
        self.tm = min(tm, self.rows)
        self.per_row_mod = per_row_mod
        if not per_row_mod:
            self.tm = min(self.tm, seq)
            assert seq % self.tm == 0
        assert self.rows % self.tm == 0
        self.n_tiles = self.rows // self.tm
        if mod is not None and not per_row_mod and mod.ndim == 3 and mod.shape[1] != 1:
            mod = mod.reshape(DEPTH * n_batch, 1, N_MOD * D_MODEL)
        self.mod = mod

    def with_tm(self, tm):
        return _Tokens(self.n_batch, self.seq, self.n_past, tm, self.per_row_mod, self.mod)

    def mod_spec(self, layer, chunk, tn=D_MODEL, col_axis=False):
        per_chunk = D_MODEL // tn
        col = (lambda n: chunk * per_chunk + n) if col_axis else (lambda n: chunk * per_chunk)
        if self.per_row_mod:
            if col_axis:
                return pl.BlockSpec((1, self.tm, tn), lambda m, n, *_: (layer, m, col(n)))
            return pl.BlockSpec((1, self.tm, tn), lambda m, *_: (layer, m, col(0)))
        tiles_per_batch = self.seq // self.tm
        base = layer * self.n_batch
        if col_axis:
            return pl.BlockSpec((1, 1, tn), lambda m, n, *_: (base + m // tiles_per_batch, 0, col(n)))
        return pl.BlockSpec((1, 1, tn), lambda m, *_: (base + m // tiles_per_batch, 0, col(0)))

    def rope_tables(self):
        half = RET_DK // 2
        pos = jnp.arange(self.seq, dtype=F32) + float(self.n_past)
        inv = ROPE_BASE ** (-jnp.arange(half, dtype=F32) / half)
        ang = pos[:, None] * inv[None, :]
        cos, sin = jnp.cos(ang), jnp.sin(ang)
        cos_full = jnp.concatenate([cos, cos], axis=1)
        sin_signed = jnp.concatenate([-sin, sin], axis=1)
        if self.per_row_mod:
            cos_full = jnp.tile(cos_full, (self.n_batch, 1))
            sin_signed = jnp.tile(sin_signed, (self.n_batch, 1))
        return cos_full, sin_signed

    def rope_spec(self):
        n_blocks = (self.rows if self.per_row_mod else self.seq) // self.tm
        return pl.BlockSpec((self.tm, RET_DK), lambda m, *_: (m % n_blocks, 0))


def _retention_tables(chunk):
    log_gamma = jnp.log1p(-jnp.exp2(-5.0 - jnp.arange(RET_HEADS, dtype=F32)))
    idx = jnp.arange(chunk, dtype=F32)
    rel = idx[:, None] - idx[None, :]
    dmask = jnp.where(rel[None] >= 0, jnp.exp(log_gamma[:, None, None] * jnp.maximum(rel, 0.0)[None]), 0.0)
    q_dec = jnp.exp(log_gamma[:, None] * (idx + 1.0)[None, :])
    k_dec = jnp.exp(log_gamma[:, None] * (chunk - 1.0 - idx)[None, :])
    chunk_dec = jnp.exp(log_gamma * chunk)
    q_dec = jnp.broadcast_to(q_dec[:, :, None], (RET_HEADS, chunk, RET_DV))
    k_dec = jnp.broadcast_to(k_dec[:, :, None], (RET_HEADS, chunk, RET_DK))
    chunk_dec = jnp.broadcast_to(chunk_dec[:, None, None], (RET_HEADS, 1, RET_DV))
    return dmask, q_dec, k_dec, chunk_dec


def _rmsnorm_mod(x, g, scale, shift):
    y = x * lax.rsqrt(jnp.mean(x * x, axis=-1, keepdims=True) + EPS) * g
    return y * (1.0 + scale) + shift


def _silu(x):
    return x * jax.nn.sigmoid(x)


def _ada_kernel(cp_ref, cs_ref, w_ref, b_ref, op_ref, os_ref):
    w = w_ref[0].astype(BF16)
    for c_ref, o_ref in ((cp_ref, op_ref), (cs_ref, os_ref)):
        o_ref[0] = jnp.dot(_silu(c_ref[...]).astype(BF16), w, preferred_element_type=F32) + b_ref[0]


def _ada(c_p, c_s, w_ada, b_ada):
    tn = 1024
    width = N_MOD * D_MODEL
    row_spec = lambda c: pl.BlockSpec(c.shape, lambda l, n: (0, 0))
    out_spec = lambda c: pl.BlockSpec((1, c.shape[0], tn), lambda l, n: (l, 0, n))
    return pl.pallas_call(
        _ada_kernel,
        grid=(DEPTH, width // tn),
        in_specs=[row_spec(c_p), row_spec(c_s),
                  pl.BlockSpec((1, D_MODEL, tn), lambda l, n: (l, 0, n)),
                  pl.BlockSpec((1, 1, tn), lambda l, n: (l, 0, n))],
        out_specs=[out_spec(c_p), out_spec(c_s)],
        out_shape=[jax.ShapeDtypeStruct((DEPTH, c.shape[0], width), F32) for c in (c_p, c_s)],
        compiler_params=_params("arbitrary", "arbitrary"),
        name="ada",
    )(c_p, c_s, w_ada, b_ada.reshape(DEPTH, 1, width))


def _inproj_kernel(x_ref, sc_ref, sh_ref, g_ref, w_ref, cos_ref, sin_ref, o_ref, u_ref, h_scr):
    n = pl.program_id(1)

    @pl.when(n == 0)
    def _():
        h_scr[...] = _rmsnorm_mod(x_ref[...], g_ref[0], sc_ref[0], sh_ref[0]).astype(BF16)

    def project():
        return jnp.dot(h_scr[...], w_ref[0].astype(BF16), preferred_element_type=F32)

    def rope(scale):
        acc = project()
        for j in range(TN // RET_DK):
            xh = acc[:, j * RET_DK:(j + 1) * RET_DK]
            r = xh * cos_ref[...] + pltpu.roll(xh, RET_DK // 2, 1) * sin_ref[...]
            if scale != 1.0:
                r = r * scale
            o_ref[:, j * RET_DK:(j + 1) * RET_DK] = r.astype(o_ref.dtype)

    @pl.when(n < O_K // TN)
    def _():
        rope(1.0)

    @pl.when(jnp.logical_and(n >= O_K // TN, n < O_V // TN))
    def _():
        rope(RET_DK ** -0.5)

    @pl.when(jnp.logical_and(n >= O_V // TN, n < O_G // TN))
    def _():
        o_ref[...] = project().astype(o_ref.dtype)

    @pl.when(jnp.logical_and(n >= O_G // TN, n < O_U // TN))
    def _():
        o_ref[...] = _silu(project()).astype(o_ref.dtype)

    @pl.when(jnp.logical_and(n >= O_U // TN, n < O_M1 // TN))
    def _():
        acc = project()
        o_ref[...] = acc.astype(o_ref.dtype)
        u_ref[...] = acc

    @pl.when(n >= O_M1 // TN)
    def _():
        o_ref[...] = jax.nn.sigmoid(project()).astype(o_ref.dtype)


def _inproj(tok, layer, x, g_mix, w_in, rope_tabs, out_dtype):
    tm = tok.tm
    n_u = D_POOL // TN
    mod3 = tok.mod
    proj, u = pl.pallas_call(
        _inproj_kernel,
        grid=(tok.n_tiles, N_IN // TN),
        in_specs=[pl.BlockSpec((tm, D_MODEL), lambda m, n: (m, 0)),
                  tok.mod_spec(layer, 1), tok.mod_spec(layer, 0),
                  pl.BlockSpec((1, 1, D_MODEL), lambda m, n: (layer, 0, 0)),
                  pl.BlockSpec((1, D_MODEL, TN), lambda m, n: (layer, 0, n)),
                  tok.rope_spec(), tok.rope_spec()],
        out_specs=[pl.BlockSpec((tm, TN), lambda m, n: (m, n)),
                   pl.BlockSpec((tm, TN), lambda m, n: (m, jnp.clip(n - O_U // TN, 0, n_u - 1)))],
        out_shape=[jax.ShapeDtypeStruct((tok.rows, N_IN), out_dtype),
                   jax.ShapeDtypeStruct((tok.rows, D_POOL), F32)],
        scratch_shapes=[pltpu.VMEM((tm, D_MODEL), BF16)],
        compiler_params=_params("arbitrary", "arbitrary"),
        name="inproj",
    )(x, mod3, mod3, g_mix, w_in, *rope_tabs)
    return proj, u


def _head_norm_gate(o, gret, gate):
    mu = jnp.mean(o, axis=-1, keepdims=True)
    d = o - mu
    var = jnp.mean(d * d, axis=-1, keepdims=True)
    return gate.astype(F32) * (d * lax.rsqrt(var + EPS) * gret)


def _ret_prompt_kernel(q_ref, k_ref, v_ref, g_ref, gret_ref, dmask_ref, qdec_ref, kdec_ref, cdec_ref,
                       *rest):
    y_ref, s_ref = rest[-2:]

    @pl.when(pl.program_id(1) == 0)
    def _():
        s_ref[...] = jnp.zeros_like(s_ref)

    for h in range(RET_HEADS):
        ks, vs = slice(h * RET_DK, (h + 1) * RET_DK), slice(h * RET_DV, (h + 1) * RET_DV)
        q, k, v = q_ref[:, ks], k_ref[:, ks], v_ref[:, vs]
        state = s_ref[0, 0, h]
        scores = lax.dot_general(q, k, (((1,), (1,)), ((), ())), preferred_element_type=F32) * dmask_ref[h]
        inner = jnp.dot(scores.astype(BF16), v, preferred_element_type=F32)
        cross = jnp.dot(q, state.astype(BF16), preferred_element_type=F32) * qdec_ref[h]
        k_decayed = (k.astype(F32) * kdec_ref[h]).astype(BF16)
        s_ref[0, 0, h] = cdec_ref[h] * state + lax.dot_general(
            k_decayed, v, (((0,), (0,)), ((), ())), preferred_element_type=F32)
        y_ref[:, vs] = _head_norm_gate(inner + cross, gret_ref[0, :, vs], g_ref[:, vs]).astype(y_ref.dtype)


def _table_specs(tables):
    return [pl.BlockSpec(t.shape, lambda *_: (0, 0, 0)) for t in tables]


def _ret_prompt(tok, layer, proj, g_ret, s_all):
    n_b, seq = tok.n_batch, tok.seq
    c = RET_CHUNK
    assert seq % c == 0
    nc = seq // c
    tables = _retention_tables(c)
    in_specs = [pl.BlockSpec((c, QK_W), lambda b, i: (b * nc + i, O_Q // QK_W)),
                pl.BlockSpec((c, QK_W), lambda b, i: (b * nc + i, O_K // QK_W)),
                pl.BlockSpec((c, V_W), lambda b, i: (b * nc + i, O_V // V_W)),
                pl.BlockSpec((c, V_W), lambda b, i: (b * nc + i, O_G // V_W)),
                pl.BlockSpec((1, 1, V_W), lambda b, i: (layer, 0, 0))] + _table_specs(tables)
    args = [proj, proj, proj, proj, g_ret, *tables]
    aliases = {}
    if s_all is not None:
        aliases = {len(args): 1}
        in_specs.append(pl.BlockSpec(memory_space=pl.ANY))
        args.append(s_all)
    y, s_all = pl.pallas_call(
        _ret_prompt_kernel,
        grid=(n_b, nc),
        in_specs=in_specs,
        out_specs=[pl.BlockSpec((c, V_W), lambda b, i: (b * nc + i, 0)),
                   pl.BlockSpec((1, 1, RET_HEADS, RET_DK, RET_DV), lambda b, i: (layer, b, 0, 0, 0))],
        out_shape=[jax.ShapeDtypeStruct((tok.rows, V_W), BF16),
                   jax.ShapeDtypeStruct((DEPTH, n_b, RET_HEADS, RET_DK, RET_DV), F32)],
        input_output_aliases=aliases,
        compiler_params=_params("arbitrary", "arbitrary"),
        name="ret_prompt",
    )(*args)
    return y, s_all


def _ret_sample_kernel(q_ref, k_ref, v_ref, g_ref, kt_ref, gret_ref, s0_ref, dm_ref, qdec_ref, kdect_ref, cdec_ref,
                       *rest, bb, seq):
    y_ref, s_ref = rest[-2:]
    hs = RET_HEADS * seq
    reps = lambda a: jnp.concatenate([a] * RET_HEADS, axis=0)
    mask_q = (lax.broadcasted_iota(jnp.int32, (hs, QK_W), 0) // seq
              == lax.broadcasted_iota(jnp.int32, (hs, QK_W), 1) // RET_DK)
    mask_v = (lax.broadcasted_iota(jnp.int32, (hs, V_W), 0) // seq
              == lax.broadcasted_iota(jnp.int32, (hs, V_W), 1) // RET_DV)
    nt = (((1,), (1,)), ((), ()))
    for b in range(bb):
        rows_b = slice(b * seq, (b + 1) * seq)
        q, k, v = q_ref[rows_b, :], k_ref[rows_b, :], v_ref[rows_b, :]
        q_blk = jnp.where(mask_q, reps(q), 0.0).astype(BF16)
        scores = lax.dot_general(q_blk, reps(k).astype(BF16), nt, preferred_element_type=F32) * dm_ref[...]
        v_stack = jnp.concatenate([v[:, h * RET_DV:(h + 1) * RET_DV] for h in range(RET_HEADS)], axis=0)
        inner = jnp.dot(scores.astype(BF16), v_stack.astype(BF16), preferred_element_type=F32)
        state = s0_ref[0, b]
        cross = jnp.dot(q_blk, state.reshape(QK_W, RET_DV).astype(BF16), preferred_element_type=F32)
        o = inner + cross * qdec_ref[...]
        mu = jnp.mean(o, axis=-1, keepdims=True)
        d = o - mu
        normed = d * lax.rsqrt(jnp.mean(d * d, axis=-1, keepdims=True) + EPS)
        k_dec_t = (kt_ref[b] * kdect_ref[...]).astype(BF16)
        v_blk = jnp.where(mask_v, reps(v), 0.0).astype(BF16)
        upd = jnp.dot(k_dec_t, v_blk, preferred_element_type=F32)
        for h in range(RET_HEADS):
            vs = slice(h * RET_DV, (h + 1) * RET_DV)
            y_ref[rows_b, vs] = g_ref[rows_b, vs] * (normed[h * seq:(h + 1) * seq] * gret_ref[0, :, vs])
            s_ref[0, b, h] = cdec_ref[h] * state[h] + upd[:, vs]


def _ret_sample(tok, layer, proj, g_ret, state_ret, s_all):
    n_b, seq = tok.n_batch, tok.seq
    bb = 4
    assert n_b % bb == 0
    hs = RET_HEADS * seq
    dmask, q_dec, k_dec, chunk_dec = _retention_tables(seq)
    dm_blk = (dmask[:, :, None, :] * jnp.eye(RET_HEADS, dtype=F32)[:, None, :, None]).reshape(hs, hs)
    tables = [dm_blk, q_dec.reshape(hs, RET_DV), k_dec[:, :, 0].reshape(1, hs), chunk_dec]
    k_t = proj[:, O_K:O_K + QK_W].reshape(n_b, seq, RET_HEADS, RET_DK).transpose(0, 3, 2, 1).reshape(n_b, RET_DK, hs)
    state_spec = pl.BlockSpec((1, bb, RET_HEADS, RET_DK, RET_DV), lambda i: (layer, i, 0, 0, 0))
    in_specs = [pl.BlockSpec((bb * seq, QK_W), lambda i: (i, O_Q // QK_W)),
                pl.BlockSpec((bb * seq, QK_W), lambda i: (i, O_K // QK_W)),
                pl.BlockSpec((bb * seq, V_W), lambda i: (i, O_V // V_W)),
                pl.BlockSpec((bb * seq, V_W), lambda i: (i, O_G // V_W)),
                pl.BlockSpec((bb, RET_DK, hs), lambda i: (i, 0, 0)),
                pl.BlockSpec((1, 1, V_W), lambda i: (layer, 0, 0)),
                state_spec] + [pl.BlockSpec(t.shape, lambda i, nd=t.ndim: (0,) * nd) for t in tables]
    args = [proj, proj, proj, proj, k_t, g_ret, state_ret, *tables]
    aliases = {}
    if s_all is not None:
        aliases = {len(args): 1}
        in_specs.append(pl.BlockSpec(memory_space=pl.ANY))
        args.append(s_all)
    y, s_all = pl.pallas_call(
        functools.partial(_ret_sample_kernel, bb=bb, seq=seq),
        grid=(n_b // bb,),
        in_specs=in_specs,
        out_specs=[pl.BlockSpec((bb * seq, V_W), lambda i: (i, 0)), state_spec],
        out_shape=[jax.ShapeDtypeStruct((tok.rows, V_W), F32),
                   jax.ShapeDtypeStruct(state_ret.shape, F32)],
        input_output_aliases=aliases,
        compiler_params=_params("arbitrary"),
        name="ret_sample",
    )(*args)
    return y, s_all


def _pool_map(pooled, wmap_ref, scale_ref, gi):
    cols = slice(gi * POOL_GW, (gi + 1) * POOL_GW)
    mixed = jnp.dot(pooled.astype(BF16), wmap_ref[0, gi].astype(BF16), preferred_element_type=F32)
    return mixed * scale_ref[0, :, cols]


def _pool_prompt_kernel(u_ref, halo_ref, wmap_ref, scale_ref, o_ref, uf_scr, *, tl):
    first = pl.program_id(1) == 0
    uf_scr[0:POOL_HALO, :] = jnp.where(first, 0.0, halo_ref[...])
    uf_scr[POOL_HALO:, :] = u_ref[...]
    row = lax.broadcasted_iota(jnp.int32, (tl, POOL_GW), 0) + pl.program_id(1) * tl
    for gi, w in enumerate(POOL_WINDOWS):
        cols = slice(gi * POOL_GW, (gi + 1) * POOL_GW)
        u = uf_scr[POOL_HALO:, cols]
        acc = u
        for j in range(1, w):
            acc = acc + uf_scr[POOL_HALO - j:POOL_HALO - j + tl, cols]
        cnt = jnp.minimum(w, row + 1).astype(F32)
        o_ref[:, cols] = _pool_map(acc / cnt - u, wmap_ref, scale_ref, gi).astype(o_ref.dtype)


def _pool_prompt(tok, layer, u, w_pool_map, pool_scale):
    n_b, seq = tok.n_batch, tok.seq
    tl = min(TL_POOL, seq)
    assert seq % tl == 0 and tl % POOL_HALO == 0
    nl = seq // tl
    per = tl // POOL_HALO
    return pl.pallas_call(
        functools.partial(_pool_prompt_kernel, tl=tl),
        grid=(n_b, nl),
        in_specs=[pl.BlockSpec((tl, D_POOL), lambda b, l: (b * nl + l, 0)),
                  pl.BlockSpec((POOL_HALO, D_POOL), lambda b, l: (jnp.maximum((b * nl + l) * per - 1, 0), 0)),
                  pl.BlockSpec((1,) + w_pool_map.shape[1:], lambda b, l: (layer, 0, 0, 0)),
                  pl.BlockSpec((1, 1, D_POOL), lambda b, l: (layer, 0, 0))],
        out_specs=pl.BlockSpec((tl, D_POOL), lambda b, l: (b * nl + l, 0)),
        out_shape=jax.ShapeDtypeStruct((tok.rows, D_POOL), BF16),
        scratch_shapes=[pltpu.VMEM((POOL_HALO + tl, D_POOL), F32)],
        compiler_params=_params("arbitrary", "arbitrary"),
        name="pool_prompt",
    )(u, u, w_pool_map, pool_scale)


def _pool_sample_kernel(buf_ref, u_ref, wmap_ref, scale_ref, o_ref, *, seq, n_b):
    def row(j, cols):
        return buf_ref[j, :, cols] if j < POOL_BUF else u_ref[j - POOL_BUF, :, cols]

    for gi, w in enumerate(POOL_WINDOWS):
        cols = slice(gi * POOL_GW, (gi + 1) * POOL_GW)
        for t in range(seq):
            acc = row(POOL_BUF + t, cols)
            for j in range(1, w):
                acc = acc + row(POOL_BUF + t - j, cols)
            pooled = acc / float(w) - row(POOL_BUF + t, cols)
            o_ref[t, :, cols] = _pool_map(pooled, wmap_ref, scale_ref, gi)


def _pool_sample(tok, layer, u, buf_tm, w_pool_map, pool_scale):
    n_b, seq = tok.n_batch, tok.seq
    assert tok.n_past >= max(POOL_WINDOWS)
    u_tm = u.reshape(n_b, seq, D_POOL).transpose(1, 0, 2)
    mixed_tm = pl.pallas_call(
        functools.partial(_pool_sample_kernel, seq=seq, n_b=n_b),
        grid=(1,),
        in_specs=[pl.BlockSpec(buf_tm.shape, lambda i: (0, 0, 0)),
                  pl.BlockSpec(u_tm.shape, lambda i: (0, 0, 0)),
                  pl.BlockSpec((1,) + w_pool_map.shape[1:], lambda i: (layer, 0, 0, 0)),
                  pl.BlockSpec((1, 1, D_POOL), lambda i: (layer, 0, 0))],
        out_specs=pl.BlockSpec(u_tm.shape, lambda i: (0, 0, 0)),
        out_shape=jax.ShapeDtypeStruct(u_tm.shape, F32),
        compiler_params=_params("arbitrary"),
        name="pool_sample",
    )(buf_tm, u_tm, w_pool_map, pool_scale)
    return mixed_tm.transpose(1, 0, 2).reshape(tok.rows, D_POOL)


def _merge_kernel(yret_ref, mixed_ref, wr_ref, wp_ref, g1_ref, g2_ref, o_ref):
    ret = jnp.dot(yret_ref[...].astype(BF16), wr_ref[0].astype(BF16), preferred_element_type=F32)
    pool = jnp.dot(mixed_ref[...].astype(BF16), wp_ref[0].astype(BF16), preferred_element_type=F32)
    o_ref[...] = (g1_ref[...].astype(F32) * ret + g2_ref[...].astype(F32) * pool).astype(o_ref.dtype)


def _merge(tok, layer, yret, mixed, proj, w_ret_out, w_pool_out):
    tm = tok.tm
    return pl.pallas_call(
        _merge_kernel,
        grid=(tok.n_tiles, D_MODEL // TN),
        in_specs=[pl.BlockSpec((tm, V_W), lambda m, n: (m, 0)),
                  pl.BlockSpec((tm, D_POOL), lambda m, n: (m, 0)),
                  pl.BlockSpec((1, V_W, TN), lambda m, n: (layer, 0, n)),
                  pl.BlockSpec((1, D_POOL, TN), lambda m, n: (layer, 0, n)),
                  pl.BlockSpec((tm, TN), lambda m, n: (m, O_M1 // TN + n)),
                  pl.BlockSpec((tm, TN), lambda m, n: (m, O_M2 // TN + n))],
        out_specs=pl.BlockSpec((tm, TN), lambda m, n: (m, n)),
        out_shape=jax.ShapeDtypeStruct((tok.rows, D_MODEL), BF16),
        compiler_params=_params("arbitrary", "arbitrary"),
        name="merge",
    )(yret, mixed, w_ret_out, w_pool_out, proj, proj)


def _wo_kernel(merged_ref, w_ref, x_ref, gt_ref, o_ref):
    mix = jnp.dot(merged_ref[...], w_ref[0].astype(BF16), preferred_element_type=F32)
    o_ref[...] = x_ref[...] + (1.0 + gt_ref[0]) * mix


def _wo(tok, layer, merged, x, w_o):
    tm = tok.tm
    mod3 = tok.mod
    return pl.pallas_call(
        _wo_kernel,
        grid=(tok.n_tiles, D_MODEL // TN),
        in_specs=[pl.BlockSpec((tm, D_MODEL), lambda m, n: (m, 0)),
                  pl.BlockSpec((1, D_MODEL, TN), lambda m, n: (layer, 0, n)),
                  pl.BlockSpec((tm, TN), lambda m, n: (m, n)),
                  tok.mod_spec(layer, 2, tn=TN, col_axis=True)],
        out_specs=pl.BlockSpec((tm, TN), lambda m, n: (m, n)),
        out_shape=jax.ShapeDtypeStruct((tok.rows, D_MODEL), F32),
        compiler_params=_params("arbitrary", "arbitrary"),
        name="wo",
    )(merged, w_o, x, mod3)


def _mat(w_ref):
    return w_ref[(0,) * (len(w_ref.shape) - 2)]


def _final_norm(y, gfin_ref):
    return y * lax.rsqrt(jnp.mean(y * y, axis=-1, keepdims=True) + EPS) * gfin_ref[...]


def _ffn_kernel(x_ref, sc_ref, sh_ref, gt_ref, g_ref, w1_ref, w3_ref, w2_ref, *rest, n_f, with_final):
    gfin_ref = rest[0] if with_final else None
    o_ref, h_scr = rest[-2:]
    f = pl.program_id(1)

    @pl.when(f == 0)
    def _():
        h_scr[...] = _rmsnorm_mod(x_ref[...], g_ref[0], sc_ref[0], sh_ref[0]).astype(BF16)
        o_ref[...] = jnp.zeros_like(o_ref)

    h = h_scr[...]
    a = jnp.dot(h, _mat(w1_ref).astype(BF16), preferred_element_type=F32)
    b = jnp.dot(h, _mat(w3_ref).astype(BF16), preferred_element_type=F32)
    o_ref[...] += jnp.dot((_silu(a) * b).astype(BF16), _mat(w2_ref).astype(BF16), preferred_element_type=F32)

    @pl.when(f == n_f - 1)
    def _():
        y = x_ref[...] + (1.0 + gt_ref[0]) * o_ref[...]
        o_ref[...] = _final_norm(y, gfin_ref) if with_final else y


def _ffn(tok, layer, x, g_ffn, w1, w3, w2, g_final=None):
    j = layer // 2
    d_ff = w1.shape[-1]
    tok_f = tok.with_tm(TM_FFN)
    tm, n_tiles = tok_f.tm, tok_f.n_tiles
    mod3 = tok_f.mod
    assert d_ff % TF == 0
    n_f = d_ff // TF
    once = pl.Buffered(1)
    in_specs = [pl.BlockSpec((tm, D_MODEL), lambda m, f: (m, 0), pipeline_mode=once),
                tok_f.mod_spec(layer, 4), tok_f.mod_spec(layer, 3), tok_f.mod_spec(layer, 5),
                pl.BlockSpec((1, 1, D_MODEL), lambda m, f: (layer, 0, 0)),
                pl.BlockSpec((1, D_MODEL, TF), lambda m, f: (j, 0, f)),
                pl.BlockSpec((1, D_MODEL, TF), lambda m, f: (j, 0, f)),
                pl.BlockSpec((1, TF, D_MODEL), lambda m, f: (j, f, 0))]
    args = [x, mod3, mod3, mod3, g_ffn, w1, w3, w2]
    if g_final is not None:
        in_specs.append(pl.BlockSpec((1, D_MODEL), lambda m, f: (0, 0)))
        args.append(g_final)
    return pl.pallas_call(
        functools.partial(_ffn_kernel, n_f=n_f, with_final=g_final is not None),
        grid=(n_tiles, n_f),
        in_specs=in_specs,
        out_specs=pl.BlockSpec((tm, D_MODEL), lambda m, f: (m, 0), pipeline_mode=once),
        out_shape=jax.ShapeDtypeStruct((tok.rows, D_MODEL), F32),
        scratch_shapes=[pltpu.VMEM((tm, D_MODEL), BF16)],
        compiler_params=_params("arbitrary", "arbitrary"),
        name="ffn",
    )(*args)


def _router_kernel(x_ref, sc_ref, sh_ref, g_ref, w_ref, b_ref, *rest):
    esel_ref, wsel_ref, h_ref = rest[-3:]
    h = _rmsnorm_mod(x_ref[...], g_ref[0], sc_ref[0], sh_ref[0])
    logits = jnp.dot(h.astype(BF16), w_ref[0].astype(BF16), preferred_element_type=F32) + b_ref[0]
    idx = lax.broadcasted_iota(jnp.int32, logits.shape, 1)
    top1 = jnp.max(logits, axis=-1, keepdims=True)
    i1 = jnp.min(jnp.where(logits == top1, idx, N_EXPERTS), axis=-1, keepdims=True)
    rest = jnp.where(idx == i1, -jnp.inf, logits)
    top2 = jnp.max(rest, axis=-1, keepdims=True)
    i2 = jnp.min(jnp.where(rest == top2, idx, N_EXPERTS), axis=-1, keepdims=True)
    e2 = jnp.exp(top2 - top1)
    denom = 1.0 + e2
    w1, w2 = 1.0 / denom, e2 / denom
    esel_ref[...] = jnp.where(idx == 0, i1, i2)
    wsel_ref[...] = jnp.where(idx == 0, w1, w2)
    h_ref[...] = h


def _router(tok, layer, x, g_ffn, w_router, b_router, h_all=None, row0=0, rows_all=None):
    j = layer // 2
    tm = tok.tm
    rows_all = tok.rows if rows_all is None else rows_all
    assert row0 % tm == 0
    small = pl.BlockSpec((tm, N_EXPERTS), lambda m: (m, 0))
    mod3 = tok.mod
    in_specs = [pl.BlockSpec((tm, D_MODEL), lambda m: (m, 0)),
                tok.mod_spec(layer, 4), tok.mod_spec(layer, 3),
                pl.BlockSpec((1, 1, D_MODEL), lambda m: (layer, 0, 0)),
                pl.BlockSpec((1, D_MODEL, N_EXPERTS), lambda m: (j, 0, 0)),
                pl.BlockSpec((1, 1, N_EXPERTS), lambda m: (j, 0, 0))]
    args = [x, mod3, mod3, g_ffn, w_router, b_router]
    aliases = {}
    if h_all is not None:
        aliases = {len(args): 2}
        in_specs.append(pl.BlockSpec(memory_space=pl.ANY))
        args.append(h_all)
    return pl.pallas_call(
        _router_kernel,
        grid=(tok.n_tiles,),
        in_specs=in_specs,
        out_specs=[small, small, pl.BlockSpec((tm, D_MODEL), lambda m: (row0 // tm + m, 0))],
        out_shape=[jax.ShapeDtypeStruct((tok.rows, N_EXPERTS), jnp.int32),
                   jax.ShapeDtypeStruct((tok.rows, N_EXPERTS), F32),
                   jax.ShapeDtypeStruct((rows_all, D_MODEL), F32)],
        input_output_aliases=aliases,
        compiler_params=_params("arbitrary"),
        name="router",
    )(*args)


def _route_plan(choice, tm):
    rows = choice.shape[0]
    n_pairs = 2 * rows
    n_tiles = n_pairs // tm + N_EXPERTS
    e_flat = choice.T.reshape(n_pairs)
    onehot = (e_flat[:, None] == jnp.arange(N_EXPERTS, dtype=jnp.int32)[None, :]).astype(jnp.int32)
    csum = jnp.cumsum(onehot, axis=0)
    rank = jnp.sum(csum * onehot, axis=1) - 1
    count = csum[-1]
    tiles_per = (count + tm - 1) // tm
    rows_per = jnp.maximum((((count + jnp.maximum(tiles_per, 1) - 1) // jnp.maximum(tiles_per, 1)) + 7) // 8 * 8, 8)
    tile_end = jnp.cumsum(tiles_per)
    tile_start = tile_end - tiles_per
    pos = (tile_start[e_flat] + rank // rows_per[e_flat]) * tm + rank % rows_per[e_flat]
    token = jnp.arange(n_pairs, dtype=jnp.int32) % rows
    src_row = jnp.zeros((n_tiles * tm,), jnp.int32).at[pos].set(token, unique_indices=True, mode='promise_in_bounds')
    tile_id = jnp.arange(n_tiles, dtype=jnp.int32)
    n_used = tile_end[-1]
    tile_expert = jnp.minimum(jnp.sum(tile_id[:, None] >= tile_end[None, :], axis=1), N_EXPERTS - 1).astype(jnp.int32)
    tile_rows = jnp.clip(count[tile_expert] - (tile_id - tile_start[tile_expert]) * rows_per[tile_expert],
                         0, rows_per[tile_expert])
    tile_rows = jnp.where(tile_id < n_used, tile_rows, 0).astype(jnp.int32)
    last_expert = tile_expert[jnp.maximum(n_used - 1, 0)]
    tile_expert = jnp.where(tile_rows > 0, tile_expert, last_expert)
    return tile_expert, tile_rows, src_row, pos.astype(jnp.int32)


def _row_copy(src_hbm, row, dst, r, sem):
    return pltpu.make_async_copy(src_hbm.at[pl.ds(row, 1)], dst.at[pl.ds(r, 1)], sem)


def _moe_kernel(te_ref, nr_ref, src_ref, h_hbm, w1_ref, w3_ref, w2_ref, o_ref, rows_scr, h_scr, sem, *, tm, sub):
    del te_ref
    m, f = pl.program_id(0), pl.program_id(1)
    group = 8

    def for_each_row(tile, fn):
        def body(i, c):
            for r8 in range(group):
                r = i * group + r8
                fn(_row_copy(h_hbm, src_ref[tile * tm + r], rows_scr, r, sem))
            return c
        lax.fori_loop(0, (nr_ref[tile] + group - 1) // group, body, 0)

    @pl.when(f == 0)
    def _():
        o_ref[...] = jnp.zeros_like(o_ref)

        @pl.when(m == 0)
        def _():
            rows_scr[...] = jnp.zeros_like(rows_scr)
            for_each_row(0, lambda cp: cp.start())

        for_each_row(m, lambda cp: cp.wait())
        h_scr[...] = rows_scr[...].astype(BF16)

        @pl.when(m + 1 < pl.num_programs(0))
        def _():
            for_each_row(m + 1, lambda cp: cp.start())

    n_sub = (nr_ref[m] + sub - 1) // sub
    for n in range(1, tm // sub + 1):
        @pl.when(n_sub == n)
        def _(n=n):
            h = h_scr[0:n * sub, :]
            a = jnp.dot(h, _mat(w1_ref).astype(BF16), preferred_element_type=F32)
            b = jnp.dot(h, _mat(w3_ref).astype(BF16), preferred_element_type=F32)
            o_ref[0:n * sub, :] += jnp.dot((_silu(a) * b).astype(BF16), _mat(w2_ref).astype(BF16),
                                           preferred_element_type=F32)


def _moe_sparse(layer, h, plan, w1, w3, w2, tm):
    j = layer // 2
    tile_expert, tile_rows, src_row, _ = plan
    n_tiles = tile_expert.shape[0]
    d_ff = w1.shape[3]
    assert d_ff % TF == 0 and tm % SUB_MOE == 0
    n_f = d_ff // TF

    def f_sel(m, f, nr):
        return jnp.where(nr[m] > 0, f, n_f - 1)

    return pl.pallas_call(
        functools.partial(_moe_kernel, tm=tm, sub=SUB_MOE),
        grid_spec=pltpu.PrefetchScalarGridSpec(
            num_scalar_prefetch=3,
            grid=(n_tiles, n_f),
            in_specs=[pl.BlockSpec(memory_space=pl.ANY),
                      pl.BlockSpec((1, 1, D_MODEL, TF), lambda m, f, te, nr, sr: (j, te[m], 0, f_sel(m, f, nr))),
                      pl.BlockSpec((1, 1, D_MODEL, TF), lambda m, f, te, nr, sr: (j, te[m], 0, f_sel(m, f, nr))),
                      pl.BlockSpec((1, 1, TF, D_MODEL), lambda m, f, te, nr, sr: (j, te[m], f_sel(m, f, nr), 0))],
            out_specs=pl.BlockSpec((tm, D_MODEL), lambda m, f, te, nr, sr: (m, 0)),
            scratch_shapes=[pltpu.VMEM((tm, D_MODEL), F32), pltpu.VMEM((tm, D_MODEL), BF16),
                            pltpu.SemaphoreType.DMA(())]),
        out_shape=jax.ShapeDtypeStruct((n_tiles * tm, D_MODEL), F32),
        compiler_params=_params("arbitrary", "arbitrary"),
        name="moe",
    )(tile_expert, tile_rows, src_row, h, w1, w3, w2)


def _combine_kernel(pos_ref, y_hbm, x_ref, gt_ref, w_ref, *rest, tm, row0, rows_all, with_final):
    gfin_ref = rest[0] if with_final else None
    o_ref, y_scr, sems = rest[-3:]
    i, n = pl.program_id(0), pl.num_programs(0)

    def for_each_row(tile, slot, fn):
        def body(r, c):
            for k in range(2):
                slot_of_pair = pos_ref[k * rows_all + row0 + tile * tm + r]
                fn(_row_copy(y_hbm, slot_of_pair, y_scr.at[slot, k], r, sems.at[slot]))
            return c
        lax.fori_loop(0, tm, body, 0, unroll=8)

    @pl.when(i == 0)
    def _():
        for_each_row(0, 0, lambda cp: cp.start())

    @pl.when(i + 1 < n)
    def _():
        for_each_row(i + 1, (i + 1) % 2, lambda cp: cp.start())

    slot = i % 2
    for_each_row(i, slot, lambda cp: cp.wait())
    w = w_ref[...]
    f = w[:, 0:1] * y_scr[slot, 0] + w[:, 1:2] * y_scr[slot, 1]
    y = x_ref[...] + (1.0 + gt_ref[0]) * f
    o_ref[...] = _final_norm(y, gfin_ref) if with_final else y


def _combine(tok, layer, y_sorted, pos, row0, x, wsel, g_final):
    tok_c = tok.with_tm(TM_COMBINE)
    tm = tok_c.tm
    mod3 = tok_c.mod
    in_specs = [pl.BlockSpec(memory_space=pl.ANY),
                pl.BlockSpec((tm, D_MODEL), lambda m, pos: (m, 0)),
                tok_c.mod_spec(layer, 5),
                pl.BlockSpec((tm, N_EXPERTS), lambda m, pos: (m, 0))]
    args = [pos, y_sorted, x, mod3, wsel]
    if g_final is not None:
        in_specs.append(pl.BlockSpec((1, D_MODEL), lambda m, pos: (0, 0)))
        args.append(g_final)
    return pl.pallas_call(
        functools.partial(_combine_kernel, tm=tm, row0=row0, rows_all=pos.shape[0] // 2,
                          with_final=g_final is not None),
        grid_spec=pltpu.PrefetchScalarGridSpec(
            num_scalar_prefetch=1,
            grid=(tok_c.n_tiles,),
            in_specs=in_specs,
            out_specs=pl.BlockSpec((tm, D_MODEL), lambda m, pos: (m, 0)),
            scratch_shapes=[pltpu.VMEM((2, 2, tm, D_MODEL), F32), pltpu.SemaphoreType.DMA((2,))]),
        out_shape=jax.ShapeDtypeStruct((tok.rows, D_MODEL), F32),
        compiler_params=_params("arbitrary"),
        name="combine",
    )(*args)


class _Stream:
    def __init__(self, tok, x, ret_state, pool_state):
        self.tok, self.x, self.ret_state, self.pool_state = tok, x, ret_state, pool_state
        self.rope_tabs = tok.rope_tables()
        self.new_state = None
        self.u_layers = []


def _mixer_layer(s, i, p):
    tok = s.tok
    is_prompt = s.ret_state is None
    proj, u = _inproj(tok, i, s.x, p['g_mix'], p['w_in'], s.rope_tabs, BF16 if is_prompt else F32)
    s.u_layers.append(u)
    if is_prompt:
        yret, s.new_state = _ret_prompt(tok, i, proj, p['g_ret'], s.new_state)
        mixed = _pool_prompt(tok, i, u, p['w_pool_map'], p['pool_scale'])
    else:
        yret, s.new_state = _ret_sample(tok, i, proj, p['g_ret'], s.ret_state, s.new_state)
        mixed = _pool_sample(tok, i, u, s.pool_state[i].transpose(1, 0, 2), p['w_pool_map'], p['pool_scale'])
    merged = _merge(tok, i, yret, mixed, proj, p['w_ret_out'], p['w_pool_out'])
    s.x = _wo(tok, i, merged, s.x, p['w_o'])


def _moe_layer(streams, i, p, g_final):
    rows_all = sum(s.tok.rows for s in streams)
    h_all, row0, choices, weights, starts = None, 0, [], [], []
    for s in streams:
        esel, wsel, h_all = _router(s.tok, i, s.x, p['g_ffn'], p['w_router'], p['b_router'],
                                    h_all=h_all, row0=row0, rows_all=rows_all)
        choices.append(esel[:, :2])
        weights.append(wsel)
        starts.append(row0)
        row0 += s.tok.rows
    plan = _route_plan(jnp.concatenate(choices, axis=0), TM_MOE)
    y_sorted = _moe_sparse(i, h_all, plan, p['w1_moe'], p['w3_moe'], p['w2_moe'], TM_MOE)
    for s, wsel, start in zip(streams, weights, starts):
        s.x = _combine(s.tok, i, y_sorted, plan[3], start, s.x, wsel, g_final)


def kernel(x_prompt, x_sample, c_prompt, c_sample, state_ret, state_pool, w_ada, b_ada, g_mix, w_in, g_ret,
           w_ret_out, w_pool_map, pool_scale, w_pool_out, w_o, g_ffn, w1_dense, w3_dense, w2_dense, w_router,
           b_router, w1_moe, w3_moe, w2_moe, g_final):
    row3 = lambda a: a.reshape(a.shape[0], 1, a.shape[1])
    p = dict(g_mix=row3(g_mix), w_in=w_in, g_ret=row3(g_ret), w_ret_out=w_ret_out, w_pool_map=w_pool_map,
             pool_scale=row3(pool_scale), w_pool_out=w_pool_out, w_o=w_o, g_ffn=row3(g_ffn),
             w1_dense=w1_dense, w3_dense=w3_dense, w2_dense=w2_dense, w_router=w_router, b_router=row3(b_router),
             w1_moe=w1_moe, w3_moe=w3_moe, w2_moe=w2_moe, g_final=g_final.reshape(1, D_MODEL))
    n_bp, seq_p, _ = x_prompt.shape
    n_bs, seq_s, _ = x_sample.shape

    mod_p, mod_s = _ada(c_prompt, jnp.repeat(c_sample, seq_s, axis=0), w_ada, b_ada)

    tok_p = _Tokens(n_bp, seq_p, 0, TM_PROMPT, per_row_mod=False, mod=mod_p)
    tok_s = _Tokens(n_bs, seq_s, PAST_LEN, TM_SAMPLE, per_row_mod=True, mod=mod_s)
    prompt = _Stream(tok_p, x_prompt.reshape(tok_p.rows, D_MODEL), None, None)
    sample = _Stream(tok_s, x_sample.reshape(tok_s.rows, D_MODEL), state_ret, state_pool)
    streams = [prompt, sample]

    for i in range(DEPTH):
        g_fin = p['g_final'] if i == DEPTH - 1 else None
        for s in streams:
            _mixer_layer(s, i, p)
        if i % 2 == 0:
            for s in streams:
                s.x = _ffn(s.tok, i, s.x, p['g_ffn'], p['w1_dense'], p['w3_dense'], p['w2_dense'], g_final=g_fin)
        else:
            _moe_layer(streams, i, p, g_fin)

    buf_prompt = jnp.stack([u.reshape(n_bp, seq_p, D_POOL)[:, seq_p - POOL_BUF:] for u in prompt.u_layers])
    buf_sample = jnp.stack([
        jnp.concatenate([state_pool[i], sample.u_layers[i].reshape(n_bs, seq_s, D_POOL)], axis=1)[:, -POOL_BUF:]
        for i in range(DEPTH)])
    return (prompt.x.reshape(x_prompt.shape), sample.x.reshape(x_sample.shape), prompt.new_state, buf_prompt,
            sample.new_state, buf_sample)
```

```python
import functools

import numpy as np
import jax
import jax.numpy as jnp
from jax import lax
from jax.experimental import pallas as pl
from jax.experimental.pallas import tpu as pltpu

F32 = jnp.float32
BF16 = jnp.bfloat16

D_MODEL = 2048
DEPTH = 2
PAST_LEN = 16384
RET_HEADS = 8
RET_DK = 128
RET_DV = 256
RET_CHUNK = 128
ROPE_BASE = 10000.0
QK_W = RET_HEADS * RET_DK
V_W = RET_HEADS * RET_DV
POOL_WINDOWS = (2, 4, 8, 16)
D_POOL = 1024
POOL_GW = 256
POOL_BUF = 15
POOL_HALO = 16
N_IN = 2 * QK_W + 2 * V_W + D_POOL + 2 * D_MODEL
N_EXPERTS = 8
EPS = 1e-6
N_MOD = 6

O_Q, O_K, O_V, O_G, O_U = 0, QK_W, 2 * QK_W, 2 * QK_W + V_W, 2 * QK_W + 2 * V_W
O_M1 = O_U + D_POOL
O_M2 = O_M1 + D_MODEL

VMEM_LIMIT_BYTES = 56 * 1024 * 1024
TN = 512
TF = 256
TM_PROMPT = 1024
TM_SAMPLE = 512
TM_FFN = 1024
TM_OUT = 2048
ROW_CHUNK = 256
NORM_ROWS = 16
NORM_UNROLL = 16
TL_POOL = 512
TM_MOE = 1024
SUB_MOE = 256
TM_COMBINE = 256


def _params(*sem):
    return pltpu.CompilerParams(dimension_semantics=sem, vmem_limit_bytes=VMEM_LIMIT_BYTES)


class _Tokens:
    def __init__(self, n_batch, seq, n_past, tm, per_row_mod, mod=None):
        self.n_batch, self.seq, self.n_past = n_batch, seq, n_past
        self.rows = n_batch * seq
        self.tm = min(tm, self.rows)
        self.per_row_mod = per_row_mod
        if not per_row_mod:
            self.tm = min(self.tm, seq)
            assert seq % self.tm == 0
        assert self.rows % self.tm == 0
        self.n_tiles = self.rows // self.tm
        if mod is not None and not per_row_mod and mod.ndim == 3 and mod.shape[1] != 1:
            mod = mod.reshape(DEPTH * n_batch, 1, N_MOD * D_MODEL)
        self.mod = mod

    def with_tm(self, tm):
        return _Tokens(self.n_batch, self.seq, self.n_past, tm, self.per_row_mod, self.mod)

    def mod_spec(self, layer, chunk, tn=D_MODEL, col_axis=False):
        per_chunk = D_MODEL // tn
        col = (lambda n: chunk * per_chunk + n) if col_axis else (lambda n: chunk * per_chunk)
        if self.per_row_mod:
            if col_axis:
                return pl.BlockSpec((1, self.tm, tn), lambda m, n, *_: (layer, m, col(n)))
            return pl.BlockSpec((1, self.tm, tn), lambda m, *_: (layer, m, col(0)))
        tiles_per_batch = self.seq // self.tm
        base = layer * self.n_batch
        if col_axis:
            return pl.BlockSpec((1, 1, tn), lambda m, n, *_: (base + m // tiles_per_batch, 0, col(n)))
        return pl.BlockSpec((1, 1, tn), lambda m, *_: (base + m // tiles_per_batch, 0, col(0)))

    def rope_tables(self):
        half = RET_DK // 2
        pos = jnp.arange(self.seq, dtype=F32) + float(self.n_past)
        inv = ROPE_BASE ** (-jnp.arange(half, dtype=F32) / half)
        ang = pos[:, None] * inv[None, :]
        cos, sin = jnp.cos(ang), jnp.sin(ang)
        cos_full = jnp.concatenate([cos, cos], axis=1)
        sin_signed = jnp.concatenate([-sin, sin], axis=1)
        if self.per_row_mod:
            cos_full = jnp.tile(cos_full, (self.n_batch, 1))
            sin_signed = jnp.tile(sin_signed, (self.n_batch, 1))
        return cos_full, sin_signed

    def rope_spec(self):
        n_blocks = (self.rows if self.per_row_mod else self.seq) // self.tm
        return pl.BlockSpec((self.tm, RET_DK), lambda m, *_: (m % n_blocks, 0))


def _retention_tables(chunk):
    log_gamma = jnp.log1p(-jnp.exp2(-5.0 - jnp.arange(RET_HEADS, dtype=F32)))
    idx = jnp.arange(chunk, dtype=F32)
    rel = idx[:, None] - idx[None, :]
    dmask = jnp.where(rel[None] >= 0, jnp.exp(log_gamma[:, None, None] * jnp.maximum(rel, 0.0)[None]), 0.0)
    q_dec = jnp.exp(log_gamma[:, None] * (idx + 1.0)[None, :])
    k_dec = jnp.exp(log_gamma[:, None] * (chunk - 1.0 - idx)[None, :])
    chunk_dec = jnp.exp(log_gamma * chunk)
    q_dec = jnp.broadcast_to(q_dec[:, :, None], (RET_HEADS, chunk, RET_DV))
    k_dec = jnp.broadcast_to(k_dec[:, :, None], (RET_HEADS, chunk, RET_DK))
    chunk_dec = jnp.broadcast_to(chunk_dec[:, None, None], (RET_HEADS, 1, RET_DV))
    return dmask, q_dec, k_dec, chunk_dec


def _rmsnorm_mod(x, g, scale, shift):
    y = x * lax.rsqrt(jnp.mean(x * x, axis=-1, keepdims=True) + EPS) * g
    return y * (1.0 + scale) + shift


def _norm_rows(x_ref, g_ref, sc_ref, sh_ref, h_ref):
    tm = x_ref.shape[0]
    step = min(NORM_ROWS, tm)
    trips = tm // step
    per_row = sc_ref.shape[1] != 1
    if not per_row:
        gain, shift = g_ref[0] * (1.0 + sc_ref[0]), sh_ref[0]

    def body(c, carry):
        rows = pl.ds(pl.multiple_of(c * step, step), step)
        x = x_ref[rows, :]
        y = x * lax.rsqrt(jnp.mean(x * x, axis=-1, keepdims=True) + EPS)
        if per_row:
            y = y * (g_ref[0] * (1.0 + sc_ref[0, rows, :])) + sh_ref[0, rows, :]
        else:
            y = y * gain + shift
        h_ref[rows, :] = y.astype(h_ref.dtype)
        return carry

    lax.fori_loop(0, trips, body, 0, unroll=NORM_UNROLL if trips % NORM_UNROLL == 0 else 1)


def _silu(x):
    return x * jax.nn.sigmoid(x)


def _ada_kernel(cp_ref, cs_ref, w_ref, b_ref, op_ref, os_ref):
    w = w_ref[0].astype(BF16)
    for c_ref, o_ref in ((cp_ref, op_ref), (cs_ref, os_ref)):
        o_ref[0] = jnp.dot(_silu(c_ref[...]).astype(BF16), w, preferred_element_type=F32) + b_ref[0]


def _ada(c_p, c_s, w_ada, b_ada):
    tn = 1024
    width = N_MOD * D_MODEL
    row_spec = lambda c: pl.BlockSpec(c.shape, lambda l, n: (0, 0))
    out_spec = lambda c: pl.BlockSpec((1, c.shape[0], tn), lambda l, n: (l, 0, n))
    return pl.pallas_call(
        _ada_kernel,
        grid=(DEPTH, width // tn),
        in_specs=[row_spec(c_p), row_spec(c_s),
                  pl.BlockSpec((1, D_MODEL, tn), lambda l, n: (l, 0, n)),
                  pl.BlockSpec((1, 1, tn), lambda l, n: (l, 0, n))],
        out_specs=[out_spec(c_p), out_spec(c_s)],
        out_shape=[jax.ShapeDtypeStruct((DEPTH, c.shape[0], width), F32) for c in (c_p, c_s)],
        compiler_params=_params("arbitrary", "arbitrary"),
        name="ada",
    )(c_p, c_s, w_ada, b_ada.reshape(DEPTH, 1, width))


def _inproj_kernel(x_ref, sc_ref, sh_ref, g_ref, w_ref, cos_ref, sin_ref, o_ref, u_ref, h_scr):
    n = pl.program_id(1)
    tm = x_ref.shape[0]
    rc = min(ROW_CHUNK, tm)

    @pl.when(n == 0)
    def _():
        _norm_rows(x_ref, g_ref, sc_ref, sh_ref, h_scr)

    def project(epilogue):
        w = w_ref[0].astype(BF16)
        for r in range(0, tm, rc):
            rows = slice(r, r + rc)
            epilogue(jnp.dot(h_scr[rows, :], w, preferred_element_type=F32), rows)

    def store(acc, rows):
        o_ref[rows, :] = acc.astype(o_ref.dtype)

    def rope(scale):
        def epilogue(acc, rows):
            for j in range(TN // RET_DK):
                cols = slice(j * RET_DK, (j + 1) * RET_DK)
                xh = acc[:, cols]
                r = xh * cos_ref[rows, :] + pltpu.roll(xh, RET_DK // 2, 1) * sin_ref[rows, :]
                if scale != 1.0:
                    r = r * scale
                o_ref[rows, cols] = r.astype(o_ref.dtype)
        project(epilogue)

    @pl.when(n < O_K // TN)
    def _():
        rope(1.0)

    @pl.when(jnp.logical_and(n >= O_K // TN, n < O_V // TN))
    def _():
        rope(RET_DK ** -0.5)

    @pl.when(jnp.logical_and(n >= O_V // TN, n < O_G // TN))
    def _():
        project(store)

    @pl.when(jnp.logical_and(n >= O_G // TN, n < O_U // TN))
    def _():
        project(lambda acc, rows: store(_silu(acc), rows))

    @pl.when(jnp.logical_and(n >= O_U // TN, n < O_M1 // TN))
    def _():
        def epilogue(acc, rows):
            store(acc, rows)
            u_ref[rows, :] = acc
        project(epilogue)

    @pl.when(n >= O_M1 // TN)
    def _():
        project(lambda acc, rows: store(jax.nn.sigmoid(acc), rows))


def _inproj(tok, layer, x, g_mix, w_in, rope_tabs, out_dtype):
    tm = tok.tm
    n_u = D_POOL // TN
    mod3 = tok.mod
    proj, u = pl.pallas_call(
        _inproj_kernel,
        grid=(tok.n_tiles, N_IN // TN),
        in_specs=[pl.BlockSpec((tm, D_MODEL), lambda m, n: (m, 0)),
                  tok.mod_spec(layer, 1), tok.mod_spec(layer, 0),
                  pl.BlockSpec((1, 1, D_MODEL), lambda m, n: (layer, 0, 0)),
                  pl.BlockSpec((1, D_MODEL, TN), lambda m, n: (layer, 0, n)),
                  tok.rope_spec(), tok.rope_spec()],
        out_specs=[pl.BlockSpec((tm, TN), lambda m, n: (m, n)),
                   pl.BlockSpec((tm, TN), lambda m, n: (m, jnp.clip(n - O_U // TN, 0, n_u - 1)))],
        out_shape=[jax.ShapeDtypeStruct((tok.rows, N_IN), out_dtype),
                   jax.ShapeDtypeStruct((tok.rows, D_POOL), F32)],
        scratch_shapes=[pltpu.VMEM((tm, D_MODEL), BF16)],
        compiler_params=_params("arbitrary", "arbitrary"),
        name="inproj",
    )(x, mod3, mod3, g_mix, w_in, *rope_tabs)
    return proj, u


def _head_norm_gate(o, gret, gate):
    mu = jnp.mean(o, axis=-1, keepdims=True)
    d = o - mu
    var = jnp.mean(d * d, axis=-1, keepdims=True)
    return gate.astype(F32) * (d * lax.rsqrt(var + EPS) * gret)


def _ret_prompt_kernel(q_ref, k_ref, v_ref, g_ref, gret_ref, dmask_ref, qdec_ref, kdec_ref, cdec_ref,
                       *rest):
    y_ref, s_ref = rest[-2:]

    @pl.when(pl.program_id(1) == 0)
    def _():
        s_ref[...] = jnp.zeros_like(s_ref)

    for h in range(RET_HEADS):
        ks, vs = slice(h * RET_DK, (h + 1) * RET_DK), slice(h * RET_DV, (h + 1) * RET_DV)
        q, k, v = q_ref[:, ks], k_ref[:, ks], v_ref[:, vs]
        state = s_ref[0, 0, h]
        scores = lax.dot_general(q, k, (((1,), (1,)), ((), ())), preferred_element_type=F32) * dmask_ref[h]
        inner = jnp.dot(scores.astype(BF16), v, preferred_element_type=F32)
        cross = jnp.dot(q, state.astype(BF16), preferred_element_type=F32) * qdec_ref[h]
        k_decayed = (k.astype(F32) * kdec_ref[h]).astype(BF16)
        s_ref[0, 0, h] = cdec_ref[h] * state + lax.dot_general(
            k_decayed, v, (((0,), (0,)), ((), ())), preferred_element_type=F32)
        y_ref[:, vs] = _head_norm_gate(inner + cross, gret_ref[0, :, vs], g_ref[:, vs]).astype(y_ref.dtype)


def _table_specs(tables):
    return [pl.BlockSpec(t.shape, lambda *_: (0, 0, 0)) for t in tables]


def _ret_prompt(tok, layer, proj, g_ret, s_all):
    n_b, seq = tok.n_batch, tok.seq
    c = RET_CHUNK
    assert seq % c == 0
    nc = seq // c
    tables = _retention_tables(c)
    in_specs = [pl.BlockSpec((c, QK_W), lambda b, i: (b * nc + i, O_Q // QK_W)),
                pl.BlockSpec((c, QK_W), lambda b, i: (b * nc + i, O_K // QK_W)),
                pl.BlockSpec((c, V_W), lambda b, i: (b * nc + i, O_V // V_W)),
                pl.BlockSpec((c, V_W), lambda b, i: (b * nc + i, O_G // V_W)),
                pl.BlockSpec((1, 1, V_W), lambda b, i: (layer, 0, 0))] + _table_specs(tables)
    args = [proj, proj, proj, proj, g_ret, *tables]
    aliases = {}
    if s_all is not None:
        aliases = {len(args): 1}
        in_specs.append(pl.BlockSpec(memory_space=pl.ANY))
        args.append(s_all)
    y, s_all = pl.pallas_call(
        _ret_prompt_kernel,
        grid=(n_b, nc),
        in_specs=in_specs,
        out_specs=[pl.BlockSpec((c, V_W), lambda b, i: (b * nc + i, 0)),
                   pl.BlockSpec((1, 1, RET_HEADS, RET_DK, RET_DV), lambda b, i: (layer, b, 0, 0, 0))],
        out_shape=[jax.ShapeDtypeStruct((tok.rows, V_W), BF16),
                   jax.ShapeDtypeStruct((DEPTH, n_b, RET_HEADS, RET_DK, RET_DV), F32)],
        input_output_aliases=aliases,
        compiler_params=_params("arbitrary", "arbitrary"),
        name="ret_prompt",
    )(*args)
    return y, s_all


def _ret_sample_kernel(q_ref, k_ref, v_ref, g_ref, kt_ref, gret_ref, s0_ref, dm_ref, qdec_ref, kdect_ref, cdec_ref,
                       *rest, bb, seq):
    y_ref, s_ref = rest[-2:]
    hs = RET_HEADS * seq
    reps = lambda a: jnp.concatenate([a] * RET_HEADS, axis=0)
    mask_q = (lax.broadcasted_iota(jnp.int32, (hs, QK_W), 0) // seq
              == lax.broadcasted_iota(jnp.int32, (hs, QK_W), 1) // RET_DK)
    mask_v = (lax.broadcasted_iota(jnp.int32, (hs, V_W), 0) // seq
              == lax.broadcasted_iota(jnp.int32, (hs, V_W), 1) // RET_DV)
    nt = (((1,), (1,)), ((), ()))
    for b in range(bb):
        rows_b = slice(b * seq, (b + 1) * seq)
        q, k, v = q_ref[rows_b, :], k_ref[rows_b, :], v_ref[rows_b, :]
        q_blk = jnp.where(mask_q, reps(q), 0.0).astype(BF16)
        scores = lax.dot_general(q_blk, reps(k).astype(BF16), nt, preferred_element_type=F32) * dm_ref[...]
        v_stack = jnp.concatenate([v[:, h * RET_DV:(h + 1) * RET_DV] for h in range(RET_HEADS)], axis=0)
        inner = jnp.dot(scores.astype(BF16), v_stack.astype(BF16), preferred_element_type=F32)
        state = s0_ref[0, b]
        cross = jnp.dot(q_blk, state.reshape(QK_W, RET_DV).astype(BF16), preferred_element_type=F32)
        o = inner + cross * qdec_ref[...]
        mu = jnp.mean(o, axis=-1, keepdims=True)
        d = o - mu
        normed = d * lax.rsqrt(jnp.mean(d * d, axis=-1, keepdims=True) + EPS)
        k_dec_t = (kt_ref[b] * kdect_ref[...]).astype(BF16)
        v_blk = jnp.where(mask_v, reps(v), 0.0).astype(BF16)
        upd = jnp.dot(k_dec_t, v_blk, preferred_element_type=F32)
        for h in range(RET_HEADS):
            vs = slice(h * RET_DV, (h + 1) * RET_DV)
            y_ref[rows_b, vs] = g_ref[rows_b, vs] * (normed[h * seq:(h + 1) * seq] * gret_ref[0, :, vs])
            s_ref[0, b, h] = cdec_ref[h] * state[h] + upd[:, vs]


def _ret_sample(tok, layer, proj, g_ret, state_ret, s_all):
    n_b, seq = tok.n_batch, tok.seq
    bb = 4
    assert n_b % bb == 0
    hs = RET_HEADS * seq
    dmask, q_dec, k_dec, chunk_dec = _retention_tables(seq)
    dm_blk = (dmask[:, :, None, :] * jnp.eye(RET_HEADS, dtype=F32)[:, None, :, None]).reshape(hs, hs)
    tables = [dm_blk, q_dec.reshape(hs, RET_DV), k_dec[:, :, 0].reshape(1, hs), chunk_dec]
    k_t = proj[:, O_K:O_K + QK_W].reshape(n_b, seq, RET_HEADS, RET_DK).transpose(0, 3, 2, 1).reshape(n_b, RET_DK, hs)
    state_spec = pl.BlockSpec((1, bb, RET_HEADS, RET_DK, RET_DV), lambda i: (layer, i, 0, 0, 0))
    in_specs = [pl.BlockSpec((bb * seq, QK_W), lambda i: (i, O_Q // QK_W)),
                pl.BlockSpec((bb * seq, QK_W), lambda i: (i, O_K // QK_W)),
                pl.BlockSpec((bb * seq, V_W), lambda i: (i, O_V // V_W)),
                pl.BlockSpec((bb * seq, V_W), lambda i: (i, O_G // V_W)),
                pl.BlockSpec((bb, RET_DK, hs), lambda i: (i, 0, 0)),
                pl.BlockSpec((1, 1, V_W), lambda i: (layer, 0, 0)),
                state_spec] + [pl.BlockSpec(t.shape, lambda i, nd=t.ndim: (0,) * nd) for t in tables]
    args = [proj, proj, proj, proj, k_t, g_ret, state_ret, *tables]
    aliases = {}
    if s_all is not None:
        aliases = {len(args): 1}
        in_specs.append(pl.BlockSpec(memory_space=pl.ANY))
        args.append(s_all)
    y, s_all = pl.pallas_call(
        functools.partial(_ret_sample_kernel, bb=bb, seq=seq),
        grid=(n_b // bb,),
        in_specs=in_specs,
        out_specs=[pl.BlockSpec((bb * seq, V_W), lambda i: (i, 0)), state_spec],
        out_shape=[jax.ShapeDtypeStruct((tok.rows, V_W), F32),
                   jax.ShapeDtypeStruct(state_ret.shape, F32)],
        input_output_aliases=aliases,
        compiler_params=_params("arbitrary"),
        name="ret_sample",
    )(*args)
    return y, s_all


def _pool_map(pooled, wmap_ref, scale_ref, gi):
    cols = slice(gi * POOL_GW, (gi + 1) * POOL_GW)
    mixed = jnp.dot(pooled.astype(BF16), wmap_ref[0, gi].astype(BF16), preferred_element_type=F32)
    return mixed * scale_ref[0, :, cols]


def _pool_prompt_kernel(u_ref, halo_ref, wmap_ref, scale_ref, o_ref, uf_scr, *, tl):
    first = pl.program_id(1) == 0
    uf_scr[0:POOL_HALO, :] = jnp.where(first, 0.0, halo_ref[...])
    uf_scr[POOL_HALO:, :] = u_ref[...]
    row = lax.broadcasted_iota(jnp.int32, (tl, POOL_GW), 0) + pl.program_id(1) * tl
    for gi, w in enumerate(POOL_WINDOWS):
        cols = slice(gi * POOL_GW, (gi + 1) * POOL_GW)
        u = uf_scr[POOL_HALO:, cols]
        acc = u
        for j in range(1, w):
            acc = acc + uf_scr[POOL_HALO - j:POOL_HALO - j + tl, cols]
        cnt = jnp.minimum(w, row + 1).astype(F32)
        o_ref[:, cols] = _pool_map(acc / cnt - u, wmap_ref, scale_ref, gi).astype(o_ref.dtype)


def _pool_prompt(tok, layer, u, w_pool_map, pool_scale):
    n_b, seq = tok.n_batch, tok.seq
    tl = min(TL_POOL, seq)
    assert seq % tl == 0 and tl % POOL_HALO == 0
    nl = seq // tl
    per = tl // POOL_HALO
    return pl.pallas_call(
        functools.partial(_pool_prompt_kernel, tl=tl),
        grid=(n_b, nl),
        in_specs=[pl.BlockSpec((tl, D_POOL), lambda b, l: (b * nl + l, 0)),
                  pl.BlockSpec((POOL_HALO, D_POOL), lambda b, l: (jnp.maximum((b * nl + l) * per - 1, 0), 0)),
                  pl.BlockSpec((1,) + w_pool_map.shape[1:], lambda b, l: (layer, 0, 0, 0)),
                  pl.BlockSpec((1, 1, D_POOL), lambda b, l: (layer, 0, 0))],
        out_specs=pl.BlockSpec((tl, D_POOL), lambda b, l: (b * nl + l, 0)),
        out_shape=jax.ShapeDtypeStruct((tok.rows, D_POOL), BF16),
        scratch_shapes=[pltpu.VMEM((POOL_HALO + tl, D_POOL), F32)],
        compiler_params=_params("arbitrary", "arbitrary"),
        name="pool_prompt",
    )(u, u, w_pool_map, pool_scale)


def _pool_sample_kernel(buf_ref, u_ref, wmap_ref, scale_ref, o_ref, *, seq, n_b):
    def row(j, cols):
        return buf_ref[j, :, cols] if j < POOL_BUF else u_ref[j - POOL_BUF, :, cols]

    for gi, w in enumerate(POOL_WINDOWS):
        cols = slice(gi * POOL_GW, (gi + 1) * POOL_GW)
        for t in range(seq):
            acc = row(POOL_BUF + t, cols)
            for j in range(1, w):
                acc = acc + row(POOL_BUF + t - j, cols)
            pooled = acc / float(w) - row(POOL_BUF + t, cols)
            o_ref[t, :, cols] = _pool_map(pooled, wmap_ref, scale_ref, gi)


def _pool_sample(tok, layer, u, buf_tm, w_pool_map, pool_scale):
    n_b, seq = tok.n_batch, tok.seq
    assert tok.n_past >= max(POOL_WINDOWS)
    u_tm = u.reshape(n_b, seq, D_POOL).transpose(1, 0, 2)
    mixed_tm = pl.pallas_call(
        functools.partial(_pool_sample_kernel, seq=seq, n_b=n_b),
        grid=(1,),
        in_specs=[pl.BlockSpec(buf_tm.shape, lambda i: (0, 0, 0)),
                  pl.BlockSpec(u_tm.shape, lambda i: (0, 0, 0)),
                  pl.BlockSpec((1,) + w_pool_map.shape[1:], lambda i: (layer, 0, 0, 0)),
                  pl.BlockSpec((1, 1, D_POOL), lambda i: (layer, 0, 0))],
        out_specs=pl.BlockSpec(u_tm.shape, lambda i: (0, 0, 0)),
        out_shape=jax.ShapeDtypeStruct(u_tm.shape, F32),
        compiler_params=_params("arbitrary"),
        name="pool_sample",
    )(buf_tm, u_tm, w_pool_map, pool_scale)
    return mixed_tm.transpose(1, 0, 2).reshape(tok.rows, D_POOL)


def _merge_kernel(yret_ref, mixed_ref, wr_ref, wp_ref, g1_ref, g2_ref, o_ref):
    ret = jnp.dot(yret_ref[...].astype(BF16), wr_ref[0].astype(BF16), preferred_element_type=F32)
    pool = jnp.dot(mixed_ref[...].astype(BF16), wp_ref[0].astype(BF16), preferred_element_type=F32)
    o_ref[...] = (g1_ref[...].astype(F32) * ret + g2_ref[...].astype(F32) * pool).astype(o_ref.dtype)


def _merge(tok, layer, yret, mixed, proj, w_ret_out, w_pool_out):
    tok = tok.with_tm(TM_OUT)
    tm = tok.tm
    once = pl.Buffered(1)
    return pl.pallas_call(
        _merge_kernel,
        grid=(tok.n_tiles, D_MODEL // TN),
        in_specs=[pl.BlockSpec((tm, V_W), lambda m, n: (m, 0), pipeline_mode=once),
                  pl.BlockSpec((tm, D_POOL), lambda m, n: (m, 0), pipeline_mode=once),
                  pl.BlockSpec((1, V_W, TN), lambda m, n: (layer, 0, n)),
                  pl.BlockSpec((1, D_POOL, TN), lambda m, n: (layer, 0, n)),
                  pl.BlockSpec((tm, TN), lambda m, n: (m, O_M1 // TN + n)),
                  pl.BlockSpec((tm, TN), lambda m, n: (m, O_M2 // TN + n))],
        out_specs=pl.BlockSpec((tm, TN), lambda m, n: (m, n)),
        out_shape=jax.ShapeDtypeStruct((tok.rows, D_MODEL), BF16),
        compiler_params=_params("arbitrary", "arbitrary"),
        name="merge",
    )(yret, mixed, w_ret_out, w_pool_out, proj, proj)


def _wo_kernel(merged_ref, w_ref, x_ref, gt_ref, o_ref):
    mix = jnp.dot(merged_ref[...], w_ref[0].astype(BF16), preferred_element_type=F32)
    o_ref[...] = x_ref[...] + (1.0 + gt_ref[0]) * mix


def _wo(tok, layer, merged, x, w_o):
    tok = tok.with_tm(TM_OUT)
    tm = tok.tm
    mod3 = tok.mod
    return pl.pallas_call(
        _wo_kernel,
        grid=(tok.n_tiles, D_MODEL // TN),
        in_specs=[pl.BlockSpec((tm, D_MODEL), lambda m, n: (m, 0), pipeline_mode=pl.Buffered(1)),
                  pl.BlockSpec((1, D_MODEL, TN), lambda m, n: (layer, 0, n)),
                  pl.BlockSpec((tm, TN), lambda m, n: (m, n)),
                  tok.mod_spec(layer, 2, tn=TN, col_axis=True)],
        out_specs=pl.BlockSpec((tm, TN), lambda m, n: (m, n)),
        out_shape=jax.ShapeDtypeStruct((tok.rows, D_MODEL), F32),
        compiler_params=_params("arbitrary", "arbitrary"),
        name="wo",
    )(merged, w_o, x, mod3)


def _mat(w_ref):
    return w_ref[(0,) * (len(w_ref.shape) - 2)]


def _final_norm(y, gfin_ref):
    return y * lax.rsqrt(jnp.mean(y * y, axis=-1, keepdims=True) + EPS) * gfin_ref[...]


def _ffn_kernel(x_ref, sc_ref, sh_ref, gt_ref, g_ref, w1_ref, w3_ref, w2_ref, *rest, n_f, with_final):
    gfin_ref = rest[0] if with_final else None
    o_ref, h_scr = rest[-2:]
    f = pl.program_id(1)

    @pl.when(f == 0)
    def _():
        _norm_rows(x_ref, g_ref, sc_ref, sh_ref, h_scr)
        o_ref[...] = jnp.zeros_like(o_ref)

    h = h_scr[...]
    a = jnp.dot(h, _mat(w1_ref).astype(BF16), preferred_element_type=F32)
    b = jnp.dot(h, _mat(w3_ref).astype(BF16), preferred_element_type=F32)
    o_ref[...] += jnp.dot((_silu(a) * b).astype(BF16), _mat(w2_ref).astype(BF16), preferred_element_type=F32)

    @pl.when(f == n_f - 1)
    def _():
        y = x_ref[...] + (1.0 + gt_ref[0]) * o_ref[...]
        o_ref[...] = _final_norm(y, gfin_ref) if with_final else y


def _ffn(tok, layer, x, g_ffn, w1, w3, w2, g_final=None):
    j = layer // 2
    d_ff = w1.shape[-1]
    tok_f = tok.with_tm(TM_FFN)
    tm, n_tiles = tok_f.tm, tok_f.n_tiles
    mod3 = tok_f.mod
    assert d_ff % TF == 0
    n_f = d_ff // TF
    once = pl.Buffered(1)
    in_specs = [pl.BlockSpec((tm, D_MODEL), lambda m, f: (m, 0), pipeline_mode=once),
                tok_f.mod_spec(layer, 4), tok_f.mod_spec(layer, 3), tok_f.mod_spec(layer, 5),
                pl.BlockSpec((1, 1, D_MODEL), lambda m, f: (layer, 0, 0)),
                pl.BlockSpec((1, D_MODEL, TF), lambda m, f: (j, 0, f)),
                pl.BlockSpec((1, D_MODEL, TF), lambda m, f: (j, 0, f)),
                pl.BlockSpec((1, TF, D_MODEL), lambda m, f: (j, f, 0))]
    args = [x, mod3, mod3, mod3, g_ffn, w1, w3, w2]
    if g_final is not None:
        in_specs.append(pl.BlockSpec((1, D_MODEL), lambda m, f: (0, 0)))
        args.append(g_final)
    return pl.pallas_call(
        functools.partial(_ffn_kernel, n_f=n_f, with_final=g_final is not None),
        grid=(n_tiles, n_f),
        in_specs=in_specs,
        out_specs=pl.BlockSpec((tm, D_MODEL), lambda m, f: (m, 0), pipeline_mode=once),
        out_shape=jax.ShapeDtypeStruct((tok.rows, D_MODEL), F32),
        scratch_shapes=[pltpu.VMEM((tm, D_MODEL), BF16)],
        compiler_params=_params("arbitrary", "arbitrary"),
        name="ffn",
    )(*args)


def _router_kernel(x_ref, sc_ref, sh_ref, g_ref, w_ref, b_ref, *rest):
    esel_ref, wsel_ref, h_ref = rest[-3:]
    h = _rmsnorm_mod(x_ref[...], g_ref[0], sc_ref[0], sh_ref[0])
    logits = jnp.dot(h.astype(BF16), w_ref[0].astype(BF16), preferred_element_type=F32) + b_ref[0]
    idx = lax.broadcasted_iota(jnp.int32, logits.shape, 1)
    top1 = jnp.max(logits, axis=-1, keepdims=True)
    i1 = jnp.min(jnp.where(logits == top1, idx, N_EXPERTS), axis=-1, keepdims=True)
    rest = jnp.where(idx == i1, -jnp.inf, logits)
    top2 = jnp.max(rest, axis=-1, keepdims=True)
    i2 = jnp.min(jnp.where(rest == top2, idx, N_EXPERTS), axis=-1, keepdims=True)
    e2 = jnp.exp(top2 - top1)
    denom = 1.0 + e2
    w1, w2 = 1.0 / denom, e2 / denom
    esel_ref[...] = jnp.where(idx == 0, i1, i2)
    wsel_ref[...] = jnp.where(idx == 0, w1, w2)
    h_ref[...] = h


def _router(tok, layer, x, g_ffn, w_router, b_router, h_all=None, row0=0, rows_all=None):
    j = layer // 2
    tm = tok.tm
    rows_all = tok.rows if rows_all is None else rows_all
    assert row0 % tm == 0
    small = pl.BlockSpec((tm, N_EXPERTS), lambda m: (m, 0))
    mod3 = tok.mod
    in_specs = [pl.BlockSpec((tm, D_MODEL), lambda m: (m, 0)),
                tok.mod_spec(layer, 4), tok.mod_spec(layer, 3),
                pl.BlockSpec((1, 1, D_MODEL), lambda m: (layer, 0, 0)),
                pl.BlockSpec((1, D_MODEL, N_EXPERTS), lambda m: (j, 0, 0)),
                pl.BlockSpec((1, 1, N_EXPERTS), lambda m: (j, 0, 0))]
    args = [x, mod3, mod3, g_ffn, w_router, b_router]
    aliases = {}
    if h_all is not None:
        aliases = {len(args): 2}
        in_specs.append(pl.BlockSpec(memory_space=pl.ANY))
        args.append(h_all)
    return pl.pallas_call(
        _router_kernel,
        grid=(tok.n_tiles,),
        in_specs=in_specs,
        out_specs=[small, small, pl.BlockSpec((tm, D_MODEL), lambda m: (row0 // tm + m, 0))],
        out_shape=[jax.ShapeDtypeStruct((tok.rows, N_EXPERTS), jnp.int32),
                   jax.ShapeDtypeStruct((tok.rows, N_EXPERTS), F32),
                   jax.ShapeDtypeStruct((rows_all, D_MODEL), F32)],
        input_output_aliases=aliases,
        compiler_params=_params("arbitrary"),
        name="router",
    )(*args)


def _route_plan(choice, tm):
    rows = choice.shape[0]
    n_pairs = 2 * rows
    n_tiles = n_pairs // tm + N_EXPERTS
    e_flat = choice.T.reshape(n_pairs)
    onehot = (e_flat[:, None] == jnp.arange(N_EXPERTS, dtype=jnp.int32)[None, :]).astype(jnp.int32)
    csum = jnp.cumsum(onehot, axis=0)
    rank = jnp.sum(csum * onehot, axis=1) - 1
    count = csum[-1]
    tiles_per = (count + tm - 1) // tm
    rows_per = jnp.maximum((((count + jnp.maximum(tiles_per, 1) - 1) // jnp.maximum(tiles_per, 1)) + 7) // 8 * 8, 8)
    tile_end = jnp.cumsum(tiles_per)
    tile_start = tile_end - tiles_per
    pos = (tile_start[e_flat] + rank // rows_per[e_flat]) * tm + rank % rows_per[e_flat]
    token = jnp.arange(n_pairs, dtype=jnp.int32) % rows
    src_row = jnp.zeros((n_tiles * tm,), jnp.int32).at[pos].set(token, unique_indices=True, mode='promise_in_bounds')
    tile_id = jnp.arange(n_tiles, dtype=jnp.int32)
    n_used = tile_end[-1]
    tile_expert = jnp.minimum(jnp.sum(tile_id[:, None] >= tile_end[None, :], axis=1), N_EXPERTS - 1).astype(jnp.int32)
    tile_rows = jnp.clip(count[tile_expert] - (tile_id - tile_start[tile_expert]) * rows_per[tile_expert],
                         0, rows_per[tile_expert])
    tile_rows = jnp.where(tile_id < n_used, tile_rows, 0).astype(jnp.int32)
    last_expert = tile_expert[jnp.maximum(n_used - 1, 0)]
    tile_expert = jnp.where(tile_rows > 0, tile_expert, last_expert)
    return tile_expert, tile_rows, src_row, pos.astype(jnp.int32)


def _row_copy(src_hbm, row, dst, r, sem):
    return pltpu.make_async_copy(src_hbm.at[pl.ds(row, 1)], dst.at[pl.ds(r, 1)], sem)


def _moe_kernel(te_ref, nr_ref, src_ref, h_hbm, w1_ref, w3_ref, w2_ref, o_ref, rows_scr, h_scr, sem, *, tm, sub):
    del te_ref
    m, f = pl.program_id(0), pl.program_id(1)
    group = 8

    def for_each_row(tile, fn):
        def body(i, c):
            for r8 in range(group):
                r = i * group + r8
                fn(_row_copy(h_hbm, src_ref[tile * tm + r], rows_scr, r, sem))
            return c
        lax.fori_loop(0, (nr_ref[tile] + group - 1) // group, body, 0)

    @pl.when(f == 0)
    def _():
        o_ref[...] = jnp.zeros_like(o_ref)

        @pl.when(m == 0)
        def _():
            rows_scr[...] = jnp.zeros_like(rows_scr)
            for_each_row(0, lambda cp: cp.start())

        for_each_row(m, lambda cp: cp.wait())
        h_scr[...] = rows_scr[...].astype(BF16)

        @pl.when(m + 1 < pl.num_programs(0))
        def _():
            for_each_row(m + 1, lambda cp: cp.start())

    n_sub = (nr_ref[m] + sub - 1) // sub
    for n in range(1, tm // sub + 1):
        @pl.when(n_sub == n)
        def _(n=n):
            h = h_scr[0:n * sub, :]
            a = jnp.dot(h, _mat(w1_ref).astype(BF16), preferred_element_type=F32)
            b = jnp.dot(h, _mat(w3_ref).astype(BF16), preferred_element_type=F32)
            o_ref[0:n * sub, :] += jnp.dot((_silu(a) * b).astype(BF16), _mat(w2_ref).astype(BF16),
                                           preferred_element_type=F32)


def _moe_sparse(layer, h, plan, w1, w3, w2, tm):
    j = layer // 2
    tile_expert, tile_rows, src_row, _ = plan
    n_tiles = tile_expert.shape[0]
    d_ff = w1.shape[3]
    assert d_ff % TF == 0 and tm % SUB_MOE == 0
    n_f = d_ff // TF

    def f_sel(m, f, nr):
        return jnp.where(nr[m] > 0, f, n_f - 1)

    return pl.pallas_call(
        functools.partial(_moe_kernel, tm=tm, sub=SUB_MOE),
        grid_spec=pltpu.PrefetchScalarGridSpec(
            num_scalar_prefetch=3,
            grid=(n_tiles, n_f),
            in_specs=[pl.BlockSpec(memory_space=pl.ANY),
                      pl.BlockSpec((1, 1, D_MODEL, TF), lambda m, f, te, nr, sr: (j, te[m], 0, f_sel(m, f, nr))),
                      pl.BlockSpec((1, 1, D_MODEL, TF), lambda m, f, te, nr, sr: (j, te[m], 0, f_sel(m, f, nr))),
                      pl.BlockSpec((1, 1, TF, D_MODEL), lambda m, f, te, nr, sr: (j, te[m], f_sel(m, f, nr), 0))],
            out_specs=pl.BlockSpec((tm, D_MODEL), lambda m, f, te, nr, sr: (m, 0)),
            scratch_shapes=[pltpu.VMEM((tm, D_MODEL), F32), pltpu.VMEM((tm, D_MODEL), BF16),
                            pltpu.SemaphoreType.DMA(())]),
        out_shape=jax.ShapeDtypeStruct((n_tiles * tm, D_MODEL), F32),
        compiler_params=_params("arbitrary", "arbitrary"),
        name="moe",
    )(tile_expert, tile_rows, src_row, h, w1, w3, w2)


def _combine_kernel(pos_ref, y_hbm, x_ref, gt_ref, w_ref, *rest, tm, row0, rows_all, with_final):
    gfin_ref = rest[0] if with_final else None
    o_ref, y_scr, sems = rest[-3:]
    i, n = pl.program_id(0), pl.num_programs(0)

    def for_each_row(tile, slot, fn):
        def body(r, c):
            for k in range(2):
                slot_of_pair = pos_ref[k * rows_all + row0 + tile * tm + r]
                fn(_row_copy(y_hbm, slot_of_pair, y_scr.at[slot, k], r, sems.at[slot]))
            return c
        lax.fori_loop(0, tm, body, 0, unroll=8)

    @pl.when(i == 0)
    def _():
        for_each_row(0, 0, lambda cp: cp.start())

    @pl.when(i + 1 < n)
    def _():
        for_each_row(i + 1, (i + 1) % 2, lambda cp: cp.start())

    slot = i % 2
    for_each_row(i, slot, lambda cp: cp.wait())
    w = w_ref[...]
    f = w[:, 0:1] * y_scr[slot, 0] + w[:, 1:2] * y_scr[slot, 1]
    y = x_ref[...] + (1.0 + gt_ref[0]) * f
    o_ref[...] = _final_norm(y, gfin_ref) if with_final else y


def _combine(tok, layer, y_sorted, pos, row0, x, wsel, g_final):
    tok_c = tok.with_tm(TM_COMBINE)
    tm = tok_c.tm
    mod3 = tok_c.mod
    in_specs = [pl.BlockSpec(memory_space=pl.ANY),
                pl.BlockSpec((tm, D_MODEL), lambda m, pos: (m, 0)),
                tok_c.mod_spec(layer, 5),
                pl.BlockSpec((tm, N_EXPERTS), lambda m, pos: (m, 0))]
    args = [pos, y_sorted, x, mod3, wsel]
    if g_final is not None:
        in_specs.append(pl.BlockSpec((1, D_MODEL), lambda m, pos: (0, 0)))
        args.append(g_final)
    return pl.pallas_call(
        functools.partial(_combine_kernel, tm=tm, row0=row0, rows_all=pos.shape[0] // 2,
                          with_final=g_final is not None),
        grid_spec=pltpu.PrefetchScalarGridSpec(
            num_scalar_prefetch=1,
            grid=(tok_c.n_tiles,),
            in_specs=in_specs,
            out_specs=pl.BlockSpec((tm, D_MODEL), lambda m, pos: (m, 0)),
            scratch_shapes=[pltpu.VMEM((2, 2, tm, D_MODEL), F32), pltpu.SemaphoreType.DMA((2,))]),
        out_shape=jax.ShapeDtypeStruct((tok.rows, D_MODEL), F32),
        compiler_params=_params("arbitrary"),
        name="combine",
    )(*args)


class _Stream:
    def __init__(self, tok, x, ret_state, pool_state):
        self.tok, self.x, self.ret_state, self.pool_state = tok, x, ret_state, pool_state
        self.rope_tabs = tok.rope_tables()
        self.new_state = None
        self.u_layers = []


def _mixer_layer(s, i, p):
    tok = s.tok
    is_prompt = s.ret_state is None
    proj, u = _inproj(tok, i, s.x, p['g_mix'], p['w_in'], s.rope_tabs, BF16 if is_prompt else F32)
    s.u_layers.append(u)
    if is_prompt:
        yret, s.new_state = _ret_prompt(tok, i, proj, p['g_ret'], s.new_state)
        mixed = _pool_prompt(tok, i, u, p['w_pool_map'], p['pool_scale'])
    else:
        yret, s.new_state = _ret_sample(tok, i, proj, p['g_ret'], s.ret_state, s.new_state)
        mixed = _pool_sample(tok, i, u, s.pool_state[i].transpose(1, 0, 2), p['w_pool_map'], p['pool_scale'])
    merged = _merge(tok, i, yret, mixed, proj, p['w_ret_out'], p['w_pool_out'])
    s.x = _wo(tok, i, merged, s.x, p['w_o'])


def _moe_layer(streams, i, p, g_final):
    rows_all = sum(s.tok.rows for s in streams)
    h_all, row0, choices, weights, starts = None, 0, [], [], []
    for s in streams:
        esel, wsel, h_all = _router(s.tok, i, s.x, p['g_ffn'], p['w_router'], p['b_router'],
                                    h_all=h_all, row0=row0, rows_all=rows_all)
        choices.append(esel[:, :2])
        weights.append(wsel)
        starts.append(row0)
        row0 += s.tok.rows
    plan = _route_plan(jnp.concatenate(choices, axis=0), TM_MOE)
    y_sorted = _moe_sparse(i, h_all, plan, p['w1_moe'], p['w3_moe'], p['w2_moe'], TM_MOE)
    for s, wsel, start in zip(streams, weights, starts):
        s.x = _combine(s.tok, i, y_sorted, plan[3], start, s.x, wsel, g_final)


def kernel(x_prompt, x_sample, c_prompt, c_sample, state_ret, state_pool, w_ada, b_ada, g_mix, w_in, g_ret,
           w_ret_out, w_pool_map, pool_scale, w_pool_out, w_o, g_ffn, w1_dense, w3_dense, w2_dense, w_router,
           b_router, w1_moe, w3_moe, w2_moe, g_final):
    row3 = lambda a: a.reshape(a.shape[0], 1, a.shape[1])
    p = dict(g_mix=row3(g_mix), w_in=w_in, g_ret=row3(g_ret), w_ret_out=w_ret_out, w_pool_map=w_pool_map,
             pool_scale=row3(pool_scale), w_pool_out=w_pool_out, w_o=w_o, g_ffn=row3(g_ffn),
             w1_dense=w1_dense, w3_dense=w3_dense, w2_dense=w2_dense, w_router=w_router, b_router=row3(b_router),
             w1_moe=w1_moe, w3_moe=w3_moe, w2_moe=w2_moe, g_final=g_final.reshape(1, D_MODEL))
    n_bp, seq_p, _ = x_prompt.shape
    n_bs, seq_s, _ = x_sample.shape

    mod_p, mod_s = _ada(c_prompt, jnp.repeat(c_sample, seq_s, axis=0), w_ada, b_ada)

    tok_p = _Tokens(n_bp, seq_p, 0, TM_PROMPT, per_row_mod=False, mod=mod_p)
    tok_s = _Tokens(n_bs, seq_s, PAST_LEN, TM_SAMPLE, per_row_mod=True, mod=mod_s)
    prompt = _Stream(tok_p, x_prompt.reshape(tok_p.rows, D_MODEL), None, None)
    sample = _Stream(tok_s, x_sample.reshape(tok_s.rows, D_MODEL), state_ret, state_pool)
    streams = [prompt, sample]

    for i in range(DEPTH):
        g_fin = p['g_final'] if i == DEPTH - 1 else None
        for s in streams:
            _mixer_layer(s, i, p)
        if i % 2 == 0:
            for s in streams:
                s.x = _ffn(s.tok, i, s.x, p['g_ffn'], p['w1_dense'], p['w3_dense'], p['w2_dense'], g_final=g_fin)
        else:
            _moe_layer(streams, i, p, g_fin)

    buf_prompt = jnp.stack([u.reshape(n_bp, seq_p, D_POOL)[:, seq_p - POOL_BUF:] for u in prompt.u_layers])
    buf_sample = jnp.stack([
        jnp.concatenate([state_pool[i], sample.u_layers[i].reshape(n_bs, seq_s, D_POOL)], axis=1)[:, -POOL_BUF:]
        for i in range(DEPTH)])
    return (prompt.x.reshape(x_prompt.shape), sample.x.reshape(x_sample.shape), prompt.new_state, buf_prompt,
            sample.new_state, buf_sample)
```

```python
import functools

import numpy as np
import jax
import jax.numpy as jnp
from jax import lax
from jax.experimental import pallas as pl
from jax.experimental.pallas import tpu as pltpu

F32 = jnp.float32
BF16 = jnp.bfloat16

D_MODEL = 2048
DEPTH = 2
PAST_LEN = 16384
RET_HEADS = 8
RET_DK = 128
RET_DV = 256
RET_CHUNK = 128
ROPE_BASE = 10000.0
QK_W = RET_HEADS * RET_DK
V_W = RET_HEADS * RET_DV
POOL_WINDOWS = (2, 4, 8, 16)
D_POOL = 1024
POOL_GW = 256
POOL_BUF = 15
POOL_HALO = 16
N_IN = 2 * QK_W + 2 * V_W + D_POOL + 2 * D_MODEL
N_EXPERTS = 8
EPS = 1e-6
N_MOD = 6

O_Q, O_K, O_V, O_G, O_U = 0, QK_W, 2 * QK_W, 2 * QK_W + V_W, 2 * QK_W + 2 * V_W
O_M1 = O_U + D_POOL
O_M2 = O_M1 + D_MODEL

VMEM_LIMIT_BYTES = 56 * 1024 * 1024
TN = 512
TF = 256
TM_PROMPT = 1024
TM_SAMPLE = 512
TM_FFN = 1024
TM_OUT = 2048
ROW_CHUNK = 256
NORM_ROWS = 16
NORM_UNROLL = 16
TL_POOL = 512
TM_MOE = 1024
SUB_MOE = 256
TM_COMBINE = 256


def _params(*sem):
    return pltpu.CompilerParams(dimension_semantics=sem, vmem_limit_bytes=VMEM_LIMIT_BYTES)


class _Tokens:
    def __init__(self, n_batch, seq, n_past, tm, per_row_mod, mod=None):
        self.n_batch, self.seq, self.n_past = n_batch, seq, n_past
        self.rows = n_batch * seq
        self.tm = min(tm, self.rows)
        self.per_row_mod = per_row_mod
        if not per_row_mod:
            self.tm = min(self.tm, seq)
            assert seq % self.tm == 0
        assert self.rows % self.tm == 0
        self.n_tiles = self.rows // self.tm
        if mod is not None and not per_row_mod and mod.ndim == 3 and mod.shape[1] != 1:
            mod = mod.reshape(DEPTH * n_batch, 1, N_MOD * D_MODEL)
        self.mod = mod

    def with_tm(self, tm):
        return _Tokens(self.n_batch, self.seq, self.n_past, tm, self.per_row_mod, self.mod)

    def mod_spec(self, layer, chunk, tn=D_MODEL, col_axis=False):
        per_chunk = D_MODEL // tn
        col = (lambda n: chunk * per_chunk + n) if col_axis else (lambda n: chunk * per_chunk)
        if self.per_row_mod:
            if col_axis:
                return pl.BlockSpec((1, self.tm, tn), lambda m, n, *_: (layer, m, col(n)))
            return pl.BlockSpec((1, self.tm, tn), lambda m, *_: (layer, m, col(0)))
        tiles_per_batch = self.seq // self.tm
        base = layer * self.n_batch
        if col_axis:
            return pl.BlockSpec((1, 1, tn), lambda m, n, *_: (base + m // tiles_per_batch, 0, col(n)))
        return pl.BlockSpec((1, 1, tn), lambda m, *_: (base + m // tiles_per_batch, 0, col(0)))

    def rope_tables(self):
        half = RET_DK // 2
        pos = jnp.arange(self.seq, dtype=F32) + float(self.n_past)
        inv = ROPE_BASE ** (-jnp.arange(half, dtype=F32) / half)
        ang = pos[:, None] * inv[None, :]
        cos, sin = jnp.cos(ang), jnp.sin(ang)
        cos_full = jnp.concatenate([cos, cos], axis=1)
        sin_signed = jnp.concatenate([-sin, sin], axis=1)
        if self.per_row_mod:
            cos_full = jnp.tile(cos_full, (self.n_batch, 1))
            sin_signed = jnp.tile(sin_signed, (self.n_batch, 1))
        return cos_full, sin_signed

    def rope_spec(self):
        n_blocks = (self.rows if self.per_row_mod else self.seq) // self.tm
        return pl.BlockSpec((self.tm, RET_DK), lambda m, *_: (m % n_blocks, 0))


def _retention_tables(chunk):
    log_gamma = jnp.log1p(-jnp.exp2(-5.0 - jnp.arange(RET_HEADS, dtype=F32)))
    idx = jnp.arange(chunk, dtype=F32)
    rel = idx[:, None] - idx[None, :]
    dmask = jnp.where(rel[None] >= 0, jnp.exp(log_gamma[:, None, None] * jnp.maximum(rel, 0.0)[None]), 0.0)
    q_dec = jnp.exp(log_gamma[:, None] * (idx + 1.0)[None, :])
    k_dec = jnp.exp(log_gamma[:, None] * (chunk - 1.0 - idx)[None, :])
    chunk_dec = jnp.exp(log_gamma * chunk)
    q_dec = jnp.broadcast_to(q_dec[:, :, None], (RET_HEADS, chunk, RET_DV))
    k_dec = jnp.broadcast_to(k_dec[:, :, None], (RET_HEADS, chunk, RET_DK))
    chunk_dec = jnp.broadcast_to(chunk_dec[:, None, None], (RET_HEADS, 1, RET_DV))
    return dmask, q_dec, k_dec, chunk_dec


def _rmsnorm_mod(x, g, scale, shift):
    y = x * lax.rsqrt(jnp.mean(x * x, axis=-1, keepdims=True) + EPS) * g
    return y * (1.0 + scale) + shift


def _norm_rows(x_ref, g_ref, sc_ref, sh_ref, h_ref):
    tm = x_ref.shape[0]
    step = min(NORM_ROWS, tm)
    trips = tm // step
    per_row = sc_ref.shape[1] != 1
    if not per_row:
        gain, shift = g_ref[0] * (1.0 + sc_ref[0]), sh_ref[0]

    def body(c, carry):
        rows = pl.ds(pl.multiple_of(c * step, step), step)
        x = x_ref[rows, :]
        y = x * lax.rsqrt(jnp.mean(x * x, axis=-1, keepdims=True) + EPS)
        if per_row:
            y = y * (g_ref[0] * (1.0 + sc_ref[0, rows, :])) + sh_ref[0, rows, :]
        else:
            y = y * gain + shift
        h_ref[rows, :] = y.astype(h_ref.dtype)
        return carry

    lax.fori_loop(0, trips, body, 0, unroll=NORM_UNROLL if trips % NORM_UNROLL == 0 else 1)


def _silu(x):
    return x * jax.nn.sigmoid(x)


def _pack_halves(y):
    half = y.shape[1] // 2
    hi = lax.bitcast_convert_type(y[:, :half].astype(jnp.bfloat16).astype(F32), jnp.uint32)
    lo = lax.bitcast_convert_type(y[:, half:].astype(jnp.bfloat16).astype(F32), jnp.uint32)
    return hi | (lo >> 16)


def _unpack_halves(u):
    return (lax.bitcast_convert_type(u & jnp.uint32(0xFFFF0000), F32),
            lax.bitcast_convert_type(u << 16, F32))


def _ada_kernel(cp_ref, cs_ref, w_ref, b_ref, op_ref, os_ref):
    w = w_ref[0].astype(BF16)
    for c_ref, o_ref in ((cp_ref, op_ref), (cs_ref, os_ref)):
        o_ref[0] = jnp.dot(_silu(c_ref[...]).astype(BF16), w, preferred_element_type=F32) + b_ref[0]


def _ada(c_p, c_s, w_ada, b_ada):
    tn = 1024
    width = N_MOD * D_MODEL
    row_spec = lambda c: pl.BlockSpec(c.shape, lambda l, n: (0, 0))
    out_spec = lambda c: pl.BlockSpec((1, c.shape[0], tn), lambda l, n: (l, 0, n))
    return pl.pallas_call(
        _ada_kernel,
        grid=(DEPTH, width // tn),
        in_specs=[row_spec(c_p), row_spec(c_s),
                  pl.BlockSpec((1, D_MODEL, tn), lambda l, n: (l, 0, n)),
                  pl.BlockSpec((1, 1, tn), lambda l, n: (l, 0, n))],
        out_specs=[out_spec(c_p), out_spec(c_s)],
        out_shape=[jax.ShapeDtypeStruct((DEPTH, c.shape[0], width), F32) for c in (c_p, c_s)],
        compiler_params=_params("arbitrary", "arbitrary"),
        name="ada",
    )(c_p, c_s, w_ada, b_ada.reshape(DEPTH, 1, width))


def _inproj_kernel(x_ref, sc_ref, sh_ref, g_ref, w_ref, cos_ref, sin_ref, o_ref, u_ref, h_scr):
    n = pl.program_id(1)
    tm = x_ref.shape[0]
    rc = min(ROW_CHUNK, tm)

    @pl.when(n == 0)
    def _():
        _norm_rows(x_ref, g_ref, sc_ref, sh_ref, h_scr)

    def project(epilogue):
        w = w_ref[0].astype(BF16)
        for r in range(0, tm, rc):
            rows = slice(r, r + rc)
            epilogue(jnp.dot(h_scr[rows, :], w, preferred_element_type=F32), rows)

    def store(acc, rows):
        o_ref[rows, :] = acc.astype(o_ref.dtype)

    def rope(scale):
        def epilogue(acc, rows):
            for j in range(TN // RET_DK):
                cols = slice(j * RET_DK, (j + 1) * RET_DK)
                xh = acc[:, cols]
                r = xh * cos_ref[rows, :] + pltpu.roll(xh, RET_DK // 2, 1) * sin_ref[rows, :]
                if scale != 1.0:
                    r = r * scale
                o_ref[rows, cols] = r.astype(o_ref.dtype)
        project(epilogue)

    @pl.when(n < O_K // TN)
    def _():
        rope(1.0)

    @pl.when(jnp.logical_and(n >= O_K // TN, n < O_V // TN))
    def _():
        rope(RET_DK ** -0.5)

    @pl.when(jnp.logical_and(n >= O_V // TN, n < O_G // TN))
    def _():
        project(store)

    @pl.when(jnp.logical_and(n >= O_G // TN, n < O_U // TN))
    def _():
        project(lambda acc, rows: store(_silu(acc), rows))

    @pl.when(jnp.logical_and(n >= O_U // TN, n < O_M1 // TN))
    def _():
        def epilogue(acc, rows):
            store(acc, rows)
            u_ref[rows, :] = acc
        project(epilogue)

    @pl.when(n >= O_M1 // TN)
    def _():
        project(lambda acc, rows: store(jax.nn.sigmoid(acc), rows))


def _inproj(tok, layer, x, g_mix, w_in, rope_tabs, out_dtype):
    tm = tok.tm
    n_u = D_POOL // TN
    mod3 = tok.mod
    proj, u = pl.pallas_call(
        _inproj_kernel,
        grid=(tok.n_tiles, N_IN // TN),
        in_specs=[pl.BlockSpec((tm, D_MODEL), lambda m, n: (m, 0)),
                  tok.mod_spec(layer, 1), tok.mod_spec(layer, 0),
                  pl.BlockSpec((1, 1, D_MODEL), lambda m, n: (layer, 0, 0)),
                  pl.BlockSpec((1, D_MODEL, TN), lambda m, n: (layer, 0, n)),
                  tok.rope_spec(), tok.rope_spec()],
        out_specs=[pl.BlockSpec((tm, TN), lambda m, n: (m, n)),
                   pl.BlockSpec((tm, TN), lambda m, n: (m, jnp.clip(n - O_U // TN, 0, n_u - 1)))],
        out_shape=[jax.ShapeDtypeStruct((tok.rows, N_IN), out_dtype),
                   jax.ShapeDtypeStruct((tok.rows, D_POOL), F32)],
        scratch_shapes=[pltpu.VMEM((tm, D_MODEL), BF16)],
        compiler_params=_params("arbitrary", "arbitrary"),
        name="inproj",
    )(x, mod3, mod3, g_mix, w_in, *rope_tabs)
    return proj, u


def _head_norm_gate(o, gret, gate):
    mu = jnp.mean(o, axis=-1, keepdims=True)
    d = o - mu
    var = jnp.mean(d * d, axis=-1, keepdims=True)
    return gate.astype(F32) * (d * lax.rsqrt(var + EPS) * gret)


def _ret_prompt_kernel(q_ref, k_ref, v_ref, g_ref, gret_ref, dmask_ref, qdec_ref, kdec_ref, cdec_ref,
                       *rest):
    y_ref, s_ref = rest[-2:]

    @pl.when(pl.program_id(1) == 0)
    def _():
        s_ref[...] = jnp.zeros_like(s_ref)

    for h in range(RET_HEADS):
        ks, vs = slice(h * RET_DK, (h + 1) * RET_DK), slice(h * RET_DV, (h + 1) * RET_DV)
        q, k, v = q_ref[:, ks], k_ref[:, ks], v_ref[:, vs]
        state = s_ref[0, 0, h]
        scores = lax.dot_general(q, k, (((1,), (1,)), ((), ())), preferred_element_type=F32) * dmask_ref[h]
        inner = jnp.dot(scores.astype(BF16), v, preferred_element_type=F32)
        cross = jnp.dot(q, state.astype(BF16), preferred_element_type=F32) * qdec_ref[h]
        k_decayed = (k.astype(F32) * kdec_ref[h]).astype(BF16)
        s_ref[0, 0, h] = cdec_ref[h] * state + lax.dot_general(
            k_decayed, v, (((0,), (0,)), ((), ())), preferred_element_type=F32)
        y_ref[:, vs] = _head_norm_gate(inner + cross, gret_ref[0, :, vs], g_ref[:, vs]).astype(y_ref.dtype)


def _table_specs(tables):
    return [pl.BlockSpec(t.shape, lambda *_: (0, 0, 0)) for t in tables]


def _ret_prompt(tok, layer, proj, g_ret, s_all):
    n_b, seq = tok.n_batch, tok.seq
    c = RET_CHUNK
    assert seq % c == 0
    nc = seq // c
    tables = _retention_tables(c)
    in_specs = [pl.BlockSpec((c, QK_W), lambda b, i: (b * nc + i, O_Q // QK_W)),
                pl.BlockSpec((c, QK_W), lambda b, i: (b * nc + i, O_K // QK_W)),
                pl.BlockSpec((c, V_W), lambda b, i: (b * nc + i, O_V // V_W)),
                pl.BlockSpec((c, V_W), lambda b, i: (b * nc + i, O_G // V_W)),
                pl.BlockSpec((1, 1, V_W), lambda b, i: (layer, 0, 0))] + _table_specs(tables)
    args = [proj, proj, proj, proj, g_ret, *tables]
    aliases = {}
    if s_all is not None:
        aliases = {len(args): 1}
        in_specs.append(pl.BlockSpec(memory_space=pl.ANY))
        args.append(s_all)
    y, s_all = pl.pallas_call(
        _ret_prompt_kernel,
        grid=(n_b, nc),
        in_specs=in_specs,
        out_specs=[pl.BlockSpec((c, V_W), lambda b, i: (b * nc + i, 0)),
                   pl.BlockSpec((1, 1, RET_HEADS, RET_DK, RET_DV), lambda b, i: (layer, b, 0, 0, 0))],
        out_shape=[jax.ShapeDtypeStruct((tok.rows, V_W), BF16),
                   jax.ShapeDtypeStruct((DEPTH, n_b, RET_HEADS, RET_DK, RET_DV), F32)],
        input_output_aliases=aliases,
        compiler_params=_params("arbitrary", "arbitrary"),
        name="ret_prompt",
    )(*args)
    return y, s_all


def _ret_sample_kernel(q_ref, k_ref, v_ref, g_ref, kt_ref, gret_ref, s0_ref, dm_ref, qdec_ref, kdect_ref, cdec_ref,
                       *rest, bb, seq):
    y_ref, s_ref = rest[-2:]
    hs = RET_HEADS * seq
    reps = lambda a: jnp.concatenate([a] * RET_HEADS, axis=0)
    mask_q = (lax.broadcasted_iota(jnp.int32, (hs, QK_W), 0) // seq
              == lax.broadcasted_iota(jnp.int32, (hs, QK_W), 1) // RET_DK)
    mask_v = (lax.broadcasted_iota(jnp.int32, (hs, V_W), 0) // seq
              == lax.broadcasted_iota(jnp.int32, (hs, V_W), 1) // RET_DV)
    nt = (((1,), (1,)), ((), ()))
    for b in range(bb):
        rows_b = slice(b * seq, (b + 1) * seq)
        q, k, v = q_ref[rows_b, :], k_ref[rows_b, :], v_ref[rows_b, :]
        q_blk = jnp.where(mask_q, reps(q), 0.0).astype(BF16)
        scores = lax.dot_general(q_blk, reps(k).astype(BF16), nt, preferred_element_type=F32) * dm_ref[...]
        v_stack = jnp.concatenate([v[:, h * RET_DV:(h + 1) * RET_DV] for h in range(RET_HEADS)], axis=0)
        inner = jnp.dot(scores.astype(BF16), v_stack.astype(BF16), preferred_element_type=F32)
        state = s0_ref[0, b]
        cross = jnp.dot(q_blk, state.reshape(QK_W, RET_DV).astype(BF16), preferred_element_type=F32)
        o = inner + cross * qdec_ref[...]
        mu = jnp.mean(o, axis=-1, keepdims=True)
        d = o - mu
        normed = d * lax.rsqrt(jnp.mean(d * d, axis=-1, keepdims=True) + EPS)
        k_dec_t = (kt_ref[b] * kdect_ref[...]).astype(BF16)
        v_blk = jnp.where(mask_v, reps(v), 0.0).astype(BF16)
        upd = jnp.dot(k_dec_t, v_blk, preferred_element_type=F32)
        for h in range(RET_HEADS):
            vs = slice(h * RET_DV, (h + 1) * RET_DV)
            y_ref[rows_b, vs] = g_ref[rows_b, vs] * (normed[h * seq:(h + 1) * seq] * gret_ref[0, :, vs])
            s_ref[0, b, h] = cdec_ref[h] * state[h] + upd[:, vs]


def _ret_sample(tok, layer, proj, g_ret, state_ret, s_all):
    n_b, seq = tok.n_batch, tok.seq
    bb = 4
    assert n_b % bb == 0
    hs = RET_HEADS * seq
    dmask, q_dec, k_dec, chunk_dec = _retention_tables(seq)
    dm_blk = (dmask[:, :, None, :] * jnp.eye(RET_HEADS, dtype=F32)[:, None, :, None]).reshape(hs, hs)
    tables = [dm_blk, q_dec.reshape(hs, RET_DV), k_dec[:, :, 0].reshape(1, hs), chunk_dec]
    k_t = proj[:, O_K:O_K + QK_W].reshape(n_b, seq, RET_HEADS, RET_DK).transpose(0, 3, 2, 1).reshape(n_b, RET_DK, hs)
    state_spec = pl.BlockSpec((1, bb, RET_HEADS, RET_DK, RET_DV), lambda i: (layer, i, 0, 0, 0))
    in_specs = [pl.BlockSpec((bb * seq, QK_W), lambda i: (i, O_Q // QK_W)),
                pl.BlockSpec((bb * seq, QK_W), lambda i: (i, O_K // QK_W)),
                pl.BlockSpec((bb * seq, V_W), lambda i: (i, O_V // V_W)),
                pl.BlockSpec((bb * seq, V_W), lambda i: (i, O_G // V_W)),
                pl.BlockSpec((bb, RET_DK, hs), lambda i: (i, 0, 0)),
                pl.BlockSpec((1, 1, V_W), lambda i: (layer, 0, 0)),
                state_spec] + [pl.BlockSpec(t.shape, lambda i, nd=t.ndim: (0,) * nd) for t in tables]
    args = [proj, proj, proj, proj, k_t, g_ret, state_ret, *tables]
    aliases = {}
    if s_all is not None:
        aliases = {len(args): 1}
        in_specs.append(pl.BlockSpec(memory_space=pl.ANY))
        args.append(s_all)
    y, s_all = pl.pallas_call(
        functools.partial(_ret_sample_kernel, bb=bb, seq=seq),
        grid=(n_b // bb,),
        in_specs=in_specs,
        out_specs=[pl.BlockSpec((bb * seq, V_W), lambda i: (i, 0)), state_spec],
        out_shape=[jax.ShapeDtypeStruct((tok.rows, V_W), F32),
                   jax.ShapeDtypeStruct(state_ret.shape, F32)],
        input_output_aliases=aliases,
        compiler_params=_params("arbitrary"),
        name="ret_sample",
    )(*args)
    return y, s_all


def _pool_map(pooled, wmap_ref, scale_ref, gi):
    cols = slice(gi * POOL_GW, (gi + 1) * POOL_GW)
    mixed = jnp.dot(pooled.astype(BF16), wmap_ref[0, gi].astype(BF16), preferred_element_type=F32)
    return mixed * scale_ref[0, :, cols]


def _pool_prompt_kernel(u_ref, halo_ref, wmap_ref, scale_ref, o_ref, uf_scr, *, tl):
    first = pl.program_id(1) == 0
    uf_scr[0:POOL_HALO, :] = jnp.where(first, 0.0, halo_ref[...])
    uf_scr[POOL_HALO:, :] = u_ref[...]
    row = lax.broadcasted_iota(jnp.int32, (tl, POOL_GW), 0) + pl.program_id(1) * tl
    for gi, w in enumerate(POOL_WINDOWS):
        cols = slice(gi * POOL_GW, (gi + 1) * POOL_GW)
        u = uf_scr[POOL_HALO:, cols]
        acc = u
        for j in range(1, w):
            acc = acc + uf_scr[POOL_HALO - j:POOL_HALO - j + tl, cols]
        cnt = jnp.minimum(w, row + 1).astype(F32)
        o_ref[:, cols] = _pool_map(acc / cnt - u, wmap_ref, scale_ref, gi).astype(o_ref.dtype)


def _pool_prompt(tok, layer, u, w_pool_map, pool_scale):
    n_b, seq = tok.n_batch, tok.seq
    tl = min(TL_POOL, seq)
    assert seq % tl == 0 and tl % POOL_HALO == 0
    nl = seq // tl
    per = tl // POOL_HALO
    return pl.pallas_call(
        functools.partial(_pool_prompt_kernel, tl=tl),
        grid=(n_b, nl),
        in_specs=[pl.BlockSpec((tl, D_POOL), lambda b, l: (b * nl + l, 0)),
                  pl.BlockSpec((POOL_HALO, D_POOL), lambda b, l: (jnp.maximum((b * nl + l) * per - 1, 0), 0)),
                  pl.BlockSpec((1,) + w_pool_map.shape[1:], lambda b, l: (layer, 0, 0, 0)),
                  pl.BlockSpec((1, 1, D_POOL), lambda b, l: (layer, 0, 0))],
        out_specs=pl.BlockSpec((tl, D_POOL), lambda b, l: (b * nl + l, 0)),
        out_shape=jax.ShapeDtypeStruct((tok.rows, D_POOL), BF16),
        scratch_shapes=[pltpu.VMEM((POOL_HALO + tl, D_POOL), F32)],
        compiler_params=_params("arbitrary", "arbitrary"),
        name="pool_prompt",
    )(u, u, w_pool_map, pool_scale)


def _pool_sample_kernel(buf_ref, u_ref, wmap_ref, scale_ref, o_ref, *, seq, n_b):
    def row(j, cols):
        return buf_ref[j, :, cols] if j < POOL_BUF else u_ref[j - POOL_BUF, :, cols]

    for gi, w in enumerate(POOL_WINDOWS):
        cols = slice(gi * POOL_GW, (gi + 1) * POOL_GW)
        for t in range(seq):
            acc = row(POOL_BUF + t, cols)
            for j in range(1, w):
                acc = acc + row(POOL_BUF + t - j, cols)
            pooled = acc / float(w) - row(POOL_BUF + t, cols)
            o_ref[t, :, cols] = _pool_map(pooled, wmap_ref, scale_ref, gi)


def _pool_sample(tok, layer, u, buf_tm, w_pool_map, pool_scale):
    n_b, seq = tok.n_batch, tok.seq
    assert tok.n_past >= max(POOL_WINDOWS)
    u_tm = u.reshape(n_b, seq, D_POOL).transpose(1, 0, 2)
    mixed_tm = pl.pallas_call(
        functools.partial(_pool_sample_kernel, seq=seq, n_b=n_b),
        grid=(1,),
        in_specs=[pl.BlockSpec(buf_tm.shape, lambda i: (0, 0, 0)),
                  pl.BlockSpec(u_tm.shape, lambda i: (0, 0, 0)),
                  pl.BlockSpec((1,) + w_pool_map.shape[1:], lambda i: (layer, 0, 0, 0)),
                  pl.BlockSpec((1, 1, D_POOL), lambda i: (layer, 0, 0))],
        out_specs=pl.BlockSpec(u_tm.shape, lambda i: (0, 0, 0)),
        out_shape=jax.ShapeDtypeStruct(u_tm.shape, F32),
        compiler_params=_params("arbitrary"),
        name="pool_sample",
    )(buf_tm, u_tm, w_pool_map, pool_scale)
    return mixed_tm.transpose(1, 0, 2).reshape(tok.rows, D_POOL)


def _merge_kernel(yret_ref, mixed_ref, wr_ref, wp_ref, g1_ref, g2_ref, o_ref):
    ret = jnp.dot(yret_ref[...].astype(BF16), wr_ref[0].astype(BF16), preferred_element_type=F32)
    pool = jnp.dot(mixed_ref[...].astype(BF16), wp_ref[0].astype(BF16), preferred_element_type=F32)
    o_ref[...] = (g1_ref[...].astype(F32) * ret + g2_ref[...].astype(F32) * pool).astype(o_ref.dtype)


def _merge(tok, layer, yret, mixed, proj, w_ret_out, w_pool_out):
    tm = tok.tm
    return pl.pallas_call(
        _merge_kernel,
        grid=(tok.n_tiles, D_MODEL // TN),
        in_specs=[pl.BlockSpec((tm, V_W), lambda m, n: (m, 0)),
                  pl.BlockSpec((tm, D_POOL), lambda m, n: (m, 0)),
                  pl.BlockSpec((1, V_W, TN), lambda m, n: (layer, 0, n)),
                  pl.BlockSpec((1, D_POOL, TN), lambda m, n: (layer, 0, n)),
                  pl.BlockSpec((tm, TN), lambda m, n: (m, O_M1 // TN + n)),
                  pl.BlockSpec((tm, TN), lambda m, n: (m, O_M2 // TN + n))],
        out_specs=pl.BlockSpec((tm, TN), lambda m, n: (m, n)),
        out_shape=jax.ShapeDtypeStruct((tok.rows, D_MODEL), BF16),
        compiler_params=_params("arbitrary", "arbitrary"),
        name="merge",
    )(yret, mixed, w_ret_out, w_pool_out, proj, proj)


def _wo_kernel(merged_ref, w_ref, x_ref, gt_ref, o_ref):
    mix = jnp.dot(merged_ref[...], w_ref[0].astype(BF16), preferred_element_type=F32)
    o_ref[...] = x_ref[...] + (1.0 + gt_ref[0]) * mix


def _wo(tok, layer, merged, x, w_o):
    tok = tok.with_tm(TM_OUT)
    tm = tok.tm
    mod3 = tok.mod
    return pl.pallas_call(
        _wo_kernel,
        grid=(tok.n_tiles, D_MODEL // TN),
        in_specs=[pl.BlockSpec((tm, D_MODEL), lambda m, n: (m, 0), pipeline_mode=pl.Buffered(1)),
                  pl.BlockSpec((1, D_MODEL, TN), lambda m, n: (layer, 0, n)),
                  pl.BlockSpec((tm, TN), lambda m, n: (m, n)),
                  tok.mod_spec(layer, 2, tn=TN, col_axis=True)],
        out_specs=pl.BlockSpec((tm, TN), lambda m, n: (m, n)),
        out_shape=jax.ShapeDtypeStruct((tok.rows, D_MODEL), F32),
        compiler_params=_params("arbitrary", "arbitrary"),
        name="wo",
    )(merged, w_o, x, mod3)


def _mat(w_ref):
    return w_ref[(0,) * (len(w_ref.shape) - 2)]


def _final_norm(y, gfin_ref):
    return y * lax.rsqrt(jnp.mean(y * y, axis=-1, keepdims=True) + EPS) * gfin_ref[...]


def _ffn_kernel(x_ref, sc_ref, sh_ref, gt_ref, g_ref, w1_ref, w3_ref, w2_ref, *rest, n_f, with_final):
    gfin_ref = rest[0] if with_final else None
    o_ref, h_scr = rest[-2:]
    f = pl.program_id(1)

    @pl.when(f == 0)
    def _():
        _norm_rows(x_ref, g_ref, sc_ref, sh_ref, h_scr)
        o_ref[...] = jnp.zeros_like(o_ref)

    h = h_scr[...]
    a = jnp.dot(h, _mat(w1_ref).astype(BF16), preferred_element_type=F32)
    b = jnp.dot(h, _mat(w3_ref).astype(BF16), preferred_element_type=F32)
    o_ref[...] += jnp.dot((_silu(a) * b).astype(BF16), _mat(w2_ref).astype(BF16), preferred_element_type=F32)

    @pl.when(f == n_f - 1)
    def _():
        y = x_ref[...] + (1.0 + gt_ref[0]) * o_ref[...]
        o_ref[...] = _final_norm(y, gfin_ref) if with_final else y


def _ffn(tok, layer, x, g_ffn, w1, w3, w2, g_final=None):
    j = layer // 2
    d_ff = w1.shape[-1]
    tok_f = tok.with_tm(TM_FFN)
    tm, n_tiles = tok_f.tm, tok_f.n_tiles
    mod3 = tok_f.mod
    assert d_ff % TF == 0
    n_f = d_ff // TF
    once = pl.Buffered(1)
    in_specs = [pl.BlockSpec((tm, D_MODEL), lambda m, f: (m, 0), pipeline_mode=once),
                tok_f.mod_spec(layer, 4), tok_f.mod_spec(layer, 3), tok_f.mod_spec(layer, 5),
                pl.BlockSpec((1, 1, D_MODEL), lambda m, f: (layer, 0, 0)),
                pl.BlockSpec((1, D_MODEL, TF), lambda m, f: (j, 0, f)),
                pl.BlockSpec((1, D_MODEL, TF), lambda m, f: (j, 0, f)),
                pl.BlockSpec((1, TF, D_MODEL), lambda m, f: (j, f, 0))]
    args = [x, mod3, mod3, mod3, g_ffn, w1, w3, w2]
    if g_final is not None:
        in_specs.append(pl.BlockSpec((1, D_MODEL), lambda m, f: (0, 0)))
        args.append(g_final)
    return pl.pallas_call(
        functools.partial(_ffn_kernel, n_f=n_f, with_final=g_final is not None),
        grid=(n_tiles, n_f),
        in_specs=in_specs,
        out_specs=pl.BlockSpec((tm, D_MODEL), lambda m, f: (m, 0), pipeline_mode=once),
        out_shape=jax.ShapeDtypeStruct((tok.rows, D_MODEL), F32),
        scratch_shapes=[pltpu.VMEM((tm, D_MODEL), BF16)],
        compiler_params=_params("arbitrary", "arbitrary"),
        name="ffn",
    )(*args)


def _router_kernel(x_ref, sc_ref, sh_ref, g_ref, w_ref, b_ref, *rest):
    esel_ref, wsel_ref, h_ref = rest[-3:]
    h = _rmsnorm_mod(x_ref[...], g_ref[0], sc_ref[0], sh_ref[0])
    logits = jnp.dot(h.astype(BF16), w_ref[0].astype(BF16), preferred_element_type=F32) + b_ref[0]
    idx = lax.broadcasted_iota(jnp.int32, logits.shape, 1)
    top1 = jnp.max(logits, axis=-1, keepdims=True)
    i1 = jnp.min(jnp.where(logits == top1, idx, N_EXPERTS), axis=-1, keepdims=True)
    rest = jnp.where(idx == i1, -jnp.inf, logits)
    top2 = jnp.max(rest, axis=-1, keepdims=True)
    i2 = jnp.min(jnp.where(rest == top2, idx, N_EXPERTS), axis=-1, keepdims=True)
    e2 = jnp.exp(top2 - top1)
    denom = 1.0 + e2
    w1, w2 = 1.0 / denom, e2 / denom
    esel_ref[...] = jnp.where(idx == 0, i1, i2)
    wsel_ref[...] = jnp.where(idx == 0, w1, w2)
    h_ref[...] = _pack_halves(h)


def _router(tok, layer, x, g_ffn, w_router, b_router, h_all=None, row0=0, rows_all=None):
    j = layer // 2
    tm = tok.tm
    rows_all = tok.rows if rows_all is None else rows_all
    assert row0 % tm == 0
    small = pl.BlockSpec((tm, N_EXPERTS), lambda m: (m, 0))
    mod3 = tok.mod
    in_specs = [pl.BlockSpec((tm, D_MODEL), lambda m: (m, 0)),
                tok.mod_spec(layer, 4), tok.mod_spec(layer, 3),
                pl.BlockSpec((1, 1, D_MODEL), lambda m: (layer, 0, 0)),
                pl.BlockSpec((1, D_MODEL, N_EXPERTS), lambda m: (j, 0, 0)),
                pl.BlockSpec((1, 1, N_EXPERTS), lambda m: (j, 0, 0))]
    args = [x, mod3, mod3, g_ffn, w_router, b_router]
    aliases = {}
    if h_all is not None:
        aliases = {len(args): 2}
        in_specs.append(pl.BlockSpec(memory_space=pl.ANY))
        args.append(h_all)
    return pl.pallas_call(
        _router_kernel,
        grid=(tok.n_tiles,),
        in_specs=in_specs,
        out_specs=[small, small, pl.BlockSpec((tm, D_MODEL // 2), lambda m: (row0 // tm + m, 0))],
        out_shape=[jax.ShapeDtypeStruct((tok.rows, N_EXPERTS), jnp.int32),
                   jax.ShapeDtypeStruct((tok.rows, N_EXPERTS), F32),
                   jax.ShapeDtypeStruct((rows_all, D_MODEL // 2), jnp.uint32)],
        input_output_aliases=aliases,
        compiler_params=_params("arbitrary"),
        name="router",
    )(*args)


def _route_plan(choice, tm):
    rows = choice.shape[0]
    n_pairs = 2 * rows
    n_tiles = n_pairs // tm + N_EXPERTS
    e_flat = choice.T.reshape(n_pairs)
    onehot = (e_flat[:, None] == jnp.arange(N_EXPERTS, dtype=jnp.int32)[None, :]).astype(jnp.int32)
    csum = jnp.cumsum(onehot, axis=0)
    rank = jnp.sum(csum * onehot, axis=1) - 1
    count = csum[-1]
    tiles_per = (count + tm - 1) // tm
    rows_per = jnp.maximum((((count + jnp.maximum(tiles_per, 1) - 1) // jnp.maximum(tiles_per, 1)) + 7) // 8 * 8, 8)
    tile_end = jnp.cumsum(tiles_per)
    tile_start = tile_end - tiles_per
    pos = (tile_start[e_flat] + rank // rows_per[e_flat]) * tm + rank % rows_per[e_flat]
    token = jnp.arange(n_pairs, dtype=jnp.int32) % rows
    src_row = jnp.zeros((n_tiles * tm,), jnp.int32).at[pos].set(token, unique_indices=True, mode='promise_in_bounds')
    tile_id = jnp.arange(n_tiles, dtype=jnp.int32)
    n_used = tile_end[-1]
    tile_expert = jnp.minimum(jnp.sum(tile_id[:, None] >= tile_end[None, :], axis=1), N_EXPERTS - 1).astype(jnp.int32)
    tile_rows = jnp.clip(count[tile_expert] - (tile_id - tile_start[tile_expert]) * rows_per[tile_expert],
                         0, rows_per[tile_expert])
    tile_rows = jnp.where(tile_id < n_used, tile_rows, 0).astype(jnp.int32)
    last_expert = tile_expert[jnp.maximum(n_used - 1, 0)]
    tile_expert = jnp.where(tile_rows > 0, tile_expert, last_expert)
    return tile_expert, tile_rows, src_row, pos.astype(jnp.int32)


def _row_copy(src_hbm, row, dst, r, sem):
    return pltpu.make_async_copy(src_hbm.at[pl.ds(row, 1)], dst.at[pl.ds(r, 1)], sem)


def _moe_kernel(te_ref, nr_ref, src_ref, h_hbm, w1_ref, w3_ref, w2_ref, o_ref, rows_scr, h_scr, acc_scr, sem,
                *, tm, sub, n_f):
    del te_ref
    m, f = pl.program_id(0), pl.program_id(1)
    half = D_MODEL // 2
    group = 8

    def for_each_row(tile, fn):
        def body(i, c):
            for r8 in range(group):
                r = i * group + r8
                fn(_row_copy(h_hbm, src_ref[tile * tm + r], rows_scr, r, sem))
            return c
        lax.fori_loop(0, (nr_ref[tile] + group - 1) // group, body, 0)

    @pl.when(f == 0)
    def _():
        acc_scr[...] = jnp.zeros_like(acc_scr)

        @pl.when(m == 0)
        def _():
            rows_scr[...] = jnp.zeros_like(rows_scr)
            for_each_row(0, lambda cp: cp.start())

        for_each_row(m, lambda cp: cp.wait())
        first, second = _unpack_halves(rows_scr[...])
        h_scr[:, :half] = first.astype(BF16)
        h_scr[:, half:] = second.astype(BF16)

        @pl.when(m + 1 < pl.num_programs(0))
        def _():
            for_each_row(m + 1, lambda cp: cp.start())

    n_sub = (nr_ref[m] + sub - 1) // sub
    for n in range(1, tm // sub + 1):
        @pl.when(n_sub == n)
        def _(n=n):
            h = h_scr[0:n * sub, :]
            a = jnp.dot(h, _mat(w1_ref).astype(BF16), preferred_element_type=F32)
            b = jnp.dot(h, _mat(w3_ref).astype(BF16), preferred_element_type=F32)
            acc_scr[0:n * sub, :] += jnp.dot((_silu(a) * b).astype(BF16), _mat(w2_ref).astype(BF16),
                                             preferred_element_type=F32)

    @pl.when(f == n_f - 1)
    def _():
        o_ref[...] = _pack_halves(acc_scr[...])


def _moe_sparse(layer, h, plan, w1, w3, w2, tm):
    j = layer // 2
    tile_expert, tile_rows, src_row, _ = plan
    n_tiles = tile_expert.shape[0]
    d_ff = w1.shape[3]
    assert d_ff % TF == 0 and tm % SUB_MOE == 0
    n_f = d_ff // TF

    def f_sel(m, f, nr):
        return jnp.where(nr[m] > 0, f, n_f - 1)

    return pl.pallas_call(
        functools.partial(_moe_kernel, tm=tm, sub=SUB_MOE, n_f=n_f),
        grid_spec=pltpu.PrefetchScalarGridSpec(
            num_scalar_prefetch=3,
            grid=(n_tiles, n_f),
            in_specs=[pl.BlockSpec(memory_space=pl.ANY),
                      pl.BlockSpec((1, 1, D_MODEL, TF), lambda m, f, te, nr, sr: (j, te[m], 0, f_sel(m, f, nr))),
                      pl.BlockSpec((1, 1, D_MODEL, TF), lambda m, f, te, nr, sr: (j, te[m], 0, f_sel(m, f, nr))),
                      pl.BlockSpec((1, 1, TF, D_MODEL), lambda m, f, te, nr, sr: (j, te[m], f_sel(m, f, nr), 0))],
            out_specs=pl.BlockSpec((tm, D_MODEL // 2), lambda m, f, te, nr, sr: (m, 0)),
            scratch_shapes=[pltpu.VMEM((tm, D_MODEL // 2), jnp.uint32), pltpu.VMEM((tm, D_MODEL), BF16),
                            pltpu.VMEM((tm, D_MODEL), F32), pltpu.SemaphoreType.DMA(())]),
        out_shape=jax.ShapeDtypeStruct((n_tiles * tm, D_MODEL // 2), jnp.uint32),
        compiler_params=_params("arbitrary", "arbitrary"),
        name="moe",
    )(tile_expert, tile_rows, src_row, h, w1, w3, w2)


def _combine_kernel(pos_ref, y_hbm, x_ref, gt_ref, w_ref, *rest, tm, row0, rows_all, with_final):
    gfin_ref = rest[0] if with_final else None
    o_ref, y_scr, sems = rest[-3:]
    i, n = pl.program_id(0), pl.num_programs(0)

    def for_each_row(tile, slot, fn):
        def body(r, c):
            for k in range(2):
                slot_of_pair = pos_ref[k * rows_all + row0 + tile * tm + r]
                fn(_row_copy(y_hbm, slot_of_pair, y_scr.at[slot, k], r, sems.at[slot]))
            return c
        lax.fori_loop(0, tm, body, 0, unroll=8)

    @pl.when(i == 0)
    def _():
        for_each_row(0, 0, lambda cp: cp.start())

    @pl.when(i + 1 < n)
    def _():
        for_each_row(i + 1, (i + 1) % 2, lambda cp: cp.start())

    slot = i % 2
    for_each_row(i, slot, lambda cp: cp.wait())
    w = w_ref[...]
    first0, second0 = _unpack_halves(y_scr[slot, 0])
    first1, second1 = _unpack_halves(y_scr[slot, 1])
    f = jnp.concatenate([w[:, 0:1] * first0 + w[:, 1:2] * first1,
                         w[:, 0:1] * second0 + w[:, 1:2] * second1], axis=1)
    y = x_ref[...] + (1.0 + gt_ref[0]) * f
    o_ref[...] = _final_norm(y, gfin_ref) if with_final else y


def _combine(tok, layer, y_sorted, pos, row0, x, wsel, g_final):
    tok_c = tok.with_tm(TM_COMBINE)
    tm = tok_c.tm
    mod3 = tok_c.mod
    in_specs = [pl.BlockSpec(memory_space=pl.ANY),
                pl.BlockSpec((tm, D_MODEL), lambda m, pos: (m, 0)),
                tok_c.mod_spec(layer, 5),
                pl.BlockSpec((tm, N_EXPERTS), lambda m, pos: (m, 0))]
    args = [pos, y_sorted, x, mod3, wsel]
    if g_final is not None:
        in_specs.append(pl.BlockSpec((1, D_MODEL), lambda m, pos: (0, 0)))
        args.append(g_final)
    return pl.pallas_call(
        functools.partial(_combine_kernel, tm=tm, row0=row0, rows_all=pos.shape[0] // 2,
                          with_final=g_final is not None),
        grid_spec=pltpu.PrefetchScalarGridSpec(
            num_scalar_prefetch=1,
            grid=(tok_c.n_tiles,),
            in_specs=in_specs,
            out_specs=pl.BlockSpec((tm, D_MODEL), lambda m, pos: (m, 0)),
            scratch_shapes=[pltpu.VMEM((2, 2, tm, D_MODEL // 2), jnp.uint32), pltpu.SemaphoreType.DMA((2,))]),
        out_shape=jax.ShapeDtypeStruct((tok.rows, D_MODEL), F32),
        compiler_params=_params("arbitrary"),
        name="combine",
    )(*args)


class _Stream:
    def __init__(self, tok, x, ret_state, pool_state):
        self.tok, self.x, self.ret_state, self.pool_state = tok, x, ret_state, pool_state
        self.rope_tabs = tok.rope_tables()
        self.new_state = None
        self.u_layers = []


def _mixer_layer(s, i, p):
    tok = s.tok
    is_prompt = s.ret_state is None
    proj, u = _inproj(tok, i, s.x, p['g_mix'], p['w_in'], s.rope_tabs, BF16 if is_prompt else F32)
    s.u_layers.append(u)
    if is_prompt:
        yret, s.new_state = _ret_prompt(tok, i, proj, p['g_ret'], s.new_state)
        mixed = _pool_prompt(tok, i, u, p['w_pool_map'], p['pool_scale'])
    else:
        yret, s.new_state = _ret_sample(tok, i, proj, p['g_ret'], s.ret_state, s.new_state)
        mixed = _pool_sample(tok, i, u, s.pool_state[i].transpose(1, 0, 2), p['w_pool_map'], p['pool_scale'])
    merged = _merge(tok, i, yret, mixed, proj, p['w_ret_out'], p['w_pool_out'])
    s.x = _wo(tok, i, merged, s.x, p['w_o'])


def _moe_layer(streams, i, p, g_final):
    rows_all = sum(s.tok.rows for s in streams)
    h_all, row0, choices, weights, starts = None, 0, [], [], []
    for s in streams:
        esel, wsel, h_all = _router(s.tok, i, s.x, p['g_ffn'], p['w_router'], p['b_router'],
                                    h_all=h_all, row0=row0, rows_all=rows_all)
        choices.append(esel[:, :2])
        weights.append(wsel)
        starts.append(row0)
        row0 += s.tok.rows
    plan = _route_plan(jnp.concatenate(choices, axis=0), TM_MOE)
    y_sorted = _moe_sparse(i, h_all, plan, p['w1_moe'], p['w3_moe'], p['w2_moe'], TM_MOE)
    for s, wsel, start in zip(streams, weights, starts):
        s.x = _combine(s.tok, i, y_sorted, plan[3], start, s.x, wsel, g_final)


def kernel(x_prompt, x_sample, c_prompt, c_sample, state_ret, state_pool, w_ada, b_ada, g_mix, w_in, g_ret,
           w_ret_out, w_pool_map, pool_scale, w_pool_out, w_o, g_ffn, w1_dense, w3_dense, w2_dense, w_router,
           b_router, w1_moe, w3_moe, w2_moe, g_final):
    row3 = lambda a: a.reshape(a.shape[0], 1, a.shape[1])
    p = dict(g_mix=row3(g_mix), w_in=w_in, g_ret=row3(g_ret), w_ret_out=w_ret_out, w_pool_map=w_pool_map,
             pool_scale=row3(pool_scale), w_pool_out=w_pool_out, w_o=w_o, g_ffn=row3(g_ffn),
             w1_dense=w1_dense, w3_dense=w3_dense, w2_dense=w2_dense, w_router=w_router, b_router=row3(b_router),
             w1_moe=w1_moe, w3_moe=w3_moe, w2_moe=w2_moe, g_final=g_final.reshape(1, D_MODEL))
    n_bp, seq_p, _ = x_prompt.shape
    n_bs, seq_s, _ = x_sample.shape

    mod_p, mod_s = _ada(c_prompt, jnp.repeat(c_sample, seq_s, axis=0), w_ada, b_ada)

    tok_p = _Tokens(n_bp, seq_p, 0, TM_PROMPT, per_row_mod=False, mod=mod_p)
    tok_s = _Tokens(n_bs, seq_s, PAST_LEN, TM_SAMPLE, per_row_mod=True, mod=mod_s)
    prompt = _Stream(tok_p, x_prompt.reshape(tok_p.rows, D_MODEL), None, None)
    sample = _Stream(tok_s, x_sample.reshape(tok_s.rows, D_MODEL), state_ret, state_pool)
    streams = [prompt, sample]

    for i in range(DEPTH):
        g_fin = p['g_final'] if i == DEPTH - 1 else None
        for s in streams:
            _mixer_layer(s, i, p)
        if i % 2 == 0:
            for s in streams:
                s.x = _ffn(s.tok, i, s.x, p['g_ffn'], p['w1_dense'], p['w3_dense'], p['w2_dense'], g_final=g_fin)
        else:
            _moe_layer(streams, i, p, g_fin)

    buf_prompt = jnp.stack([u.reshape(n_bp, seq_p, D_POOL)[:, seq_p - POOL_BUF:] for u in prompt.u_layers])
    buf_sample = jnp.stack([
        jnp.concatenate([state_pool[i], sample.u_layers[i].reshape(n_bs, seq_s, D_POOL)], axis=1)[:, -POOL_BUF:]
        for i in range(DEPTH)])
    return (prompt.x.reshape(x_prompt.shape), sample.x.reshape(x_sample.shape), prompt.new_state, buf_prompt,
            sample.new_state, buf_sample)
```

```python
import functools

import numpy as np
import jax
import jax.numpy as jnp
from jax import lax
from jax.experimental import pallas as pl
from jax.experimental.pallas import tpu as pltpu

F32 = jnp.float32
BF16 = jnp.bfloat16

D_MODEL = 2048
DEPTH = 2
PAST_LEN = 16384
RET_HEADS = 8
RET_DK = 128
RET_DV = 256
RET_CHUNK = 128
ROPE_BASE = 10000.0
QK_W = RET_HEADS * RET_DK
V_W = RET_HEADS * RET_DV
POOL_WINDOWS = (2, 4, 8, 16)
D_POOL = 1024
POOL_GW = 256
POOL_BUF = 15
POOL_HALO = 16
N_IN = 2 * QK_W + 2 * V_W + D_POOL + 2 * D_MODEL
N_EXPERTS = 8
EPS = 1e-6
N_MOD = 6

O_Q, O_K, O_V, O_G, O_U = 0, QK_W, 2 * QK_W, 2 * QK_W + V_W, 2 * QK_W + 2 * V_W
O_M1 = O_U + D_POOL
O_M2 = O_M1 + D_MODEL

VMEM_LIMIT_BYTES = 56 * 1024 * 1024
TN = 512
TF = 256
TM_PROMPT = 1024
TM_SAMPLE = 512
TM_FFN = 1024
TM_OUT = 2048
ROW_CHUNK = 256
NORM_ROWS = 16
NORM_UNROLL = 16
TL_POOL = 512
TM_MOE = 1024
SUB_MOE = 256
TM_COMBINE = 256


def _params(*sem):
    return pltpu.CompilerParams(dimension_semantics=sem, vmem_limit_bytes=VMEM_LIMIT_BYTES)


class _Tokens:
    def __init__(self, n_batch, seq, n_past, tm, per_row_mod, mod=None):
        self.n_batch, self.seq, self.n_past = n_batch, seq, n_past
        self.rows = n_batch * seq
        self.tm = min(tm, self.rows)
        self.per_row_mod = per_row_mod
        if not per_row_mod:
            self.tm = min(self.tm, seq)
            assert seq % self.tm == 0
        assert self.rows % self.tm == 0
        self.n_tiles = self.rows // self.tm
        if mod is not None and not per_row_mod and mod.ndim == 3 and mod.shape[1] != 1:
            mod = mod.reshape(DEPTH * n_batch, 1, N_MOD * D_MODEL)
        self.mod = mod

    def with_tm(self, tm):
        return _Tokens(self.n_batch, self.seq, self.n_past, tm, self.per_row_mod, self.mod)

    def mod_spec(self, layer, chunk, tn=D_MODEL, col_axis=False):
        per_chunk = D_MODEL // tn
        col = (lambda n: chunk * per_chunk + n) if col_axis else (lambda n: chunk * per_chunk)
        if self.per_row_mod:
            if col_axis:
                return pl.BlockSpec((1, self.tm, tn), lambda m, n, *_: (layer, m, col(n)))
            return pl.BlockSpec((1, self.tm, tn), lambda m, *_: (layer, m, col(0)))
        tiles_per_batch = self.seq // self.tm
        base = layer * self.n_batch
        if col_axis:
            return pl.BlockSpec((1, 1, tn), lambda m, n, *_: (base + m // tiles_per_batch, 0, col(n)))
        return pl.BlockSpec((1, 1, tn), lambda m, *_: (base + m // tiles_per_batch, 0, col(0)))

    def rope_tables(self):
        half = RET_DK // 2
        pos = jnp.arange(self.seq, dtype=F32) + float(self.n_past)
        inv = ROPE_BASE ** (-jnp.arange(half, dtype=F32) / half)
        ang = pos[:, None] * inv[None, :]
        cos, sin = jnp.cos(ang), jnp.sin(ang)
        cos_full = jnp.concatenate([cos, cos], axis=1)
        sin_signed = jnp.concatenate([-sin, sin], axis=1)
        if self.per_row_mod:
            cos_full = jnp.tile(cos_full, (self.n_batch, 1))
            sin_signed = jnp.tile(sin_signed, (self.n_batch, 1))
        return cos_full, sin_signed

    def rope_spec(self):
        n_blocks = (self.rows if self.per_row_mod else self.seq) // self.tm
        return pl.BlockSpec((self.tm, RET_DK), lambda m, *_: (m % n_blocks, 0))


def _retention_tables(chunk):
    log_gamma = jnp.log1p(-jnp.exp2(-5.0 - jnp.arange(RET_HEADS, dtype=F32)))
    idx = jnp.arange(chunk, dtype=F32)
    rel = idx[:, None] - idx[None, :]
    dmask = jnp.where(rel[None] >= 0, jnp.exp(log_gamma[:, None, None] * jnp.maximum(rel, 0.0)[None]), 0.0)
    q_dec = jnp.exp(log_gamma[:, None] * (idx + 1.0)[None, :])
    k_dec = jnp.exp(log_gamma[:, None] * (chunk - 1.0 - idx)[None, :])
    chunk_dec = jnp.exp(log_gamma * chunk)
    q_dec = jnp.broadcast_to(q_dec[:, :, None], (RET_HEADS, chunk, RET_DV))
    k_dec = jnp.broadcast_to(k_dec[:, :, None], (RET_HEADS, chunk, RET_DK))
    chunk_dec = jnp.broadcast_to(chunk_dec[:, None, None], (RET_HEADS, 1, RET_DV))
    return dmask, q_dec, k_dec, chunk_dec


def _rmsnorm_mod(x, g, scale, shift):
    y = x * lax.rsqrt(jnp.mean(x * x, axis=-1, keepdims=True) + EPS) * g
    return y * (1.0 + scale) + shift


def _norm_rows(x_ref, g_ref, sc_ref, sh_ref, h_ref):
    tm = x_ref.shape[0]
    step = min(NORM_ROWS, tm)
    trips = tm // step
    per_row = sc_ref.shape[1] != 1
    if not per_row:
        gain, shift = g_ref[0] * (1.0 + sc_ref[0]), sh_ref[0]

    def body(c, carry):
        rows = pl.ds(pl.multiple_of(c * step, step), step)
        x = x_ref[rows, :]
        y = x * lax.rsqrt(jnp.mean(x * x, axis=-1, keepdims=True) + EPS)
        if per_row:
            y = y * (g_ref[0] * (1.0 + sc_ref[0, rows, :])) + sh_ref[0, rows, :]
        else:
            y = y * gain + shift
        h_ref[rows, :] = y.astype(h_ref.dtype)
        return carry

    lax.fori_loop(0, trips, body, 0, unroll=NORM_UNROLL if trips % NORM_UNROLL == 0 else 1)


def _silu(x):
    return x * jax.nn.sigmoid(x)


def _pack_halves(y):
    half = y.shape[1] // 2
    hi = lax.bitcast_convert_type(y[:, :half].astype(jnp.bfloat16).astype(F32), jnp.uint32)
    lo = lax.bitcast_convert_type(y[:, half:].astype(jnp.bfloat16).astype(F32), jnp.uint32)
    return hi | (lo >> 16)


def _unpack_halves(u):
    return (lax.bitcast_convert_type(u & jnp.uint32(0xFFFF0000), F32),
            lax.bitcast_convert_type(u << 16, F32))


def _ada_kernel(cp_ref, cs_ref, w_ref, b_ref, op_ref, os_ref):
    w = w_ref[0].astype(BF16)
    for c_ref, o_ref in ((cp_ref, op_ref), (cs_ref, os_ref)):
        o_ref[0] = jnp.dot(_silu(c_ref[...]).astype(BF16), w, preferred_element_type=F32) + b_ref[0]


def _ada(c_p, c_s, w_ada, b_ada):
    tn = 1024
    width = N_MOD * D_MODEL
    row_spec = lambda c: pl.BlockSpec(c.shape, lambda l, n: (0, 0))
    out_spec = lambda c: pl.BlockSpec((1, c.shape[0], tn), lambda l, n: (l, 0, n))
    return pl.pallas_call(
        _ada_kernel,
        grid=(DEPTH, width // tn),
        in_specs=[row_spec(c_p), row_spec(c_s),
                  pl.BlockSpec((1, D_MODEL, tn), lambda l, n: (l, 0, n)),
                  pl.BlockSpec((1, 1, tn), lambda l, n: (l, 0, n))],
        out_specs=[out_spec(c_p), out_spec(c_s)],
        out_shape=[jax.ShapeDtypeStruct((DEPTH, c.shape[0], width), F32) for c in (c_p, c_s)],
        compiler_params=_params("arbitrary", "arbitrary"),
        name="ada",
    )(c_p, c_s, w_ada, b_ada.reshape(DEPTH, 1, width))


def _inproj_kernel(x_ref, sc_ref, sh_ref, g_ref, w_ref, cos_ref, sin_ref, o_ref, u_ref, h_scr):
    n = pl.program_id(1)
    tm = x_ref.shape[0]
    rc = min(ROW_CHUNK, tm)

    @pl.when(n == 0)
    def _():
        _norm_rows(x_ref, g_ref, sc_ref, sh_ref, h_scr)

    def project(epilogue):
        w = w_ref[0].astype(BF16)
        for r in range(0, tm, rc):
            rows = slice(r, r + rc)
            epilogue(jnp.dot(h_scr[rows, :], w, preferred_element_type=F32), rows)

    def store(acc, rows):
        o_ref[rows, :] = acc.astype(o_ref.dtype)

    def rope(scale):
        def epilogue(acc, rows):
            for j in range(TN // RET_DK):
                cols = slice(j * RET_DK, (j + 1) * RET_DK)
                xh = acc[:, cols]
                r = xh * cos_ref[rows, :] + pltpu.roll(xh, RET_DK // 2, 1) * sin_ref[rows, :]
                if scale != 1.0:
                    r = r * scale
                o_ref[rows, cols] = r.astype(o_ref.dtype)
        project(epilogue)

    @pl.when(n < O_K // TN)
    def _():
        rope(1.0)

    @pl.when(jnp.logical_and(n >= O_K // TN, n < O_V // TN))
    def _():
        rope(RET_DK ** -0.5)

    @pl.when(jnp.logical_and(n >= O_V // TN, n < O_G // TN))
    def _():
        project(store)

    @pl.when(jnp.logical_and(n >= O_G // TN, n < O_U // TN))
    def _():
        project(lambda acc, rows: store(_silu(acc), rows))

    @pl.when(jnp.logical_and(n >= O_U // TN, n < O_M1 // TN))
    def _():
        def epilogue(acc, rows):
            store(acc, rows)
            u_ref[rows, :] = acc
        project(epilogue)

    @pl.when(n >= O_M1 // TN)
    def _():
        project(lambda acc, rows: store(jax.nn.sigmoid(acc), rows))


def _inproj(tok, layer, x, g_mix, w_in, rope_tabs, out_dtype):
    tm = tok.tm
    n_u = D_POOL // TN
    mod3 = tok.mod
    proj, u = pl.pallas_call(
        _inproj_kernel,
        grid=(tok.n_tiles, N_IN // TN),
        in_specs=[pl.BlockSpec((tm, D_MODEL), lambda m, n: (m, 0)),
                  tok.mod_spec(layer, 1), tok.mod_spec(layer, 0),
                  pl.BlockSpec((1, 1, D_MODEL), lambda m, n: (layer, 0, 0)),
                  pl.BlockSpec((1, D_MODEL, TN), lambda m, n: (layer, 0, n)),
                  tok.rope_spec(), tok.rope_spec()],
        out_specs=[pl.BlockSpec((tm, TN), lambda m, n: (m, n)),
                   pl.BlockSpec((tm, TN), lambda m, n: (m, jnp.clip(n - O_U // TN, 0, n_u - 1)))],
        out_shape=[jax.ShapeDtypeStruct((tok.rows, N_IN), out_dtype),
                   jax.ShapeDtypeStruct((tok.rows, D_POOL), F32)],
        scratch_shapes=[pltpu.VMEM((tm, D_MODEL), BF16)],
        compiler_params=_params("arbitrary", "arbitrary"),
        name="inproj",
    )(x, mod3, mod3, g_mix, w_in, *rope_tabs)
    return proj, u


def _head_norm_gate(o, gret, gate):
    mu = jnp.mean(o, axis=-1, keepdims=True)
    d = o - mu
    var = jnp.mean(d * d, axis=-1, keepdims=True)
    return gate.astype(F32) * (d * lax.rsqrt(var + EPS) * gret)


def _ret_prompt_kernel(q_ref, k_ref, v_ref, g_ref, gret_ref, dmask_ref, qdec_ref, kdec_ref, cdec_ref,
                       *rest):
    y_ref, s_ref = rest[-2:]

    @pl.when(pl.program_id(1) == 0)
    def _():
        s_ref[...] = jnp.zeros_like(s_ref)

    for h in range(RET_HEADS):
        ks, vs = slice(h * RET_DK, (h + 1) * RET_DK), slice(h * RET_DV, (h + 1) * RET_DV)
        q, k, v = q_ref[:, ks], k_ref[:, ks], v_ref[:, vs]
        state = s_ref[0, 0, h]
        scores = lax.dot_general(q, k, (((1,), (1,)), ((), ())), preferred_element_type=F32) * dmask_ref[h]
        inner = jnp.dot(scores.astype(BF16), v, preferred_element_type=F32)
        cross = jnp.dot(q, state.astype(BF16), preferred_element_type=F32) * qdec_ref[h]
        k_decayed = (k.astype(F32) * kdec_ref[h]).astype(BF16)
        s_ref[0, 0, h] = cdec_ref[h] * state + lax.dot_general(
            k_decayed, v, (((0,), (0,)), ((), ())), preferred_element_type=F32)
        y_ref[:, vs] = _head_norm_gate(inner + cross, gret_ref[0, :, vs], g_ref[:, vs]).astype(y_ref.dtype)


def _table_specs(tables):
    return [pl.BlockSpec(t.shape, lambda *_: (0, 0, 0)) for t in tables]


def _ret_prompt(tok, layer, proj, g_ret, s_all):
    n_b, seq = tok.n_batch, tok.seq
    c = RET_CHUNK
    assert seq % c == 0
    nc = seq // c
    tables = _retention_tables(c)
    in_specs = [pl.BlockSpec((c, QK_W), lambda b, i: (b * nc + i, O_Q // QK_W)),
                pl.BlockSpec((c, QK_W), lambda b, i: (b * nc + i, O_K // QK_W)),
                pl.BlockSpec((c, V_W), lambda b, i: (b * nc + i, O_V // V_W)),
                pl.BlockSpec((c, V_W), lambda b, i: (b * nc + i, O_G // V_W)),
                pl.BlockSpec((1, 1, V_W), lambda b, i: (layer, 0, 0))] + _table_specs(tables)
    args = [proj, proj, proj, proj, g_ret, *tables]
    aliases = {}
    if s_all is not None:
        aliases = {len(args): 1}
        in_specs.append(pl.BlockSpec(memory_space=pl.ANY))
        args.append(s_all)
    y, s_all = pl.pallas_call(
        _ret_prompt_kernel,
        grid=(n_b, nc),
        in_specs=in_specs,
        out_specs=[pl.BlockSpec((c, V_W), lambda b, i: (b * nc + i, 0)),
                   pl.BlockSpec((1, 1, RET_HEADS, RET_DK, RET_DV), lambda b, i: (layer, b, 0, 0, 0))],
        out_shape=[jax.ShapeDtypeStruct((tok.rows, V_W), BF16),
                   jax.ShapeDtypeStruct((DEPTH, n_b, RET_HEADS, RET_DK, RET_DV), F32)],
        input_output_aliases=aliases,
        compiler_params=_params("arbitrary", "arbitrary"),
        name="ret_prompt",
    )(*args)
    return y, s_all


def _ret_sample_kernel(q_ref, k_ref, v_ref, g_ref, kt_ref, gret_ref, s0_ref, dm_ref, qdec_ref, kdect_ref, cdec_ref,
                       *rest, bb, seq):
    y_ref, s_ref = rest[-2:]
    hs = RET_HEADS * seq
    reps = lambda a: jnp.concatenate([a] * RET_HEADS, axis=0)
    mask_q = (lax.broadcasted_iota(jnp.int32, (hs, QK_W), 0) // seq
              == lax.broadcasted_iota(jnp.int32, (hs, QK_W), 1) // RET_DK)
    mask_v = (lax.broadcasted_iota(jnp.int32, (hs, V_W), 0) // seq
              == lax.broadcasted_iota(jnp.int32, (hs, V_W), 1) // RET_DV)
    nt = (((1,), (1,)), ((), ()))
    for b in range(bb):
        rows_b = slice(b * seq, (b + 1) * seq)
        q, k, v = q_ref[rows_b, :], k_ref[rows_b, :], v_ref[rows_b, :]
        q_blk = jnp.where(mask_q, reps(q), 0.0).astype(BF16)
        scores = lax.dot_general(q_blk, reps(k).astype(BF16), nt, preferred_element_type=F32) * dm_ref[...]
        v_stack = jnp.concatenate([v[:, h * RET_DV:(h + 1) * RET_DV] for h in range(RET_HEADS)], axis=0)
        inner = jnp.dot(scores.astype(BF16), v_stack.astype(BF16), preferred_element_type=F32)
        state = s0_ref[0, b]
        cross = jnp.dot(q_blk, state.reshape(QK_W, RET_DV).astype(BF16), preferred_element_type=F32)
        o = inner + cross * qdec_ref[...]
        mu = jnp.mean(o, axis=-1, keepdims=True)
        d = o - mu
        normed = d * lax.rsqrt(jnp.mean(d * d, axis=-1, keepdims=True) + EPS)
        k_dec_t = (kt_ref[b] * kdect_ref[...]).astype(BF16)
        v_blk = jnp.where(mask_v, reps(v), 0.0).astype(BF16)
        upd = jnp.dot(k_dec_t, v_blk, preferred_element_type=F32)
        for h in range(RET_HEADS):
            vs = slice(h * RET_DV, (h + 1) * RET_DV)
            y_ref[rows_b, vs] = g_ref[rows_b, vs] * (normed[h * seq:(h + 1) * seq] * gret_ref[0, :, vs])
            s_ref[0, b, h] = cdec_ref[h] * state[h] + upd[:, vs]


def _ret_sample(tok, layer, proj, g_ret, state_ret, s_all):
    n_b, seq = tok.n_batch, tok.seq
    bb = 4
    assert n_b % bb == 0
    hs = RET_HEADS * seq
    dmask, q_dec, k_dec, chunk_dec = _retention_tables(seq)
    dm_blk = (dmask[:, :, None, :] * jnp.eye(RET_HEADS, dtype=F32)[:, None, :, None]).reshape(hs, hs)
    tables = [dm_blk, q_dec.reshape(hs, RET_DV), k_dec[:, :, 0].reshape(1, hs), chunk_dec]
    k_t = proj[:, O_K:O_K + QK_W].reshape(n_b, seq, RET_HEADS, RET_DK).transpose(0, 3, 2, 1).reshape(n_b, RET_DK, hs)
    state_spec = pl.BlockSpec((1, bb, RET_HEADS, RET_DK, RET_DV), lambda i: (layer, i, 0, 0, 0))
    in_specs = [pl.BlockSpec((bb * seq, QK_W), lambda i: (i, O_Q // QK_W)),
                pl.BlockSpec((bb * seq, QK_W), lambda i: (i, O_K // QK_W)),
                pl.BlockSpec((bb * seq, V_W), lambda i: (i, O_V // V_W)),
                pl.BlockSpec((bb * seq, V_W), lambda i: (i, O_G // V_W)),
                pl.BlockSpec((bb, RET_DK, hs), lambda i: (i, 0, 0)),
                pl.BlockSpec((1, 1, V_W), lambda i: (layer, 0, 0)),
                state_spec] + [pl.BlockSpec(t.shape, lambda i, nd=t.ndim: (0,) * nd) for t in tables]
    args = [proj, proj, proj, proj, k_t, g_ret, state_ret, *tables]
    aliases = {}
    if s_all is not None:
        aliases = {len(args): 1}
        in_specs.append(pl.BlockSpec(memory_space=pl.ANY))
        args.append(s_all)
    y, s_all = pl.pallas_call(
        functools.partial(_ret_sample_kernel, bb=bb, seq=seq),
        grid=(n_b // bb,),
        in_specs=in_specs,
        out_specs=[pl.BlockSpec((bb * seq, V_W), lambda i: (i, 0)), state_spec],
        out_shape=[jax.ShapeDtypeStruct((tok.rows, V_W), F32),
                   jax.ShapeDtypeStruct(state_ret.shape, F32)],
        input_output_aliases=aliases,
        compiler_params=_params("arbitrary"),
        name="ret_sample",
    )(*args)
    return y, s_all


def _pool_map(pooled, wmap_ref, scale_ref, gi):
    cols = slice(gi * POOL_GW, (gi + 1) * POOL_GW)
    mixed = jnp.dot(pooled.astype(BF16), wmap_ref[0, gi].astype(BF16), preferred_element_type=F32)
    return mixed * scale_ref[0, :, cols]


def _pool_prompt_kernel(u_ref, halo_ref, wmap_ref, scale_ref, o_ref, uf_scr, *, tl):
    first = pl.program_id(1) == 0
    uf_scr[0:POOL_HALO, :] = jnp.where(first, 0.0, halo_ref[...])
    uf_scr[POOL_HALO:, :] = u_ref[...]
    row = lax.broadcasted_iota(jnp.int32, (tl, POOL_GW), 0) + pl.program_id(1) * tl
    for gi, w in enumerate(POOL_WINDOWS):
        cols = slice(gi * POOL_GW, (gi + 1) * POOL_GW)
        u = uf_scr[POOL_HALO:, cols]
        acc = u
        for j in range(1, w):
            acc = acc + uf_scr[POOL_HALO - j:POOL_HALO - j + tl, cols]
        cnt = jnp.minimum(w, row + 1).astype(F32)
        o_ref[:, cols] = _pool_map(acc / cnt - u, wmap_ref, scale_ref, gi).astype(o_ref.dtype)


def _pool_prompt(tok, layer, u, w_pool_map, pool_scale):
    n_b, seq = tok.n_batch, tok.seq
    tl = min(TL_POOL, seq)
    assert seq % tl == 0 and tl % POOL_HALO == 0
    nl = seq // tl
    per = tl // POOL_HALO
    return pl.pallas_call(
        functools.partial(_pool_prompt_kernel, tl=tl),
        grid=(n_b, nl),
        in_specs=[pl.BlockSpec((tl, D_POOL), lambda b, l: (b * nl + l, 0)),
                  pl.BlockSpec((POOL_HALO, D_POOL), lambda b, l: (jnp.maximum((b * nl + l) * per - 1, 0), 0)),
                  pl.BlockSpec((1,) + w_pool_map.shape[1:], lambda b, l: (layer, 0, 0, 0)),
                  pl.BlockSpec((1, 1, D_POOL), lambda b, l: (layer, 0, 0))],
        out_specs=pl.BlockSpec((tl, D_POOL), lambda b, l: (b * nl + l, 0)),
        out_shape=jax.ShapeDtypeStruct((tok.rows, D_POOL), BF16),
        scratch_shapes=[pltpu.VMEM((POOL_HALO + tl, D_POOL), F32)],
        compiler_params=_params("arbitrary", "arbitrary"),
        name="pool_prompt",
    )(u, u, w_pool_map, pool_scale)


def _pool_sample_kernel(buf_ref, u_ref, wmap_ref, scale_ref, o_ref, *, seq, n_b):
    def row(j, cols):
        return buf_ref[j, :, cols] if j < POOL_BUF else u_ref[j - POOL_BUF, :, cols]

    for gi, w in enumerate(POOL_WINDOWS):
        cols = slice(gi * POOL_GW, (gi + 1) * POOL_GW)
        for t in range(seq):
            acc = row(POOL_BUF + t, cols)
            for j in range(1, w):
                acc = acc + row(POOL_BUF + t - j, cols)
            pooled = acc / float(w) - row(POOL_BUF + t, cols)
            o_ref[t, :, cols] = _pool_map(pooled, wmap_ref, scale_ref, gi)


def _pool_sample(tok, layer, u, buf_tm, w_pool_map, pool_scale):
    n_b, seq = tok.n_batch, tok.seq
    assert tok.n_past >= max(POOL_WINDOWS)
    u_tm = u.reshape(n_b, seq, D_POOL).transpose(1, 0, 2)
    mixed_tm = pl.pallas_call(
        functools.partial(_pool_sample_kernel, seq=seq, n_b=n_b),
        grid=(1,),
        in_specs=[pl.BlockSpec(buf_tm.shape, lambda i: (0, 0, 0)),
                  pl.BlockSpec(u_tm.shape, lambda i: (0, 0, 0)),
                  pl.BlockSpec((1,) + w_pool_map.shape[1:], lambda i: (layer, 0, 0, 0)),
                  pl.BlockSpec((1, 1, D_POOL), lambda i: (layer, 0, 0))],
        out_specs=pl.BlockSpec(u_tm.shape, lambda i: (0, 0, 0)),
        out_shape=jax.ShapeDtypeStruct(u_tm.shape, F32),
        compiler_params=_params("arbitrary"),
        name="pool_sample",
    )(buf_tm, u_tm, w_pool_map, pool_scale)
    return mixed_tm.transpose(1, 0, 2).reshape(tok.rows, D_POOL)


def _merge_kernel(yret_ref, mixed_ref, wr_ref, wp_ref, g1_ref, g2_ref, o_ref):
    ret = jnp.dot(yret_ref[...].astype(BF16), wr_ref[0].astype(BF16), preferred_element_type=F32)
    pool = jnp.dot(mixed_ref[...].astype(BF16), wp_ref[0].astype(BF16), preferred_element_type=F32)
    o_ref[...] = (g1_ref[...].astype(F32) * ret + g2_ref[...].astype(F32) * pool).astype(o_ref.dtype)


def _merge(tok, layer, yret, mixed, proj, w_ret_out, w_pool_out):
    tm = tok.tm
    return pl.pallas_call(
        _merge_kernel,
        grid=(tok.n_tiles, D_MODEL // TN),
        in_specs=[pl.BlockSpec((tm, V_W), lambda m, n: (m, 0)),
                  pl.BlockSpec((tm, D_POOL), lambda m, n: (m, 0)),
                  pl.BlockSpec((1, V_W, TN), lambda m, n: (layer, 0, n)),
                  pl.BlockSpec((1, D_POOL, TN), lambda m, n: (layer, 0, n)),
                  pl.BlockSpec((tm, TN), lambda m, n: (m, O_M1 // TN + n)),
                  pl.BlockSpec((tm, TN), lambda m, n: (m, O_M2 // TN + n))],
        out_specs=pl.BlockSpec((tm, TN), lambda m, n: (m, n)),
        out_shape=jax.ShapeDtypeStruct((tok.rows, D_MODEL), BF16),
        compiler_params=_params("arbitrary", "arbitrary"),
        name="merge",
    )(yret, mixed, w_ret_out, w_pool_out, proj, proj)


def _wo_kernel(merged_ref, w_ref, x_ref, gt_ref, o_ref):
    mix = jnp.dot(merged_ref[...], w_ref[0].astype(BF16), preferred_element_type=F32)
    o_ref[...] = x_ref[...] + (1.0 + gt_ref[0]) * mix


def _wo(tok, layer, merged, x, w_o):
    tok = tok.with_tm(TM_OUT)
    tm = tok.tm
    mod3 = tok.mod
    return pl.pallas_call(
        _wo_kernel,
        grid=(tok.n_tiles, D_MODEL // TN),
        in_specs=[pl.BlockSpec((tm, D_MODEL), lambda m, n: (m, 0), pipeline_mode=pl.Buffered(1)),
                  pl.BlockSpec((1, D_MODEL, TN), lambda m, n: (layer, 0, n)),
                  pl.BlockSpec((tm, TN), lambda m, n: (m, n)),
                  tok.mod_spec(layer, 2, tn=TN, col_axis=True)],
        out_specs=pl.BlockSpec((tm, TN), lambda m, n: (m, n)),
        out_shape=jax.ShapeDtypeStruct((tok.rows, D_MODEL), F32),
        compiler_params=_params("arbitrary", "arbitrary"),
        name="wo",
    )(merged, w_o, x, mod3)


def _mat(w_ref):
    return w_ref[(0,) * (len(w_ref.shape) - 2)]


def _final_norm(y, gfin_ref):
    return y * lax.rsqrt(jnp.mean(y * y, axis=-1, keepdims=True) + EPS) * gfin_ref[...]


def _ffn_kernel(x_ref, sc_ref, sh_ref, gt_ref, g_ref, w1_ref, w3_ref, w2_ref, *rest, n_f, with_final):
    gfin_ref = rest[0] if with_final else None
    o_ref, h_scr = rest[-2:]
    f = pl.program_id(1)

    @pl.when(f == 0)
    def _():
        _norm_rows(x_ref, g_ref, sc_ref, sh_ref, h_scr)
        o_ref[...] = jnp.zeros_like(o_ref)

    h = h_scr[...]
    a = jnp.dot(h, _mat(w1_ref).astype(BF16), preferred_element_type=F32)
    b = jnp.dot(h, _mat(w3_ref).astype(BF16), preferred_element_type=F32)
    o_ref[...] += jnp.dot((_silu(a) * b).astype(BF16), _mat(w2_ref).astype(BF16), preferred_element_type=F32)

    @pl.when(f == n_f - 1)
    def _():
        y = x_ref[...] + (1.0 + gt_ref[0]) * o_ref[...]
        o_ref[...] = _final_norm(y, gfin_ref) if with_final else y


def _ffn(tok, layer, x, g_ffn, w1, w3, w2, g_final=None):
    j = layer // 2
    d_ff = w1.shape[-1]
    tok_f = tok.with_tm(TM_FFN)
    tm, n_tiles = tok_f.tm, tok_f.n_tiles
    mod3 = tok_f.mod
    assert d_ff % TF == 0
    n_f = d_ff // TF
    once = pl.Buffered(1)
    in_specs = [pl.BlockSpec((tm, D_MODEL), lambda m, f: (m, 0), pipeline_mode=once),
                tok_f.mod_spec(layer, 4), tok_f.mod_spec(layer, 3), tok_f.mod_spec(layer, 5),
                pl.BlockSpec((1, 1, D_MODEL), lambda m, f: (layer, 0, 0)),
                pl.BlockSpec((1, D_MODEL, TF), lambda m, f: (j, 0, f)),
                pl.BlockSpec((1, D_MODEL, TF), lambda m, f: (j, 0, f)),
                pl.BlockSpec((1, TF, D_MODEL), lambda m, f: (j, f, 0))]
    args = [x, mod3, mod3, mod3, g_ffn, w1, w3, w2]
    if g_final is not None:
        in_specs.append(pl.BlockSpec((1, D_MODEL), lambda m, f: (0, 0)))
        args.append(g_final)
    return pl.pallas_call(
        functools.partial(_ffn_kernel, n_f=n_f, with_final=g_final is not None),
        grid=(n_tiles, n_f),
        in_specs=in_specs,
        out_specs=pl.BlockSpec((tm, D_MODEL), lambda m, f: (m, 0), pipeline_mode=once),
        out_shape=jax.ShapeDtypeStruct((tok.rows, D_MODEL), F32),
        scratch_shapes=[pltpu.VMEM((tm, D_MODEL), BF16)],
        compiler_params=_params("arbitrary", "arbitrary"),
        name="ffn",
    )(*args)


def _router_kernel(x_ref, sc_ref, sh_ref, g_ref, w_ref, b_ref, *rest):
    esel_ref, wsel_ref, h_ref = rest[-3:]
    h = _rmsnorm_mod(x_ref[...], g_ref[0], sc_ref[0], sh_ref[0])
    logits = jnp.dot(h.astype(BF16), w_ref[0].astype(BF16), preferred_element_type=F32) + b_ref[0]
    idx = lax.broadcasted_iota(jnp.int32, logits.shape, 1)
    top1 = jnp.max(logits, axis=-1, keepdims=True)
    i1 = jnp.min(jnp.where(logits == top1, idx, N_EXPERTS), axis=-1, keepdims=True)
    rest = jnp.where(idx == i1, -jnp.inf, logits)
    top2 = jnp.max(rest, axis=-1, keepdims=True)
    i2 = jnp.min(jnp.where(rest == top2, idx, N_EXPERTS), axis=-1, keepdims=True)
    e2 = jnp.exp(top2 - top1)
    denom = 1.0 + e2
    w1, w2 = 1.0 / denom, e2 / denom
    esel_ref[...] = jnp.where(idx == 0, i1, i2)
    wsel_ref[...] = jnp.where(idx == 0, w1, w2)
    h_ref[...] = _pack_halves(h)


def _router(tok, layer, x, g_ffn, w_router, b_router, h_all=None, row0=0, rows_all=None):
    j = layer // 2
    tm = tok.tm
    rows_all = tok.rows if rows_all is None else rows_all
    assert row0 % tm == 0
    small = pl.BlockSpec((tm, N_EXPERTS), lambda m: (m, 0))
    mod3 = tok.mod
    in_specs = [pl.BlockSpec((tm, D_MODEL), lambda m: (m, 0)),
                tok.mod_spec(layer, 4), tok.mod_spec(layer, 3),
                pl.BlockSpec((1, 1, D_MODEL), lambda m: (layer, 0, 0)),
                pl.BlockSpec((1, D_MODEL, N_EXPERTS), lambda m: (j, 0, 0)),
                pl.BlockSpec((1, 1, N_EXPERTS), lambda m: (j, 0, 0))]
    args = [x, mod3, mod3, g_ffn, w_router, b_router]
    aliases = {}
    if h_all is not None:
        aliases = {len(args): 2}
        in_specs.append(pl.BlockSpec(memory_space=pl.ANY))
        args.append(h_all)
    return pl.pallas_call(
        _router_kernel,
        grid=(tok.n_tiles,),
        in_specs=in_specs,
        out_specs=[small, small, pl.BlockSpec((tm, D_MODEL // 2), lambda m: (row0 // tm + m, 0))],
        out_shape=[jax.ShapeDtypeStruct((tok.rows, N_EXPERTS), jnp.int32),
                   jax.ShapeDtypeStruct((tok.rows, N_EXPERTS), F32),
                   jax.ShapeDtypeStruct((rows_all, D_MODEL // 2), jnp.uint32)],
        input_output_aliases=aliases,
        compiler_params=_params("arbitrary"),
        name="router",
    )(*args)


def _route_plan(choice, tm):
    rows = choice.shape[0]
    n_pairs = 2 * rows
    n_tiles = n_pairs // tm + N_EXPERTS
    e_flat = choice.T.reshape(n_pairs)
    onehot = (e_flat[:, None] == jnp.arange(N_EXPERTS, dtype=jnp.int32)[None, :]).astype(jnp.int32)
    csum = jnp.cumsum(onehot, axis=0)
    rank = jnp.sum(csum * onehot, axis=1) - 1
    count = csum[-1]
    tiles_per = (count + tm - 1) // tm
    rows_per = jnp.maximum((((count + jnp.maximum(tiles_per, 1) - 1) // jnp.maximum(tiles_per, 1)) + 7) // 8 * 8, 8)
    tile_end = jnp.cumsum(tiles_per)
    tile_start = tile_end - tiles_per
    pos = (tile_start[e_flat] + rank // rows_per[e_flat]) * tm + rank % rows_per[e_flat]
    token = jnp.arange(n_pairs, dtype=jnp.int32) % rows
    src_row = jnp.zeros((n_tiles * tm,), jnp.int32).at[pos].set(token, unique_indices=True, mode='promise_in_bounds')
    tile_id = jnp.arange(n_tiles, dtype=jnp.int32)
    n_used = tile_end[-1]
    tile_expert = jnp.minimum(jnp.sum(tile_id[:, None] >= tile_end[None, :], axis=1), N_EXPERTS - 1).astype(jnp.int32)
    tile_rows = jnp.clip(count[tile_expert] - (tile_id - tile_start[tile_expert]) * rows_per[tile_expert],
                         0, rows_per[tile_expert])
    tile_rows = jnp.where(tile_id < n_used, tile_rows, 0).astype(jnp.int32)
    last_expert = tile_expert[jnp.maximum(n_used - 1, 0)]
    tile_expert = jnp.where(tile_rows > 0, tile_expert, last_expert)
    return tile_expert, tile_rows, src_row, pos.astype(jnp.int32)


def _row_copy(src_hbm, row, dst, r, sem):
    return pltpu.make_async_copy(src_hbm.at[pl.ds(row, 1)], dst.at[pl.ds(r, 1)], sem)


def _moe_kernel(te_ref, nr_ref, src_ref, h_hbm, w1_ref, w3_ref, w2_ref, o_ref, rows_scr, h_scr, acc_scr, sem,
                *, tm, sub, n_f):
    del te_ref
    m, f = pl.program_id(0), pl.program_id(1)
    n_m = pl.num_programs(0)
    half = D_MODEL // 2
    n_land = rows_scr.shape[0]
    per_step = n_land // n_f

    def start_row(tile, r):
        _row_copy(h_hbm, src_ref[tile * tm + jnp.minimum(r, tm - 1)], rows_scr, r, sem).start()

    def wait_tile():
        pltpu.make_async_copy(h_hbm.at[pl.ds(0, n_land)], rows_scr, sem).wait()

    @pl.when(f == 0)
    def _():
        acc_scr[...] = jnp.zeros_like(acc_scr)

        @pl.when(m == 0)
        def _():
            def body(r, c):
                start_row(0, r)
                return c
            lax.fori_loop(0, n_land, body, 0, unroll=8)

        wait_tile()
        first, second = _unpack_halves(rows_scr[0:tm, :])
        h_scr[:, :half] = first.astype(BF16)
        h_scr[:, half:] = second.astype(BF16)

    next_tile = jnp.minimum(m + 1, n_m - 1)

    def issue_slice():
        for j in range(per_step):
            start_row(next_tile, f * per_step + j)

    n_sub = (nr_ref[m] + sub - 1) // sub
    for n in range(0, tm // sub + 1):
        @pl.when(n_sub == n)
        def _(n=n):
            issue_slice()
            if n > 0:
                h = h_scr[0:n * sub, :]
                a = jnp.dot(h, _mat(w1_ref).astype(BF16), preferred_element_type=F32)
                b = jnp.dot(h, _mat(w3_ref).astype(BF16), preferred_element_type=F32)
                acc_scr[0:n * sub, :] += jnp.dot((_silu(a) * b).astype(BF16), _mat(w2_ref).astype(BF16),
                                                 preferred_element_type=F32)

    @pl.when(f == n_f - 1)
    def _():
        o_ref[...] = _pack_halves(acc_scr[...])

        @pl.when(m == n_m - 1)
        def _():
            wait_tile()


def _moe_sparse(layer, h, plan, w1, w3, w2, tm):
    j = layer // 2
    tile_expert, tile_rows, src_row, _ = plan
    n_tiles = tile_expert.shape[0]
    d_ff = w1.shape[3]
    assert d_ff % TF == 0 and tm % SUB_MOE == 0
    n_f = d_ff // TF

    def f_sel(m, f, nr):
        return jnp.where(nr[m] > 0, f, n_f - 1)

    per_step = -(-tm // n_f)
    per_step += -per_step % 8
    assert h.shape[0] >= n_f * per_step

    return pl.pallas_call(
        functools.partial(_moe_kernel, tm=tm, sub=SUB_MOE, n_f=n_f),
        grid_spec=pltpu.PrefetchScalarGridSpec(
            num_scalar_prefetch=3,
            grid=(n_tiles, n_f),
            in_specs=[pl.BlockSpec(memory_space=pl.ANY),
                      pl.BlockSpec((1, 1, D_MODEL, TF), lambda m, f, te, nr, sr: (j, te[m], 0, f_sel(m, f, nr))),
                      pl.BlockSpec((1, 1, D_MODEL, TF), lambda m, f, te, nr, sr: (j, te[m], 0, f_sel(m, f, nr))),
                      pl.BlockSpec((1, 1, TF, D_MODEL), lambda m, f, te, nr, sr: (j, te[m], f_sel(m, f, nr), 0))],
            out_specs=pl.BlockSpec((tm, D_MODEL // 2), lambda m, f, te, nr, sr: (m, 0)),
            scratch_shapes=[pltpu.VMEM((n_f * per_step, D_MODEL // 2), jnp.uint32), pltpu.VMEM((tm, D_MODEL), BF16),
                            pltpu.VMEM((tm, D_MODEL), F32), pltpu.SemaphoreType.DMA(())]),
        out_shape=jax.ShapeDtypeStruct((n_tiles * tm, D_MODEL // 2), jnp.uint32),
        compiler_params=_params("arbitrary", "arbitrary"),
        name="moe",
    )(tile_expert, tile_rows, src_row, h, w1, w3, w2)


def _combine_kernel(pos_ref, y_hbm, x_ref, gt_ref, w_ref, *rest, tm, row0, rows_all, with_final):
    gfin_ref = rest[0] if with_final else None
    o_ref, y_scr, sems = rest[-3:]
    i, n = pl.program_id(0), pl.num_programs(0)

    def for_each_row(tile, slot, fn):
        def body(r, c):
            for k in range(2):
                slot_of_pair = pos_ref[k * rows_all + row0 + tile * tm + r]
                fn(_row_copy(y_hbm, slot_of_pair, y_scr.at[slot, k], r, sems.at[slot]))
            return c
        lax.fori_loop(0, tm, body, 0, unroll=8)

    @pl.when(i == 0)
    def _():
        for_each_row(0, 0, lambda cp: cp.start())

    @pl.when(i + 1 < n)
    def _():
        for_each_row(i + 1, (i + 1) % 2, lambda cp: cp.start())

    slot = i % 2
    for k in range(2):
        pltpu.make_async_copy(y_hbm.at[pl.ds(0, tm)], y_scr.at[slot, k], sems.at[slot]).wait()
    w = w_ref[...]
    first0, second0 = _unpack_halves(y_scr[slot, 0])
    first1, second1 = _unpack_halves(y_scr[slot, 1])
    f = jnp.concatenate([w[:, 0:1] * first0 + w[:, 1:2] * first1,
                         w[:, 0:1] * second0 + w[:, 1:2] * second1], axis=1)
    y = x_ref[...] + (1.0 + gt_ref[0]) * f
    o_ref[...] = _final_norm(y, gfin_ref) if with_final else y


def _combine(tok, layer, y_sorted, pos, row0, x, wsel, g_final):
    tok_c = tok.with_tm(TM_COMBINE)
    tm = tok_c.tm
    mod3 = tok_c.mod
    in_specs = [pl.BlockSpec(memory_space=pl.ANY),
                pl.BlockSpec((tm, D_MODEL), lambda m, pos: (m, 0)),
                tok_c.mod_spec(layer, 5),
                pl.BlockSpec((tm, N_EXPERTS), lambda m, pos: (m, 0))]
    args = [pos, y_sorted, x, mod3, wsel]
    if g_final is not None:
        in_specs.append(pl.BlockSpec((1, D_MODEL), lambda m, pos: (0, 0)))
        args.append(g_final)
    return pl.pallas_call(
        functools.partial(_combine_kernel, tm=tm, row0=row0, rows_all=pos.shape[0] // 2,
                          with_final=g_final is not None),
        grid_spec=pltpu.PrefetchScalarGridSpec(
            num_scalar_prefetch=1,
            grid=(tok_c.n_tiles,),
            in_specs=in_specs,
            out_specs=pl.BlockSpec((tm, D_MODEL), lambda m, pos: (m, 0)),
            scratch_shapes=[pltpu.VMEM((2, 2, tm, D_MODEL // 2), jnp.uint32), pltpu.SemaphoreType.DMA((2,))]),
        out_shape=jax.ShapeDtypeStruct((tok.rows, D_MODEL), F32),
        compiler_params=_params("arbitrary"),
        name="combine",
    )(*args)


class _Stream:
    def __init__(self, tok, x, ret_state, pool_state):
        self.tok, self.x, self.ret_state, self.pool_state = tok, x, ret_state, pool_state
        self.rope_tabs = tok.rope_tables()
        self.new_state = None
        self.u_layers = []


def _mixer_layer(s, i, p):
    tok = s.tok
    is_prompt = s.ret_state is None
    proj, u = _inproj(tok, i, s.x, p['g_mix'], p['w_in'], s.rope_tabs, BF16 if is_prompt else F32)
    s.u_layers.append(u)
    if is_prompt:
        yret, s.new_state = _ret_prompt(tok, i, proj, p['g_ret'], s.new_state)
        mixed = _pool_prompt(tok, i, u, p['w_pool_map'], p['pool_scale'])
    else:
        yret, s.new_state = _ret_sample(tok, i, proj, p['g_ret'], s.ret_state, s.new_state)
        mixed = _pool_sample(tok, i, u, s.pool_state[i].transpose(1, 0, 2), p['w_pool_map'], p['pool_scale'])
    merged = _merge(tok, i, yret, mixed, proj, p['w_ret_out'], p['w_pool_out'])
    s.x = _wo(tok, i, merged, s.x, p['w_o'])


def _moe_layer(streams, i, p, g_final):
    rows_all = sum(s.tok.rows for s in streams)
    h_all, row0, choices, weights, starts = None, 0, [], [], []
    for s in streams:
        esel, wsel, h_all = _router(s.tok, i, s.x, p['g_ffn'], p['w_router'], p['b_router'],
                                    h_all=h_all, row0=row0, rows_all=rows_all)
        choices.append(esel[:, :2])
        weights.append(wsel)
        starts.append(row0)
        row0 += s.tok.rows
    plan = _route_plan(jnp.concatenate(choices, axis=0), TM_MOE)
    y_sorted = _moe_sparse(i, h_all, plan, p['w1_moe'], p['w3_moe'], p['w2_moe'], TM_MOE)
    for s, wsel, start in zip(streams, weights, starts):
        s.x = _combine(s.tok, i, y_sorted, plan[3], start, s.x, wsel, g_final)


def kernel(x_prompt, x_sample, c_prompt, c_sample, state_ret, state_pool, w_ada, b_ada, g_mix, w_in, g_ret,
           w_ret_out, w_pool_map, pool_scale, w_pool_out, w_o, g_ffn, w1_dense, w3_dense, w2_dense, w_router,
           b_router, w1_moe, w3_moe, w2_moe, g_final):
    row3 = lambda a: a.reshape(a.shape[0], 1, a.shape[1])
    p = dict(g_mix=row3(g_mix), w_in=w_in, g_ret=row3(g_ret), w_ret_out=w_ret_out, w_pool_map=w_pool_map,
             pool_scale=row3(pool_scale), w_pool_out=w_pool_out, w_o=w_o, g_ffn=row3(g_ffn),
             w1_dense=w1_dense, w3_dense=w3_dense, w2_dense=w2_dense, w_router=w_router, b_router=row3(b_router),
             w1_moe=w1_moe, w3_moe=w3_moe, w2_moe=w2_moe, g_final=g_final.reshape(1, D_MODEL))
    n_bp, seq_p, _ = x_prompt.shape
    n_bs, seq_s, _ = x_sample.shape

    mod_p, mod_s = _ada(c_prompt, jnp.repeat(c_sample, seq_s, axis=0), w_ada, b_ada)

    tok_p = _Tokens(n_bp, seq_p, 0, TM_PROMPT, per_row_mod=False, mod=mod_p)
    tok_s = _Tokens(n_bs, seq_s, PAST_LEN, TM_SAMPLE, per_row_mod=True, mod=mod_s)
    prompt = _Stream(tok_p, x_prompt.reshape(tok_p.rows, D_MODEL), None, None)
    sample = _Stream(tok_s, x_sample.reshape(tok_s.rows, D_MODEL), state_ret, state_pool)
    streams = [prompt, sample]

    for i in range(DEPTH):
        g_fin = p['g_final'] if i == DEPTH - 1 else None
        for s in streams:
            _mixer_layer(s, i, p)
        if i % 2 == 0:
            for s in streams:
                s.x = _ffn(s.tok, i, s.x, p['g_ffn'], p['w1_dense'], p['w3_dense'], p['w2_dense'], g_final=g_fin)
        else:
            _moe_layer(streams, i, p, g_fin)

    buf_prompt = jnp.stack([u.reshape(n_bp, seq_p, D_POOL)[:, seq_p - POOL_BUF:] for u in prompt.u_layers])
    buf_sample = jnp.stack([
        jnp.concatenate([state_pool[i], sample.u_layers[i].reshape(n_bs, seq_s, D_POOL)], axis=1)[:, -POOL_BUF:]
        for i in range(DEPTH)])
    return (prompt.x.reshape(x_prompt.shape), sample.x.reshape(x_sample.shape), prompt.new_state, buf_prompt,
            sample.new_state, buf_sample)
```

```python
import functools

import numpy as np
import jax
import jax.numpy as jnp
from jax import lax
from jax.experimental import pallas as pl
from jax.experimental.pallas import tpu as pltpu

F32 = jnp.float32
BF16 = jnp.bfloat16

D_MODEL = 2048
DEPTH = 2
PAST_LEN = 16384
RET_HEADS = 8
RET_DK = 128
RET_DV = 256
RET_CHUNK = 128
ROPE_BASE = 10000.0
QK_W = RET_HEADS * RET_DK
V_W = RET_HEADS * RET_DV
POOL_WINDOWS = (2, 4, 8, 16)
D_POOL = 1024
POOL_GW = 256
POOL_BUF = 15
POOL_HALO = 16
N_IN = 2 * QK_W + 2 * V_W + D_POOL + 2 * D_MODEL
N_EXPERTS = 8
EPS = 1e-6
N_MOD = 6

O_Q, O_K, O_V, O_G, O_U = 0, QK_W, 2 * QK_W, 2 * QK_W + V_W, 2 * QK_W + 2 * V_W
O_M1 = O_U + D_POOL
O_M2 = O_M1 + D_MODEL

VMEM_LIMIT_BYTES = 56 * 1024 * 1024
ROW_DMA_PRIORITY = 1
TN = 512
TF = 256
TM_PROMPT = 1024
TM_SAMPLE = 512
TM_FFN = 1024
TM_OUT = 2048
ROW_CHUNK = 256
NORM_ROWS = 16
NORM_UNROLL = 16
TL_POOL = 512
TM_MOE = 1024
SUB_MOE = 256
TM_COMBINE = 256


def _params(*sem):
    return pltpu.CompilerParams(dimension_semantics=sem, vmem_limit_bytes=VMEM_LIMIT_BYTES)


class _Tokens:
    def __init__(self, n_batch, seq, n_past, tm, per_row_mod, mod=None):
        self.n_batch, self.seq, self.n_past = n_batch, seq, n_past
        self.rows = n_batch * seq
        self.tm = min(tm, self.rows)
        self.per_row_mod = per_row_mod
        if not per_row_mod:
            self.tm = min(self.tm, seq)
            assert seq % self.tm == 0
        assert self.rows % self.tm == 0
        self.n_tiles = self.rows // self.tm
        if mod is not None and not per_row_mod and mod.ndim == 3 and mod.shape[1] != 1:
            mod = mod.reshape(DEPTH * n_batch, 1, N_MOD * D_MODEL)
        self.mod = mod

    def with_tm(self, tm):
        return _Tokens(self.n_batch, self.seq, self.n_past, tm, self.per_row_mod, self.mod)

    def mod_spec(self, layer, chunk, tn=D_MODEL, col_axis=False):
        per_chunk = D_MODEL // tn
        col = (lambda n: chunk * per_chunk + n) if col_axis else (lambda n: chunk * per_chunk)
        if self.per_row_mod:
            if col_axis:
                return pl.BlockSpec((1, self.tm, tn), lambda m, n, *_: (layer, m, col(n)))
            return pl.BlockSpec((1, self.tm, tn), lambda m, *_: (layer, m, col(0)))
        tiles_per_batch = self.seq // self.tm
        base = layer * self.n_batch
        if col_axis:
            return pl.BlockSpec((1, 1, tn), lambda m, n, *_: (base + m // tiles_per_batch, 0, col(n)))
        return pl.BlockSpec((1, 1, tn), lambda m, *_: (base + m // tiles_per_batch, 0, col(0)))

    def rope_tables(self):
        half = RET_DK // 2
        pos = jnp.arange(self.seq, dtype=F32) + float(self.n_past)
        inv = ROPE_BASE ** (-jnp.arange(half, dtype=F32) / half)
        ang = pos[:, None] * inv[None, :]
        cos, sin = jnp.cos(ang), jnp.sin(ang)
        cos_full = jnp.concatenate([cos, cos], axis=1)
        sin_signed = jnp.concatenate([-sin, sin], axis=1)
        if self.per_row_mod:
            cos_full = jnp.tile(cos_full, (self.n_batch, 1))
            sin_signed = jnp.tile(sin_signed, (self.n_batch, 1))
        return cos_full, sin_signed

    def rope_spec(self):
        n_blocks = (self.rows if self.per_row_mod else self.seq) // self.tm
        return pl.BlockSpec((self.tm, RET_DK), lambda m, *_: (m % n_blocks, 0))


def _retention_tables(chunk):
    log_gamma = jnp.log1p(-jnp.exp2(-5.0 - jnp.arange(RET_HEADS, dtype=F32)))
    idx = jnp.arange(chunk, dtype=F32)
    rel = idx[:, None] - idx[None, :]
    dmask = jnp.where(rel[None] >= 0, jnp.exp(log_gamma[:, None, None] * jnp.maximum(rel, 0.0)[None]), 0.0)
    q_dec = jnp.exp(log_gamma[:, None] * (idx + 1.0)[None, :])
    k_dec = jnp.exp(log_gamma[:, None] * (chunk - 1.0 - idx)[None, :])
    chunk_dec = jnp.exp(log_gamma * chunk)
    q_dec = jnp.broadcast_to(q_dec[:, :, None], (RET_HEADS, chunk, RET_DV))
    k_dec = jnp.broadcast_to(k_dec[:, :, None], (RET_HEADS, chunk, RET_DK))
    chunk_dec = jnp.broadcast_to(chunk_dec[:, None, None], (RET_HEADS, 1, RET_DV))
    return dmask, q_dec, k_dec, chunk_dec


def _rmsnorm_mod(x, g, scale, shift):
    y = x * lax.rsqrt(jnp.mean(x * x, axis=-1, keepdims=True) + EPS) * g
    return y * (1.0 + scale) + shift


def _norm_rows(x_ref, g_ref, sc_ref, sh_ref, h_ref):
    tm = x_ref.shape[0]
    step = min(NORM_ROWS, tm)
    trips = tm // step
    per_row = sc_ref.shape[1] != 1
    if not per_row:
        gain, shift = g_ref[0] * (1.0 + sc_ref[0]), sh_ref[0]

    def body(c, carry):
        rows = pl.ds(pl.multiple_of(c * step, step), step)
        x = x_ref[rows, :]
        y = x * lax.rsqrt(jnp.mean(x * x, axis=-1, keepdims=True) + EPS)
        if per_row:
            y = y * (g_ref[0] * (1.0 + sc_ref[0, rows, :])) + sh_ref[0, rows, :]
        else:
            y = y * gain + shift
        h_ref[rows, :] = y.astype(h_ref.dtype)
        return carry

    lax.fori_loop(0, trips, body, 0, unroll=NORM_UNROLL if trips % NORM_UNROLL == 0 else 1)


def _silu(x):
    return x * jax.nn.sigmoid(x)


def _pack_halves(y):
    half = y.shape[1] // 2
    hi = lax.bitcast_convert_type(y[:, :half].astype(jnp.bfloat16).astype(F32), jnp.uint32)
    lo = lax.bitcast_convert_type(y[:, half:].astype(jnp.bfloat16).astype(F32), jnp.uint32)
    return hi | (lo >> 16)


def _unpack_halves(u):
    return (lax.bitcast_convert_type(u & jnp.uint32(0xFFFF0000), F32),
            lax.bitcast_convert_type(u << 16, F32))


def _ada_kernel(cp_ref, cs_ref, w_ref, b_ref, op_ref, os_ref):
    w = w_ref[0].astype(BF16)
    for c_ref, o_ref in ((cp_ref, op_ref), (cs_ref, os_ref)):
        o_ref[0] = jnp.dot(_silu(c_ref[...]).astype(BF16), w, preferred_element_type=F32) + b_ref[0]


def _ada(c_p, c_s, w_ada, b_ada):
    tn = 1024
    width = N_MOD * D_MODEL
    row_spec = lambda c: pl.BlockSpec(c.shape, lambda l, n: (0, 0))
    out_spec = lambda c: pl.BlockSpec((1, c.shape[0], tn), lambda l, n: (l, 0, n))
    return pl.pallas_call(
        _ada_kernel,
        grid=(DEPTH, width // tn),
        in_specs=[row_spec(c_p), row_spec(c_s),
                  pl.BlockSpec((1, D_MODEL, tn), lambda l, n: (l, 0, n)),
                  pl.BlockSpec((1, 1, tn), lambda l, n: (l, 0, n))],
        out_specs=[out_spec(c_p), out_spec(c_s)],
        out_shape=[jax.ShapeDtypeStruct((DEPTH, c.shape[0], width), F32) for c in (c_p, c_s)],
        compiler_params=_params("arbitrary", "arbitrary"),
        name="ada",
    )(c_p, c_s, w_ada, b_ada.reshape(DEPTH, 1, width))


def _inproj_kernel(x_ref, sc_ref, sh_ref, g_ref, w_ref, cos_ref, sin_ref, o_ref, u_ref, h_scr):
    n = pl.program_id(1)
    tm = x_ref.shape[0]
    rc = min(ROW_CHUNK, tm)

    @pl.when(n == 0)
    def _():
        _norm_rows(x_ref, g_ref, sc_ref, sh_ref, h_scr)

    def project(epilogue):
        w = w_ref[0].astype(BF16)
        for r in range(0, tm, rc):
            rows = slice(r, r + rc)
            epilogue(jnp.dot(h_scr[rows, :], w, preferred_element_type=F32), rows)

    def store(acc, rows):
        o_ref[rows, :] = acc.astype(o_ref.dtype)

    def rope(scale):
        def epilogue(acc, rows):
            for j in range(TN // RET_DK):
                cols = slice(j * RET_DK, (j + 1) * RET_DK)
                xh = acc[:, cols]
                r = xh * cos_ref[rows, :] + pltpu.roll(xh, RET_DK // 2, 1) * sin_ref[rows, :]
                if scale != 1.0:
                    r = r * scale
                o_ref[rows, cols] = r.astype(o_ref.dtype)
        project(epilogue)

    @pl.when(n < O_K // TN)
    def _():
        rope(1.0)

    @pl.when(jnp.logical_and(n >= O_K // TN, n < O_V // TN))
    def _():
        rope(RET_DK ** -0.5)

    @pl.when(jnp.logical_and(n >= O_V // TN, n < O_G // TN))
    def _():
        project(store)

    @pl.when(jnp.logical_and(n >= O_G // TN, n < O_U // TN))
    def _():
        project(lambda acc, rows: store(_silu(acc), rows))

    @pl.when(jnp.logical_and(n >= O_U // TN, n < O_M1 // TN))
    def _():
        def epilogue(acc, rows):
            store(acc, rows)
            u_ref[rows, :] = acc
        project(epilogue)

    @pl.when(n >= O_M1 // TN)
    def _():
        project(lambda acc, rows: store(jax.nn.sigmoid(acc), rows))


def _inproj(tok, layer, x, g_mix, w_in, rope_tabs, out_dtype):
    tm = tok.tm
    n_u = D_POOL // TN
    mod3 = tok.mod
    proj, u = pl.pallas_call(
        _inproj_kernel,
        grid=(tok.n_tiles, N_IN // TN),
        in_specs=[pl.BlockSpec((tm, D_MODEL), lambda m, n: (m, 0)),
                  tok.mod_spec(layer, 1), tok.mod_spec(layer, 0),
                  pl.BlockSpec((1, 1, D_MODEL), lambda m, n: (layer, 0, 0)),
                  pl.BlockSpec((1, D_MODEL, TN), lambda m, n: (layer, 0, n)),
                  tok.rope_spec(), tok.rope_spec()],
        out_specs=[pl.BlockSpec((tm, TN), lambda m, n: (m, n)),
                   pl.BlockSpec((tm, TN), lambda m, n: (m, jnp.clip(n - O_U // TN, 0, n_u - 1)))],
        out_shape=[jax.ShapeDtypeStruct((tok.rows, N_IN), out_dtype),
                   jax.ShapeDtypeStruct((tok.rows, D_POOL), F32)],
        scratch_shapes=[pltpu.VMEM((tm, D_MODEL), BF16)],
        compiler_params=_params("arbitrary", "arbitrary"),
        name="inproj",
    )(x, mod3, mod3, g_mix, w_in, *rope_tabs)
    return proj, u


def _head_norm_gate(o, gret, gate):
    mu = jnp.mean(o, axis=-1, keepdims=True)
    d = o - mu
    var = jnp.mean(d * d, axis=-1, keepdims=True)
    return gate.astype(F32) * (d * lax.rsqrt(var + EPS) * gret)


def _ret_prompt_kernel(q_ref, k_ref, v_ref, g_ref, gret_ref, dmask_ref, qdec_ref, kdec_ref, cdec_ref,
                       *rest):
    y_ref, s_ref = rest[-2:]

    @pl.when(pl.program_id(1) == 0)
    def _():
        s_ref[...] = jnp.zeros_like(s_ref)

    for h in range(RET_HEADS):
        ks, vs = slice(h * RET_DK, (h + 1) * RET_DK), slice(h * RET_DV, (h + 1) * RET_DV)
        q, k, v = q_ref[:, ks], k_ref[:, ks], v_ref[:, vs]
        state = s_ref[0, 0, h]
        scores = lax.dot_general(q, k, (((1,), (1,)), ((), ())), preferred_element_type=F32) * dmask_ref[h]
        inner = jnp.dot(scores.astype(BF16), v, preferred_element_type=F32)
        cross = jnp.dot(q, state.astype(BF16), preferred_element_type=F32) * qdec_ref[h]
        k_decayed = (k.astype(F32) * kdec_ref[h]).astype(BF16)
        s_ref[0, 0, h] = cdec_ref[h] * state + lax.dot_general(
            k_decayed, v, (((0,), (0,)), ((), ())), preferred_element_type=F32)
        y_ref[:, vs] = _head_norm_gate(inner + cross, gret_ref[0, :, vs], g_ref[:, vs]).astype(y_ref.dtype)


def _table_specs(tables):
    return [pl.BlockSpec(t.shape, lambda *_: (0, 0, 0)) for t in tables]


def _ret_prompt(tok, layer, proj, g_ret, s_all):
    n_b, seq = tok.n_batch, tok.seq
    c = RET_CHUNK
    assert seq % c == 0
    nc = seq // c
    tables = _retention_tables(c)
    in_specs = [pl.BlockSpec((c, QK_W), lambda b, i: (b * nc + i, O_Q // QK_W)),
                pl.BlockSpec((c, QK_W), lambda b, i: (b * nc + i, O_K // QK_W)),
                pl.BlockSpec((c, V_W), lambda b, i: (b * nc + i, O_V // V_W)),
                pl.BlockSpec((c, V_W), lambda b, i: (b * nc + i, O_G // V_W)),
                pl.BlockSpec((1, 1, V_W), lambda b, i: (layer, 0, 0))] + _table_specs(tables)
    args = [proj, proj, proj, proj, g_ret, *tables]
    aliases = {}
    if s_all is not None:
        aliases = {len(args): 1}
        in_specs.append(pl.BlockSpec(memory_space=pl.ANY))
        args.append(s_all)
    y, s_all = pl.pallas_call(
        _ret_prompt_kernel,
        grid=(n_b, nc),
        in_specs=in_specs,
        out_specs=[pl.BlockSpec((c, V_W), lambda b, i: (b * nc + i, 0)),
                   pl.BlockSpec((1, 1, RET_HEADS, RET_DK, RET_DV), lambda b, i: (layer, b, 0, 0, 0))],
        out_shape=[jax.ShapeDtypeStruct((tok.rows, V_W), BF16),
                   jax.ShapeDtypeStruct((DEPTH, n_b, RET_HEADS, RET_DK, RET_DV), F32)],
        input_output_aliases=aliases,
        compiler_params=_params("arbitrary", "arbitrary"),
        name="ret_prompt",
    )(*args)
    return y, s_all


def _ret_sample_kernel(q_ref, k_ref, v_ref, g_ref, kt_ref, gret_ref, s0_ref, dm_ref, qdec_ref, kdect_ref, cdec_ref,
                       *rest, bb, seq):
    y_ref, s_ref = rest[-2:]
    hs = RET_HEADS * seq
    reps = lambda a: jnp.concatenate([a] * RET_HEADS, axis=0)
    mask_q = (lax.broadcasted_iota(jnp.int32, (hs, QK_W), 0) // seq
              == lax.broadcasted_iota(jnp.int32, (hs, QK_W), 1) // RET_DK)
    mask_v = (lax.broadcasted_iota(jnp.int32, (hs, V_W), 0) // seq
              == lax.broadcasted_iota(jnp.int32, (hs, V_W), 1) // RET_DV)
    nt = (((1,), (1,)), ((), ()))
    for b in range(bb):
        rows_b = slice(b * seq, (b + 1) * seq)
        q, k, v = q_ref[rows_b, :], k_ref[rows_b, :], v_ref[rows_b, :]
        q_blk = jnp.where(mask_q, reps(q), 0.0).astype(BF16)
        scores = lax.dot_general(q_blk, reps(k).astype(BF16), nt, preferred_element_type=F32) * dm_ref[...]
        v_stack = jnp.concatenate([v[:, h * RET_DV:(h + 1) * RET_DV] for h in range(RET_HEADS)], axis=0)
        inner = jnp.dot(scores.astype(BF16), v_stack.astype(BF16), preferred_element_type=F32)
        state = s0_ref[0, b]
        cross = jnp.dot(q_blk, state.reshape(QK_W, RET_DV).astype(BF16), preferred_element_type=F32)
        o = inner + cross * qdec_ref[...]
        mu = jnp.mean(o, axis=-1, keepdims=True)
        d = o - mu
        normed = d * lax.rsqrt(jnp.mean(d * d, axis=-1, keepdims=True) + EPS)
        k_dec_t = (kt_ref[b] * kdect_ref[...]).astype(BF16)
        v_blk = jnp.where(mask_v, reps(v), 0.0).astype(BF16)
        upd = jnp.dot(k_dec_t, v_blk, preferred_element_type=F32)
        for h in range(RET_HEADS):
            vs = slice(h * RET_DV, (h + 1) * RET_DV)
            y_ref[rows_b, vs] = g_ref[rows_b, vs] * (normed[h * seq:(h + 1) * seq] * gret_ref[0, :, vs])
            s_ref[0, b, h] = cdec_ref[h] * state[h] + upd[:, vs]


def _ret_sample(tok, layer, proj, g_ret, state_ret, s_all):
    n_b, seq = tok.n_batch, tok.seq
    bb = 4
    assert n_b % bb == 0
    hs = RET_HEADS * seq
    dmask, q_dec, k_dec, chunk_dec = _retention_tables(seq)
    dm_blk = (dmask[:, :, None, :] * jnp.eye(RET_HEADS, dtype=F32)[:, None, :, None]).reshape(hs, hs)
    tables = [dm_blk, q_dec.reshape(hs, RET_DV), k_dec[:, :, 0].reshape(1, hs), chunk_dec]
    k_t = proj[:, O_K:O_K + QK_W].reshape(n_b, seq, RET_HEADS, RET_DK).transpose(0, 3, 2, 1).reshape(n_b, RET_DK, hs)
    state_spec = pl.BlockSpec((1, bb, RET_HEADS, RET_DK, RET_DV), lambda i: (layer, i, 0, 0, 0))
    in_specs = [pl.BlockSpec((bb * seq, QK_W), lambda i: (i, O_Q // QK_W)),
                pl.BlockSpec((bb * seq, QK_W), lambda i: (i, O_K // QK_W)),
                pl.BlockSpec((bb * seq, V_W), lambda i: (i, O_V // V_W)),
                pl.BlockSpec((bb * seq, V_W), lambda i: (i, O_G // V_W)),
                pl.BlockSpec((bb, RET_DK, hs), lambda i: (i, 0, 0)),
                pl.BlockSpec((1, 1, V_W), lambda i: (layer, 0, 0)),
                state_spec] + [pl.BlockSpec(t.shape, lambda i, nd=t.ndim: (0,) * nd) for t in tables]
    args = [proj, proj, proj, proj, k_t, g_ret, state_ret, *tables]
    aliases = {}
    if s_all is not None:
        aliases = {len(args): 1}
        in_specs.append(pl.BlockSpec(memory_space=pl.ANY))
        args.append(s_all)
    y, s_all = pl.pallas_call(
        functools.partial(_ret_sample_kernel, bb=bb, seq=seq),
        grid=(n_b // bb,),
        in_specs=in_specs,
        out_specs=[pl.BlockSpec((bb * seq, V_W), lambda i: (i, 0)), state_spec],
        out_shape=[jax.ShapeDtypeStruct((tok.rows, V_W), F32),
                   jax.ShapeDtypeStruct(state_ret.shape, F32)],
        input_output_aliases=aliases,
        compiler_params=_params("arbitrary"),
        name="ret_sample",
    )(*args)
    return y, s_all


def _pool_map(pooled, wmap_ref, scale_ref, gi):
    cols = slice(gi * POOL_GW, (gi + 1) * POOL_GW)
    mixed = jnp.dot(pooled.astype(BF16), wmap_ref[0, gi].astype(BF16), preferred_element_type=F32)
    return mixed * scale_ref[0, :, cols]


def _pool_prompt_kernel(u_ref, halo_ref, wmap_ref, scale_ref, o_ref, uf_scr, *, tl):
    first = pl.program_id(1) == 0
    uf_scr[0:POOL_HALO, :] = jnp.where(first, 0.0, halo_ref[...])
    uf_scr[POOL_HALO:, :] = u_ref[...]
    row = lax.broadcasted_iota(jnp.int32, (tl, POOL_GW), 0) + pl.program_id(1) * tl
    for gi, w in enumerate(POOL_WINDOWS):
        cols = slice(gi * POOL_GW, (gi + 1) * POOL_GW)
        u = uf_scr[POOL_HALO:, cols]
        acc = u
        for j in range(1, w):
            acc = acc + uf_scr[POOL_HALO - j:POOL_HALO - j + tl, cols]
        cnt = jnp.minimum(w, row + 1).astype(F32)
        o_ref[:, cols] = _pool_map(acc / cnt - u, wmap_ref, scale_ref, gi).astype(o_ref.dtype)


def _pool_prompt(tok, layer, u, w_pool_map, pool_scale):
    n_b, seq = tok.n_batch, tok.seq
    tl = min(TL_POOL, seq)
    assert seq % tl == 0 and tl % POOL_HALO == 0
    nl = seq // tl
    per = tl // POOL_HALO
    return pl.pallas_call(
        functools.partial(_pool_prompt_kernel, tl=tl),
        grid=(n_b, nl),
        in_specs=[pl.BlockSpec((tl, D_POOL), lambda b, l: (b * nl + l, 0)),
                  pl.BlockSpec((POOL_HALO, D_POOL), lambda b, l: (jnp.maximum((b * nl + l) * per - 1, 0), 0)),
                  pl.BlockSpec((1,) + w_pool_map.shape[1:], lambda b, l: (layer, 0, 0, 0)),
                  pl.BlockSpec((1, 1, D_POOL), lambda b, l: (layer, 0, 0))],
        out_specs=pl.BlockSpec((tl, D_POOL), lambda b, l: (b * nl + l, 0)),
        out_shape=jax.ShapeDtypeStruct((tok.rows, D_POOL), BF16),
        scratch_shapes=[pltpu.VMEM((POOL_HALO + tl, D_POOL), F32)],
        compiler_params=_params("arbitrary", "arbitrary"),
        name="pool_prompt",
    )(u, u, w_pool_map, pool_scale)


def _pool_sample_kernel(buf_ref, u_ref, wmap_ref, scale_ref, o_ref, *, seq, n_b):
    def row(j, cols):
        return buf_ref[j, :, cols] if j < POOL_BUF else u_ref[j - POOL_BUF, :, cols]

    for gi, w in enumerate(POOL_WINDOWS):
        cols = slice(gi * POOL_GW, (gi + 1) * POOL_GW)
        for t in range(seq):
            acc = row(POOL_BUF + t, cols)
            for j in range(1, w):
                acc = acc + row(POOL_BUF + t - j, cols)
            pooled = acc / float(w) - row(POOL_BUF + t, cols)
            o_ref[t, :, cols] = _pool_map(pooled, wmap_ref, scale_ref, gi)


def _pool_sample(tok, layer, u, buf_tm, w_pool_map, pool_scale):
    n_b, seq = tok.n_batch, tok.seq
    assert tok.n_past >= max(POOL_WINDOWS)
    u_tm = u.reshape(n_b, seq, D_POOL).transpose(1, 0, 2)
    mixed_tm = pl.pallas_call(
        functools.partial(_pool_sample_kernel, seq=seq, n_b=n_b),
        grid=(1,),
        in_specs=[pl.BlockSpec(buf_tm.shape, lambda i: (0, 0, 0)),
                  pl.BlockSpec(u_tm.shape, lambda i: (0, 0, 0)),
                  pl.BlockSpec((1,) + w_pool_map.shape[1:], lambda i: (layer, 0, 0, 0)),
                  pl.BlockSpec((1, 1, D_POOL), lambda i: (layer, 0, 0))],
        out_specs=pl.BlockSpec(u_tm.shape, lambda i: (0, 0, 0)),
        out_shape=jax.ShapeDtypeStruct(u_tm.shape, F32),
        compiler_params=_params("arbitrary"),
        name="pool_sample",
    )(buf_tm, u_tm, w_pool_map, pool_scale)
    return mixed_tm.transpose(1, 0, 2).reshape(tok.rows, D_POOL)


def _merge_kernel(yret_ref, mixed_ref, wr_ref, wp_ref, g1_ref, g2_ref, o_ref):
    ret = jnp.dot(yret_ref[...].astype(BF16), wr_ref[0].astype(BF16), preferred_element_type=F32)
    pool = jnp.dot(mixed_ref[...].astype(BF16), wp_ref[0].astype(BF16), preferred_element_type=F32)
    o_ref[...] = (g1_ref[...].astype(F32) * ret + g2_ref[...].astype(F32) * pool).astype(o_ref.dtype)


def _merge(tok, layer, yret, mixed, proj, w_ret_out, w_pool_out):
    tm = tok.tm
    return pl.pallas_call(
        _merge_kernel,
        grid=(tok.n_tiles, D_MODEL // TN),
        in_specs=[pl.BlockSpec((tm, V_W), lambda m, n: (m, 0)),
                  pl.BlockSpec((tm, D_POOL), lambda m, n: (m, 0)),
                  pl.BlockSpec((1, V_W, TN), lambda m, n: (layer, 0, n)),
                  pl.BlockSpec((1, D_POOL, TN), lambda m, n: (layer, 0, n)),
                  pl.BlockSpec((tm, TN), lambda m, n: (m, O_M1 // TN + n)),
                  pl.BlockSpec((tm, TN), lambda m, n: (m, O_M2 // TN + n))],
        out_specs=pl.BlockSpec((tm, TN), lambda m, n: (m, n)),
        out_shape=jax.ShapeDtypeStruct((tok.rows, D_MODEL), BF16),
        compiler_params=_params("arbitrary", "arbitrary"),
        name="merge",
    )(yret, mixed, w_ret_out, w_pool_out, proj, proj)


def _wo_kernel(merged_ref, w_ref, x_ref, gt_ref, o_ref):
    mix = jnp.dot(merged_ref[...], w_ref[0].astype(BF16), preferred_element_type=F32)
    o_ref[...] = x_ref[...] + (1.0 + gt_ref[0]) * mix


def _wo(tok, layer, merged, x, w_o):
    tok = tok.with_tm(TM_OUT)
    tm = tok.tm
    mod3 = tok.mod
    return pl.pallas_call(
        _wo_kernel,
        grid=(tok.n_tiles, D_MODEL // TN),
        in_specs=[pl.BlockSpec((tm, D_MODEL), lambda m, n: (m, 0), pipeline_mode=pl.Buffered(1)),
                  pl.BlockSpec((1, D_MODEL, TN), lambda m, n: (layer, 0, n)),
                  pl.BlockSpec((tm, TN), lambda m, n: (m, n)),
                  tok.mod_spec(layer, 2, tn=TN, col_axis=True)],
        out_specs=pl.BlockSpec((tm, TN), lambda m, n: (m, n)),
        out_shape=jax.ShapeDtypeStruct((tok.rows, D_MODEL), F32),
        compiler_params=_params("arbitrary", "arbitrary"),
        name="wo",
    )(merged, w_o, x, mod3)


def _mat(w_ref):
    return w_ref[(0,) * (len(w_ref.shape) - 2)]


def _final_norm(y, gfin_ref):
    return y * lax.rsqrt(jnp.mean(y * y, axis=-1, keepdims=True) + EPS) * gfin_ref[...]


def _ffn_kernel(x_ref, sc_ref, sh_ref, gt_ref, g_ref, w1_ref, w3_ref, w2_ref, *rest, n_f, with_final):
    gfin_ref = rest[0] if with_final else None
    o_ref, h_scr = rest[-2:]
    f = pl.program_id(1)

    @pl.when(f == 0)
    def _():
        _norm_rows(x_ref, g_ref, sc_ref, sh_ref, h_scr)
        o_ref[...] = jnp.zeros_like(o_ref)

    h = h_scr[...]
    a = jnp.dot(h, _mat(w1_ref).astype(BF16), preferred_element_type=F32)
    b = jnp.dot(h, _mat(w3_ref).astype(BF16), preferred_element_type=F32)
    o_ref[...] += jnp.dot((_silu(a) * b).astype(BF16), _mat(w2_ref).astype(BF16), preferred_element_type=F32)

    @pl.when(f == n_f - 1)
    def _():
        y = x_ref[...] + (1.0 + gt_ref[0]) * o_ref[...]
        o_ref[...] = _final_norm(y, gfin_ref) if with_final else y


def _ffn(tok, layer, x, g_ffn, w1, w3, w2, g_final=None):
    j = layer // 2
    d_ff = w1.shape[-1]
    tok_f = tok.with_tm(TM_FFN)
    tm, n_tiles = tok_f.tm, tok_f.n_tiles
    mod3 = tok_f.mod
    assert d_ff % TF == 0
    n_f = d_ff // TF
    once = pl.Buffered(1)
    in_specs = [pl.BlockSpec((tm, D_MODEL), lambda m, f: (m, 0), pipeline_mode=once),
                tok_f.mod_spec(layer, 4), tok_f.mod_spec(layer, 3), tok_f.mod_spec(layer, 5),
                pl.BlockSpec((1, 1, D_MODEL), lambda m, f: (layer, 0, 0)),
                pl.BlockSpec((1, D_MODEL, TF), lambda m, f: (j, 0, f)),
                pl.BlockSpec((1, D_MODEL, TF), lambda m, f: (j, 0, f)),
                pl.BlockSpec((1, TF, D_MODEL), lambda m, f: (j, f, 0))]
    args = [x, mod3, mod3, mod3, g_ffn, w1, w3, w2]
    if g_final is not None:
        in_specs.append(pl.BlockSpec((1, D_MODEL), lambda m, f: (0, 0)))
        args.append(g_final)
    return pl.pallas_call(
        functools.partial(_ffn_kernel, n_f=n_f, with_final=g_final is not None),
        grid=(n_tiles, n_f),
        in_specs=in_specs,
        out_specs=pl.BlockSpec((tm, D_MODEL), lambda m, f: (m, 0), pipeline_mode=once),
        out_shape=jax.ShapeDtypeStruct((tok.rows, D_MODEL), F32),
        scratch_shapes=[pltpu.VMEM((tm, D_MODEL), BF16)],
        compiler_params=_params("arbitrary", "arbitrary"),
        name="ffn",
    )(*args)


def _router_kernel(x_ref, sc_ref, sh_ref, g_ref, w_ref, b_ref, *rest):
    esel_ref, wsel_ref, h_ref = rest[-3:]
    h = _rmsnorm_mod(x_ref[...], g_ref[0], sc_ref[0], sh_ref[0])
    logits = jnp.dot(h.astype(BF16), w_ref[0].astype(BF16), preferred_element_type=F32) + b_ref[0]
    idx = lax.broadcasted_iota(jnp.int32, logits.shape, 1)
    top1 = jnp.max(logits, axis=-1, keepdims=True)
    i1 = jnp.min(jnp.where(logits == top1, idx, N_EXPERTS), axis=-1, keepdims=True)
    rest = jnp.where(idx == i1, -jnp.inf, logits)
    top2 = jnp.max(rest, axis=-1, keepdims=True)
    i2 = jnp.min(jnp.where(rest == top2, idx, N_EXPERTS), axis=-1, keepdims=True)
    e2 = jnp.exp(top2 - top1)
    denom = 1.0 + e2
    w1, w2 = 1.0 / denom, e2 / denom
    esel_ref[...] = jnp.where(idx == 0, i1, i2)
    wsel_ref[...] = jnp.where(idx == 0, w1, w2)
    h_ref[...] = _pack_halves(h)


def _router(tok, layer, x, g_ffn, w_router, b_router, h_all=None, row0=0, rows_all=None):
    j = layer // 2
    tm = tok.tm
    rows_all = tok.rows if rows_all is None else rows_all
    assert row0 % tm == 0
    small = pl.BlockSpec((tm, N_EXPERTS), lambda m: (m, 0))
    mod3 = tok.mod
    in_specs = [pl.BlockSpec((tm, D_MODEL), lambda m: (m, 0)),
                tok.mod_spec(layer, 4), tok.mod_spec(layer, 3),
                pl.BlockSpec((1, 1, D_MODEL), lambda m: (layer, 0, 0)),
                pl.BlockSpec((1, D_MODEL, N_EXPERTS), lambda m: (j, 0, 0)),
                pl.BlockSpec((1, 1, N_EXPERTS), lambda m: (j, 0, 0))]
    args = [x, mod3, mod3, g_ffn, w_router, b_router]
    aliases = {}
    if h_all is not None:
        aliases = {len(args): 2}
        in_specs.append(pl.BlockSpec(memory_space=pl.ANY))
        args.append(h_all)
    return pl.pallas_call(
        _router_kernel,
        grid=(tok.n_tiles,),
        in_specs=in_specs,
        out_specs=[small, small, pl.BlockSpec((tm, D_MODEL // 2), lambda m: (row0 // tm + m, 0))],
        out_shape=[jax.ShapeDtypeStruct((tok.rows, N_EXPERTS), jnp.int32),
                   jax.ShapeDtypeStruct((tok.rows, N_EXPERTS), F32),
                   jax.ShapeDtypeStruct((rows_all, D_MODEL // 2), jnp.uint32)],
        input_output_aliases=aliases,
        compiler_params=_params("arbitrary"),
        name="router",
    )(*args)


def _route_plan(choice, tm):
    rows = choice.shape[0]
    n_pairs = 2 * rows
    n_tiles = n_pairs // tm + N_EXPERTS
    e_flat = choice.T.reshape(n_pairs)
    onehot = (e_flat[:, None] == jnp.arange(N_EXPERTS, dtype=jnp.int32)[None, :]).astype(jnp.int32)
    csum = jnp.cumsum(onehot, axis=0)
    rank = jnp.sum(csum * onehot, axis=1) - 1
    count = csum[-1]
    tiles_per = (count + tm - 1) // tm
    rows_per = jnp.maximum((((count + jnp.maximum(tiles_per, 1) - 1) // jnp.maximum(tiles_per, 1)) + 7) // 8 * 8, 8)
    tile_end = jnp.cumsum(tiles_per)
    tile_start = tile_end - tiles_per
    pos = (tile_start[e_flat] + rank // rows_per[e_flat]) * tm + rank % rows_per[e_flat]
    token = jnp.arange(n_pairs, dtype=jnp.int32) % rows
    src_row = jnp.zeros((n_tiles * tm,), jnp.int32).at[pos].set(token, unique_indices=True, mode='promise_in_bounds')
    tile_id = jnp.arange(n_tiles, dtype=jnp.int32)
    n_used = tile_end[-1]
    tile_expert = jnp.minimum(jnp.sum(tile_id[:, None] >= tile_end[None, :], axis=1), N_EXPERTS - 1).astype(jnp.int32)
    tile_rows = jnp.clip(count[tile_expert] - (tile_id - tile_start[tile_expert]) * rows_per[tile_expert],
                         0, rows_per[tile_expert])
    tile_rows = jnp.where(tile_id < n_used, tile_rows, 0).astype(jnp.int32)
    last_expert = tile_expert[jnp.maximum(n_used - 1, 0)]
    tile_expert = jnp.where(tile_rows > 0, tile_expert, last_expert)
    return tile_expert, tile_rows, src_row, pos.astype(jnp.int32)


def _row_copy(src_hbm, row, dst, r, sem):
    return pltpu.make_async_copy(src_hbm.at[pl.ds(row, 1)], dst.at[pl.ds(r, 1)], sem)


def _moe_kernel(te_ref, nr_ref, src_ref, h_hbm, w1_ref, w3_ref, w2_ref, o_ref, rows_scr, h_scr, acc_scr, sem,
                *, tm, sub, n_f):
    del te_ref
    m, f = pl.program_id(0), pl.program_id(1)
    n_m = pl.num_programs(0)
    half = D_MODEL // 2
    n_land = rows_scr.shape[0]
    per_step = n_land // n_f

    def start_row(tile, r):
        _row_copy(h_hbm, src_ref[tile * tm + jnp.minimum(r, tm - 1)], rows_scr, r, sem).start(
            priority=ROW_DMA_PRIORITY)

    def wait_tile():
        pltpu.make_async_copy(h_hbm.at[pl.ds(0, n_land)], rows_scr, sem).wait()

    @pl.when(f == 0)
    def _():
        acc_scr[...] = jnp.zeros_like(acc_scr)

        @pl.when(m == 0)
        def _():
            def body(r, c):
                start_row(0, r)
                return c
            lax.fori_loop(0, n_land, body, 0, unroll=8)

        wait_tile()
        first, second = _unpack_halves(rows_scr[0:tm, :])
        h_scr[:, :half] = first.astype(BF16)
        h_scr[:, half:] = second.astype(BF16)

    next_tile = jnp.minimum(m + 1, n_m - 1)

    def issue_slice():
        for j in range(per_step):
            start_row(next_tile, f * per_step + j)

    n_sub = (nr_ref[m] + sub - 1) // sub
    for n in range(0, tm // sub + 1):
        @pl.when(n_sub == n)
        def _(n=n):
            issue_slice()
            if n > 0:
                h = h_scr[0:n * sub, :]
                a = jnp.dot(h, _mat(w1_ref).astype(BF16), preferred_element_type=F32)
                b = jnp.dot(h, _mat(w3_ref).astype(BF16), preferred_element_type=F32)
                acc_scr[0:n * sub, :] += jnp.dot((_silu(a) * b).astype(BF16), _mat(w2_ref).astype(BF16),
                                                 preferred_element_type=F32)

    @pl.when(f == n_f - 1)
    def _():
        o_ref[...] = _pack_halves(acc_scr[...])

        @pl.when(m == n_m - 1)
        def _():
            wait_tile()


def _moe_sparse(layer, h, plan, w1, w3, w2, tm):
    j = layer // 2
    tile_expert, tile_rows, src_row, _ = plan
    n_tiles = tile_expert.shape[0]
    d_ff = w1.shape[3]
    assert d_ff % TF == 0 and tm % SUB_MOE == 0
    n_f = d_ff // TF

    def f_sel(m, f, nr):
        return jnp.where(nr[m] > 0, f, n_f - 1)

    per_step = -(-tm // n_f)
    per_step += -per_step % 8
    assert h.shape[0] >= n_f * per_step

    return pl.pallas_call(
        functools.partial(_moe_kernel, tm=tm, sub=SUB_MOE, n_f=n_f),
        grid_spec=pltpu.PrefetchScalarGridSpec(
            num_scalar_prefetch=3,
            grid=(n_tiles, n_f),
            in_specs=[pl.BlockSpec(memory_space=pl.ANY),
                      pl.BlockSpec((1, 1, D_MODEL, TF), lambda m, f, te, nr, sr: (j, te[m], 0, f_sel(m, f, nr))),
                      pl.BlockSpec((1, 1, D_MODEL, TF), lambda m, f, te, nr, sr: (j, te[m], 0, f_sel(m, f, nr))),
                      pl.BlockSpec((1, 1, TF, D_MODEL), lambda m, f, te, nr, sr: (j, te[m], f_sel(m, f, nr), 0))],
            out_specs=pl.BlockSpec((tm, D_MODEL // 2), lambda m, f, te, nr, sr: (m, 0)),
            scratch_shapes=[pltpu.VMEM((n_f * per_step, D_MODEL // 2), jnp.uint32), pltpu.VMEM((tm, D_MODEL), BF16),
                            pltpu.VMEM((tm, D_MODEL), F32), pltpu.SemaphoreType.DMA(())]),
        out_shape=jax.ShapeDtypeStruct((n_tiles * tm, D_MODEL // 2), jnp.uint32),
        compiler_params=_params("arbitrary", "arbitrary"),
        name="moe",
    )(tile_expert, tile_rows, src_row, h, w1, w3, w2)


def _combine_kernel(pos_ref, y_hbm, x_ref, gt_ref, w_ref, *rest, tm, row0, rows_all, with_final):
    gfin_ref = rest[0] if with_final else None
    o_ref, y_scr, sems = rest[-3:]
    i, n = pl.program_id(0), pl.num_programs(0)

    def for_each_row(tile, slot, fn):
        def body(r, c):
            for k in range(2):
                slot_of_pair = pos_ref[k * rows_all + row0 + tile * tm + r]
                fn(_row_copy(y_hbm, slot_of_pair, y_scr.at[slot, k], r, sems.at[slot]))
            return c
        lax.fori_loop(0, tm, body, 0, unroll=8)

    @pl.when(i == 0)
    def _():
        for_each_row(0, 0, lambda cp: cp.start(priority=ROW_DMA_PRIORITY))

    @pl.when(i + 1 < n)
    def _():
        for_each_row(i + 1, (i + 1) % 2, lambda cp: cp.start(priority=ROW_DMA_PRIORITY))

    slot = i % 2
    for k in range(2):
        pltpu.make_async_copy(y_hbm.at[pl.ds(0, tm)], y_scr.at[slot, k], sems.at[slot]).wait()
    w = w_ref[...]
    first0, second0 = _unpack_halves(y_scr[slot, 0])
    first1, second1 = _unpack_halves(y_scr[slot, 1])
    f = jnp.concatenate([w[:, 0:1] * first0 + w[:, 1:2] * first1,
                         w[:, 0:1] * second0 + w[:, 1:2] * second1], axis=1)
    y = x_ref[...] + (1.0 + gt_ref[0]) * f
    o_ref[...] = _final_norm(y, gfin_ref) if with_final else y


def _combine(tok, layer, y_sorted, pos, row0, x, wsel, g_final):
    tok_c = tok.with_tm(TM_COMBINE)
    tm = tok_c.tm
    mod3 = tok_c.mod
    in_specs = [pl.BlockSpec(memory_space=pl.ANY),
                pl.BlockSpec((tm, D_MODEL), lambda m, pos: (m, 0)),
                tok_c.mod_spec(layer, 5),
                pl.BlockSpec((tm, N_EXPERTS), lambda m, pos: (m, 0))]
    args = [pos, y_sorted, x, mod3, wsel]
    if g_final is not None:
        in_specs.append(pl.BlockSpec((1, D_MODEL), lambda m, pos: (0, 0)))
        args.append(g_final)
    return pl.pallas_call(
        functools.partial(_combine_kernel, tm=tm, row0=row0, rows_all=pos.shape[0] // 2,
                          with_final=g_final is not None),
        grid_spec=pltpu.PrefetchScalarGridSpec(
            num_scalar_prefetch=1,
            grid=(tok_c.n_tiles,),
            in_specs=in_specs,
            out_specs=pl.BlockSpec((tm, D_MODEL), lambda m, pos: (m, 0)),
            scratch_shapes=[pltpu.VMEM((2, 2, tm, D_MODEL // 2), jnp.uint32), pltpu.SemaphoreType.DMA((2,))]),
        out_shape=jax.ShapeDtypeStruct((tok.rows, D_MODEL), F32),
        compiler_params=_params("arbitrary"),
        name="combine",
    )(*args)


class _Stream:
    def __init__(self, tok, x, ret_state, pool_state):
        self.tok, self.x, self.ret_state, self.pool_state = tok, x, ret_state, pool_state
        self.rope_tabs = tok.rope_tables()
        self.new_state = None
        self.u_layers = []


def _mixer_layer(s, i, p):
    tok = s.tok
    is_prompt = s.ret_state is None
    proj, u = _inproj(tok, i, s.x, p['g_mix'], p['w_in'], s.rope_tabs, BF16 if is_prompt else F32)
    s.u_layers.append(u)
    if is_prompt:
        yret, s.new_state = _ret_prompt(tok, i, proj, p['g_ret'], s.new_state)
        mixed = _pool_prompt(tok, i, u, p['w_pool_map'], p['pool_scale'])
    else:
        yret, s.new_state = _ret_sample(tok, i, proj, p['g_ret'], s.ret_state, s.new_state)
        mixed = _pool_sample(tok, i, u, s.pool_state[i].transpose(1, 0, 2), p['w_pool_map'], p['pool_scale'])
    merged = _merge(tok, i, yret, mixed, proj, p['w_ret_out'], p['w_pool_out'])
    s.x = _wo(tok, i, merged, s.x, p['w_o'])


def _moe_layer(streams, i, p, g_final):
    rows_all = sum(s.tok.rows for s in streams)
    h_all, row0, choices, weights, starts = None, 0, [], [], []
    for s in streams:
        esel, wsel, h_all = _router(s.tok, i, s.x, p['g_ffn'], p['w_router'], p['b_router'],
                                    h_all=h_all, row0=row0, rows_all=rows_all)
        choices.append(esel[:, :2])
        weights.append(wsel)
        starts.append(row0)
        row0 += s.tok.rows
    plan = _route_plan(jnp.concatenate(choices, axis=0), TM_MOE)
    y_sorted = _moe_sparse(i, h_all, plan, p['w1_moe'], p['w3_moe'], p['w2_moe'], TM_MOE)
    for s, wsel, start in zip(streams, weights, starts):
        s.x = _combine(s.tok, i, y_sorted, plan[3], start, s.x, wsel, g_final)


def kernel(x_prompt, x_sample, c_prompt, c_sample, state_ret, state_pool, w_ada, b_ada, g_mix, w_in, g_ret,
           w_ret_out, w_pool_map, pool_scale, w_pool_out, w_o, g_ffn, w1_dense, w3_dense, w2_dense, w_router,
           b_router, w1_moe, w3_moe, w2_moe, g_final):
    row3 = lambda a: a.reshape(a.shape[0], 1, a.shape[1])
    p = dict(g_mix=row3(g_mix), w_in=w_in, g_ret=row3(g_ret), w_ret_out=w_ret_out, w_pool_map=w_pool_map,
             pool_scale=row3(pool_scale), w_pool_out=w_pool_out, w_o=w_o, g_ffn=row3(g_ffn),
             w1_dense=w1_dense, w3_dense=w3_dense, w2_dense=w2_dense, w_router=w_router, b_router=row3(b_router),
             w1_moe=w1_moe, w3_moe=w3_moe, w2_moe=w2_moe, g_final=g_final.reshape(1, D_MODEL))
    n_bp, seq_p, _ = x_prompt.shape
    n_bs, seq_s, _ = x_sample.shape

    mod_p, mod_s = _ada(c_prompt, jnp.repeat(c_sample, seq_s, axis=0), w_ada, b_ada)

    tok_p = _Tokens(n_bp, seq_p, 0, TM_PROMPT, per_row_mod=False, mod=mod_p)
    tok_s = _Tokens(n_bs, seq_s, PAST_LEN, TM_SAMPLE, per_row_mod=True, mod=mod_s)
    prompt = _Stream(tok_p, x_prompt.reshape(tok_p.rows, D_MODEL), None, None)
    sample = _Stream(tok_s, x_sample.reshape(tok_s.rows, D_MODEL), state_ret, state_pool)
    streams = [prompt, sample]

    for i in range(DEPTH):
        g_fin = p['g_final'] if i == DEPTH - 1 else None
        for s in streams:
            _mixer_layer(s, i, p)
        if i % 2 == 0:
            for s in streams:
                s.x = _ffn(s.tok, i, s.x, p['g_ffn'], p['w1_dense'], p['w3_dense'], p['w2_dense'], g_final=g_fin)
        else:
            _moe_layer(streams, i, p, g_fin)

    buf_prompt = jnp.stack([u.reshape(n_bp, seq_p, D_POOL)[:, seq_p - POOL_BUF:] for u in prompt.u_layers])
    buf_sample = jnp.stack([
        jnp.concatenate([state_pool[i], sample.u_layers[i].reshape(n_bs, seq_s, D_POOL)], axis=1)[:, -POOL_BUF:]
        for i in range(DEPTH)])
    return (prompt.x.reshape(x_prompt.shape), sample.x.reshape(x_sample.shape), prompt.new_state, buf_prompt,
            sample.new_state, buf_sample)
```

```python
import functools

import numpy as np
import jax
import jax.numpy as jnp
from jax import lax
from jax.experimental import pallas as pl
from jax.experimental.pallas import tpu as pltpu

F32 = jnp.float32
BF16 = jnp.bfloat16

D_MODEL = 2048
DEPTH = 2
PAST_LEN = 16384
RET_HEADS = 8
RET_DK = 128
RET_DV = 256
RET_CHUNK = 128
ROPE_BASE = 10000.0
QK_W = RET_HEADS * RET_DK
V_W = RET_HEADS * RET_DV
POOL_WINDOWS = (2, 4, 8, 16)
D_POOL = 1024
POOL_GW = 256
POOL_BUF = 15
POOL_HALO = 16
N_IN = 2 * QK_W + 2 * V_W + D_POOL + 2 * D_MODEL
N_EXPERTS = 8
EPS = 1e-6
N_MOD = 6

O_Q, O_K, O_V, O_G, O_U = 0, QK_W, 2 * QK_W, 2 * QK_W + V_W, 2 * QK_W + 2 * V_W
O_M1 = O_U + D_POOL
O_M2 = O_M1 + D_MODEL

VMEM_LIMIT_BYTES = 56 * 1024 * 1024
SUBLANES = 8
TN = 512
TF = 256
TM_PROMPT = 1024
TM_SAMPLE = 512
TM_FFN = 1024
TM_OUT = 2048
ROW_CHUNK = 256
NORM_ROWS = 16
NORM_UNROLL = 16
TL_POOL = 512
TM_MOE = 1024
SUB_MOE = 256
TM_COMBINE = 256


def _params(*sem):
    return pltpu.CompilerParams(dimension_semantics=sem, vmem_limit_bytes=VMEM_LIMIT_BYTES)


class _Tokens:
    def __init__(self, n_batch, seq, n_past, tm, per_row_mod, mod=None):
        self.n_batch, self.seq, self.n_past = n_batch, seq, n_past
        self.rows = n_batch * seq
        self.tm = min(tm, self.rows)
        self.per_row_mod = per_row_mod
        if not per_row_mod:
            self.tm = min(self.tm, seq)
            assert seq % self.tm == 0
        assert self.rows % self.tm == 0
        self.n_tiles = self.rows // self.tm
        if mod is not None and not per_row_mod and mod.ndim == 3 and mod.shape[1] != 1:
            mod = mod.reshape(DEPTH * n_batch, 1, N_MOD * D_MODEL)
        self.mod = mod

    def with_tm(self, tm):
        return _Tokens(self.n_batch, self.seq, self.n_past, tm, self.per_row_mod, self.mod)

    def mod_spec(self, layer, chunk, tn=D_MODEL, col_axis=False):
        per_chunk = D_MODEL // tn
        col = (lambda n: chunk * per_chunk + n) if col_axis else (lambda n: chunk * per_chunk)
        if self.per_row_mod:
            if col_axis:
                return pl.BlockSpec((1, self.tm, tn), lambda m, n, *_: (layer, m, col(n)))
            return pl.BlockSpec((1, self.tm, tn), lambda m, *_: (layer, m, col(0)))
        tiles_per_batch = self.seq // self.tm
        base = layer * self.n_batch
        if col_axis:
            return pl.BlockSpec((1, 1, tn), lambda m, n, *_: (base + m // tiles_per_batch, 0, col(n)))
        return pl.BlockSpec((1, 1, tn), lambda m, *_: (base + m // tiles_per_batch, 0, col(0)))

    def rope_tables(self):
        half = RET_DK // 2
        pos = jnp.arange(self.seq, dtype=F32) + float(self.n_past)
        inv = ROPE_BASE ** (-jnp.arange(half, dtype=F32) / half)
        ang = pos[:, None] * inv[None, :]
        cos, sin = jnp.cos(ang), jnp.sin(ang)
        cos_full = jnp.concatenate([cos, cos], axis=1)
        sin_signed = jnp.concatenate([-sin, sin], axis=1)
        if self.per_row_mod:
            cos_full = jnp.tile(cos_full, (self.n_batch, 1))
            sin_signed = jnp.tile(sin_signed, (self.n_batch, 1))
        return cos_full, sin_signed

    def rope_spec(self):
        n_blocks = (self.rows if self.per_row_mod else self.seq) // self.tm
        return pl.BlockSpec((self.tm, RET_DK), lambda m, *_: (m % n_blocks, 0))


def _retention_tables(chunk):
    log_gamma = jnp.log1p(-jnp.exp2(-5.0 - jnp.arange(RET_HEADS, dtype=F32)))
    idx = jnp.arange(chunk, dtype=F32)
    rel = idx[:, None] - idx[None, :]
    dmask = jnp.where(rel[None] >= 0, jnp.exp(log_gamma[:, None, None] * jnp.maximum(rel, 0.0)[None]), 0.0)
    q_dec = jnp.exp(log_gamma[:, None] * (idx + 1.0)[None, :])
    k_dec = jnp.exp(log_gamma[:, None] * (chunk - 1.0 - idx)[None, :])
    chunk_dec = jnp.exp(log_gamma * chunk)
    q_dec = jnp.broadcast_to(q_dec[:, :, None], (RET_HEADS, chunk, RET_DV))
    k_dec = jnp.broadcast_to(k_dec[:, :, None], (RET_HEADS, chunk, RET_DK))
    chunk_dec = jnp.broadcast_to(chunk_dec[:, None, None], (RET_HEADS, 1, RET_DV))
    return dmask, q_dec, k_dec, chunk_dec


def _rmsnorm_mod(x, g, scale, shift):
    y = x * lax.rsqrt(jnp.mean(x * x, axis=-1, keepdims=True) + EPS) * g
    return y * (1.0 + scale) + shift


def _norm_rows(x_ref, g_ref, sc_ref, sh_ref, h_ref):
    tm = x_ref.shape[0]
    step = min(NORM_ROWS, tm)
    trips = tm // step
    per_row = sc_ref.shape[1] != 1
    if not per_row:
        gain, shift = g_ref[0] * (1.0 + sc_ref[0]), sh_ref[0]

    def body(c, carry):
        rows = pl.ds(pl.multiple_of(c * step, step), step)
        x = x_ref[rows, :]
        y = x * lax.rsqrt(jnp.mean(x * x, axis=-1, keepdims=True) + EPS)
        if per_row:
            y = y * (g_ref[0] * (1.0 + sc_ref[0, rows, :])) + sh_ref[0, rows, :]
        else:
            y = y * gain + shift
        h_ref[rows, :] = y.astype(h_ref.dtype)
        return carry

    lax.fori_loop(0, trips, body, 0, unroll=NORM_UNROLL if trips % NORM_UNROLL == 0 else 1)


def _silu(x):
    return x * jax.nn.sigmoid(x)


def _pack_halves(y):
    half = y.shape[1] // 2
    hi = lax.bitcast_convert_type(y[:, :half].astype(jnp.bfloat16).astype(F32), jnp.uint32)
    lo = lax.bitcast_convert_type(y[:, half:].astype(jnp.bfloat16).astype(F32), jnp.uint32)
    return hi | (lo >> 16)


def _unpack_halves(u):
    return (lax.bitcast_convert_type(u & jnp.uint32(0xFFFF0000), F32),
            lax.bitcast_convert_type(u << 16, F32))


def _ada_kernel(cp_ref, cs_ref, w_ref, b_ref, op_ref, os_ref):
    w = w_ref[0].astype(BF16)
    for c_ref, o_ref in ((cp_ref, op_ref), (cs_ref, os_ref)):
        o_ref[0] = jnp.dot(_silu(c_ref[...]).astype(BF16), w, preferred_element_type=F32) + b_ref[0]


def _ada(c_p, c_s, w_ada, b_ada):
    tn = 1024
    width = N_MOD * D_MODEL
    row_spec = lambda c: pl.BlockSpec(c.shape, lambda l, n: (0, 0))
    out_spec = lambda c: pl.BlockSpec((1, c.shape[0], tn), lambda l, n: (l, 0, n))
    return pl.pallas_call(
        _ada_kernel,
        grid=(DEPTH, width // tn),
        in_specs=[row_spec(c_p), row_spec(c_s),
                  pl.BlockSpec((1, D_MODEL, tn), lambda l, n: (l, 0, n)),
                  pl.BlockSpec((1, 1, tn), lambda l, n: (l, 0, n))],
        out_specs=[out_spec(c_p), out_spec(c_s)],
        out_shape=[jax.ShapeDtypeStruct((DEPTH, c.shape[0], width), F32) for c in (c_p, c_s)],
        compiler_params=_params("arbitrary", "arbitrary"),
        name="ada",
    )(c_p, c_s, w_ada, b_ada.reshape(DEPTH, 1, width))


def _inproj_kernel(x_ref, sc_ref, sh_ref, g_ref, w_ref, cos_ref, sin_ref, o_ref, u_ref, h_scr):
    n = pl.program_id(1)
    tm = x_ref.shape[0]
    rc = min(ROW_CHUNK, tm)

    @pl.when(n == 0)
    def _():
        _norm_rows(x_ref, g_ref, sc_ref, sh_ref, h_scr)

    def project(epilogue):
        w = w_ref[0].astype(BF16)
        for r in range(0, tm, rc):
            rows = slice(r, r + rc)
            epilogue(jnp.dot(h_scr[rows, :], w, preferred_element_type=F32), rows)

    def store(acc, rows):
        o_ref[rows, :] = acc.astype(o_ref.dtype)

    def rope(scale):
        def epilogue(acc, rows):
            for j in range(TN // RET_DK):
                cols = slice(j * RET_DK, (j + 1) * RET_DK)
                xh = acc[:, cols]
                r = xh * cos_ref[rows, :] + pltpu.roll(xh, RET_DK // 2, 1) * sin_ref[rows, :]
                if scale != 1.0:
                    r = r * scale
                o_ref[rows, cols] = r.astype(o_ref.dtype)
        project(epilogue)

    @pl.when(n < O_K // TN)
    def _():
        rope(1.0)

    @pl.when(jnp.logical_and(n >= O_K // TN, n < O_V // TN))
    def _():
        rope(RET_DK ** -0.5)

    @pl.when(jnp.logical_and(n >= O_V // TN, n < O_G // TN))
    def _():
        project(store)

    @pl.when(jnp.logical_and(n >= O_G // TN, n < O_U // TN))
    def _():
        project(lambda acc, rows: store(_silu(acc), rows))

    @pl.when(jnp.logical_and(n >= O_U // TN, n < O_M1 // TN))
    def _():
        def epilogue(acc, rows):
            store(acc, rows)
            u_ref[rows, :] = acc
        project(epilogue)

    @pl.when(n >= O_M1 // TN)
    def _():
        project(lambda acc, rows: store(jax.nn.sigmoid(acc), rows))


def _inproj(tok, layer, x, g_mix, w_in, rope_tabs, out_dtype):
    tm = tok.tm
    n_u = D_POOL // TN
    mod3 = tok.mod
    proj, u = pl.pallas_call(
        _inproj_kernel,
        grid=(tok.n_tiles, N_IN // TN),
        in_specs=[pl.BlockSpec((tm, D_MODEL), lambda m, n: (m, 0)),
                  tok.mod_spec(layer, 1), tok.mod_spec(layer, 0),
                  pl.BlockSpec((1, 1, D_MODEL), lambda m, n: (layer, 0, 0)),
                  pl.BlockSpec((1, D_MODEL, TN), lambda m, n: (layer, 0, n)),
                  tok.rope_spec(), tok.rope_spec()],
        out_specs=[pl.BlockSpec((tm, TN), lambda m, n: (m, n)),
                   pl.BlockSpec((tm, TN), lambda m, n: (m, jnp.clip(n - O_U // TN, 0, n_u - 1)))],
        out_shape=[jax.ShapeDtypeStruct((tok.rows, N_IN), out_dtype),
                   jax.ShapeDtypeStruct((tok.rows, D_POOL), F32)],
        scratch_shapes=[pltpu.VMEM((tm, D_MODEL), BF16)],
        compiler_params=_params("arbitrary", "arbitrary"),
        name="inproj",
    )(x, mod3, mod3, g_mix, w_in, *rope_tabs)
    return proj, u


def _head_norm_gate(o, gret, gate):
    mu = jnp.mean(o, axis=-1, keepdims=True)
    d = o - mu
    var = jnp.mean(d * d, axis=-1, keepdims=True)
    return gate.astype(F32) * (d * lax.rsqrt(var + EPS) * gret)


def _ret_prompt_kernel(q_ref, k_ref, v_ref, g_ref, gret_ref, dmask_ref, qdec_ref, kdec_ref, cdec_ref,
                       *rest):
    y_ref, s_ref = rest[-2:]

    @pl.when(pl.program_id(1) == 0)
    def _():
        s_ref[...] = jnp.zeros_like(s_ref)

    for h in range(RET_HEADS):
        ks, vs = slice(h * RET_DK, (h + 1) * RET_DK), slice(h * RET_DV, (h + 1) * RET_DV)
        q, k, v = q_ref[:, ks], k_ref[:, ks], v_ref[:, vs]
        state = s_ref[0, 0, h]
        scores = lax.dot_general(q, k, (((1,), (1,)), ((), ())), preferred_element_type=F32) * dmask_ref[h]
        inner = jnp.dot(scores.astype(BF16), v, preferred_element_type=F32)
        cross = jnp.dot(q, state.astype(BF16), preferred_element_type=F32) * qdec_ref[h]
        k_decayed = (k.astype(F32) * kdec_ref[h]).astype(BF16)
        s_ref[0, 0, h] = cdec_ref[h] * state + lax.dot_general(
            k_decayed, v, (((0,), (0,)), ((), ())), preferred_element_type=F32)
        y_ref[:, vs] = _head_norm_gate(inner + cross, gret_ref[0, :, vs], g_ref[:, vs]).astype(y_ref.dtype)


def _table_specs(tables):
    return [pl.BlockSpec(t.shape, lambda *_: (0, 0, 0)) for t in tables]


def _ret_prompt(tok, layer, proj, g_ret, s_all):
    n_b, seq = tok.n_batch, tok.seq
    c = RET_CHUNK
    assert seq % c == 0
    nc = seq // c
    tables = _retention_tables(c)
    in_specs = [pl.BlockSpec((c, QK_W), lambda b, i: (b * nc + i, O_Q // QK_W)),
                pl.BlockSpec((c, QK_W), lambda b, i: (b * nc + i, O_K // QK_W)),
                pl.BlockSpec((c, V_W), lambda b, i: (b * nc + i, O_V // V_W)),
                pl.BlockSpec((c, V_W), lambda b, i: (b * nc + i, O_G // V_W)),
                pl.BlockSpec((1, 1, V_W), lambda b, i: (layer, 0, 0))] + _table_specs(tables)
    args = [proj, proj, proj, proj, g_ret, *tables]
    aliases = {}
    if s_all is not None:
        aliases = {len(args): 1}
        in_specs.append(pl.BlockSpec(memory_space=pl.ANY))
        args.append(s_all)
    y, s_all = pl.pallas_call(
        _ret_prompt_kernel,
        grid=(n_b, nc),
        in_specs=in_specs,
        out_specs=[pl.BlockSpec((c, V_W), lambda b, i: (b * nc + i, 0)),
                   pl.BlockSpec((1, 1, RET_HEADS, RET_DK, RET_DV), lambda b, i: (layer, b, 0, 0, 0))],
        out_shape=[jax.ShapeDtypeStruct((tok.rows, V_W), BF16),
                   jax.ShapeDtypeStruct((DEPTH, n_b, RET_HEADS, RET_DK, RET_DV), F32)],
        input_output_aliases=aliases,
        compiler_params=_params("arbitrary", "arbitrary"),
        name="ret_prompt",
    )(*args)
    return y, s_all


def _ret_sample_kernel(q_ref, k_ref, v_ref, g_ref, kt_ref, gret_ref, s0_ref, dm_ref, qdec_ref, kdect_ref, cdec_ref,
                       *rest, bb, seq):
    y_ref, s_ref = rest[-2:]
    hs = RET_HEADS * seq
    reps = lambda a: jnp.concatenate([a] * RET_HEADS, axis=0)
    mask_q = (lax.broadcasted_iota(jnp.int32, (hs, QK_W), 0) // seq
              == lax.broadcasted_iota(jnp.int32, (hs, QK_W), 1) // RET_DK)
    mask_v = (lax.broadcasted_iota(jnp.int32, (hs, V_W), 0) // seq
              == lax.broadcasted_iota(jnp.int32, (hs, V_W), 1) // RET_DV)
    nt = (((1,), (1,)), ((), ()))
    for b in range(bb):
        rows_b = slice(b * seq, (b + 1) * seq)
        q, k, v = q_ref[rows_b, :], k_ref[rows_b, :], v_ref[rows_b, :]
        q_blk = jnp.where(mask_q, reps(q), 0.0).astype(BF16)
        scores = lax.dot_general(q_blk, reps(k).astype(BF16), nt, preferred_element_type=F32) * dm_ref[...]
        v_stack = jnp.concatenate([v[:, h * RET_DV:(h + 1) * RET_DV] for h in range(RET_HEADS)], axis=0)
        inner = jnp.dot(scores.astype(BF16), v_stack.astype(BF16), preferred_element_type=F32)
        state = s0_ref[0, b]
        cross = jnp.dot(q_blk, state.reshape(QK_W, RET_DV).astype(BF16), preferred_element_type=F32)
        o = inner + cross * qdec_ref[...]
        mu = jnp.mean(o, axis=-1, keepdims=True)
        d = o - mu
        normed = d * lax.rsqrt(jnp.mean(d * d, axis=-1, keepdims=True) + EPS)
        k_dec_t = (kt_ref[b] * kdect_ref[...]).astype(BF16)
        v_blk = jnp.where(mask_v, reps(v), 0.0).astype(BF16)
        upd = jnp.dot(k_dec_t, v_blk, preferred_element_type=F32)
        for h in range(RET_HEADS):
            vs = slice(h * RET_DV, (h + 1) * RET_DV)
            y_ref[rows_b, vs] = g_ref[rows_b, vs] * (normed[h * seq:(h + 1) * seq] * gret_ref[0, :, vs])
            s_ref[0, b, h] = cdec_ref[h] * state[h] + upd[:, vs]


def _ret_sample(tok, layer, proj, g_ret, state_ret, s_all):
    n_b, seq = tok.n_batch, tok.seq
    bb = 4
    assert n_b % bb == 0
    hs = RET_HEADS * seq
    dmask, q_dec, k_dec, chunk_dec = _retention_tables(seq)
    dm_blk = (dmask[:, :, None, :] * jnp.eye(RET_HEADS, dtype=F32)[:, None, :, None]).reshape(hs, hs)
    tables = [dm_blk, q_dec.reshape(hs, RET_DV), k_dec[:, :, 0].reshape(1, hs), chunk_dec]
    k_t = proj[:, O_K:O_K + QK_W].reshape(n_b, seq, RET_HEADS, RET_DK).transpose(0, 3, 2, 1).reshape(n_b, RET_DK, hs)
    state_spec = pl.BlockSpec((1, bb, RET_HEADS, RET_DK, RET_DV), lambda i: (layer, i, 0, 0, 0))
    in_specs = [pl.BlockSpec((bb * seq, QK_W), lambda i: (i, O_Q // QK_W)),
                pl.BlockSpec((bb * seq, QK_W), lambda i: (i, O_K // QK_W)),
                pl.BlockSpec((bb * seq, V_W), lambda i: (i, O_V // V_W)),
                pl.BlockSpec((bb * seq, V_W), lambda i: (i, O_G // V_W)),
                pl.BlockSpec((bb, RET_DK, hs), lambda i: (i, 0, 0)),
                pl.BlockSpec((1, 1, V_W), lambda i: (layer, 0, 0)),
                state_spec] + [pl.BlockSpec(t.shape, lambda i, nd=t.ndim: (0,) * nd) for t in tables]
    args = [proj, proj, proj, proj, k_t, g_ret, state_ret, *tables]
    aliases = {}
    if s_all is not None:
        aliases = {len(args): 1}
        in_specs.append(pl.BlockSpec(memory_space=pl.ANY))
        args.append(s_all)
    y, s_all = pl.pallas_call(
        functools.partial(_ret_sample_kernel, bb=bb, seq=seq),
        grid=(n_b // bb,),
        in_specs=in_specs,
        out_specs=[pl.BlockSpec((bb * seq, V_W), lambda i: (i, 0)), state_spec],
        out_shape=[jax.ShapeDtypeStruct((tok.rows, V_W), F32),
                   jax.ShapeDtypeStruct(state_ret.shape, F32)],
        input_output_aliases=aliases,
        compiler_params=_params("arbitrary"),
        name="ret_sample",
    )(*args)
    return y, s_all


def _pool_map(pooled, wmap_ref, scale_ref, gi):
    cols = slice(gi * POOL_GW, (gi + 1) * POOL_GW)
    mixed = jnp.dot(pooled.astype(BF16), wmap_ref[0, gi].astype(BF16), preferred_element_type=F32)
    return mixed * scale_ref[0, :, cols]


def _pool_prompt_kernel(u_ref, halo_ref, wmap_ref, scale_ref, o_ref, uf_scr, *, tl):
    first = pl.program_id(1) == 0
    uf_scr[0:POOL_HALO, :] = jnp.where(first, 0.0, halo_ref[...])
    uf_scr[POOL_HALO:, :] = u_ref[...]
    row = lax.broadcasted_iota(jnp.int32, (tl, POOL_GW), 0) + pl.program_id(1) * tl
    for gi, w in enumerate(POOL_WINDOWS):
        cols = slice(gi * POOL_GW, (gi + 1) * POOL_GW)
        u = uf_scr[POOL_HALO:, cols]
        acc = u
        for j in range(1, w):
            acc = acc + uf_scr[POOL_HALO - j:POOL_HALO - j + tl, cols]
        cnt = jnp.minimum(w, row + 1).astype(F32)
        o_ref[:, cols] = _pool_map(acc / cnt - u, wmap_ref, scale_ref, gi).astype(o_ref.dtype)


def _pool_prompt(tok, layer, u, w_pool_map, pool_scale):
    n_b, seq = tok.n_batch, tok.seq
    tl = min(TL_POOL, seq)
    assert seq % tl == 0 and tl % POOL_HALO == 0
    nl = seq // tl
    per = tl // POOL_HALO
    return pl.pallas_call(
        functools.partial(_pool_prompt_kernel, tl=tl),
        grid=(n_b, nl),
        in_specs=[pl.BlockSpec((tl, D_POOL), lambda b, l: (b * nl + l, 0)),
                  pl.BlockSpec((POOL_HALO, D_POOL), lambda b, l: (jnp.maximum((b * nl + l) * per - 1, 0), 0)),
                  pl.BlockSpec((1,) + w_pool_map.shape[1:], lambda b, l: (layer, 0, 0, 0)),
                  pl.BlockSpec((1, 1, D_POOL), lambda b, l: (layer, 0, 0))],
        out_specs=pl.BlockSpec((tl, D_POOL), lambda b, l: (b * nl + l, 0)),
        out_shape=jax.ShapeDtypeStruct((tok.rows, D_POOL), BF16),
        scratch_shapes=[pltpu.VMEM((POOL_HALO + tl, D_POOL), F32)],
        compiler_params=_params("arbitrary", "arbitrary"),
        name="pool_prompt",
    )(u, u, w_pool_map, pool_scale)


def _pool_sample_kernel(buf_ref, u_ref, wmap_ref, scale_ref, o_ref, *, seq, n_b):
    def row(j, cols):
        return buf_ref[j, :, cols] if j < POOL_BUF else u_ref[j - POOL_BUF, :, cols]

    for gi, w in enumerate(POOL_WINDOWS):
        cols = slice(gi * POOL_GW, (gi + 1) * POOL_GW)
        for t in range(seq):
            acc = row(POOL_BUF + t, cols)
            for j in range(1, w):
                acc = acc + row(POOL_BUF + t - j, cols)
            pooled = acc / float(w) - row(POOL_BUF + t, cols)
            o_ref[t, :, cols] = _pool_map(pooled, wmap_ref, scale_ref, gi)


def _pool_sample(tok, layer, u, buf_tm, w_pool_map, pool_scale):
    n_b, seq = tok.n_batch, tok.seq
    assert tok.n_past >= max(POOL_WINDOWS)
    u_tm = u.reshape(n_b, seq, D_POOL).transpose(1, 0, 2)
    mixed_tm = pl.pallas_call(
        functools.partial(_pool_sample_kernel, seq=seq, n_b=n_b),
        grid=(1,),
        in_specs=[pl.BlockSpec(buf_tm.shape, lambda i: (0, 0, 0)),
                  pl.BlockSpec(u_tm.shape, lambda i: (0, 0, 0)),
                  pl.BlockSpec((1,) + w_pool_map.shape[1:], lambda i: (layer, 0, 0, 0)),
                  pl.BlockSpec((1, 1, D_POOL), lambda i: (layer, 0, 0))],
        out_specs=pl.BlockSpec(u_tm.shape, lambda i: (0, 0, 0)),
        out_shape=jax.ShapeDtypeStruct(u_tm.shape, F32),
        compiler_params=_params("arbitrary"),
        name="pool_sample",
    )(buf_tm, u_tm, w_pool_map, pool_scale)
    return mixed_tm.transpose(1, 0, 2).reshape(tok.rows, D_POOL)


def _merge_kernel(yret_ref, mixed_ref, wr_ref, wp_ref, g1_ref, g2_ref, o_ref):
    ret = jnp.dot(yret_ref[...].astype(BF16), wr_ref[0].astype(BF16), preferred_element_type=F32)
    pool = jnp.dot(mixed_ref[...].astype(BF16), wp_ref[0].astype(BF16), preferred_element_type=F32)
    o_ref[...] = (g1_ref[...].astype(F32) * ret + g2_ref[...].astype(F32) * pool).astype(o_ref.dtype)


def _merge(tok, layer, yret, mixed, proj, w_ret_out, w_pool_out):
    tm = tok.tm
    return pl.pallas_call(
        _merge_kernel,
        grid=(tok.n_tiles, D_MODEL // TN),
        in_specs=[pl.BlockSpec((tm, V_W), lambda m, n: (m, 0)),
                  pl.BlockSpec((tm, D_POOL), lambda m, n: (m, 0)),
                  pl.BlockSpec((1, V_W, TN), lambda m, n: (layer, 0, n)),
                  pl.BlockSpec((1, D_POOL, TN), lambda m, n: (layer, 0, n)),
                  pl.BlockSpec((tm, TN), lambda m, n: (m, O_M1 // TN + n)),
                  pl.BlockSpec((tm, TN), lambda m, n: (m, O_M2 // TN + n))],
        out_specs=pl.BlockSpec((tm, TN), lambda m, n: (m, n)),
        out_shape=jax.ShapeDtypeStruct((tok.rows, D_MODEL), BF16),
        compiler_params=_params("arbitrary", "arbitrary"),
        name="merge",
    )(yret, mixed, w_ret_out, w_pool_out, proj, proj)


def _wo_kernel(merged_ref, w_ref, x_ref, gt_ref, o_ref):
    mix = jnp.dot(merged_ref[...], w_ref[0].astype(BF16), preferred_element_type=F32)
    o_ref[...] = x_ref[...] + (1.0 + gt_ref[0]) * mix


def _wo(tok, layer, merged, x, w_o):
    tok = tok.with_tm(TM_OUT)
    tm = tok.tm
    mod3 = tok.mod
    return pl.pallas_call(
        _wo_kernel,
        grid=(tok.n_tiles, D_MODEL // TN),
        in_specs=[pl.BlockSpec((tm, D_MODEL), lambda m, n: (m, 0), pipeline_mode=pl.Buffered(1)),
                  pl.BlockSpec((1, D_MODEL, TN), lambda m, n: (layer, 0, n)),
                  pl.BlockSpec((tm, TN), lambda m, n: (m, n)),
                  tok.mod_spec(layer, 2, tn=TN, col_axis=True)],
        out_specs=pl.BlockSpec((tm, TN), lambda m, n: (m, n)),
        out_shape=jax.ShapeDtypeStruct((tok.rows, D_MODEL), F32),
        compiler_params=_params("arbitrary", "arbitrary"),
        name="wo",
    )(merged, w_o, x, mod3)


def _mat(w_ref):
    return w_ref[(0,) * (len(w_ref.shape) - 2)]


def _final_norm(y, gfin_ref):
    return y * lax.rsqrt(jnp.mean(y * y, axis=-1, keepdims=True) + EPS) * gfin_ref[...]


def _ffn_kernel(x_ref, sc_ref, sh_ref, gt_ref, g_ref, w1_ref, w3_ref, w2_ref, *rest, n_f, with_final):
    gfin_ref = rest[0] if with_final else None
    o_ref, h_scr = rest[-2:]
    f = pl.program_id(1)

    @pl.when(f == 0)
    def _():
        _norm_rows(x_ref, g_ref, sc_ref, sh_ref, h_scr)
        o_ref[...] = jnp.zeros_like(o_ref)

    h = h_scr[...]
    a = jnp.dot(h, _mat(w1_ref).astype(BF16), preferred_element_type=F32)
    b = jnp.dot(h, _mat(w3_ref).astype(BF16), preferred_element_type=F32)
    o_ref[...] += jnp.dot((_silu(a) * b).astype(BF16), _mat(w2_ref).astype(BF16), preferred_element_type=F32)

    @pl.when(f == n_f - 1)
    def _():
        y = x_ref[...] + (1.0 + gt_ref[0]) * o_ref[...]
        o_ref[...] = _final_norm(y, gfin_ref) if with_final else y


def _ffn(tok, layer, x, g_ffn, w1, w3, w2, g_final=None):
    j = layer // 2
    d_ff = w1.shape[-1]
    tok_f = tok.with_tm(TM_FFN)
    tm, n_tiles = tok_f.tm, tok_f.n_tiles
    mod3 = tok_f.mod
    assert d_ff % TF == 0
    n_f = d_ff // TF
    once = pl.Buffered(1)
    in_specs = [pl.BlockSpec((tm, D_MODEL), lambda m, f: (m, 0), pipeline_mode=once),
                tok_f.mod_spec(layer, 4), tok_f.mod_spec(layer, 3), tok_f.mod_spec(layer, 5),
                pl.BlockSpec((1, 1, D_MODEL), lambda m, f: (layer, 0, 0)),
                pl.BlockSpec((1, D_MODEL, TF), lambda m, f: (j, 0, f)),
                pl.BlockSpec((1, D_MODEL, TF), lambda m, f: (j, 0, f)),
                pl.BlockSpec((1, TF, D_MODEL), lambda m, f: (j, f, 0))]
    args = [x, mod3, mod3, mod3, g_ffn, w1, w3, w2]
    if g_final is not None:
        in_specs.append(pl.BlockSpec((1, D_MODEL), lambda m, f: (0, 0)))
        args.append(g_final)
    return pl.pallas_call(
        functools.partial(_ffn_kernel, n_f=n_f, with_final=g_final is not None),
        grid=(n_tiles, n_f),
        in_specs=in_specs,
        out_specs=pl.BlockSpec((tm, D_MODEL), lambda m, f: (m, 0), pipeline_mode=once),
        out_shape=jax.ShapeDtypeStruct((tok.rows, D_MODEL), F32),
        scratch_shapes=[pltpu.VMEM((tm, D_MODEL), BF16)],
        compiler_params=_params("arbitrary", "arbitrary"),
        name="ffn",
    )(*args)


def _router_kernel(x_ref, sc_ref, sh_ref, g_ref, w_ref, b_ref, *rest):
    esel_ref, wsel_ref, h_ref = rest[-3:]
    h = _rmsnorm_mod(x_ref[...], g_ref[0], sc_ref[0], sh_ref[0])
    logits = jnp.dot(h.astype(BF16), w_ref[0].astype(BF16), preferred_element_type=F32) + b_ref[0]
    idx = lax.broadcasted_iota(jnp.int32, logits.shape, 1)
    top1 = jnp.max(logits, axis=-1, keepdims=True)
    i1 = jnp.min(jnp.where(logits == top1, idx, N_EXPERTS), axis=-1, keepdims=True)
    rest = jnp.where(idx == i1, -jnp.inf, logits)
    top2 = jnp.max(rest, axis=-1, keepdims=True)
    i2 = jnp.min(jnp.where(rest == top2, idx, N_EXPERTS), axis=-1, keepdims=True)
    e2 = jnp.exp(top2 - top1)
    denom = 1.0 + e2
    w1, w2 = 1.0 / denom, e2 / denom
    esel_ref[...] = jnp.where(idx == 0, i1, i2)
    wsel_ref[...] = jnp.where(idx == 0, w1, w2)
    h_ref[...] = _pack_halves(h)


def _router(tok, layer, x, g_ffn, w_router, b_router, h_all=None, row0=0, rows_all=None):
    j = layer // 2
    tm = tok.tm
    rows_all = tok.rows if rows_all is None else rows_all
    assert row0 % tm == 0
    small = pl.BlockSpec((tm, N_EXPERTS), lambda m: (m, 0))
    mod3 = tok.mod
    in_specs = [pl.BlockSpec((tm, D_MODEL), lambda m: (m, 0)),
                tok.mod_spec(layer, 4), tok.mod_spec(layer, 3),
                pl.BlockSpec((1, 1, D_MODEL), lambda m: (layer, 0, 0)),
                pl.BlockSpec((1, D_MODEL, N_EXPERTS), lambda m: (j, 0, 0)),
                pl.BlockSpec((1, 1, N_EXPERTS), lambda m: (j, 0, 0))]
    args = [x, mod3, mod3, g_ffn, w_router, b_router]
    aliases = {}
    if h_all is not None:
        aliases = {len(args): 2}
        in_specs.append(pl.BlockSpec(memory_space=pl.ANY))
        args.append(h_all)
    return pl.pallas_call(
        _router_kernel,
        grid=(tok.n_tiles,),
        in_specs=in_specs,
        out_specs=[small, small, pl.BlockSpec((tm, D_MODEL // 2), lambda m: (row0 // tm + m, 0))],
        out_shape=[jax.ShapeDtypeStruct((tok.rows, N_EXPERTS), jnp.int32),
                   jax.ShapeDtypeStruct((tok.rows, N_EXPERTS), F32),
                   jax.ShapeDtypeStruct((rows_all, D_MODEL // 2), jnp.uint32)],
        input_output_aliases=aliases,
        compiler_params=_params("arbitrary"),
        name="router",
    )(*args)


def _route_plan(choice, tm):
    rows = choice.shape[0]
    n_pairs = 2 * rows
    n_tiles = n_pairs // tm + N_EXPERTS
    e_flat = choice.T.reshape(n_pairs)
    onehot = (e_flat[:, None] == jnp.arange(N_EXPERTS, dtype=jnp.int32)[None, :]).astype(jnp.int32)
    csum = jnp.cumsum(onehot, axis=0)
    rank = jnp.sum(csum * onehot, axis=1) - 1
    count = csum[-1]
    tiles_per = (count + tm - 1) // tm
    rows_per = jnp.maximum((((count + jnp.maximum(tiles_per, 1) - 1) // jnp.maximum(tiles_per, 1)) + 7) // 8 * 8, 8)
    tile_end = jnp.cumsum(tiles_per)
    tile_start = tile_end - tiles_per
    pos = (tile_start[e_flat] + rank // rows_per[e_flat]) * tm + rank % rows_per[e_flat]
    token = jnp.arange(n_pairs, dtype=jnp.int32) % rows
    src_row = jnp.zeros((n_tiles * tm,), jnp.int32).at[pos].set(token, unique_indices=True, mode='promise_in_bounds')
    tile_id = jnp.arange(n_tiles, dtype=jnp.int32)
    n_used = tile_end[-1]
    tile_expert = jnp.minimum(jnp.sum(tile_id[:, None] >= tile_end[None, :], axis=1), N_EXPERTS - 1).astype(jnp.int32)
    tile_rows = jnp.clip(count[tile_expert] - (tile_id - tile_start[tile_expert]) * rows_per[tile_expert],
                         0, rows_per[tile_expert])
    tile_rows = jnp.where(tile_id < n_used, tile_rows, 0).astype(jnp.int32)
    last_expert = tile_expert[jnp.maximum(n_used - 1, 0)]
    tile_expert = jnp.where(tile_rows > 0, tile_expert, last_expert)
    return tile_expert, tile_rows, src_row, pos.astype(jnp.int32)


def _moe_kernel(te_ref, nr_ref, src_ref, h_hbm, w1_ref, w3_ref, w2_ref, o_ref, rows_scr, h_scr, acc_scr, sem,
                *, tm, sub, n_f):
    del te_ref
    m, f = pl.program_id(0), pl.program_id(1)
    half = D_MODEL // 2
    group = rows_scr.shape[1]

    def n_groups(tile):
        return (nr_ref[tile] + group - 1) // group

    def start_rows(tile):
        def body(i, c):
            for j in range(group):
                row = src_ref[tile * tm + i * group + j]
                pltpu.make_async_copy(h_hbm.at[pl.ds(row, 1)], rows_scr.at[i, pl.ds(j, 1)], sem).start()
            return c
        lax.fori_loop(0, n_groups(tile), body, 0)

    def wait_rows(tile):
        def body(i, c):
            pltpu.make_async_copy(h_hbm.at[pl.ds(0, group)], rows_scr.at[i], sem).wait()
            return c
        lax.fori_loop(0, n_groups(tile), body, 0)

    @pl.when(f == 0)
    def _():
        acc_scr[...] = jnp.zeros_like(acc_scr)

        @pl.when(m == 0)
        def _():
            rows_scr[...] = jnp.zeros_like(rows_scr)
            start_rows(0)

        wait_rows(m)
        first, second = _unpack_halves(rows_scr[...].reshape(tm, half))
        h_scr[:, :half] = first.astype(BF16)
        h_scr[:, half:] = second.astype(BF16)

        @pl.when(m + 1 < pl.num_programs(0))
        def _():
            start_rows(m + 1)

    n_sub = (nr_ref[m] + sub - 1) // sub
    for n in range(1, tm // sub + 1):
        @pl.when(n_sub == n)
        def _(n=n):
            h = h_scr[0:n * sub, :]
            a = jnp.dot(h, _mat(w1_ref).astype(BF16), preferred_element_type=F32)
            b = jnp.dot(h, _mat(w3_ref).astype(BF16), preferred_element_type=F32)
            acc_scr[0:n * sub, :] += jnp.dot((_silu(a) * b).astype(BF16), _mat(w2_ref).astype(BF16),
                                             preferred_element_type=F32)

    @pl.when(f == n_f - 1)
    def _():
        o_ref[...] = _pack_halves(acc_scr[...])


def _moe_sparse(layer, h, plan, w1, w3, w2, tm):
    j = layer // 2
    tile_expert, tile_rows, src_row, _ = plan
    n_tiles = tile_expert.shape[0]
    d_ff = w1.shape[3]
    assert d_ff % TF == 0 and tm % SUB_MOE == 0
    n_f = d_ff // TF

    def f_sel(m, f, nr):
        return jnp.where(nr[m] > 0, f, n_f - 1)

    return pl.pallas_call(
        functools.partial(_moe_kernel, tm=tm, sub=SUB_MOE, n_f=n_f),
        grid_spec=pltpu.PrefetchScalarGridSpec(
            num_scalar_prefetch=3,
            grid=(n_tiles, n_f),
            in_specs=[pl.BlockSpec(memory_space=pl.ANY),
                      pl.BlockSpec((1, 1, D_MODEL, TF), lambda m, f, te, nr, sr: (j, te[m], 0, f_sel(m, f, nr))),
                      pl.BlockSpec((1, 1, D_MODEL, TF), lambda m, f, te, nr, sr: (j, te[m], 0, f_sel(m, f, nr))),
                      pl.BlockSpec((1, 1, TF, D_MODEL), lambda m, f, te, nr, sr: (j, te[m], f_sel(m, f, nr), 0))],
            out_specs=pl.BlockSpec((tm, D_MODEL // 2), lambda m, f, te, nr, sr: (m, 0)),
            scratch_shapes=[pltpu.VMEM((tm // SUBLANES, SUBLANES, D_MODEL // 2), jnp.uint32),
                            pltpu.VMEM((tm, D_MODEL), BF16),
                            pltpu.VMEM((tm, D_MODEL), F32), pltpu.SemaphoreType.DMA(())]),
        out_shape=jax.ShapeDtypeStruct((n_tiles * tm, D_MODEL // 2), jnp.uint32),
        compiler_params=_params("arbitrary", "arbitrary"),
        name="moe",
    )(tile_expert, tile_rows, src_row, h, w1, w3, w2)


def _combine_kernel(pos_ref, y_hbm, x_ref, gt_ref, w_ref, *rest, tm, row0, rows_all, with_final):
    gfin_ref = rest[0] if with_final else None
    o_ref, y_scr, sems = rest[-3:]
    i, n = pl.program_id(0), pl.num_programs(0)

    half = D_MODEL // 2
    group = y_scr.shape[3]

    def start_rows(tile, slot):
        def body(g, c):
            for k in range(2):
                for j in range(group):
                    slot_of_pair = pos_ref[k * rows_all + row0 + tile * tm + g * group + j]
                    pltpu.make_async_copy(y_hbm.at[pl.ds(slot_of_pair, 1)], y_scr.at[slot, k, g, pl.ds(j, 1)],
                                          sems.at[slot]).start()
            return c
        lax.fori_loop(0, tm // group, body, 0)

    @pl.when(i == 0)
    def _():
        start_rows(0, 0)

    @pl.when(i + 1 < n)
    def _():
        start_rows(i + 1, (i + 1) % 2)

    slot = i % 2
    def wait_group(g, c):
        for k in range(2):
            pltpu.make_async_copy(y_hbm.at[pl.ds(0, group)], y_scr.at[slot, k, g], sems.at[slot]).wait()
        return c

    lax.fori_loop(0, tm // group, wait_group, 0)
    w = w_ref[...]
    first0, second0 = _unpack_halves(y_scr[slot, 0].reshape(tm, half))
    first1, second1 = _unpack_halves(y_scr[slot, 1].reshape(tm, half))
    f = jnp.concatenate([w[:, 0:1] * first0 + w[:, 1:2] * first1,
                         w[:, 0:1] * second0 + w[:, 1:2] * second1], axis=1)
    y = x_ref[...] + (1.0 + gt_ref[0]) * f
    o_ref[...] = _final_norm(y, gfin_ref) if with_final else y


def _combine(tok, layer, y_sorted, pos, row0, x, wsel, g_final):
    tok_c = tok.with_tm(TM_COMBINE)
    tm = tok_c.tm
    mod3 = tok_c.mod
    in_specs = [pl.BlockSpec(memory_space=pl.ANY),
                pl.BlockSpec((tm, D_MODEL), lambda m, pos: (m, 0)),
                tok_c.mod_spec(layer, 5),
                pl.BlockSpec((tm, N_EXPERTS), lambda m, pos: (m, 0))]
    args = [pos, y_sorted, x, mod3, wsel]
    if g_final is not None:
        in_specs.append(pl.BlockSpec((1, D_MODEL), lambda m, pos: (0, 0)))
        args.append(g_final)
    return pl.pallas_call(
        functools.partial(_combine_kernel, tm=tm, row0=row0, rows_all=pos.shape[0] // 2,
                          with_final=g_final is not None),
        grid_spec=pltpu.PrefetchScalarGridSpec(
            num_scalar_prefetch=1,
            grid=(tok_c.n_tiles,),
            in_specs=in_specs,
            out_specs=pl.BlockSpec((tm, D_MODEL), lambda m, pos: (m, 0)),
            scratch_shapes=[pltpu.VMEM((2, 2, tm // SUBLANES, SUBLANES, D_MODEL // 2), jnp.uint32),
                            pltpu.SemaphoreType.DMA((2,))]),
        out_shape=jax.ShapeDtypeStruct((tok.rows, D_MODEL), F32),
        compiler_params=_params("arbitrary"),
        name="combine",
    )(*args)


class _Stream:
    def __init__(self, tok, x, ret_state, pool_state):
        self.tok, self.x, self.ret_state, self.pool_state = tok, x, ret_state, pool_state
        self.rope_tabs = tok.rope_tables()
        self.new_state = None
        self.u_layers = []


def _mixer_layer(s, i, p):
    tok = s.tok
    is_prompt = s.ret_state is None
    proj, u = _inproj(tok, i, s.x, p['g_mix'], p['w_in'], s.rope_tabs, BF16 if is_prompt else F32)
    s.u_layers.append(u)
    if is_prompt:
        yret, s.new_state = _ret_prompt(tok, i, proj, p['g_ret'], s.new_state)
        mixed = _pool_prompt(tok, i, u, p['w_pool_map'], p['pool_scale'])
    else:
        yret, s.new_state = _ret_sample(tok, i, proj, p['g_ret'], s.ret_state, s.new_state)
        mixed = _pool_sample(tok, i, u, s.pool_state[i].transpose(1, 0, 2), p['w_pool_map'], p['pool_scale'])
    merged = _merge(tok, i, yret, mixed, proj, p['w_ret_out'], p['w_pool_out'])
    s.x = _wo(tok, i, merged, s.x, p['w_o'])


def _moe_layer(streams, i, p, g_final):
    rows_all = sum(s.tok.rows for s in streams)
    h_all, row0, choices, weights, starts = None, 0, [], [], []
    for s in streams:
        esel, wsel, h_all = _router(s.tok, i, s.x, p['g_ffn'], p['w_router'], p['b_router'],
                                    h_all=h_all, row0=row0, rows_all=rows_all)
        choices.append(esel[:, :2])
        weights.append(wsel)
        starts.append(row0)
        row0 += s.tok.rows
    plan = _route_plan(jnp.concatenate(choices, axis=0), TM_MOE)
    y_sorted = _moe_sparse(i, h_all, plan, p['w1_moe'], p['w3_moe'], p['w2_moe'], TM_MOE)
    for s, wsel, start in zip(streams, weights, starts):
        s.x = _combine(s.tok, i, y_sorted, plan[3], start, s.x, wsel, g_final)


def kernel(x_prompt, x_sample, c_prompt, c_sample, state_ret, state_pool, w_ada, b_ada, g_mix, w_in, g_ret,
           w_ret_out, w_pool_map, pool_scale, w_pool_out, w_o, g_ffn, w1_dense, w3_dense, w2_dense, w_router,
           b_router, w1_moe, w3_moe, w2_moe, g_final):
    row3 = lambda a: a.reshape(a.shape[0], 1, a.shape[1])
    p = dict(g_mix=row3(g_mix), w_in=w_in, g_ret=row3(g_ret), w_ret_out=w_ret_out, w_pool_map=w_pool_map,
             pool_scale=row3(pool_scale), w_pool_out=w_pool_out, w_o=w_o, g_ffn=row3(g_ffn),
             w1_dense=w1_dense, w3_dense=w3_dense, w2_dense=w2_dense, w_router=w_router, b_router=row3(b_router),
             w1_moe=w1_moe, w3_moe=w3_moe, w2_moe=w2_moe, g_final=g_final.reshape(1, D_MODEL))
    n_bp, seq_p, _ = x_prompt.shape
    n_bs, seq_s, _ = x_sample.shape

    mod_p, mod_s = _ada(c_prompt, jnp.repeat(c_sample, seq_s, axis=0), w_ada, b_ada)

    tok_p = _Tokens(n_bp, seq_p, 0, TM_PROMPT, per_row_mod=False, mod=mod_p)
    tok_s = _Tokens(n_bs, seq_s, PAST_LEN, TM_SAMPLE, per_row_mod=True, mod=mod_s)
    prompt = _Stream(tok_p, x_prompt.reshape(tok_p.rows, D_MODEL), None, None)
    sample = _Stream(tok_s, x_sample.reshape(tok_s.rows, D_MODEL), state_ret, state_pool)
    streams = [prompt, sample]

    for i in range(DEPTH):
        g_fin = p['g_final'] if i == DEPTH - 1 else None
        for s in streams:
            _mixer_layer(s, i, p)
        if i % 2 == 0:
            for s in streams:
                s.x = _ffn(s.tok, i, s.x, p['g_ffn'], p['w1_dense'], p['w3_dense'], p['w2_dense'], g_final=g_fin)
        else:
            _moe_layer(streams, i, p, g_fin)

    buf_prompt = jnp.stack([u.reshape(n_bp, seq_p, D_POOL)[:, seq_p - POOL_BUF:] for u in prompt.u_layers])
    buf_sample = jnp.stack([
        jnp.concatenate([state_pool[i], sample.u_layers[i].reshape(n_bs, seq_s, D_POOL)], axis=1)[:, -POOL_BUF:]
        for i in range(DEPTH)])
    return (prompt.x.reshape(x_prompt.shape), sample.x.reshape(x_sample.shape), prompt.new_state, buf_prompt,
            sample.new_state, buf_sample)
```

```python
import functools

import numpy as np
import jax
import jax.numpy as jnp
from jax import lax
from jax.experimental import pallas as pl
from jax.experimental.pallas import tpu as pltpu

F32 = jnp.float32
BF16 = jnp.bfloat16

D_MODEL = 2048
DEPTH = 2
PAST_LEN = 16384
RET_HEADS = 8
RET_DK = 128
RET_DV = 256
RET_CHUNK = 128
ROPE_BASE = 10000.0
QK_W = RET_HEADS * RET_DK
V_W = RET_HEADS * RET_DV
POOL_WINDOWS = (2, 4, 8, 16)
D_POOL = 1024
POOL_GW = 256
POOL_BUF = 15
POOL_HALO = 16
N_IN = 2 * QK_W + 2 * V_W + D_POOL + 2 * D_MODEL
N_EXPERTS = 8
EPS = 1e-6
N_MOD = 6

O_Q, O_K, O_V, O_G, O_U = 0, QK_W, 2 * QK_W, 2 * QK_W + V_W, 2 * QK_W + 2 * V_W
O_M1 = O_U + D_POOL
O_M2 = O_M1 + D_MODEL

VMEM_LIMIT_BYTES = 56 * 1024 * 1024
SUBLANES = 8
TN = 512
TN_IN = 1024
TF = 256
TM_PROMPT = 1024
TM_SAMPLE = 512
TM_FFN = 1024
TM_OUT = 2048
ROW_CHUNK = 256
NORM_ROWS = 16
NORM_UNROLL = 16
TL_POOL = 512
TM_MOE = 1024
SUB_MOE = 256
TM_COMBINE = 512


def _params(*sem):
    return pltpu.CompilerParams(dimension_semantics=sem, vmem_limit_bytes=VMEM_LIMIT_BYTES)


class _Tokens:
    def __init__(self, n_batch, seq, n_past, tm, per_row_mod, mod=None):
        self.n_batch, self.seq, self.n_past = n_batch, seq, n_past
        self.rows = n_batch * seq
        self.tm = min(tm, self.rows)
        self.per_row_mod = per_row_mod
        if not per_row_mod:
            self.tm = min(self.tm, seq)
            assert seq % self.tm == 0
        assert self.rows % self.tm == 0
        self.n_tiles = self.rows // self.tm
        if mod is not None and not per_row_mod and mod.ndim == 3 and mod.shape[1] != 1:
            mod = mod.reshape(DEPTH * n_batch, 1, N_MOD * D_MODEL)
        self.mod = mod

    def with_tm(self, tm):
        return _Tokens(self.n_batch, self.seq, self.n_past, tm, self.per_row_mod, self.mod)

    def mod_spec(self, layer, chunk, tn=D_MODEL, col_axis=False):
        per_chunk = D_MODEL // tn
        col = (lambda n: chunk * per_chunk + n) if col_axis else (lambda n: chunk * per_chunk)
        if self.per_row_mod:
            if col_axis:
                return pl.BlockSpec((1, self.tm, tn), lambda m, n, *_: (layer, m, col(n)))
            return pl.BlockSpec((1, self.tm, tn), lambda m, *_: (layer, m, col(0)))
        tiles_per_batch = self.seq // self.tm
        base = layer * self.n_batch
        if col_axis:
            return pl.BlockSpec((1, 1, tn), lambda m, n, *_: (base + m // tiles_per_batch, 0, col(n)))
        return pl.BlockSpec((1, 1, tn), lambda m, *_: (base + m // tiles_per_batch, 0, col(0)))

    def rope_tables(self):
        half = RET_DK // 2
        pos = jnp.arange(self.seq, dtype=F32) + float(self.n_past)
        inv = ROPE_BASE ** (-jnp.arange(half, dtype=F32) / half)
        ang = pos[:, None] * inv[None, :]
        cos, sin = jnp.cos(ang), jnp.sin(ang)
        cos_full = jnp.concatenate([cos, cos], axis=1)
        sin_signed = jnp.concatenate([-sin, sin], axis=1)
        if self.per_row_mod:
            cos_full = jnp.tile(cos_full, (self.n_batch, 1))
            sin_signed = jnp.tile(sin_signed, (self.n_batch, 1))
        return cos_full, sin_signed

    def rope_spec(self):
        n_blocks = (self.rows if self.per_row_mod else self.seq) // self.tm
        return pl.BlockSpec((self.tm, RET_DK), lambda m, *_: (m % n_blocks, 0))


def _retention_tables(chunk):
    log_gamma = jnp.log1p(-jnp.exp2(-5.0 - jnp.arange(RET_HEADS, dtype=F32)))
    idx = jnp.arange(chunk, dtype=F32)
    rel = idx[:, None] - idx[None, :]
    dmask = jnp.where(rel[None] >= 0, jnp.exp(log_gamma[:, None, None] * jnp.maximum(rel, 0.0)[None]), 0.0)
    q_dec = jnp.exp(log_gamma[:, None] * (idx + 1.0)[None, :])
    k_dec = jnp.exp(log_gamma[:, None] * (chunk - 1.0 - idx)[None, :])
    chunk_dec = jnp.exp(log_gamma * chunk)
    q_dec = jnp.broadcast_to(q_dec[:, :, None], (RET_HEADS, chunk, RET_DV))
    k_dec = jnp.broadcast_to(k_dec[:, :, None], (RET_HEADS, chunk, RET_DK))
    chunk_dec = jnp.broadcast_to(chunk_dec[:, None, None], (RET_HEADS, 1, RET_DV))
    return dmask, q_dec, k_dec, chunk_dec


def _rmsnorm_mod(x, g, scale, shift):
    y = x * lax.rsqrt(jnp.mean(x * x, axis=-1, keepdims=True) + EPS) * g
    return y * (1.0 + scale) + shift


def _norm_rows(x_ref, g_ref, sc_ref, sh_ref, h_ref):
    tm = x_ref.shape[0]
    step = min(NORM_ROWS, tm)
    trips = tm // step
    per_row = sc_ref.shape[1] != 1
    if not per_row:
        gain, shift = g_ref[0] * (1.0 + sc_ref[0]), sh_ref[0]

    def body(c, carry):
        rows = pl.ds(pl.multiple_of(c * step, step), step)
        x = x_ref[rows, :]
        y = x * lax.rsqrt(jnp.mean(x * x, axis=-1, keepdims=True) + EPS)
        if per_row:
            y = y * (g_ref[0] * (1.0 + sc_ref[0, rows, :])) + sh_ref[0, rows, :]
        else:
            y = y * gain + shift
        h_ref[rows, :] = y.astype(h_ref.dtype)
        return carry

    lax.fori_loop(0, trips, body, 0, unroll=NORM_UNROLL if trips % NORM_UNROLL == 0 else 1)


def _silu(x):
    return x * jax.nn.sigmoid(x)


def _pack_halves(y):
    half = y.shape[1] // 2
    hi = lax.bitcast_convert_type(y[:, :half].astype(jnp.bfloat16).astype(F32), jnp.uint32)
    lo = lax.bitcast_convert_type(y[:, half:].astype(jnp.bfloat16).astype(F32), jnp.uint32)
    return hi | (lo >> 16)


def _unpack_halves(u):
    return (lax.bitcast_convert_type(u & jnp.uint32(0xFFFF0000), F32),
            lax.bitcast_convert_type(u << 16, F32))


def _ada_kernel(cp_ref, cs_ref, w_ref, b_ref, op_ref, os_ref):
    w = w_ref[0].astype(BF16)
    for c_ref, o_ref in ((cp_ref, op_ref), (cs_ref, os_ref)):
        o_ref[0] = jnp.dot(_silu(c_ref[...]).astype(BF16), w, preferred_element_type=F32) + b_ref[0]


def _ada(c_p, c_s, w_ada, b_ada):
    tn = 1024
    width = N_MOD * D_MODEL
    row_spec = lambda c: pl.BlockSpec(c.shape, lambda l, n: (0, 0))
    out_spec = lambda c: pl.BlockSpec((1, c.shape[0], tn), lambda l, n: (l, 0, n))
    return pl.pallas_call(
        _ada_kernel,
        grid=(DEPTH, width // tn),
        in_specs=[row_spec(c_p), row_spec(c_s),
                  pl.BlockSpec((1, D_MODEL, tn), lambda l, n: (l, 0, n)),
                  pl.BlockSpec((1, 1, tn), lambda l, n: (l, 0, n))],
        out_specs=[out_spec(c_p), out_spec(c_s)],
        out_shape=[jax.ShapeDtypeStruct((DEPTH, c.shape[0], width), F32) for c in (c_p, c_s)],
        compiler_params=_params("arbitrary", "arbitrary"),
        name="ada",
    )(c_p, c_s, w_ada, b_ada.reshape(DEPTH, 1, width))


def _inproj_kernel(x_ref, sc_ref, sh_ref, g_ref, w_ref, cos_ref, sin_ref, o_ref, u_ref, h_scr):
    n = pl.program_id(1)
    tm = x_ref.shape[0]
    rc = min(ROW_CHUNK, tm)

    @pl.when(n == 0)
    def _():
        _norm_rows(x_ref, g_ref, sc_ref, sh_ref, h_scr)

    def project(epilogue):
        w = w_ref[0].astype(BF16)
        for r in range(0, tm, rc):
            rows = slice(r, r + rc)
            epilogue(jnp.dot(h_scr[rows, :], w, preferred_element_type=F32), rows)

    def store(acc, rows):
        o_ref[rows, :] = acc.astype(o_ref.dtype)

    def rope(scale):
        def epilogue(acc, rows):
            for j in range(TN_IN // RET_DK):
                cols = slice(j * RET_DK, (j + 1) * RET_DK)
                xh = acc[:, cols]
                r = xh * cos_ref[rows, :] + pltpu.roll(xh, RET_DK // 2, 1) * sin_ref[rows, :]
                if scale != 1.0:
                    r = r * scale
                o_ref[rows, cols] = r.astype(o_ref.dtype)
        project(epilogue)

    @pl.when(n < O_K // TN_IN)
    def _():
        rope(1.0)

    @pl.when(jnp.logical_and(n >= O_K // TN_IN, n < O_V // TN_IN))
    def _():
        rope(RET_DK ** -0.5)

    @pl.when(jnp.logical_and(n >= O_V // TN_IN, n < O_G // TN_IN))
    def _():
        project(store)

    @pl.when(jnp.logical_and(n >= O_G // TN_IN, n < O_U // TN_IN))
    def _():
        project(lambda acc, rows: store(_silu(acc), rows))

    @pl.when(jnp.logical_and(n >= O_U // TN_IN, n < O_M1 // TN_IN))
    def _():
        def epilogue(acc, rows):
            store(acc, rows)
            u_ref[rows, :] = acc
        project(epilogue)

    @pl.when(n >= O_M1 // TN_IN)
    def _():
        project(lambda acc, rows: store(jax.nn.sigmoid(acc), rows))


def _inproj(tok, layer, x, g_mix, w_in, rope_tabs, out_dtype):
    tm = tok.tm
    tn = TN_IN
    assert all(o % tn == 0 for o in (O_K, O_V, O_G, O_U, O_M1, N_IN))
    n_u = D_POOL // tn
    mod3 = tok.mod
    proj, u = pl.pallas_call(
        _inproj_kernel,
        grid=(tok.n_tiles, N_IN // tn),
        in_specs=[pl.BlockSpec((tm, D_MODEL), lambda m, n: (m, 0), pipeline_mode=pl.Buffered(1)),
                  tok.mod_spec(layer, 1), tok.mod_spec(layer, 0),
                  pl.BlockSpec((1, 1, D_MODEL), lambda m, n: (layer, 0, 0)),
                  pl.BlockSpec((1, D_MODEL, tn), lambda m, n: (layer, 0, n)),
                  tok.rope_spec(), tok.rope_spec()],
        out_specs=[pl.BlockSpec((tm, tn), lambda m, n: (m, n)),
                   pl.BlockSpec((tm, tn), lambda m, n: (m, jnp.clip(n - O_U // tn, 0, n_u - 1)))],
        out_shape=[jax.ShapeDtypeStruct((tok.rows, N_IN), out_dtype),
                   jax.ShapeDtypeStruct((tok.rows, D_POOL), F32)],
        scratch_shapes=[pltpu.VMEM((tm, D_MODEL), BF16)],
        compiler_params=_params("arbitrary", "arbitrary"),
        name="inproj",
    )(x, mod3, mod3, g_mix, w_in, *rope_tabs)
    return proj, u


def _head_norm_gate(o, gret, gate):
    mu = jnp.mean(o, axis=-1, keepdims=True)
    d = o - mu
    var = jnp.mean(d * d, axis=-1, keepdims=True)
    return gate.astype(F32) * (d * lax.rsqrt(var + EPS) * gret)


def _ret_prompt_kernel(q_ref, k_ref, v_ref, g_ref, gret_ref, dmask_ref, qdec_ref, kdec_ref, cdec_ref,
                       *rest):
    y_ref, s_ref = rest[-2:]

    @pl.when(pl.program_id(1) == 0)
    def _():
        s_ref[...] = jnp.zeros_like(s_ref)

    for h in range(RET_HEADS):
        ks, vs = slice(h * RET_DK, (h + 1) * RET_DK), slice(h * RET_DV, (h + 1) * RET_DV)
        q, k, v = q_ref[:, ks], k_ref[:, ks], v_ref[:, vs]
        state = s_ref[0, 0, h]
        scores = lax.dot_general(q, k, (((1,), (1,)), ((), ())), preferred_element_type=F32) * dmask_ref[h]
        inner = jnp.dot(scores.astype(BF16), v, preferred_element_type=F32)
        cross = jnp.dot(q, state.astype(BF16), preferred_element_type=F32) * qdec_ref[h]
        k_decayed = (k.astype(F32) * kdec_ref[h]).astype(BF16)
        s_ref[0, 0, h] = cdec_ref[h] * state + lax.dot_general(
            k_decayed, v, (((0,), (0,)), ((), ())), preferred_element_type=F32)
        y_ref[:, vs] = _head_norm_gate(inner + cross, gret_ref[0, :, vs], g_ref[:, vs]).astype(y_ref.dtype)


def _table_specs(tables):
    return [pl.BlockSpec(t.shape, lambda *_: (0, 0, 0)) for t in tables]


def _ret_prompt(tok, layer, proj, g_ret, s_all):
    n_b, seq = tok.n_batch, tok.seq
    c = RET_CHUNK
    assert seq % c == 0
    nc = seq // c
    tables = _retention_tables(c)
    in_specs = [pl.BlockSpec((c, QK_W), lambda b, i: (b * nc + i, O_Q // QK_W)),
                pl.BlockSpec((c, QK_W), lambda b, i: (b * nc + i, O_K // QK_W)),
                pl.BlockSpec((c, V_W), lambda b, i: (b * nc + i, O_V // V_W)),
                pl.BlockSpec((c, V_W), lambda b, i: (b * nc + i, O_G // V_W)),
                pl.BlockSpec((1, 1, V_W), lambda b, i: (layer, 0, 0))] + _table_specs(tables)
    args = [proj, proj, proj, proj, g_ret, *tables]
    aliases = {}
    if s_all is not None:
        aliases = {len(args): 1}
        in_specs.append(pl.BlockSpec(memory_space=pl.ANY))
        args.append(s_all)
    y, s_all = pl.pallas_call(
        _ret_prompt_kernel,
        grid=(n_b, nc),
        in_specs=in_specs,
        out_specs=[pl.BlockSpec((c, V_W), lambda b, i: (b * nc + i, 0)),
                   pl.BlockSpec((1, 1, RET_HEADS, RET_DK, RET_DV), lambda b, i: (layer, b, 0, 0, 0))],
        out_shape=[jax.ShapeDtypeStruct((tok.rows, V_W), BF16),
                   jax.ShapeDtypeStruct((DEPTH, n_b, RET_HEADS, RET_DK, RET_DV), F32)],
        input_output_aliases=aliases,
        compiler_params=_params("arbitrary", "arbitrary"),
        name="ret_prompt",
    )(*args)
    return y, s_all


def _ret_sample_kernel(q_ref, k_ref, v_ref, g_ref, kt_ref, gret_ref, s0_ref, dm_ref, qdec_ref, kdect_ref, cdec_ref,
                       *rest, bb, seq):
    y_ref, s_ref = rest[-2:]
    hs = RET_HEADS * seq
    reps = lambda a: jnp.concatenate([a] * RET_HEADS, axis=0)
    mask_q = (lax.broadcasted_iota(jnp.int32, (hs, QK_W), 0) // seq
              == lax.broadcasted_iota(jnp.int32, (hs, QK_W), 1) // RET_DK)
    mask_v = (lax.broadcasted_iota(jnp.int32, (hs, V_W), 0) // seq
              == lax.broadcasted_iota(jnp.int32, (hs, V_W), 1) // RET_DV)
    nt = (((1,), (1,)), ((), ()))
    for b in range(bb):
        rows_b = slice(b * seq, (b + 1) * seq)
        q, k, v = q_ref[rows_b, :], k_ref[rows_b, :], v_ref[rows_b, :]
        q_blk = jnp.where(mask_q, reps(q), 0.0).astype(BF16)
        scores = lax.dot_general(q_blk, reps(k).astype(BF16), nt, preferred_element_type=F32) * dm_ref[...]
        v_stack = jnp.concatenate([v[:, h * RET_DV:(h + 1) * RET_DV] for h in range(RET_HEADS)], axis=0)
        inner = jnp.dot(scores.astype(BF16), v_stack.astype(BF16), preferred_element_type=F32)
        state = s0_ref[0, b]
        cross = jnp.dot(q_blk, state.reshape(QK_W, RET_DV).astype(BF16), preferred_element_type=F32)
        o = inner + cross * qdec_ref[...]
        mu = jnp.mean(o, axis=-1, keepdims=True)
        d = o - mu
        normed = d * lax.rsqrt(jnp.mean(d * d, axis=-1, keepdims=True) + EPS)
        k_dec_t = (kt_ref[b] * kdect_ref[...]).astype(BF16)
        v_blk = jnp.where(mask_v, reps(v), 0.0).astype(BF16)
        upd = jnp.dot(k_dec_t, v_blk, preferred_element_type=F32)
        for h in range(RET_HEADS):
            vs = slice(h * RET_DV, (h + 1) * RET_DV)
            y_ref[rows_b, vs] = g_ref[rows_b, vs] * (normed[h * seq:(h + 1) * seq] * gret_ref[0, :, vs])
            s_ref[0, b, h] = cdec_ref[h] * state[h] + upd[:, vs]


def _ret_sample(tok, layer, proj, g_ret, state_ret, s_all):
    n_b, seq = tok.n_batch, tok.seq
    bb = 4
    assert n_b % bb == 0
    hs = RET_HEADS * seq
    dmask, q_dec, k_dec, chunk_dec = _retention_tables(seq)
    dm_blk = (dmask[:, :, None, :] * jnp.eye(RET_HEADS, dtype=F32)[:, None, :, None]).reshape(hs, hs)
    tables = [dm_blk, q_dec.reshape(hs, RET_DV), k_dec[:, :, 0].reshape(1, hs), chunk_dec]
    k_t = proj[:, O_K:O_K + QK_W].reshape(n_b, seq, RET_HEADS, RET_DK).transpose(0, 3, 2, 1).reshape(n_b, RET_DK, hs)
    state_spec = pl.BlockSpec((1, bb, RET_HEADS, RET_DK, RET_DV), lambda i: (layer, i, 0, 0, 0))
    in_specs = [pl.BlockSpec((bb * seq, QK_W), lambda i: (i, O_Q // QK_W)),
                pl.BlockSpec((bb * seq, QK_W), lambda i: (i, O_K // QK_W)),
                pl.BlockSpec((bb * seq, V_W), lambda i: (i, O_V // V_W)),
                pl.BlockSpec((bb * seq, V_W), lambda i: (i, O_G // V_W)),
                pl.BlockSpec((bb, RET_DK, hs), lambda i: (i, 0, 0)),
                pl.BlockSpec((1, 1, V_W), lambda i: (layer, 0, 0)),
                state_spec] + [pl.BlockSpec(t.shape, lambda i, nd=t.ndim: (0,) * nd) for t in tables]
    args = [proj, proj, proj, proj, k_t, g_ret, state_ret, *tables]
    aliases = {}
    if s_all is not None:
        aliases = {len(args): 1}
        in_specs.append(pl.BlockSpec(memory_space=pl.ANY))
        args.append(s_all)
    y, s_all = pl.pallas_call(
        functools.partial(_ret_sample_kernel, bb=bb, seq=seq),
        grid=(n_b // bb,),
        in_specs=in_specs,
        out_specs=[pl.BlockSpec((bb * seq, V_W), lambda i: (i, 0)), state_spec],
        out_shape=[jax.ShapeDtypeStruct((tok.rows, V_W), F32),
                   jax.ShapeDtypeStruct(state_ret.shape, F32)],
        input_output_aliases=aliases,
        compiler_params=_params("arbitrary"),
        name="ret_sample",
    )(*args)
    return y, s_all


def _pool_map(pooled, wmap_ref, scale_ref, gi):
    cols = slice(gi * POOL_GW, (gi + 1) * POOL_GW)
    mixed = jnp.dot(pooled.astype(BF16), wmap_ref[0, gi].astype(BF16), preferred_element_type=F32)
    return mixed * scale_ref[0, :, cols]


def _pool_prompt_kernel(u_ref, halo_ref, wmap_ref, scale_ref, o_ref, uf_scr, *, tl):
    first = pl.program_id(1) == 0
    uf_scr[0:POOL_HALO, :] = jnp.where(first, 0.0, halo_ref[...])
    uf_scr[POOL_HALO:, :] = u_ref[...]
    row = lax.broadcasted_iota(jnp.int32, (tl, POOL_GW), 0) + pl.program_id(1) * tl
    for gi, w in enumerate(POOL_WINDOWS):
        cols = slice(gi * POOL_GW, (gi + 1) * POOL_GW)
        u = uf_scr[POOL_HALO:, cols]
        acc = u
        for j in range(1, w):
            acc = acc + uf_scr[POOL_HALO - j:POOL_HALO - j + tl, cols]
        cnt = jnp.minimum(w, row + 1).astype(F32)
        o_ref[:, cols] = _pool_map(acc / cnt - u, wmap_ref, scale_ref, gi).astype(o_ref.dtype)


def _pool_prompt(tok, layer, u, w_pool_map, pool_scale):
    n_b, seq = tok.n_batch, tok.seq
    tl = min(TL_POOL, seq)
    assert seq % tl == 0 and tl % POOL_HALO == 0
    nl = seq // tl
    per = tl // POOL_HALO
    return pl.pallas_call(
        functools.partial(_pool_prompt_kernel, tl=tl),
        grid=(n_b, nl),
        in_specs=[pl.BlockSpec((tl, D_POOL), lambda b, l: (b * nl + l, 0)),
                  pl.BlockSpec((POOL_HALO, D_POOL), lambda b, l: (jnp.maximum((b * nl + l) * per - 1, 0), 0)),
                  pl.BlockSpec((1,) + w_pool_map.shape[1:], lambda b, l: (layer, 0, 0, 0)),
                  pl.BlockSpec((1, 1, D_POOL), lambda b, l: (layer, 0, 0))],
        out_specs=pl.BlockSpec((tl, D_POOL), lambda b, l: (b * nl + l, 0)),
        out_shape=jax.ShapeDtypeStruct((tok.rows, D_POOL), BF16),
        scratch_shapes=[pltpu.VMEM((POOL_HALO + tl, D_POOL), F32)],
        compiler_params=_params("arbitrary", "arbitrary"),
        name="pool_prompt",
    )(u, u, w_pool_map, pool_scale)


def _pool_sample_kernel(buf_ref, u_ref, wmap_ref, scale_ref, o_ref, *, seq, n_b):
    def row(j, cols):
        return buf_ref[j, :, cols] if j < POOL_BUF else u_ref[j - POOL_BUF, :, cols]

    for gi, w in enumerate(POOL_WINDOWS):
        cols = slice(gi * POOL_GW, (gi + 1) * POOL_GW)
        for t in range(seq):
            acc = row(POOL_BUF + t, cols)
            for j in range(1, w):
                acc = acc + row(POOL_BUF + t - j, cols)
            pooled = acc / float(w) - row(POOL_BUF + t, cols)
            o_ref[t, :, cols] = _pool_map(pooled, wmap_ref, scale_ref, gi)


def _pool_sample(tok, layer, u, buf_tm, w_pool_map, pool_scale):
    n_b, seq = tok.n_batch, tok.seq
    assert tok.n_past >= max(POOL_WINDOWS)
    u_tm = u.reshape(n_b, seq, D_POOL).transpose(1, 0, 2)
    mixed_tm = pl.pallas_call(
        functools.partial(_pool_sample_kernel, seq=seq, n_b=n_b),
        grid=(1,),
        in_specs=[pl.BlockSpec(buf_tm.shape, lambda i: (0, 0, 0)),
                  pl.BlockSpec(u_tm.shape, lambda i: (0, 0, 0)),
                  pl.BlockSpec((1,) + w_pool_map.shape[1:], lambda i: (layer, 0, 0, 0)),
                  pl.BlockSpec((1, 1, D_POOL), lambda i: (layer, 0, 0))],
        out_specs=pl.BlockSpec(u_tm.shape, lambda i: (0, 0, 0)),
        out_shape=jax.ShapeDtypeStruct(u_tm.shape, F32),
        compiler_params=_params("arbitrary"),
        name="pool_sample",
    )(buf_tm, u_tm, w_pool_map, pool_scale)
    return mixed_tm.transpose(1, 0, 2).reshape(tok.rows, D_POOL)


def _merge_kernel(yret_ref, mixed_ref, wr_ref, wp_ref, g1_ref, g2_ref, o_ref):
    ret = jnp.dot(yret_ref[...].astype(BF16), wr_ref[0].astype(BF16), preferred_element_type=F32)
    pool = jnp.dot(mixed_ref[...].astype(BF16), wp_ref[0].astype(BF16), preferred_element_type=F32)
    o_ref[...] = (g1_ref[...].astype(F32) * ret + g2_ref[...].astype(F32) * pool).astype(o_ref.dtype)


def _merge(tok, layer, yret, mixed, proj, w_ret_out, w_pool_out):
    tm = tok.tm
    return pl.pallas_call(
        _merge_kernel,
        grid=(tok.n_tiles, D_MODEL // TN),
        in_specs=[pl.BlockSpec((tm, V_W), lambda m, n: (m, 0)),
                  pl.BlockSpec((tm, D_POOL), lambda m, n: (m, 0)),
                  pl.BlockSpec((1, V_W, TN), lambda m, n: (layer, 0, n)),
                  pl.BlockSpec((1, D_POOL, TN), lambda m, n: (layer, 0, n)),
                  pl.BlockSpec((tm, TN), lambda m, n: (m, O_M1 // TN + n)),
                  pl.BlockSpec((tm, TN), lambda m, n: (m, O_M2 // TN + n))],
        out_specs=pl.BlockSpec((tm, TN), lambda m, n: (m, n)),
        out_shape=jax.ShapeDtypeStruct((tok.rows, D_MODEL), BF16),
        compiler_params=_params("arbitrary", "arbitrary"),
        name="merge",
    )(yret, mixed, w_ret_out, w_pool_out, proj, proj)


def _wo_kernel(merged_ref, w_ref, x_ref, gt_ref, o_ref):
    mix = jnp.dot(merged_ref[...], w_ref[0].astype(BF16), preferred_element_type=F32)
    o_ref[...] = x_ref[...] + (1.0 + gt_ref[0]) * mix


def _wo(tok, layer, merged, x, w_o):
    tok = tok.with_tm(TM_OUT)
    tm = tok.tm
    mod3 = tok.mod
    return pl.pallas_call(
        _wo_kernel,
        grid=(tok.n_tiles, D_MODEL // TN),
        in_specs=[pl.BlockSpec((tm, D_MODEL), lambda m, n: (m, 0), pipeline_mode=pl.Buffered(1)),
                  pl.BlockSpec((1, D_MODEL, TN), lambda m, n: (layer, 0, n)),
                  pl.BlockSpec((tm, TN), lambda m, n: (m, n)),
                  tok.mod_spec(layer, 2, tn=TN, col_axis=True)],
        out_specs=pl.BlockSpec((tm, TN), lambda m, n: (m, n)),
        out_shape=jax.ShapeDtypeStruct((tok.rows, D_MODEL), F32),
        compiler_params=_params("arbitrary", "arbitrary"),
        name="wo",
    )(merged, w_o, x, mod3)


def _mat(w_ref):
    return w_ref[(0,) * (len(w_ref.shape) - 2)]


def _final_norm(y, gfin_ref):
    return y * lax.rsqrt(jnp.mean(y * y, axis=-1, keepdims=True) + EPS) * gfin_ref[...]


def _ffn_kernel(x_ref, sc_ref, sh_ref, gt_ref, g_ref, w1_ref, w3_ref, w2_ref, *rest, n_f, with_final):
    gfin_ref = rest[0] if with_final else None
    o_ref, h_scr = rest[-2:]
    f = pl.program_id(1)

    @pl.when(f == 0)
    def _():
        _norm_rows(x_ref, g_ref, sc_ref, sh_ref, h_scr)
        o_ref[...] = jnp.zeros_like(o_ref)

    h = h_scr[...]
    a = jnp.dot(h, _mat(w1_ref).astype(BF16), preferred_element_type=F32)
    b = jnp.dot(h, _mat(w3_ref).astype(BF16), preferred_element_type=F32)
    o_ref[...] += jnp.dot((_silu(a) * b).astype(BF16), _mat(w2_ref).astype(BF16), preferred_element_type=F32)

    @pl.when(f == n_f - 1)
    def _():
        y = x_ref[...] + (1.0 + gt_ref[0]) * o_ref[...]
        o_ref[...] = _final_norm(y, gfin_ref) if with_final else y


def _ffn(tok, layer, x, g_ffn, w1, w3, w2, g_final=None):
    j = layer // 2
    d_ff = w1.shape[-1]
    tok_f = tok.with_tm(TM_FFN)
    tm, n_tiles = tok_f.tm, tok_f.n_tiles
    mod3 = tok_f.mod
    assert d_ff % TF == 0
    n_f = d_ff // TF
    once = pl.Buffered(1)
    in_specs = [pl.BlockSpec((tm, D_MODEL), lambda m, f: (m, 0), pipeline_mode=once),
                tok_f.mod_spec(layer, 4), tok_f.mod_spec(layer, 3), tok_f.mod_spec(layer, 5),
                pl.BlockSpec((1, 1, D_MODEL), lambda m, f: (layer, 0, 0)),
                pl.BlockSpec((1, D_MODEL, TF), lambda m, f: (j, 0, f)),
                pl.BlockSpec((1, D_MODEL, TF), lambda m, f: (j, 0, f)),
                pl.BlockSpec((1, TF, D_MODEL), lambda m, f: (j, f, 0))]
    args = [x, mod3, mod3, mod3, g_ffn, w1, w3, w2]
    if g_final is not None:
        in_specs.append(pl.BlockSpec((1, D_MODEL), lambda m, f: (0, 0)))
        args.append(g_final)
    return pl.pallas_call(
        functools.partial(_ffn_kernel, n_f=n_f, with_final=g_final is not None),
        grid=(n_tiles, n_f),
        in_specs=in_specs,
        out_specs=pl.BlockSpec((tm, D_MODEL), lambda m, f: (m, 0), pipeline_mode=once),
        out_shape=jax.ShapeDtypeStruct((tok.rows, D_MODEL), F32),
        scratch_shapes=[pltpu.VMEM((tm, D_MODEL), BF16)],
        compiler_params=_params("arbitrary", "arbitrary"),
        name="ffn",
    )(*args)


def _router_kernel(x_ref, sc_ref, sh_ref, g_ref, w_ref, b_ref, *rest):
    esel_ref, wsel_ref, h_ref = rest[-3:]
    h = _rmsnorm_mod(x_ref[...], g_ref[0], sc_ref[0], sh_ref[0])
    logits = jnp.dot(h.astype(BF16), w_ref[0].astype(BF16), preferred_element_type=F32) + b_ref[0]
    idx = lax.broadcasted_iota(jnp.int32, logits.shape, 1)
    top1 = jnp.max(logits, axis=-1, keepdims=True)
    i1 = jnp.min(jnp.where(logits == top1, idx, N_EXPERTS), axis=-1, keepdims=True)
    rest = jnp.where(idx == i1, -jnp.inf, logits)
    top2 = jnp.max(rest, axis=-1, keepdims=True)
    i2 = jnp.min(jnp.where(rest == top2, idx, N_EXPERTS), axis=-1, keepdims=True)
    e2 = jnp.exp(top2 - top1)
    denom = 1.0 + e2
    w1, w2 = 1.0 / denom, e2 / denom
    esel_ref[...] = jnp.where(idx == 0, i1, i2)
    wsel_ref[...] = jnp.where(idx == 0, w1, w2)
    h_ref[...] = _pack_halves(h)


def _router(tok, layer, x, g_ffn, w_router, b_router, h_all=None, row0=0, rows_all=None):
    j = layer // 2
    tm = tok.tm
    rows_all = tok.rows if rows_all is None else rows_all
    assert row0 % tm == 0
    small = pl.BlockSpec((tm, N_EXPERTS), lambda m: (m, 0))
    mod3 = tok.mod
    in_specs = [pl.BlockSpec((tm, D_MODEL), lambda m: (m, 0)),
                tok.mod_spec(layer, 4), tok.mod_spec(layer, 3),
                pl.BlockSpec((1, 1, D_MODEL), lambda m: (layer, 0, 0)),
                pl.BlockSpec((1, D_MODEL, N_EXPERTS), lambda m: (j, 0, 0)),
                pl.BlockSpec((1, 1, N_EXPERTS), lambda m: (j, 0, 0))]
    args = [x, mod3, mod3, g_ffn, w_router, b_router]
    aliases = {}
    if h_all is not None:
        aliases = {len(args): 2}
        in_specs.append(pl.BlockSpec(memory_space=pl.ANY))
        args.append(h_all)
    return pl.pallas_call(
        _router_kernel,
        grid=(tok.n_tiles,),
        in_specs=in_specs,
        out_specs=[small, small, pl.BlockSpec((tm, D_MODEL // 2), lambda m: (row0 // tm + m, 0))],
        out_shape=[jax.ShapeDtypeStruct((tok.rows, N_EXPERTS), jnp.int32),
                   jax.ShapeDtypeStruct((tok.rows, N_EXPERTS), F32),
                   jax.ShapeDtypeStruct((rows_all, D_MODEL // 2), jnp.uint32)],
        input_output_aliases=aliases,
        compiler_params=_params("arbitrary"),
        name="router",
    )(*args)


def _route_plan(choice, tm):
    rows = choice.shape[0]
    n_pairs = 2 * rows
    n_tiles = n_pairs // tm + N_EXPERTS
    e_flat = choice.T.reshape(n_pairs)
    onehot = (e_flat[:, None] == jnp.arange(N_EXPERTS, dtype=jnp.int32)[None, :]).astype(jnp.int32)
    csum = jnp.cumsum(onehot, axis=0)
    rank = jnp.sum(csum * onehot, axis=1) - 1
    count = csum[-1]
    tiles_per = (count + tm - 1) // tm
    rows_per = jnp.maximum((((count + jnp.maximum(tiles_per, 1) - 1) // jnp.maximum(tiles_per, 1)) + 7) // 8 * 8, 8)
    tile_end = jnp.cumsum(tiles_per)
    tile_start = tile_end - tiles_per
    pos = (tile_start[e_flat] + rank // rows_per[e_flat]) * tm + rank % rows_per[e_flat]
    token = jnp.arange(n_pairs, dtype=jnp.int32) % rows
    src_row = jnp.zeros((n_tiles * tm,), jnp.int32).at[pos].set(token, unique_indices=True, mode='promise_in_bounds')
    tile_id = jnp.arange(n_tiles, dtype=jnp.int32)
    n_used = tile_end[-1]
    tile_expert = jnp.minimum(jnp.sum(tile_id[:, None] >= tile_end[None, :], axis=1), N_EXPERTS - 1).astype(jnp.int32)
    tile_rows = jnp.clip(count[tile_expert] - (tile_id - tile_start[tile_expert]) * rows_per[tile_expert],
                         0, rows_per[tile_expert])
    tile_rows = jnp.where(tile_id < n_used, tile_rows, 0).astype(jnp.int32)
    last_expert = tile_expert[jnp.maximum(n_used - 1, 0)]
    tile_expert = jnp.where(tile_rows > 0, tile_expert, last_expert)
    return tile_expert, tile_rows, src_row, pos.astype(jnp.int32)


def _moe_kernel(te_ref, nr_ref, src_ref, h_hbm, w1_ref, w3_ref, w2_ref, o_ref, rows_scr, h_scr, acc_scr, sem,
                *, tm, sub, n_f):
    del te_ref
    m, f = pl.program_id(0), pl.program_id(1)
    half = D_MODEL // 2
    group = rows_scr.shape[1]

    def n_groups(tile):
        return (nr_ref[tile] + group - 1) // group

    def start_rows(tile):
        def body(i, c):
            for j in range(group):
                row = src_ref[tile * tm + i * group + j]
                pltpu.make_async_copy(h_hbm.at[pl.ds(row, 1)], rows_scr.at[i, pl.ds(j, 1)], sem).start()
            return c
        lax.fori_loop(0, n_groups(tile), body, 0)

    def wait_rows(tile):
        def body(i, c):
            pltpu.make_async_copy(h_hbm.at[pl.ds(0, group)], rows_scr.at[i], sem).wait()
            return c
        lax.fori_loop(0, n_groups(tile), body, 0)

    @pl.when(f == 0)
    def _():
        acc_scr[...] = jnp.zeros_like(acc_scr)

        @pl.when(m == 0)
        def _():
            rows_scr[...] = jnp.zeros_like(rows_scr)
            start_rows(0)

        wait_rows(m)
        first, second = _unpack_halves(rows_scr[...].reshape(tm, half))
        h_scr[:, :half] = first.astype(BF16)
        h_scr[:, half:] = second.astype(BF16)

        @pl.when(m + 1 < pl.num_programs(0))
        def _():
            start_rows(m + 1)

    n_sub = (nr_ref[m] + sub - 1) // sub
    for n in range(1, tm // sub + 1):
        @pl.when(n_sub == n)
        def _(n=n):
            h = h_scr[0:n * sub, :]
            a = jnp.dot(h, _mat(w1_ref).astype(BF16), preferred_element_type=F32)
            b = jnp.dot(h, _mat(w3_ref).astype(BF16), preferred_element_type=F32)
            acc_scr[0:n * sub, :] += jnp.dot((_silu(a) * b).astype(BF16), _mat(w2_ref).astype(BF16),
                                             preferred_element_type=F32)

    @pl.when(f == n_f - 1)
    def _():
        o_ref[...] = _pack_halves(acc_scr[...])


def _moe_sparse(layer, h, plan, w1, w3, w2, tm):
    j = layer // 2
    tile_expert, tile_rows, src_row, _ = plan
    n_tiles = tile_expert.shape[0]
    d_ff = w1.shape[3]
    assert d_ff % TF == 0 and tm % SUB_MOE == 0
    n_f = d_ff // TF

    def f_sel(m, f, nr):
        return jnp.where(nr[m] > 0, f, n_f - 1)

    return pl.pallas_call(
        functools.partial(_moe_kernel, tm=tm, sub=SUB_MOE, n_f=n_f),
        grid_spec=pltpu.PrefetchScalarGridSpec(
            num_scalar_prefetch=3,
            grid=(n_tiles, n_f),
            in_specs=[pl.BlockSpec(memory_space=pl.ANY),
                      pl.BlockSpec((1, 1, D_MODEL, TF), lambda m, f, te, nr, sr: (j, te[m], 0, f_sel(m, f, nr))),
                      pl.BlockSpec((1, 1, D_MODEL, TF), lambda m, f, te, nr, sr: (j, te[m], 0, f_sel(m, f, nr))),
                      pl.BlockSpec((1, 1, TF, D_MODEL), lambda m, f, te, nr, sr: (j, te[m], f_sel(m, f, nr), 0))],
            out_specs=pl.BlockSpec((tm, D_MODEL // 2), lambda m, f, te, nr, sr: (m, 0)),
            scratch_shapes=[pltpu.VMEM((tm // SUBLANES, SUBLANES, D_MODEL // 2), jnp.uint32),
                            pltpu.VMEM((tm, D_MODEL), BF16),
                            pltpu.VMEM((tm, D_MODEL), F32), pltpu.SemaphoreType.DMA(())]),
        out_shape=jax.ShapeDtypeStruct((n_tiles * tm, D_MODEL // 2), jnp.uint32),
        compiler_params=_params("arbitrary", "arbitrary"),
        name="moe",
    )(tile_expert, tile_rows, src_row, h, w1, w3, w2)


def _combine_kernel(pos_ref, y_hbm, x_ref, gt_ref, w_ref, *rest, tm, row0, rows_all, with_final):
    gfin_ref = rest[0] if with_final else None
    o_ref, y_scr, sems = rest[-3:]
    i, n = pl.program_id(0), pl.num_programs(0)

    half = D_MODEL // 2
    group = y_scr.shape[3]

    def start_rows(tile, slot):
        def body(g, c):
            for k in range(2):
                for j in range(group):
                    slot_of_pair = pos_ref[k * rows_all + row0 + tile * tm + g * group + j]
                    pltpu.make_async_copy(y_hbm.at[pl.ds(slot_of_pair, 1)], y_scr.at[slot, k, g, pl.ds(j, 1)],
                                          sems.at[slot]).start()
            return c
        lax.fori_loop(0, tm // group, body, 0)

    @pl.when(i == 0)
    def _():
        start_rows(0, 0)

    @pl.when(i + 1 < n)
    def _():
        start_rows(i + 1, (i + 1) % 2)

    slot = i % 2
    def wait_group(g, c):
        for k in range(2):
            pltpu.make_async_copy(y_hbm.at[pl.ds(0, group)], y_scr.at[slot, k, g], sems.at[slot]).wait()
        return c

    lax.fori_loop(0, tm // group, wait_group, 0)
    w = w_ref[...]
    first0, second0 = _unpack_halves(y_scr[slot, 0].reshape(tm, half))
    first1, second1 = _unpack_halves(y_scr[slot, 1].reshape(tm, half))
    f = jnp.concatenate([w[:, 0:1] * first0 + w[:, 1:2] * first1,
                         w[:, 0:1] * second0 + w[:, 1:2] * second1], axis=1)
    y = x_ref[...] + (1.0 + gt_ref[0]) * f
    o_ref[...] = _final_norm(y, gfin_ref) if with_final else y


def _combine(tok, layer, y_sorted, pos, row0, x, wsel, g_final):
    tok_c = tok.with_tm(TM_COMBINE)
    tm = tok_c.tm
    mod3 = tok_c.mod
    in_specs = [pl.BlockSpec(memory_space=pl.ANY),
                pl.BlockSpec((tm, D_MODEL), lambda m, pos: (m, 0)),
                tok_c.mod_spec(layer, 5),
                pl.BlockSpec((tm, N_EXPERTS), lambda m, pos: (m, 0))]
    args = [pos, y_sorted, x, mod3, wsel]
    if g_final is not None:
        in_specs.append(pl.BlockSpec((1, D_MODEL), lambda m, pos: (0, 0)))
        args.append(g_final)
    return pl.pallas_call(
        functools.partial(_combine_kernel, tm=tm, row0=row0, rows_all=pos.shape[0] // 2,
                          with_final=g_final is not None),
        grid_spec=pltpu.PrefetchScalarGridSpec(
            num_scalar_prefetch=1,
            grid=(tok_c.n_tiles,),
            in_specs=in_specs,
            out_specs=pl.BlockSpec((tm, D_MODEL), lambda m, pos: (m, 0)),
            scratch_shapes=[pltpu.VMEM((2, 2, tm // SUBLANES, SUBLANES, D_MODEL // 2), jnp.uint32),
                            pltpu.SemaphoreType.DMA((2,))]),
        out_shape=jax.ShapeDtypeStruct((tok.rows, D_MODEL), F32),
        compiler_params=_params("arbitrary"),
        name="combine",
    )(*args)


class _Stream:
    def __init__(self, tok, x, ret_state, pool_state):
        self.tok, self.x, self.ret_state, self.pool_state = tok, x, ret_state, pool_state
        self.rope_tabs = tok.rope_tables()
        self.new_state = None
        self.u_layers = []


def _mixer_layer(s, i, p):
    tok = s.tok
    is_prompt = s.ret_state is None
    proj, u = _inproj(tok, i, s.x, p['g_mix'], p['w_in'], s.rope_tabs, BF16 if is_prompt else F32)
    s.u_layers.append(u)
    if is_prompt:
        yret, s.new_state = _ret_prompt(tok, i, proj, p['g_ret'], s.new_state)
        mixed = _pool_prompt(tok, i, u, p['w_pool_map'], p['pool_scale'])
    else:
        yret, s.new_state = _ret_sample(tok, i, proj, p['g_ret'], s.ret_state, s.new_state)
        mixed = _pool_sample(tok, i, u, s.pool_state[i].transpose(1, 0, 2), p['w_pool_map'], p['pool_scale'])
    merged = _merge(tok, i, yret, mixed, proj, p['w_ret_out'], p['w_pool_out'])
    s.x = _wo(tok, i, merged, s.x, p['w_o'])


def _moe_layer(streams, i, p, g_final):
    rows_all = sum(s.tok.rows for s in streams)
    h_all, row0, choices, weights, starts = None, 0, [], [], []
    for s in streams:
        esel, wsel, h_all = _router(s.tok, i, s.x, p['g_ffn'], p['w_router'], p['b_router'],
                                    h_all=h_all, row0=row0, rows_all=rows_all)
        choices.append(esel[:, :2])
        weights.append(wsel)
        starts.append(row0)
        row0 += s.tok.rows
    plan = _route_plan(jnp.concatenate(choices, axis=0), TM_MOE)
    y_sorted = _moe_sparse(i, h_all, plan, p['w1_moe'], p['w3_moe'], p['w2_moe'], TM_MOE)
    for s, wsel, start in zip(streams, weights, starts):
        s.x = _combine(s.tok, i, y_sorted, plan[3], start, s.x, wsel, g_final)


def kernel(x_prompt, x_sample, c_prompt, c_sample, state_ret, state_pool, w_ada, b_ada, g_mix, w_in, g_ret,
           w_ret_out, w_pool_map, pool_scale, w_pool_out, w_o, g_ffn, w1_dense, w3_dense, w2_dense, w_router,
           b_router, w1_moe, w3_moe, w2_moe, g_final):
    row3 = lambda a: a.reshape(a.shape[0], 1, a.shape[1])
    p = dict(g_mix=row3(g_mix), w_in=w_in, g_ret=row3(g_ret), w_ret_out=w_ret_out, w_pool_map=w_pool_map,
             pool_scale=row3(pool_scale), w_pool_out=w_pool_out, w_o=w_o, g_ffn=row3(g_ffn),
             w1_dense=w1_dense, w3_dense=w3_dense, w2_dense=w2_dense, w_router=w_router, b_router=row3(b_router),
             w1_moe=w1_moe, w3_moe=w3_moe, w2_moe=w2_moe, g_final=g_final.reshape(1, D_MODEL))
    n_bp, seq_p, _ = x_prompt.shape
    n_bs, seq_s, _ = x_sample.shape

    mod_p, mod_s = _ada(c_prompt, jnp.repeat(c_sample, seq_s, axis=0), w_ada, b_ada)

    tok_p = _Tokens(n_bp, seq_p, 0, TM_PROMPT, per_row_mod=False, mod=mod_p)
    tok_s = _Tokens(n_bs, seq_s, PAST_LEN, TM_SAMPLE, per_row_mod=True, mod=mod_s)
    prompt = _Stream(tok_p, x_prompt.reshape(tok_p.rows, D_MODEL), None, None)
    sample = _Stream(tok_s, x_sample.reshape(tok_s.rows, D_MODEL), state_ret, state_pool)
    streams = [prompt, sample]

    for i in range(DEPTH):
        g_fin = p['g_final'] if i == DEPTH - 1 else None
        for s in streams:
            _mixer_layer(s, i, p)
        if i % 2 == 0:
            for s in streams:
                s.x = _ffn(s.tok, i, s.x, p['g_ffn'], p['w1_dense'], p['w3_dense'], p['w2_dense'], g_final=g_fin)
        else:
            _moe_layer(streams, i, p, g_fin)

    buf_prompt = jnp.stack([u.reshape(n_bp, seq_p, D_POOL)[:, seq_p - POOL_BUF:] for u in prompt.u_layers])
    buf_sample = jnp.stack([
        jnp.concatenate([state_pool[i], sample.u_layers[i].reshape(n_bs, seq_s, D_POOL)], axis=1)[:, -POOL_BUF:]
        for i in range(DEPTH)])
    return (prompt.x.reshape(x_prompt.shape), sample.x.reshape(x_sample.shape), prompt.new_state, buf_prompt,
            sample.new_state, buf_sample)
```

```python
import functools

import jax
import jax.numpy as jnp
from jax import lax
from jax.experimental import pallas as pl
from jax.experimental.pallas import tpu as pltpu

F32 = jnp.float32
BF16 = jnp.bfloat16

D_MODEL = 2048
DEPTH = 2
PAST_LEN = 16384
RET_HEADS = 8
RET_DK = 128
RET_DV = 256
RET_CHUNK = 128
ROPE_BASE = 10000.0
QK_W = RET_HEADS * RET_DK
V_W = RET_HEADS * RET_DV
POOL_WINDOWS = (2, 4, 8, 16)
D_POOL = 1024
POOL_GW = 256
POOL_BUF = 15
POOL_HALO = 16
N_IN = 2 * QK_W + 2 * V_W + D_POOL + 2 * D_MODEL
N_EXPERTS = 8
EPS = 1e-6
N_MOD = 6

O_Q, O_K, O_V, O_G, O_U = 0, QK_W, 2 * QK_W, 2 * QK_W + V_W, 2 * QK_W + 2 * V_W
O_M1 = O_U + D_POOL
O_M2 = O_M1 + D_MODEL

VMEM_LIMIT_BYTES = 56 * 1024 * 1024
SUBLANES = 8
TN = 512
TN_IN = 1024
TF = 256
TM_PROMPT = 1024
TM_SAMPLE = 512
TM_FFN = 1024
TM_OUT = 2048
ROW_CHUNK = 256
NORM_ROWS = 16
NORM_UNROLL = 16
TL_POOL = 512
TM_MOE = 1024
SUB_MOE = 256
TM_COMBINE = 256
RET_PROMPT_BATCH = 4


def _params(*sem):
    return pltpu.CompilerParams(dimension_semantics=sem, vmem_limit_bytes=VMEM_LIMIT_BYTES)


class _Tokens:
    def __init__(self, n_batch, seq, n_past, tm, per_row_mod, mod=None):
        self.n_batch, self.seq, self.n_past = n_batch, seq, n_past
        self.rows = n_batch * seq
        self.tm = min(tm, self.rows)
        self.per_row_mod = per_row_mod
        if not per_row_mod:
            self.tm = min(self.tm, seq)
            assert seq % self.tm == 0
        assert self.rows % self.tm == 0
        self.n_tiles = self.rows // self.tm
        if mod is not None and not per_row_mod and mod.ndim == 3 and mod.shape[1] != 1:
            mod = mod.reshape(DEPTH * n_batch, 1, N_MOD * D_MODEL)
        self.mod = mod

    def with_tm(self, tm):
        return _Tokens(self.n_batch, self.seq, self.n_past, tm, self.per_row_mod, self.mod)

    def mod_spec(self, layer, chunk, tn=D_MODEL, col_axis=False):
        per_chunk = D_MODEL // tn
        col = (lambda n: chunk * per_chunk + n) if col_axis else (lambda n: chunk * per_chunk)
        if self.per_row_mod:
            if col_axis:
                return pl.BlockSpec((1, self.tm, tn), lambda m, n, *_: (layer, m, col(n)))
            return pl.BlockSpec((1, self.tm, tn), lambda m, *_: (layer, m, col(0)))
        tiles_per_batch = self.seq // self.tm
        base = layer * self.n_batch
        if col_axis:
            return pl.BlockSpec((1, 1, tn), lambda m, n, *_: (base + m // tiles_per_batch, 0, col(n)))
        return pl.BlockSpec((1, 1, tn), lambda m, *_: (base + m // tiles_per_batch, 0, col(0)))

    def rope_tables(self):
        half = RET_DK // 2
        pos = jnp.arange(self.seq, dtype=F32) + float(self.n_past)
        inv = ROPE_BASE ** (-jnp.arange(half, dtype=F32) / half)
        ang = pos[:, None] * inv[None, :]
        cos, sin = jnp.cos(ang), jnp.sin(ang)
        cos_full = jnp.concatenate([cos, cos], axis=1)
        sin_signed = jnp.concatenate([-sin, sin], axis=1)
        if self.per_row_mod:
            cos_full = jnp.tile(cos_full, (self.n_batch, 1))
            sin_signed = jnp.tile(sin_signed, (self.n_batch, 1))
        return cos_full, sin_signed

    def rope_spec(self):
        n_blocks = (self.rows if self.per_row_mod else self.seq) // self.tm
        return pl.BlockSpec((self.tm, RET_DK), lambda m, *_: (m % n_blocks, 0))


def _retention_tables(chunk):
    log_gamma = jnp.log1p(-jnp.exp2(-5.0 - jnp.arange(RET_HEADS, dtype=F32)))
    idx = jnp.arange(chunk, dtype=F32)
    rel = idx[:, None] - idx[None, :]
    dmask = jnp.where(rel[None] >= 0, jnp.exp(log_gamma[:, None, None] * jnp.maximum(rel, 0.0)[None]), 0.0)
    q_dec = jnp.exp(log_gamma[:, None] * (idx + 1.0)[None, :])
    k_dec = jnp.exp(log_gamma[:, None] * (chunk - 1.0 - idx)[None, :])
    chunk_dec = jnp.exp(log_gamma * chunk)
    q_dec = jnp.broadcast_to(q_dec[:, :, None], (RET_HEADS, chunk, RET_DV))
    k_dec = jnp.broadcast_to(k_dec[:, :, None], (RET_HEADS, chunk, RET_DK))
    chunk_dec = jnp.broadcast_to(chunk_dec[:, None, None], (RET_HEADS, 1, RET_DV))
    return dmask, q_dec, k_dec, chunk_dec


def _norm_rows(x_ref, g_ref, sc_ref, sh_ref, h_ref):
    tm = x_ref.shape[0]
    step = min(NORM_ROWS, tm)
    trips = tm // step
    per_row = sc_ref.shape[1] != 1
    if not per_row:
        gain, shift = g_ref[0] * (1.0 + sc_ref[0]), sh_ref[0]

    def body(c, carry):
        rows = pl.ds(pl.multiple_of(c * step, step), step)
        x = x_ref[rows, :]
        y = x * lax.rsqrt(jnp.mean(x * x, axis=-1, keepdims=True) + EPS)
        if per_row:
            y = y * (g_ref[0] * (1.0 + sc_ref[0, rows, :])) + sh_ref[0, rows, :]
        else:
            y = y * gain + shift
        h_ref[rows, :] = y.astype(h_ref.dtype)
        return carry

    lax.fori_loop(0, trips, body, 0, unroll=NORM_UNROLL if trips % NORM_UNROLL == 0 else 1)


def _silu(x):
    return x * jax.nn.sigmoid(x)


def _pack_halves(y):
    half = y.shape[1] // 2
    hi = lax.bitcast_convert_type(y[:, :half].astype(jnp.bfloat16).astype(F32), jnp.uint32)
    lo = lax.bitcast_convert_type(y[:, half:].astype(jnp.bfloat16).astype(F32), jnp.uint32)
    return hi | (lo >> 16)


def _unpack_halves(u):
    return (lax.bitcast_convert_type(u & jnp.uint32(0xFFFF0000), F32),
            lax.bitcast_convert_type(u << 16, F32))


def _ada_kernel(cp_ref, cs_ref, w_ref, b_ref, op_ref, os_ref):
    w = w_ref[0].astype(BF16)
    for c_ref, o_ref in ((cp_ref, op_ref), (cs_ref, os_ref)):
        o_ref[0] = jnp.dot(_silu(c_ref[...]).astype(BF16), w, preferred_element_type=F32) + b_ref[0]


def _ada(c_p, c_s, w_ada, b_ada):
    tn = 1024
    width = N_MOD * D_MODEL
    row_spec = lambda c: pl.BlockSpec(c.shape, lambda l, n: (0, 0))
    out_spec = lambda c: pl.BlockSpec((1, c.shape[0], tn), lambda l, n: (l, 0, n))
    return pl.pallas_call(
        _ada_kernel,
        grid=(DEPTH, width // tn),
        in_specs=[row_spec(c_p), row_spec(c_s),
                  pl.BlockSpec((1, D_MODEL, tn), lambda l, n: (l, 0, n)),
                  pl.BlockSpec((1, 1, tn), lambda l, n: (l, 0, n))],
        out_specs=[out_spec(c_p), out_spec(c_s)],
        out_shape=[jax.ShapeDtypeStruct((DEPTH, c.shape[0], width), F32) for c in (c_p, c_s)],
        compiler_params=_params("arbitrary", "arbitrary"),
        name="ada",
    )(c_p, c_s, w_ada, b_ada.reshape(DEPTH, 1, width))


def _inproj_kernel(x_ref, sc_ref, sh_ref, g_ref, w_ref, cos_ref, sin_ref, o_ref, u_ref, h_scr):
    n = pl.program_id(1)
    tm = x_ref.shape[0]
    rc = min(ROW_CHUNK, tm)

    @pl.when(n == 0)
    def _():
        _norm_rows(x_ref, g_ref, sc_ref, sh_ref, h_scr)

    def project(epilogue):
        w = w_ref[0].astype(BF16)
        for r in range(0, tm, rc):
            rows = slice(r, r + rc)
            epilogue(jnp.dot(h_scr[rows, :], w, preferred_element_type=F32), rows)

    def store(acc, rows):
        o_ref[rows, :] = acc.astype(o_ref.dtype)

    def rope(scale):
        def epilogue(acc, rows):
            for j in range(TN_IN // RET_DK):
                cols = slice(j * RET_DK, (j + 1) * RET_DK)
                xh = acc[:, cols]
                r = xh * cos_ref[rows, :] + pltpu.roll(xh, RET_DK // 2, 1) * sin_ref[rows, :]
                if scale != 1.0:
                    r = r * scale
                o_ref[rows, cols] = r.astype(o_ref.dtype)
        project(epilogue)

    @pl.when(n < O_K // TN_IN)
    def _():
        rope(1.0)

    @pl.when(jnp.logical_and(n >= O_K // TN_IN, n < O_V // TN_IN))
    def _():
        rope(RET_DK ** -0.5)

    @pl.when(jnp.logical_and(n >= O_V // TN_IN, n < O_G // TN_IN))
    def _():
        project(store)

    @pl.when(jnp.logical_and(n >= O_G // TN_IN, n < O_U // TN_IN))
    def _():
        project(lambda acc, rows: store(_silu(acc), rows))

    @pl.when(jnp.logical_and(n >= O_U // TN_IN, n < O_M1 // TN_IN))
    def _():
        def epilogue(acc, rows):
            store(acc, rows)
            u_ref[rows, :] = acc
        project(epilogue)

    @pl.when(n >= O_M1 // TN_IN)
    def _():
        project(lambda acc, rows: store(jax.nn.sigmoid(acc), rows))


def _inproj(tok, layer, x, g_mix, w_in, rope_tabs, out_dtype):
    tm = tok.tm
    tn = TN_IN
    assert all(o % tn == 0 for o in (O_K, O_V, O_G, O_U, O_M1, N_IN))
    n_u = D_POOL // tn
    mod3 = tok.mod
    proj, u = pl.pallas_call(
        _inproj_kernel,
        grid=(tok.n_tiles, N_IN // tn),
        in_specs=[pl.BlockSpec((tm, D_MODEL), lambda m, n: (m, 0), pipeline_mode=pl.Buffered(1)),
                  tok.mod_spec(layer, 1), tok.mod_spec(layer, 0),
                  pl.BlockSpec((1, 1, D_MODEL), lambda m, n: (layer, 0, 0)),
                  pl.BlockSpec((1, D_MODEL, tn), lambda m, n: (layer, 0, n)),
                  tok.rope_spec(), tok.rope_spec()],
        out_specs=[pl.BlockSpec((tm, tn), lambda m, n: (m, n)),
                   pl.BlockSpec((tm, tn), lambda m, n: (m, jnp.clip(n - O_U // tn, 0, n_u - 1)))],
        out_shape=[jax.ShapeDtypeStruct((tok.rows, N_IN), out_dtype),
                   jax.ShapeDtypeStruct((tok.rows, D_POOL), F32)],
        scratch_shapes=[pltpu.VMEM((tm, D_MODEL), BF16)],
        compiler_params=_params("arbitrary", "arbitrary"),
        name="inproj",
    )(x, mod3, mod3, g_mix, w_in, *rope_tabs)
    return proj, u


def _head_norm_gate(o, gret, gate):
    mu = jnp.mean(o, axis=-1, keepdims=True)
    d = o - mu
    var = jnp.mean(d * d, axis=-1, keepdims=True)
    return gate.astype(F32) * (d * lax.rsqrt(var + EPS) * gret)


def _ret_prompt_kernel(q_ref, k_ref, v_ref, g_ref, gret_ref, dmask_ref, qdec_ref, kdec_ref, cdec_ref,
                       *rest):
    y_ref, s_ref = rest[-2:]

    @pl.when(pl.program_id(1) == 0)
    def _():
        s_ref[...] = jnp.zeros_like(s_ref)

    for b in range(q_ref.shape[0]):
        for h in range(RET_HEADS):
            ks, vs = slice(h * RET_DK, (h + 1) * RET_DK), slice(h * RET_DV, (h + 1) * RET_DV)
            q, k, v = q_ref[b, :, ks], k_ref[b, :, ks], v_ref[b, :, vs]
            state = s_ref[0, b, h]
            scores = lax.dot_general(q, k, (((1,), (1,)), ((), ())), preferred_element_type=F32) * dmask_ref[h]
            inner = jnp.dot(scores.astype(BF16), v, preferred_element_type=F32)
            cross = jnp.dot(q, state.astype(BF16), preferred_element_type=F32) * qdec_ref[h]
            k_decayed = (k.astype(F32) * kdec_ref[h]).astype(BF16)
            s_ref[0, b, h] = cdec_ref[h] * state + lax.dot_general(
                k_decayed, v, (((0,), (0,)), ((), ())), preferred_element_type=F32)
            y_ref[b, :, vs] = _head_norm_gate(inner + cross, gret_ref[0, :, vs], g_ref[b, :, vs]).astype(y_ref.dtype)


def _table_specs(tables):
    return [pl.BlockSpec(t.shape, lambda *_: (0, 0, 0)) for t in tables]


def _ret_prompt(tok, layer, proj, g_ret, s_all):
    n_b, seq = tok.n_batch, tok.seq
    c = RET_CHUNK
    assert seq % c == 0
    nc = seq // c
    tables = _retention_tables(c)
    bb = RET_PROMPT_BATCH if n_b % RET_PROMPT_BATCH == 0 else 1
    proj3 = proj.reshape(n_b, seq, N_IN)
    in_specs = [pl.BlockSpec((bb, c, QK_W), lambda b, i: (b, i, O_Q // QK_W)),
                pl.BlockSpec((bb, c, QK_W), lambda b, i: (b, i, O_K // QK_W)),
                pl.BlockSpec((bb, c, V_W), lambda b, i: (b, i, O_V // V_W)),
                pl.BlockSpec((bb, c, V_W), lambda b, i: (b, i, O_G // V_W)),
                pl.BlockSpec((1, 1, V_W), lambda b, i: (layer, 0, 0))] + _table_specs(tables)
    args = [proj3, proj3, proj3, proj3, g_ret, *tables]
    aliases = {}
    if s_all is not None:
        aliases = {len(args): 1}
        in_specs.append(pl.BlockSpec(memory_space=pl.ANY))
        args.append(s_all)
    y, s_all = pl.pallas_call(
        _ret_prompt_kernel,
        grid=(n_b // bb, nc),
        in_specs=in_specs,
        out_specs=[pl.BlockSpec((bb, c, V_W), lambda b, i: (b, i, 0)),
                   pl.BlockSpec((1, bb, RET_HEADS, RET_DK, RET_DV), lambda b, i: (layer, b, 0, 0, 0))],
        out_shape=[jax.ShapeDtypeStruct((n_b, seq, V_W), BF16),
                   jax.ShapeDtypeStruct((DEPTH, n_b, RET_HEADS, RET_DK, RET_DV), F32)],
        input_output_aliases=aliases,
        compiler_params=_params("arbitrary", "arbitrary"),
        name="ret_prompt",
    )(*args)
    return y.reshape(tok.rows, V_W), s_all


def _ret_sample_kernel(q_ref, k_ref, v_ref, g_ref, kt_ref, gret_ref, s0_ref, dm_ref, qdec_ref, kdect_ref, cdec_ref,
                       *rest, bb, seq):
    y_ref, s_ref = rest[-2:]
    hs = RET_HEADS * seq
    reps = lambda a: jnp.concatenate([a] * RET_HEADS, axis=0)
    mask_q = (lax.broadcasted_iota(jnp.int32, (hs, QK_W), 0) // seq
              == lax.broadcasted_iota(jnp.int32, (hs, QK_W), 1) // RET_DK)
    mask_v = (lax.broadcasted_iota(jnp.int32, (hs, V_W), 0) // seq
              == lax.broadcasted_iota(jnp.int32, (hs, V_W), 1) // RET_DV)
    nt = (((1,), (1,)), ((), ()))
    for b in range(bb):
        rows_b = slice(b * seq, (b + 1) * seq)
        q, k, v = q_ref[rows_b, :], k_ref[rows_b, :], v_ref[rows_b, :]
        q_blk = jnp.where(mask_q, reps(q), 0.0).astype(BF16)
        scores = lax.dot_general(q_blk, reps(k).astype(BF16), nt, preferred_element_type=F32) * dm_ref[...]
        v_stack = jnp.concatenate([v[:, h * RET_DV:(h + 1) * RET_DV] for h in range(RET_HEADS)], axis=0)
        inner = jnp.dot(scores.astype(BF16), v_stack.astype(BF16), preferred_element_type=F32)
        state = s0_ref[0, b]
        cross = jnp.dot(q_blk, state.reshape(QK_W, RET_DV).astype(BF16), preferred_element_type=F32)
        o = inner + cross * qdec_ref[...]
        mu = jnp.mean(o, axis=-1, keepdims=True)
        d = o - mu
        normed = d * lax.rsqrt(jnp.mean(d * d, axis=-1, keepdims=True) + EPS)
        k_dec_t = (kt_ref[b] * kdect_ref[...]).astype(BF16)
        v_blk = jnp.where(mask_v, reps(v), 0.0).astype(BF16)
        upd = jnp.dot(k_dec_t, v_blk, preferred_element_type=F32)
        for h in range(RET_HEADS):
            vs = slice(h * RET_DV, (h + 1) * RET_DV)
            y_ref[rows_b, vs] = g_ref[rows_b, vs] * (normed[h * seq:(h + 1) * seq] * gret_ref[0, :, vs])
            s_ref[0, b, h] = cdec_ref[h] * state[h] + upd[:, vs]


def _ret_sample(tok, layer, proj, g_ret, state_ret, s_all):
    n_b, seq = tok.n_batch, tok.seq
    bb = 4
    assert n_b % bb == 0
    hs = RET_HEADS * seq
    dmask, q_dec, k_dec, chunk_dec = _retention_tables(seq)
    dm_blk = (dmask[:, :, None, :] * jnp.eye(RET_HEADS, dtype=F32)[:, None, :, None]).reshape(hs, hs)
    tables = [dm_blk, q_dec.reshape(hs, RET_DV), k_dec[:, :, 0].reshape(1, hs), chunk_dec]
    k_t = proj[:, O_K:O_K + QK_W].reshape(n_b, seq, RET_HEADS, RET_DK).transpose(0, 3, 2, 1).reshape(n_b, RET_DK, hs)
    state_spec = pl.BlockSpec((1, bb, RET_HEADS, RET_DK, RET_DV), lambda i: (layer, i, 0, 0, 0))
    in_specs = [pl.BlockSpec((bb * seq, QK_W), lambda i: (i, O_Q // QK_W)),
                pl.BlockSpec((bb * seq, QK_W), lambda i: (i, O_K // QK_W)),
                pl.BlockSpec((bb * seq, V_W), lambda i: (i, O_V // V_W)),
                pl.BlockSpec((bb * seq, V_W), lambda i: (i, O_G // V_W)),
                pl.BlockSpec((bb, RET_DK, hs), lambda i: (i, 0, 0)),
                pl.BlockSpec((1, 1, V_W), lambda i: (layer, 0, 0)),
                state_spec] + [pl.BlockSpec(t.shape, lambda i, nd=t.ndim: (0,) * nd) for t in tables]
    args = [proj, proj, proj, proj, k_t, g_ret, state_ret, *tables]
    aliases = {}
    if s_all is not None:
        aliases = {len(args): 1}
        in_specs.append(pl.BlockSpec(memory_space=pl.ANY))
        args.append(s_all)
    y, s_all = pl.pallas_call(
        functools.partial(_ret_sample_kernel, bb=bb, seq=seq),
        grid=(n_b // bb,),
        in_specs=in_specs,
        out_specs=[pl.BlockSpec((bb * seq, V_W), lambda i: (i, 0)), state_spec],
        out_shape=[jax.ShapeDtypeStruct((tok.rows, V_W), F32),
                   jax.ShapeDtypeStruct(state_ret.shape, F32)],
        input_output_aliases=aliases,
        compiler_params=_params("arbitrary"),
        name="ret_sample",
    )(*args)
    return y, s_all


def _pool_map(pooled, wmap_ref, scale_ref, gi):
    cols = slice(gi * POOL_GW, (gi + 1) * POOL_GW)
    mixed = jnp.dot(pooled.astype(BF16), wmap_ref[0, gi].astype(BF16), preferred_element_type=F32)
    return mixed * scale_ref[0, :, cols]


def _pool_prompt_kernel(u_ref, halo_ref, wmap_ref, scale_ref, o_ref, uf_scr, a_scr, b_scr, *, tl):
    pad = SUBLANES
    top = pad + POOL_HALO
    first = pl.program_id(1) == 0
    uf_scr[0:pad, :] = jnp.zeros((pad, D_POOL), F32)
    a_scr[0:pad, :] = jnp.zeros((pad, POOL_GW), F32)
    b_scr[0:pad, :] = jnp.zeros((pad, POOL_GW), F32)
    uf_scr[pad:top, :] = jnp.where(first, 0.0, halo_ref[...])
    uf_scr[top:, :] = u_ref[...]
    n_rows = POOL_HALO + tl
    row = lax.broadcasted_iota(jnp.int32, (tl, POOL_GW), 0) + pl.program_id(1) * tl
    for gi, w in enumerate(POOL_WINDOWS):
        cols = slice(gi * POOL_GW, (gi + 1) * POOL_GW)
        src = lambda lo, cols=cols: uf_scr[lo:lo + n_rows, cols]
        dst_scr, other = a_scr, b_scr
        shift = 1
        while shift < w:
            dst_scr[pad:, :] = src(pad) + src(pad - shift)
            src = lambda lo, s=dst_scr: s[lo:lo + n_rows, :]
            dst_scr, other = other, dst_scr
            shift *= 2
        acc = src(pad)[POOL_HALO:, :]
        u = uf_scr[top:, cols]
        cnt = jnp.minimum(w, row + 1).astype(F32)
        o_ref[:, cols] = _pool_map(acc / cnt - u, wmap_ref, scale_ref, gi).astype(o_ref.dtype)


def _pool_prompt(tok, layer, u, w_pool_map, pool_scale):
    n_b, seq = tok.n_batch, tok.seq
    tl = min(TL_POOL, seq)
    assert seq % tl == 0 and tl % POOL_HALO == 0
    nl = seq // tl
    per = tl // POOL_HALO
    return pl.pallas_call(
        functools.partial(_pool_prompt_kernel, tl=tl),
        grid=(n_b, nl),
        in_specs=[pl.BlockSpec((tl, D_POOL), lambda b, l: (b * nl + l, 0)),
                  pl.BlockSpec((POOL_HALO, D_POOL), lambda b, l: (jnp.maximum((b * nl + l) * per - 1, 0), 0)),
                  pl.BlockSpec((1,) + w_pool_map.shape[1:], lambda b, l: (layer, 0, 0, 0)),
                  pl.BlockSpec((1, 1, D_POOL), lambda b, l: (layer, 0, 0))],
        out_specs=pl.BlockSpec((tl, D_POOL), lambda b, l: (b * nl + l, 0)),
        out_shape=jax.ShapeDtypeStruct((tok.rows, D_POOL), BF16),
        scratch_shapes=[pltpu.VMEM((SUBLANES + POOL_HALO + tl, D_POOL), F32),
                        pltpu.VMEM((SUBLANES + POOL_HALO + tl, POOL_GW), F32),
                        pltpu.VMEM((SUBLANES + POOL_HALO + tl, POOL_GW), F32)],
        compiler_params=_params("arbitrary", "arbitrary"),
        name="pool_prompt",
    )(u, u, w_pool_map, pool_scale)


def _pool_sample_kernel(buf_ref, u_ref, wmap_ref, scale_ref, o_ref, *, seq, n_b):
    def row(j, cols):
        return buf_ref[j, :, cols] if j < POOL_BUF else u_ref[j - POOL_BUF, :, cols]

    for gi, w in enumerate(POOL_WINDOWS):
        cols = slice(gi * POOL_GW, (gi + 1) * POOL_GW)
        for t in range(seq):
            acc = row(POOL_BUF + t, cols)
            for j in range(1, w):
                acc = acc + row(POOL_BUF + t - j, cols)
            pooled = acc / float(w) - row(POOL_BUF + t, cols)
            o_ref[t, :, cols] = _pool_map(pooled, wmap_ref, scale_ref, gi)


def _pool_sample(tok, layer, u, buf_tm, w_pool_map, pool_scale):
    n_b, seq = tok.n_batch, tok.seq
    assert tok.n_past >= max(POOL_WINDOWS)
    u_tm = u.reshape(n_b, seq, D_POOL).transpose(1, 0, 2)
    mixed_tm = pl.pallas_call(
        functools.partial(_pool_sample_kernel, seq=seq, n_b=n_b),
        grid=(1,),
        in_specs=[pl.BlockSpec(buf_tm.shape, lambda i: (0, 0, 0)),
                  pl.BlockSpec(u_tm.shape, lambda i: (0, 0, 0)),
                  pl.BlockSpec((1,) + w_pool_map.shape[1:], lambda i: (layer, 0, 0, 0)),
                  pl.BlockSpec((1, 1, D_POOL), lambda i: (layer, 0, 0))],
        out_specs=pl.BlockSpec(u_tm.shape, lambda i: (0, 0, 0)),
        out_shape=jax.ShapeDtypeStruct(u_tm.shape, F32),
        compiler_params=_params("arbitrary"),
        name="pool_sample",
    )(buf_tm, u_tm, w_pool_map, pool_scale)
    return mixed_tm.transpose(1, 0, 2).reshape(tok.rows, D_POOL)


def _merge_kernel(yret_ref, mixed_ref, wr_ref, wp_ref, g1_ref, g2_ref, o_ref):
    ret = jnp.dot(yret_ref[...].astype(BF16), wr_ref[0].astype(BF16), preferred_element_type=F32)
    pool = jnp.dot(mixed_ref[...].astype(BF16), wp_ref[0].astype(BF16), preferred_element_type=F32)
    o_ref[...] = (g1_ref[...].astype(F32) * ret + g2_ref[...].astype(F32) * pool).astype(o_ref.dtype)


def _merge(tok, layer, yret, mixed, proj, w_ret_out, w_pool_out):
    tm = tok.tm
    return pl.pallas_call(
        _merge_kernel,
        grid=(tok.n_tiles, D_MODEL // TN),
        in_specs=[pl.BlockSpec((tm, V_W), lambda m, n: (m, 0)),
                  pl.BlockSpec((tm, D_POOL), lambda m, n: (m, 0)),
                  pl.BlockSpec((1, V_W, TN), lambda m, n: (layer, 0, n)),
                  pl.BlockSpec((1, D_POOL, TN), lambda m, n: (layer, 0, n)),
                  pl.BlockSpec((tm, TN), lambda m, n: (m, O_M1 // TN + n)),
                  pl.BlockSpec((tm, TN), lambda m, n: (m, O_M2 // TN + n))],
        out_specs=pl.BlockSpec((tm, TN), lambda m, n: (m, n)),
        out_shape=jax.ShapeDtypeStruct((tok.rows, D_MODEL), BF16),
        compiler_params=_params("arbitrary", "arbitrary"),
        name="merge",
    )(yret, mixed, w_ret_out, w_pool_out, proj, proj)


def _wo_kernel(merged_ref, w_ref, x_ref, gt_ref, o_ref):
    mix = jnp.dot(merged_ref[...], w_ref[0].astype(BF16), preferred_element_type=F32)
    o_ref[...] = x_ref[...] + (1.0 + gt_ref[0]) * mix


def _wo(tok, layer, merged, x, w_o):
    tok = tok.with_tm(TM_OUT)
    tm = tok.tm
    mod3 = tok.mod
    return pl.pallas_call(
        _wo_kernel,
        grid=(tok.n_tiles, D_MODEL // TN),
        in_specs=[pl.BlockSpec((tm, D_MODEL), lambda m, n: (m, 0), pipeline_mode=pl.Buffered(1)),
                  pl.BlockSpec((1, D_MODEL, TN), lambda m, n: (layer, 0, n)),
                  pl.BlockSpec((tm, TN), lambda m, n: (m, n)),
                  tok.mod_spec(layer, 2, tn=TN, col_axis=True)],
        out_specs=pl.BlockSpec((tm, TN), lambda m, n: (m, n)),
        out_shape=jax.ShapeDtypeStruct((tok.rows, D_MODEL), F32),
        compiler_params=_params("arbitrary", "arbitrary"),
        name="wo",
    )(merged, w_o, x, mod3)


def _mat(w_ref):
    return w_ref[(0,) * (len(w_ref.shape) - 2)]


def _final_norm(y, gfin_ref):
    return y * lax.rsqrt(jnp.mean(y * y, axis=-1, keepdims=True) + EPS) * gfin_ref[...]


def _ffn_kernel(x_ref, sc_ref, sh_ref, gt_ref, g_ref, w1_ref, w3_ref, w2_ref, *rest, n_f, with_final):
    gfin_ref = rest[0] if with_final else None
    o_ref, h_scr = rest[-2:]
    f = pl.program_id(1)

    @pl.when(f == 0)
    def _():
        _norm_rows(x_ref, g_ref, sc_ref, sh_ref, h_scr)
        o_ref[...] = jnp.zeros_like(o_ref)

    h = h_scr[...]
    a = jnp.dot(h, _mat(w1_ref).astype(BF16), preferred_element_type=F32)
    b = jnp.dot(h, _mat(w3_ref).astype(BF16), preferred_element_type=F32)
    o_ref[...] += jnp.dot((_silu(a) * b).astype(BF16), _mat(w2_ref).astype(BF16), preferred_element_type=F32)

    @pl.when(f == n_f - 1)
    def _():
        y = x_ref[...] + (1.0 + gt_ref[0]) * o_ref[...]
        o_ref[...] = _final_norm(y, gfin_ref) if with_final else y


def _ffn(tok, layer, x, g_ffn, w1, w3, w2, g_final=None):
    j = layer // 2
    d_ff = w1.shape[-1]
    tok_f = tok.with_tm(TM_FFN)
    tm, n_tiles = tok_f.tm, tok_f.n_tiles
    mod3 = tok_f.mod
    assert d_ff % TF == 0
    n_f = d_ff // TF
    once = pl.Buffered(1)
    in_specs = [pl.BlockSpec((tm, D_MODEL), lambda m, f: (m, 0), pipeline_mode=once),
                tok_f.mod_spec(layer, 4), tok_f.mod_spec(layer, 3), tok_f.mod_spec(layer, 5),
                pl.BlockSpec((1, 1, D_MODEL), lambda m, f: (layer, 0, 0)),
                pl.BlockSpec((1, D_MODEL, TF), lambda m, f: (j, 0, f)),
                pl.BlockSpec((1, D_MODEL, TF), lambda m, f: (j, 0, f)),
                pl.BlockSpec((1, TF, D_MODEL), lambda m, f: (j, f, 0))]
    args = [x, mod3, mod3, mod3, g_ffn, w1, w3, w2]
    if g_final is not None:
        in_specs.append(pl.BlockSpec((1, D_MODEL), lambda m, f: (0, 0)))
        args.append(g_final)
    return pl.pallas_call(
        functools.partial(_ffn_kernel, n_f=n_f, with_final=g_final is not None),
        grid=(n_tiles, n_f),
        in_specs=in_specs,
        out_specs=pl.BlockSpec((tm, D_MODEL), lambda m, f: (m, 0), pipeline_mode=once),
        out_shape=jax.ShapeDtypeStruct((tok.rows, D_MODEL), F32),
        scratch_shapes=[pltpu.VMEM((tm, D_MODEL), BF16)],
        compiler_params=_params("arbitrary", "arbitrary"),
        name="ffn",
    )(*args)


def _router_kernel(x_ref, sc_ref, sh_ref, g_ref, w_ref, b_ref, *rest):
    esel_ref, wsel_ref, h_ref, h_scr = rest[-4:]
    _norm_rows(x_ref, g_ref, sc_ref, sh_ref, h_scr)
    h = h_scr[...]
    logits = jnp.dot(h, w_ref[0].astype(BF16), preferred_element_type=F32) + b_ref[0]
    idx = lax.broadcasted_iota(jnp.int32, logits.shape, 1)
    top1 = jnp.max(logits, axis=-1, keepdims=True)
    i1 = jnp.min(jnp.where(logits == top1, idx, N_EXPERTS), axis=-1, keepdims=True)
    others = jnp.where(idx == i1, -jnp.inf, logits)
    top2 = jnp.max(others, axis=-1, keepdims=True)
    i2 = jnp.min(jnp.where(others == top2, idx, N_EXPERTS), axis=-1, keepdims=True)
    e2 = jnp.exp(top2 - top1)
    denom = 1.0 + e2
    w1, w2 = 1.0 / denom, e2 / denom
    esel_ref[...] = jnp.where(idx == 0, i1, i2)
    wsel_ref[...] = jnp.where(idx == 0, w1, w2)
    h_ref[...] = _pack_halves(h.astype(F32))


def _router(tok, layer, x, g_ffn, w_router, b_router, h_all=None, row0=0, rows_all=None):
    j = layer // 2
    tm = tok.tm
    rows_all = tok.rows if rows_all is None else rows_all
    assert row0 % tm == 0
    small = pl.BlockSpec((tm, N_EXPERTS), lambda m: (m, 0))
    mod3 = tok.mod
    in_specs = [pl.BlockSpec((tm, D_MODEL), lambda m: (m, 0)),
                tok.mod_spec(layer, 4), tok.mod_spec(layer, 3),
                pl.BlockSpec((1, 1, D_MODEL), lambda m: (layer, 0, 0)),
                pl.BlockSpec((1, D_MODEL, N_EXPERTS), lambda m: (j, 0, 0)),
                pl.BlockSpec((1, 1, N_EXPERTS), lambda m: (j, 0, 0))]
    args = [x, mod3, mod3, g_ffn, w_router, b_router]
    aliases = {}
    if h_all is not None:
        aliases = {len(args): 2}
        in_specs.append(pl.BlockSpec(memory_space=pl.ANY))
        args.append(h_all)
    return pl.pallas_call(
        _router_kernel,
        grid=(tok.n_tiles,),
        in_specs=in_specs,
        out_specs=[small, small, pl.BlockSpec((tm, D_MODEL // 2), lambda m: (row0 // tm + m, 0))],
        out_shape=[jax.ShapeDtypeStruct((tok.rows, N_EXPERTS), jnp.int32),
                   jax.ShapeDtypeStruct((tok.rows, N_EXPERTS), F32),
                   jax.ShapeDtypeStruct((rows_all, D_MODEL // 2), jnp.uint32)],
        input_output_aliases=aliases,
        scratch_shapes=[pltpu.VMEM((tm, D_MODEL), BF16)],
        compiler_params=_params("arbitrary"),
        name="router",
    )(*args)


def _route_plan(choice, tm):
    rows = choice.shape[0]
    n_pairs = 2 * rows
    n_tiles = n_pairs // tm + N_EXPERTS
    e_flat = choice.T.reshape(n_pairs)
    onehot = (e_flat[:, None] == jnp.arange(N_EXPERTS, dtype=jnp.int32)[None, :]).astype(jnp.int32)
    csum = jnp.cumsum(onehot, axis=0)
    rank = jnp.sum(csum * onehot, axis=1) - 1
    count = csum[-1]
    tiles_per = (count + tm - 1) // tm
    rows_per = jnp.maximum((((count + jnp.maximum(tiles_per, 1) - 1) // jnp.maximum(tiles_per, 1)) + 7) // 8 * 8, 8)
    tile_end = jnp.cumsum(tiles_per)
    tile_start = tile_end - tiles_per
    pos = (tile_start[e_flat] + rank // rows_per[e_flat]) * tm + rank % rows_per[e_flat]
    token = jnp.arange(n_pairs, dtype=jnp.int32) % rows
    src_row = jnp.zeros((n_tiles * tm,), jnp.int32).at[pos].set(token, unique_indices=True, mode='promise_in_bounds')
    tile_id = jnp.arange(n_tiles, dtype=jnp.int32)
    n_used = tile_end[-1]
    tile_expert = jnp.minimum(jnp.sum(tile_id[:, None] >= tile_end[None, :], axis=1), N_EXPERTS - 1).astype(jnp.int32)
    tile_rows = jnp.clip(count[tile_expert] - (tile_id - tile_start[tile_expert]) * rows_per[tile_expert],
                         0, rows_per[tile_expert])
    tile_rows = jnp.where(tile_id < n_used, tile_rows, 0).astype(jnp.int32)
    last_expert = tile_expert[jnp.maximum(n_used - 1, 0)]
    tile_expert = jnp.where(tile_rows > 0, tile_expert, last_expert)
    return tile_expert, tile_rows, src_row, pos.astype(jnp.int32)


def _moe_kernel(te_ref, nr_ref, src_ref, h_hbm, w1_ref, w3_ref, w2_ref, o_ref, rows_scr, h_scr, acc_scr, sem,
                *, tm, sub, n_f):
    del te_ref
    m, f = pl.program_id(0), pl.program_id(1)
    half = D_MODEL // 2
    group = rows_scr.shape[1]

    def n_groups(tile):
        return (nr_ref[tile] + group - 1) // group

    def start_rows(tile):
        def body(i, c):
            for j in range(group):
                row = src_ref[tile * tm + i * group + j]
                pltpu.make_async_copy(h_hbm.at[pl.ds(row, 1)], rows_scr.at[i, pl.ds(j, 1)], sem).start()
            return c
        lax.fori_loop(0, n_groups(tile), body, 0)

    def wait_rows(tile):
        def body(i, c):
            pltpu.make_async_copy(h_hbm.at[pl.ds(0, group)], rows_scr.at[i], sem).wait()
            return c
        lax.fori_loop(0, n_groups(tile), body, 0)

    @pl.when(f == 0)
    def _():
        acc_scr[...] = jnp.zeros_like(acc_scr)

        @pl.when(m == 0)
        def _():
            rows_scr[...] = jnp.zeros_like(rows_scr)
            start_rows(0)

        wait_rows(m)
        first, second = _unpack_halves(rows_scr[...].reshape(tm, half))
        h_scr[:, :half] = first.astype(BF16)
        h_scr[:, half:] = second.astype(BF16)

        @pl.when(m + 1 < pl.num_programs(0))
        def _():
            start_rows(m + 1)

    n_sub = (nr_ref[m] + sub - 1) // sub
    for n in range(1, tm // sub + 1):
        @pl.when(n_sub == n)
        def _(n=n):
            h = h_scr[0:n * sub, :]
            a = jnp.dot(h, _mat(w1_ref).astype(BF16), preferred_element_type=F32)
            b = jnp.dot(h, _mat(w3_ref).astype(BF16), preferred_element_type=F32)
            acc_scr[0:n * sub, :] += jnp.dot((_silu(a) * b).astype(BF16), _mat(w2_ref).astype(BF16),
                                             preferred_element_type=F32)

    @pl.when(f == n_f - 1)
    def _():
        o_ref[...] = _pack_halves(acc_scr[...])


def _moe_sparse(layer, h, plan, w1, w3, w2, tm):
    j = layer // 2
    tile_expert, tile_rows, src_row, _ = plan
    n_tiles = tile_expert.shape[0]
    d_ff = w1.shape[3]
    assert d_ff % TF == 0 and tm % SUB_MOE == 0
    n_f = d_ff // TF

    def f_sel(m, f, nr):
        return jnp.where(nr[m] > 0, f, n_f - 1)

    return pl.pallas_call(
        functools.partial(_moe_kernel, tm=tm, sub=SUB_MOE, n_f=n_f),
        grid_spec=pltpu.PrefetchScalarGridSpec(
            num_scalar_prefetch=3,
            grid=(n_tiles, n_f),
            in_specs=[pl.BlockSpec(memory_space=pl.ANY),
                      pl.BlockSpec((1, 1, D_MODEL, TF), lambda m, f, te, nr, sr: (j, te[m], 0, f_sel(m, f, nr))),
                      pl.BlockSpec((1, 1, D_MODEL, TF), lambda m, f, te, nr, sr: (j, te[m], 0, f_sel(m, f, nr))),
                      pl.BlockSpec((1, 1, TF, D_MODEL), lambda m, f, te, nr, sr: (j, te[m], f_sel(m, f, nr), 0))],
            out_specs=pl.BlockSpec((tm, D_MODEL // 2), lambda m, f, te, nr, sr: (m, 0)),
            scratch_shapes=[pltpu.VMEM((tm // SUBLANES, SUBLANES, D_MODEL // 2), jnp.uint32),
                            pltpu.VMEM((tm, D_MODEL), BF16),
                            pltpu.VMEM((tm, D_MODEL), F32), pltpu.SemaphoreType.DMA(())]),
        out_shape=jax.ShapeDtypeStruct((n_tiles * tm, D_MODEL // 2), jnp.uint32),
        compiler_params=_params("arbitrary", "arbitrary"),
        name="moe",
    )(tile_expert, tile_rows, src_row, h, w1, w3, w2)


def _combine_kernel(pos_ref, y_hbm, x_ref, gt_ref, w_ref, *rest, tm, row0, rows_all, with_final):
    gfin_ref = rest[0] if with_final else None
    o_ref, y_scr, sems = rest[-3:]
    i, n = pl.program_id(0), pl.num_programs(0)

    half = D_MODEL // 2
    group = y_scr.shape[3]

    def start_rows(tile, slot):
        def body(g, c):
            for k in range(2):
                for j in range(group):
                    slot_of_pair = pos_ref[k * rows_all + row0 + tile * tm + g * group + j]
                    pltpu.make_async_copy(y_hbm.at[pl.ds(slot_of_pair, 1)], y_scr.at[slot, k, g, pl.ds(j, 1)],
                                          sems.at[slot]).start()
            return c
        lax.fori_loop(0, tm // group, body, 0)

    @pl.when(i == 0)
    def _():
        start_rows(0, 0)

    @pl.when(i + 1 < n)
    def _():
        start_rows(i + 1, (i + 1) % 2)

    slot = i % 2
    def wait_group(g, c):
        for k in range(2):
            pltpu.make_async_copy(y_hbm.at[pl.ds(0, group)], y_scr.at[slot, k, g], sems.at[slot]).wait()
        return c

    lax.fori_loop(0, tm // group, wait_group, 0)
    w = w_ref[...]
    first0, second0 = _unpack_halves(y_scr[slot, 0].reshape(tm, half))
    first1, second1 = _unpack_halves(y_scr[slot, 1].reshape(tm, half))
    f = jnp.concatenate([w[:, 0:1] * first0 + w[:, 1:2] * first1,
                         w[:, 0:1] * second0 + w[:, 1:2] * second1], axis=1)
    y = x_ref[...] + (1.0 + gt_ref[0]) * f
    o_ref[...] = _final_norm(y, gfin_ref) if with_final else y


def _combine(tok, layer, y_sorted, pos, row0, x, wsel, g_final):
    tok_c = tok.with_tm(TM_COMBINE)
    tm = tok_c.tm
    mod3 = tok_c.mod
    in_specs = [pl.BlockSpec(memory_space=pl.ANY),
                pl.BlockSpec((tm, D_MODEL), lambda m, pos: (m, 0)),
                tok_c.mod_spec(layer, 5),
                pl.BlockSpec((tm, N_EXPERTS), lambda m, pos: (m, 0))]
    args = [pos, y_sorted, x, mod3, wsel]
    if g_final is not None:
        in_specs.append(pl.BlockSpec((1, D_MODEL), lambda m, pos: (0, 0)))
        args.append(g_final)
    return pl.pallas_call(
        functools.partial(_combine_kernel, tm=tm, row0=row0, rows_all=pos.shape[0] // 2,
                          with_final=g_final is not None),
        grid_spec=pltpu.PrefetchScalarGridSpec(
            num_scalar_prefetch=1,
            grid=(tok_c.n_tiles,),
            in_specs=in_specs,
            out_specs=pl.BlockSpec((tm, D_MODEL), lambda m, pos: (m, 0)),
            scratch_shapes=[pltpu.VMEM((2, 2, tm // SUBLANES, SUBLANES, D_MODEL // 2), jnp.uint32),
                            pltpu.SemaphoreType.DMA((2,))]),
        out_shape=jax.ShapeDtypeStruct((tok.rows, D_MODEL), F32),
        compiler_params=_params("arbitrary"),
        name="combine",
    )(*args)


class _Stream:
    def __init__(self, tok, x, ret_state, pool_state):
        self.tok, self.x, self.ret_state, self.pool_state = tok, x, ret_state, pool_state
        self.rope_tabs = tok.rope_tables()
        self.new_state = None
        self.u_layers = []


def _mixer_layer(s, i, p):
    tok = s.tok
    is_prompt = s.ret_state is None
    proj, u = _inproj(tok, i, s.x, p['g_mix'], p['w_in'], s.rope_tabs, BF16 if is_prompt else F32)
    s.u_layers.append(u)
    if is_prompt:
        yret, s.new_state = _ret_prompt(tok, i, proj, p['g_ret'], s.new_state)
        mixed = _pool_prompt(tok, i, u, p['w_pool_map'], p['pool_scale'])
    else:
        yret, s.new_state = _ret_sample(tok, i, proj, p['g_ret'], s.ret_state, s.new_state)
        mixed = _pool_sample(tok, i, u, s.pool_state[i].transpose(1, 0, 2), p['w_pool_map'], p['pool_scale'])
    merged = _merge(tok, i, yret, mixed, proj, p['w_ret_out'], p['w_pool_out'])
    s.x = _wo(tok, i, merged, s.x, p['w_o'])


def _moe_layer(streams, i, p, g_final):
    rows_all = sum(s.tok.rows for s in streams)
    h_all, row0, choices, weights, starts = None, 0, [], [], []
    for s in streams:
        esel, wsel, h_all = _router(s.tok, i, s.x, p['g_ffn'], p['w_router'], p['b_router'],
                                    h_all=h_all, row0=row0, rows_all=rows_all)
        choices.append(esel[:, :2])
        weights.append(wsel)
        starts.append(row0)
        row0 += s.tok.rows
    plan = _route_plan(jnp.concatenate(choices, axis=0), TM_MOE)
    y_sorted = _moe_sparse(i, h_all, plan, p['w1_moe'], p['w3_moe'], p['w2_moe'], TM_MOE)
    for s, wsel, start in zip(streams, weights, starts):
        s.x = _combine(s.tok, i, y_sorted, plan[3], start, s.x, wsel, g_final)


def kernel(x_prompt, x_sample, c_prompt, c_sample, state_ret, state_pool, w_ada, b_ada, g_mix, w_in, g_ret,
           w_ret_out, w_pool_map, pool_scale, w_pool_out, w_o, g_ffn, w1_dense, w3_dense, w2_dense, w_router,
           b_router, w1_moe, w3_moe, w2_moe, g_final):
    row3 = lambda a: a.reshape(a.shape[0], 1, a.shape[1])
    p = dict(g_mix=row3(g_mix), w_in=w_in, g_ret=row3(g_ret), w_ret_out=w_ret_out, w_pool_map=w_pool_map,
             pool_scale=row3(pool_scale), w_pool_out=w_pool_out, w_o=w_o, g_ffn=row3(g_ffn),
             w1_dense=w1_dense, w3_dense=w3_dense, w2_dense=w2_dense, w_router=w_router, b_router=row3(b_router),
             w1_moe=w1_moe, w3_moe=w3_moe, w2_moe=w2_moe, g_final=g_final.reshape(1, D_MODEL))
    n_bp, seq_p, _ = x_prompt.shape
    n_bs, seq_s, _ = x_sample.shape

    mod_p, mod_s = _ada(c_prompt, jnp.repeat(c_sample, seq_s, axis=0), w_ada, b_ada)

    tok_p = _Tokens(n_bp, seq_p, 0, TM_PROMPT, per_row_mod=False, mod=mod_p)
    tok_s = _Tokens(n_bs, seq_s, PAST_LEN, TM_SAMPLE, per_row_mod=True, mod=mod_s)
    prompt = _Stream(tok_p, x_prompt.reshape(tok_p.rows, D_MODEL), None, None)
    sample = _Stream(tok_s, x_sample.reshape(tok_s.rows, D_MODEL), state_ret, state_pool)
    streams = [prompt, sample]

    for i in range(DEPTH):
        g_fin = p['g_final'] if i == DEPTH - 1 else None
        for s in streams:
            _mixer_layer(s, i, p)
        if i % 2 == 0:
            for s in streams:
                s.x = _ffn(s.tok, i, s.x, p['g_ffn'], p['w1_dense'], p['w3_dense'], p['w2_dense'], g_final=g_fin)
        else:
            _moe_layer(streams, i, p, g_fin)

    buf_prompt = jnp.stack([u.reshape(n_bp, seq_p, D_POOL)[:, seq_p - POOL_BUF:] for u in prompt.u_layers])
    buf_sample = jnp.stack([
        jnp.concatenate([state_pool[i], sample.u_layers[i].reshape(n_bs, seq_s, D_POOL)], axis=1)[:, -POOL_BUF:]
        for i in range(DEPTH)])
    return (prompt.x.reshape(x_prompt.shape), sample.x.reshape(x_sample.shape), prompt.new_state, buf_prompt,
            sample.new_state, buf_sample)
```

```python
import functools

import jax
import jax.numpy as jnp
from jax import lax
from jax.experimental import pallas as pl
from jax.experimental.pallas import tpu as pltpu

F32 = jnp.float32
BF16 = jnp.bfloat16

D_MODEL = 2048
DEPTH = 2
PAST_LEN = 16384
RET_HEADS = 8
RET_DK = 128
RET_DV = 256
RET_CHUNK = 128
ROPE_BASE = 10000.0
QK_W = RET_HEADS * RET_DK
V_W = RET_HEADS * RET_DV
POOL_WINDOWS = (2, 4, 8, 16)
D_POOL = 1024
POOL_GW = 256
POOL_BUF = 15
POOL_HALO = 16
N_IN = 2 * QK_W + 2 * V_W + D_POOL + 2 * D_MODEL
N_EXPERTS = 8
EPS = 1e-6
N_MOD = 6

O_Q, O_K, O_V, O_G, O_U = 0, QK_W, 2 * QK_W, 2 * QK_W + V_W, 2 * QK_W + 2 * V_W
O_M1 = O_U + D_POOL
O_M2 = O_M1 + D_MODEL

VMEM_LIMIT_BYTES = 56 * 1024 * 1024
SUBLANES = 8
TN = 512
TN_IN = 1024
TN_MERGE = 1024
TF = 256
TM_PROMPT = 1024
TM_SAMPLE = 512
TM_FFN = 1024
TM_OUT = 2048
ROW_CHUNK = 256
NORM_ROWS = 16
NORM_UNROLL = 16
TL_POOL = 512
TM_MOE = 1024
SUB_MOE = 256
TM_COMBINE = 256
RET_PROMPT_BATCH = 4


def _params(*sem):
    return pltpu.CompilerParams(dimension_semantics=sem, vmem_limit_bytes=VMEM_LIMIT_BYTES)


class _Tokens:
    def __init__(self, n_batch, seq, n_past, tm, per_row_mod, mod=None):
        self.n_batch, self.seq, self.n_past = n_batch, seq, n_past
        self.rows = n_batch * seq
        self.tm = min(tm, self.rows)
        self.per_row_mod = per_row_mod
        if not per_row_mod:
            self.tm = min(self.tm, seq)
            assert seq % self.tm == 0
        assert self.rows % self.tm == 0
        self.n_tiles = self.rows // self.tm
        if mod is not None and not per_row_mod and mod.ndim == 3 and mod.shape[1] != 1:
            mod = mod.reshape(DEPTH * n_batch, 1, N_MOD * D_MODEL)
        self.mod = mod

    def with_tm(self, tm):
        return _Tokens(self.n_batch, self.seq, self.n_past, tm, self.per_row_mod, self.mod)

    def mod_spec(self, layer, chunk, tn=D_MODEL, col_axis=False):
        per_chunk = D_MODEL // tn
        col = (lambda n: chunk * per_chunk + n) if col_axis else (lambda n: chunk * per_chunk)
        if self.per_row_mod:
            if col_axis:
                return pl.BlockSpec((1, self.tm, tn), lambda m, n, *_: (layer, m, col(n)))
            return pl.BlockSpec((1, self.tm, tn), lambda m, *_: (layer, m, col(0)))
        tiles_per_batch = self.seq // self.tm
        base = layer * self.n_batch
        if col_axis:
            return pl.BlockSpec((1, 1, tn), lambda m, n, *_: (base + m // tiles_per_batch, 0, col(n)))
        return pl.BlockSpec((1, 1, tn), lambda m, *_: (base + m // tiles_per_batch, 0, col(0)))

    def rope_tables(self):
        half = RET_DK // 2
        pos = jnp.arange(self.seq, dtype=F32) + float(self.n_past)
        inv = ROPE_BASE ** (-jnp.arange(half, dtype=F32) / half)
        ang = pos[:, None] * inv[None, :]
        cos, sin = jnp.cos(ang), jnp.sin(ang)
        cos_full = jnp.concatenate([cos, cos], axis=1)
        sin_signed = jnp.concatenate([-sin, sin], axis=1)
        if self.per_row_mod:
            cos_full = jnp.tile(cos_full, (self.n_batch, 1))
            sin_signed = jnp.tile(sin_signed, (self.n_batch, 1))
        return cos_full, sin_signed

    def rope_spec(self):
        n_blocks = (self.rows if self.per_row_mod else self.seq) // self.tm
        return pl.BlockSpec((self.tm, RET_DK), lambda m, *_: (m % n_blocks, 0))


def _retention_tables(chunk):
    log_gamma = jnp.log1p(-jnp.exp2(-5.0 - jnp.arange(RET_HEADS, dtype=F32)))
    idx = jnp.arange(chunk, dtype=F32)
    rel = idx[:, None] - idx[None, :]
    dmask = jnp.where(rel[None] >= 0, jnp.exp(log_gamma[:, None, None] * jnp.maximum(rel, 0.0)[None]), 0.0)
    q_dec = jnp.exp(log_gamma[:, None] * (idx + 1.0)[None, :])
    k_dec = jnp.exp(log_gamma[:, None] * (chunk - 1.0 - idx)[None, :])
    chunk_dec = jnp.exp(log_gamma * chunk)
    q_dec = jnp.broadcast_to(q_dec[:, :, None], (RET_HEADS, chunk, RET_DV))
    k_dec = jnp.broadcast_to(k_dec[:, :, None], (RET_HEADS, chunk, RET_DK))
    chunk_dec = jnp.broadcast_to(chunk_dec[:, None, None], (RET_HEADS, 1, RET_DV))
    return dmask, q_dec, k_dec, chunk_dec


def _norm_rows(x_ref, g_ref, sc_ref, sh_ref, h_ref):
    tm = x_ref.shape[0]
    step = min(NORM_ROWS, tm)
    trips = tm // step
    per_row = sc_ref.shape[1] != 1
    if not per_row:
        gain, shift = g_ref[0] * (1.0 + sc_ref[0]), sh_ref[0]

    def body(c, carry):
        rows = pl.ds(pl.multiple_of(c * step, step), step)
        x = x_ref[rows, :]
        y = x * lax.rsqrt(jnp.mean(x * x, axis=-1, keepdims=True) + EPS)
        if per_row:
            y = y * (g_ref[0] * (1.0 + sc_ref[0, rows, :])) + sh_ref[0, rows, :]
        else:
            y = y * gain + shift
        h_ref[rows, :] = y.astype(h_ref.dtype)
        return carry

    lax.fori_loop(0, trips, body, 0, unroll=NORM_UNROLL if trips % NORM_UNROLL == 0 else 1)


def _silu(x):
    return x * jax.nn.sigmoid(x)


def _pack_halves(y):
    half = y.shape[1] // 2
    hi = lax.bitcast_convert_type(y[:, :half].astype(jnp.bfloat16).astype(F32), jnp.uint32)
    lo = lax.bitcast_convert_type(y[:, half:].astype(jnp.bfloat16).astype(F32), jnp.uint32)
    return hi | (lo >> 16)


def _unpack_halves(u):
    return (lax.bitcast_convert_type(u & jnp.uint32(0xFFFF0000), F32),
            lax.bitcast_convert_type(u << 16, F32))


def _ada_kernel(cp_ref, cs_ref, w_ref, b_ref, op_ref, os_ref):
    w = w_ref[0].astype(BF16)
    for c_ref, o_ref in ((cp_ref, op_ref), (cs_ref, os_ref)):
        o_ref[0] = jnp.dot(_silu(c_ref[...]).astype(BF16), w, preferred_element_type=F32) + b_ref[0]


def _ada(c_p, c_s, w_ada, b_ada):
    tn = 1024
    width = N_MOD * D_MODEL
    row_spec = lambda c: pl.BlockSpec(c.shape, lambda l, n: (0, 0))
    out_spec = lambda c: pl.BlockSpec((1, c.shape[0], tn), lambda l, n: (l, 0, n))
    return pl.pallas_call(
        _ada_kernel,
        grid=(DEPTH, width // tn),
        in_specs=[row_spec(c_p), row_spec(c_s),
                  pl.BlockSpec((1, D_MODEL, tn), lambda l, n: (l, 0, n)),
                  pl.BlockSpec((1, 1, tn), lambda l, n: (l, 0, n))],
        out_specs=[out_spec(c_p), out_spec(c_s)],
        out_shape=[jax.ShapeDtypeStruct((DEPTH, c.shape[0], width), F32) for c in (c_p, c_s)],
        compiler_params=_params("arbitrary", "arbitrary"),
        name="ada",
    )(c_p, c_s, w_ada, b_ada.reshape(DEPTH, 1, width))


def _inproj_kernel(x_ref, sc_ref, sh_ref, g_ref, w_ref, cos_ref, sin_ref, o_ref, u_ref, h_scr):
    n = pl.program_id(1)
    tm = x_ref.shape[0]
    rc = min(ROW_CHUNK, tm)

    @pl.when(n == 0)
    def _():
        _norm_rows(x_ref, g_ref, sc_ref, sh_ref, h_scr)

    def project(epilogue):
        w = w_ref[0].astype(BF16)
        for r in range(0, tm, rc):
            rows = slice(r, r + rc)
            epilogue(jnp.dot(h_scr[rows, :], w, preferred_element_type=F32), rows)

    def store(acc, rows):
        o_ref[rows, :] = acc.astype(o_ref.dtype)

    def rope(scale):
        def epilogue(acc, rows):
            for j in range(TN_IN // RET_DK):
                cols = slice(j * RET_DK, (j + 1) * RET_DK)
                xh = acc[:, cols]
                r = xh * cos_ref[rows, :] + pltpu.roll(xh, RET_DK // 2, 1) * sin_ref[rows, :]
                if scale != 1.0:
                    r = r * scale
                o_ref[rows, cols] = r.astype(o_ref.dtype)
        project(epilogue)

    @pl.when(n < O_K // TN_IN)
    def _():
        rope(1.0)

    @pl.when(jnp.logical_and(n >= O_K // TN_IN, n < O_V // TN_IN))
    def _():
        rope(RET_DK ** -0.5)

    @pl.when(jnp.logical_and(n >= O_V // TN_IN, n < O_G // TN_IN))
    def _():
        project(store)

    @pl.when(jnp.logical_and(n >= O_G // TN_IN, n < O_U // TN_IN))
    def _():
        project(lambda acc, rows: store(_silu(acc), rows))

    @pl.when(jnp.logical_and(n >= O_U // TN_IN, n < O_M1 // TN_IN))
    def _():
        def epilogue(acc, rows):
            store(acc, rows)
            u_ref[rows, :] = acc
        project(epilogue)

    @pl.when(n >= O_M1 // TN_IN)
    def _():
        project(lambda acc, rows: store(jax.nn.sigmoid(acc), rows))


def _inproj(tok, layer, x, g_mix, w_in, rope_tabs, out_dtype):
    tm = tok.tm
    tn = TN_IN
    assert all(o % tn == 0 for o in (O_K, O_V, O_G, O_U, O_M1, N_IN))
    n_u = D_POOL // tn
    mod3 = tok.mod
    proj, u = pl.pallas_call(
        _inproj_kernel,
        grid=(tok.n_tiles, N_IN // tn),
        in_specs=[pl.BlockSpec((tm, D_MODEL), lambda m, n: (m, 0), pipeline_mode=pl.Buffered(1)),
                  tok.mod_spec(layer, 1), tok.mod_spec(layer, 0),
                  pl.BlockSpec((1, 1, D_MODEL), lambda m, n: (layer, 0, 0)),
                  pl.BlockSpec((1, D_MODEL, tn), lambda m, n: (layer, 0, n)),
                  tok.rope_spec(), tok.rope_spec()],
        out_specs=[pl.BlockSpec((tm, tn), lambda m, n: (m, n)),
                   pl.BlockSpec((tm, tn), lambda m, n: (m, jnp.clip(n - O_U // tn, 0, n_u - 1)))],
        out_shape=[jax.ShapeDtypeStruct((tok.rows, N_IN), out_dtype),
                   jax.ShapeDtypeStruct((tok.rows, D_POOL), F32)],
        scratch_shapes=[pltpu.VMEM((tm, D_MODEL), BF16)],
        compiler_params=_params("arbitrary", "arbitrary"),
        name="inproj",
    )(x, mod3, mod3, g_mix, w_in, *rope_tabs)
    return proj, u


def _head_norm_gate(o, gret, gate):
    mu = jnp.mean(o, axis=-1, keepdims=True)
    d = o - mu
    var = jnp.mean(d * d, axis=-1, keepdims=True)
    return gate.astype(F32) * (d * lax.rsqrt(var + EPS) * gret)


def _ret_prompt_kernel(q_ref, k_ref, v_ref, g_ref, gret_ref, dmask_ref, qdec_ref, kdec_ref, cdec_ref,
                       *rest):
    y_ref, s_ref = rest[-2:]

    @pl.when(pl.program_id(1) == 0)
    def _():
        s_ref[...] = jnp.zeros_like(s_ref)

    for b in range(q_ref.shape[0]):
        for h in range(RET_HEADS):
            ks, vs = slice(h * RET_DK, (h + 1) * RET_DK), slice(h * RET_DV, (h + 1) * RET_DV)
            q, k, v = q_ref[b, :, ks], k_ref[b, :, ks], v_ref[b, :, vs]
            state = s_ref[0, b, h]
            scores = lax.dot_general(q, k, (((1,), (1,)), ((), ())), preferred_element_type=F32) * dmask_ref[h]
            inner = jnp.dot(scores.astype(BF16), v, preferred_element_type=F32)
            cross = jnp.dot(q, state.astype(BF16), preferred_element_type=F32) * qdec_ref[h]
            k_decayed = (k.astype(F32) * kdec_ref[h]).astype(BF16)
            s_ref[0, b, h] = cdec_ref[h] * state + lax.dot_general(
                k_decayed, v, (((0,), (0,)), ((), ())), preferred_element_type=F32)
            y_ref[b, :, vs] = _head_norm_gate(inner + cross, gret_ref[0, :, vs], g_ref[b, :, vs]).astype(y_ref.dtype)


def _table_specs(tables):
    return [pl.BlockSpec(t.shape, lambda *_: (0, 0, 0)) for t in tables]


def _ret_prompt(tok, layer, proj, g_ret, s_all):
    n_b, seq = tok.n_batch, tok.seq
    c = RET_CHUNK
    assert seq % c == 0
    nc = seq // c
    tables = _retention_tables(c)
    bb = RET_PROMPT_BATCH if n_b % RET_PROMPT_BATCH == 0 else 1
    proj3 = proj.reshape(n_b, seq, N_IN)
    in_specs = [pl.BlockSpec((bb, c, QK_W), lambda b, i: (b, i, O_Q // QK_W)),
                pl.BlockSpec((bb, c, QK_W), lambda b, i: (b, i, O_K // QK_W)),
                pl.BlockSpec((bb, c, V_W), lambda b, i: (b, i, O_V // V_W)),
                pl.BlockSpec((bb, c, V_W), lambda b, i: (b, i, O_G // V_W)),
                pl.BlockSpec((1, 1, V_W), lambda b, i: (layer, 0, 0))] + _table_specs(tables)
    args = [proj3, proj3, proj3, proj3, g_ret, *tables]
    aliases = {}
    if s_all is not None:
        aliases = {len(args): 1}
        in_specs.append(pl.BlockSpec(memory_space=pl.ANY))
        args.append(s_all)
    y, s_all = pl.pallas_call(
        _ret_prompt_kernel,
        grid=(n_b // bb, nc),
        in_specs=in_specs,
        out_specs=[pl.BlockSpec((bb, c, V_W), lambda b, i: (b, i, 0)),
                   pl.BlockSpec((1, bb, RET_HEADS, RET_DK, RET_DV), lambda b, i: (layer, b, 0, 0, 0))],
        out_shape=[jax.ShapeDtypeStruct((n_b, seq, V_W), BF16),
                   jax.ShapeDtypeStruct((DEPTH, n_b, RET_HEADS, RET_DK, RET_DV), F32)],
        input_output_aliases=aliases,
        compiler_params=_params("arbitrary", "arbitrary"),
        name="ret_prompt",
    )(*args)
    return y.reshape(tok.rows, V_W), s_all


def _ret_sample_kernel(q_ref, k_ref, v_ref, g_ref, kt_ref, gret_ref, s0_ref, dm_ref, qdec_ref, kdect_ref, cdec_ref,
                       *rest, bb, seq):
    y_ref, s_ref = rest[-2:]
    hs = RET_HEADS * seq
    reps = lambda a: jnp.concatenate([a] * RET_HEADS, axis=0)
    mask_q = (lax.broadcasted_iota(jnp.int32, (hs, QK_W), 0) // seq
              == lax.broadcasted_iota(jnp.int32, (hs, QK_W), 1) // RET_DK)
    mask_v = (lax.broadcasted_iota(jnp.int32, (hs, V_W), 0) // seq
              == lax.broadcasted_iota(jnp.int32, (hs, V_W), 1) // RET_DV)
    nt = (((1,), (1,)), ((), ()))
    for b in range(bb):
        rows_b = slice(b * seq, (b + 1) * seq)
        q, k, v = q_ref[rows_b, :], k_ref[rows_b, :], v_ref[rows_b, :]
        q_blk = jnp.where(mask_q, reps(q), 0.0).astype(BF16)
        scores = lax.dot_general(q_blk, reps(k).astype(BF16), nt, preferred_element_type=F32) * dm_ref[...]
        v_stack = jnp.concatenate([v[:, h * RET_DV:(h + 1) * RET_DV] for h in range(RET_HEADS)], axis=0)
        inner = jnp.dot(scores.astype(BF16), v_stack.astype(BF16), preferred_element_type=F32)
        state = s0_ref[0, b]
        cross = jnp.dot(q_blk, state.reshape(QK_W, RET_DV).astype(BF16), preferred_element_type=F32)
        o = inner + cross * qdec_ref[...]
        mu = jnp.mean(o, axis=-1, keepdims=True)
        d = o - mu
        normed = d * lax.rsqrt(jnp.mean(d * d, axis=-1, keepdims=True) + EPS)
        k_dec_t = (kt_ref[b] * kdect_ref[...]).astype(BF16)
        v_blk = jnp.where(mask_v, reps(v), 0.0).astype(BF16)
        upd = jnp.dot(k_dec_t, v_blk, preferred_element_type=F32)
        for h in range(RET_HEADS):
            vs = slice(h * RET_DV, (h + 1) * RET_DV)
            y_ref[rows_b, vs] = g_ref[rows_b, vs] * (normed[h * seq:(h + 1) * seq] * gret_ref[0, :, vs])
            s_ref[0, b, h] = cdec_ref[h] * state[h] + upd[:, vs]


def _ret_sample(tok, layer, proj, g_ret, state_ret, s_all):
    n_b, seq = tok.n_batch, tok.seq
    bb = 4
    assert n_b % bb == 0
    hs = RET_HEADS * seq
    dmask, q_dec, k_dec, chunk_dec = _retention_tables(seq)
    dm_blk = (dmask[:, :, None, :] * jnp.eye(RET_HEADS, dtype=F32)[:, None, :, None]).reshape(hs, hs)
    tables = [dm_blk, q_dec.reshape(hs, RET_DV), k_dec[:, :, 0].reshape(1, hs), chunk_dec]
    k_t = proj[:, O_K:O_K + QK_W].reshape(n_b, seq, RET_HEADS, RET_DK).transpose(0, 3, 2, 1).reshape(n_b, RET_DK, hs)
    state_spec = pl.BlockSpec((1, bb, RET_HEADS, RET_DK, RET_DV), lambda i: (layer, i, 0, 0, 0))
    in_specs = [pl.BlockSpec((bb * seq, QK_W), lambda i: (i, O_Q // QK_W)),
                pl.BlockSpec((bb * seq, QK_W), lambda i: (i, O_K // QK_W)),
                pl.BlockSpec((bb * seq, V_W), lambda i: (i, O_V // V_W)),
                pl.BlockSpec((bb * seq, V_W), lambda i: (i, O_G // V_W)),
                pl.BlockSpec((bb, RET_DK, hs), lambda i: (i, 0, 0)),
                pl.BlockSpec((1, 1, V_W), lambda i: (layer, 0, 0)),
                state_spec] + [pl.BlockSpec(t.shape, lambda i, nd=t.ndim: (0,) * nd) for t in tables]
    args = [proj, proj, proj, proj, k_t, g_ret, state_ret, *tables]
    aliases = {}
    if s_all is not None:
        aliases = {len(args): 1}
        in_specs.append(pl.BlockSpec(memory_space=pl.ANY))
        args.append(s_all)
    y, s_all = pl.pallas_call(
        functools.partial(_ret_sample_kernel, bb=bb, seq=seq),
        grid=(n_b // bb,),
        in_specs=in_specs,
        out_specs=[pl.BlockSpec((bb * seq, V_W), lambda i: (i, 0)), state_spec],
        out_shape=[jax.ShapeDtypeStruct((tok.rows, V_W), F32),
                   jax.ShapeDtypeStruct(state_ret.shape, F32)],
        input_output_aliases=aliases,
        compiler_params=_params("arbitrary"),
        name="ret_sample",
    )(*args)
    return y, s_all


def _pool_map(pooled, wmap_ref, scale_ref, gi):
    cols = slice(gi * POOL_GW, (gi + 1) * POOL_GW)
    mixed = jnp.dot(pooled.astype(BF16), wmap_ref[0, gi].astype(BF16), preferred_element_type=F32)
    return mixed * scale_ref[0, :, cols]


def _pool_prompt_kernel(u_ref, halo_ref, wmap_ref, scale_ref, o_ref, uf_scr, a_scr, b_scr, *, tl):
    pad = SUBLANES
    top = pad + POOL_HALO
    first = pl.program_id(1) == 0
    uf_scr[0:pad, :] = jnp.zeros((pad, D_POOL), F32)
    a_scr[0:pad, :] = jnp.zeros((pad, POOL_GW), F32)
    b_scr[0:pad, :] = jnp.zeros((pad, POOL_GW), F32)
    uf_scr[pad:top, :] = jnp.where(first, 0.0, halo_ref[...])
    uf_scr[top:, :] = u_ref[...]
    n_rows = POOL_HALO + tl
    row = lax.broadcasted_iota(jnp.int32, (tl, POOL_GW), 0) + pl.program_id(1) * tl
    for gi, w in enumerate(POOL_WINDOWS):
        cols = slice(gi * POOL_GW, (gi + 1) * POOL_GW)
        src = lambda lo, cols=cols: uf_scr[lo:lo + n_rows, cols]
        dst_scr, other = a_scr, b_scr
        shift = 1
        while shift < w:
            dst_scr[pad:, :] = src(pad) + src(pad - shift)
            src = lambda lo, s=dst_scr: s[lo:lo + n_rows, :]
            dst_scr, other = other, dst_scr
            shift *= 2
        acc = src(pad)[POOL_HALO:, :]
        u = uf_scr[top:, cols]
        cnt = jnp.minimum(w, row + 1).astype(F32)
        o_ref[:, cols] = _pool_map(acc / cnt - u, wmap_ref, scale_ref, gi).astype(o_ref.dtype)


def _pool_prompt(tok, layer, u, w_pool_map, pool_scale):
    n_b, seq = tok.n_batch, tok.seq
    tl = min(TL_POOL, seq)
    assert seq % tl == 0 and tl % POOL_HALO == 0
    nl = seq // tl
    per = tl // POOL_HALO
    return pl.pallas_call(
        functools.partial(_pool_prompt_kernel, tl=tl),
        grid=(n_b, nl),
        in_specs=[pl.BlockSpec((tl, D_POOL), lambda b, l: (b * nl + l, 0)),
                  pl.BlockSpec((POOL_HALO, D_POOL), lambda b, l: (jnp.maximum((b * nl + l) * per - 1, 0), 0)),
                  pl.BlockSpec((1,) + w_pool_map.shape[1:], lambda b, l: (layer, 0, 0, 0)),
                  pl.BlockSpec((1, 1, D_POOL), lambda b, l: (layer, 0, 0))],
        out_specs=pl.BlockSpec((tl, D_POOL), lambda b, l: (b * nl + l, 0)),
        out_shape=jax.ShapeDtypeStruct((tok.rows, D_POOL), BF16),
        scratch_shapes=[pltpu.VMEM((SUBLANES + POOL_HALO + tl, D_POOL), F32),
                        pltpu.VMEM((SUBLANES + POOL_HALO + tl, POOL_GW), F32),
                        pltpu.VMEM((SUBLANES + POOL_HALO + tl, POOL_GW), F32)],
        compiler_params=_params("arbitrary", "arbitrary"),
        name="pool_prompt",
    )(u, u, w_pool_map, pool_scale)


def _pool_sample_kernel(buf_ref, u_ref, wmap_ref, scale_ref, o_ref, *, seq, n_b):
    def row(j, cols):
        return buf_ref[j, :, cols] if j < POOL_BUF else u_ref[j - POOL_BUF, :, cols]

    for gi, w in enumerate(POOL_WINDOWS):
        cols = slice(gi * POOL_GW, (gi + 1) * POOL_GW)
        for t in range(seq):
            acc = row(POOL_BUF + t, cols)
            for j in range(1, w):
                acc = acc + row(POOL_BUF + t - j, cols)
            pooled = acc / float(w) - row(POOL_BUF + t, cols)
            o_ref[t, :, cols] = _pool_map(pooled, wmap_ref, scale_ref, gi)


def _pool_sample(tok, layer, u, buf_tm, w_pool_map, pool_scale):
    n_b, seq = tok.n_batch, tok.seq
    assert tok.n_past >= max(POOL_WINDOWS)
    u_tm = u.reshape(n_b, seq, D_POOL).transpose(1, 0, 2)
    mixed_tm = pl.pallas_call(
        functools.partial(_pool_sample_kernel, seq=seq, n_b=n_b),
        grid=(1,),
        in_specs=[pl.BlockSpec(buf_tm.shape, lambda i: (0, 0, 0)),
                  pl.BlockSpec(u_tm.shape, lambda i: (0, 0, 0)),
                  pl.BlockSpec((1,) + w_pool_map.shape[1:], lambda i: (layer, 0, 0, 0)),
                  pl.BlockSpec((1, 1, D_POOL), lambda i: (layer, 0, 0))],
        out_specs=pl.BlockSpec(u_tm.shape, lambda i: (0, 0, 0)),
        out_shape=jax.ShapeDtypeStruct(u_tm.shape, F32),
        compiler_params=_params("arbitrary"),
        name="pool_sample",
    )(buf_tm, u_tm, w_pool_map, pool_scale)
    return mixed_tm.transpose(1, 0, 2).reshape(tok.rows, D_POOL)


def _merge_kernel(yret_ref, mixed_ref, wr_ref, wp_ref, g1_ref, g2_ref, o_ref, wr_scr, wp_scr):
    @pl.when(pl.program_id(1) == 0)
    def _():
        wr_scr[...] = wr_ref[0].astype(BF16)
        wp_scr[...] = wp_ref[0].astype(BF16)

    ret = jnp.dot(yret_ref[...].astype(BF16), wr_scr[...], preferred_element_type=F32)
    pool = jnp.dot(mixed_ref[...].astype(BF16), wp_scr[...], preferred_element_type=F32)
    o_ref[...] = (g1_ref[...].astype(F32) * ret + g2_ref[...].astype(F32) * pool).astype(o_ref.dtype)


def _merge(tok, layer, yret, mixed, proj, w_ret_out, w_pool_out):
    tm, tn = tok.tm, TN_MERGE
    once = pl.Buffered(1)
    return pl.pallas_call(
        _merge_kernel,
        grid=(D_MODEL // tn, tok.n_tiles),
        in_specs=[pl.BlockSpec((tm, V_W), lambda n, m: (m, 0)),
                  pl.BlockSpec((tm, D_POOL), lambda n, m: (m, 0)),
                  pl.BlockSpec((1, V_W, tn), lambda n, m: (layer, 0, n), pipeline_mode=once),
                  pl.BlockSpec((1, D_POOL, tn), lambda n, m: (layer, 0, n), pipeline_mode=once),
                  pl.BlockSpec((tm, tn), lambda n, m: (m, O_M1 // tn + n)),
                  pl.BlockSpec((tm, tn), lambda n, m: (m, O_M2 // tn + n))],
        out_specs=pl.BlockSpec((tm, tn), lambda n, m: (m, n)),
        out_shape=jax.ShapeDtypeStruct((tok.rows, D_MODEL), BF16),
        scratch_shapes=[pltpu.VMEM((V_W, tn), BF16), pltpu.VMEM((D_POOL, tn), BF16)],
        compiler_params=_params("arbitrary", "arbitrary"),
        name="merge",
    )(yret, mixed, w_ret_out, w_pool_out, proj, proj)


def _wo_kernel(merged_ref, w_ref, x_ref, gt_ref, o_ref):
    mix = jnp.dot(merged_ref[...], w_ref[0].astype(BF16), preferred_element_type=F32)
    o_ref[...] = x_ref[...] + (1.0 + gt_ref[0]) * mix


def _wo(tok, layer, merged, x, w_o):
    tok = tok.with_tm(TM_OUT)
    tm = tok.tm
    mod3 = tok.mod
    return pl.pallas_call(
        _wo_kernel,
        grid=(tok.n_tiles, D_MODEL // TN),
        in_specs=[pl.BlockSpec((tm, D_MODEL), lambda m, n: (m, 0), pipeline_mode=pl.Buffered(1)),
                  pl.BlockSpec((1, D_MODEL, TN), lambda m, n: (layer, 0, n)),
                  pl.BlockSpec((tm, TN), lambda m, n: (m, n)),
                  tok.mod_spec(layer, 2, tn=TN, col_axis=True)],
        out_specs=pl.BlockSpec((tm, TN), lambda m, n: (m, n)),
        out_shape=jax.ShapeDtypeStruct((tok.rows, D_MODEL), F32),
        compiler_params=_params("arbitrary", "arbitrary"),
        name="wo",
    )(merged, w_o, x, mod3)


def _mat(w_ref):
    return w_ref[(0,) * (len(w_ref.shape) - 2)]


def _final_norm(y, gfin_ref):
    return y * lax.rsqrt(jnp.mean(y * y, axis=-1, keepdims=True) + EPS) * gfin_ref[...]


def _ffn_kernel(x_ref, sc_ref, sh_ref, gt_ref, g_ref, w1_ref, w3_ref, w2_ref, *rest, n_f, with_final):
    gfin_ref = rest[0] if with_final else None
    o_ref, h_scr = rest[-2:]
    f = pl.program_id(1)

    @pl.when(f == 0)
    def _():
        _norm_rows(x_ref, g_ref, sc_ref, sh_ref, h_scr)
        o_ref[...] = jnp.zeros_like(o_ref)

    h = h_scr[...]
    a = jnp.dot(h, _mat(w1_ref).astype(BF16), preferred_element_type=F32)
    b = jnp.dot(h, _mat(w3_ref).astype(BF16), preferred_element_type=F32)
    o_ref[...] += jnp.dot((_silu(a) * b).astype(BF16), _mat(w2_ref).astype(BF16), preferred_element_type=F32)

    @pl.when(f == n_f - 1)
    def _():
        y = x_ref[...] + (1.0 + gt_ref[0]) * o_ref[...]
        o_ref[...] = _final_norm(y, gfin_ref) if with_final else y


def _ffn(tok, layer, x, g_ffn, w1, w3, w2, g_final=None):
    j = layer // 2
    d_ff = w1.shape[-1]
    tok_f = tok.with_tm(TM_FFN)
    tm, n_tiles = tok_f.tm, tok_f.n_tiles
    mod3 = tok_f.mod
    assert d_ff % TF == 0
    n_f = d_ff // TF
    once = pl.Buffered(1)
    in_specs = [pl.BlockSpec((tm, D_MODEL), lambda m, f: (m, 0), pipeline_mode=once),
                tok_f.mod_spec(layer, 4), tok_f.mod_spec(layer, 3), tok_f.mod_spec(layer, 5),
                pl.BlockSpec((1, 1, D_MODEL), lambda m, f: (layer, 0, 0)),
                pl.BlockSpec((1, D_MODEL, TF), lambda m, f: (j, 0, f)),
                pl.BlockSpec((1, D_MODEL, TF), lambda m, f: (j, 0, f)),
                pl.BlockSpec((1, TF, D_MODEL), lambda m, f: (j, f, 0))]
    args = [x, mod3, mod3, mod3, g_ffn, w1, w3, w2]
    if g_final is not None:
        in_specs.append(pl.BlockSpec((1, D_MODEL), lambda m, f: (0, 0)))
        args.append(g_final)
    return pl.pallas_call(
        functools.partial(_ffn_kernel, n_f=n_f, with_final=g_final is not None),
        grid=(n_tiles, n_f),
        in_specs=in_specs,
        out_specs=pl.BlockSpec((tm, D_MODEL), lambda m, f: (m, 0), pipeline_mode=once),
        out_shape=jax.ShapeDtypeStruct((tok.rows, D_MODEL), F32),
        scratch_shapes=[pltpu.VMEM((tm, D_MODEL), BF16)],
        compiler_params=_params("arbitrary", "arbitrary"),
        name="ffn",
    )(*args)


def _router_kernel(x_ref, sc_ref, sh_ref, g_ref, w_ref, b_ref, *rest):
    esel_ref, wsel_ref, h_ref, h_scr = rest[-4:]
    _norm_rows(x_ref, g_ref, sc_ref, sh_ref, h_scr)
    h = h_scr[...]
    logits = jnp.dot(h, w_ref[0].astype(BF16), preferred_element_type=F32) + b_ref[0]
    idx = lax.broadcasted_iota(jnp.int32, logits.shape, 1)
    top1 = jnp.max(logits, axis=-1, keepdims=True)
    i1 = jnp.min(jnp.where(logits == top1, idx, N_EXPERTS), axis=-1, keepdims=True)
    others = jnp.where(idx == i1, -jnp.inf, logits)
    top2 = jnp.max(others, axis=-1, keepdims=True)
    i2 = jnp.min(jnp.where(others == top2, idx, N_EXPERTS), axis=-1, keepdims=True)
    e2 = jnp.exp(top2 - top1)
    denom = 1.0 + e2
    w1, w2 = 1.0 / denom, e2 / denom
    esel_ref[...] = jnp.where(idx == 0, i1, i2)
    wsel_ref[...] = jnp.where(idx == 0, w1, w2)
    h_ref[...] = _pack_halves(h.astype(F32))


def _router(tok, layer, x, g_ffn, w_router, b_router, h_all=None, row0=0, rows_all=None):
    j = layer // 2
    tm = tok.tm
    rows_all = tok.rows if rows_all is None else rows_all
    assert row0 % tm == 0
    small = pl.BlockSpec((tm, N_EXPERTS), lambda m: (m, 0))
    mod3 = tok.mod
    in_specs = [pl.BlockSpec((tm, D_MODEL), lambda m: (m, 0)),
                tok.mod_spec(layer, 4), tok.mod_spec(layer, 3),
                pl.BlockSpec((1, 1, D_MODEL), lambda m: (layer, 0, 0)),
                pl.BlockSpec((1, D_MODEL, N_EXPERTS), lambda m: (j, 0, 0)),
                pl.BlockSpec((1, 1, N_EXPERTS), lambda m: (j, 0, 0))]
    args = [x, mod3, mod3, g_ffn, w_router, b_router]
    aliases = {}
    if h_all is not None:
        aliases = {len(args): 2}
        in_specs.append(pl.BlockSpec(memory_space=pl.ANY))
        args.append(h_all)
    return pl.pallas_call(
        _router_kernel,
        grid=(tok.n_tiles,),
        in_specs=in_specs,
        out_specs=[small, small, pl.BlockSpec((tm, D_MODEL // 2), lambda m: (row0 // tm + m, 0))],
        out_shape=[jax.ShapeDtypeStruct((tok.rows, N_EXPERTS), jnp.int32),
                   jax.ShapeDtypeStruct((tok.rows, N_EXPERTS), F32),
                   jax.ShapeDtypeStruct((rows_all, D_MODEL // 2), jnp.uint32)],
        input_output_aliases=aliases,
        scratch_shapes=[pltpu.VMEM((tm, D_MODEL), BF16)],
        compiler_params=_params("arbitrary"),
        name="router",
    )(*args)


def _route_plan(choice, tm):
    rows = choice.shape[0]
    n_pairs = 2 * rows
    n_tiles = n_pairs // tm + N_EXPERTS
    e_flat = choice.T.reshape(n_pairs)
    onehot = (e_flat[:, None] == jnp.arange(N_EXPERTS, dtype=jnp.int32)[None, :]).astype(jnp.int32)
    csum = jnp.cumsum(onehot, axis=0)
    rank = jnp.sum(csum * onehot, axis=1) - 1
    count = csum[-1]
    tiles_per = (count + tm - 1) // tm
    rows_per = jnp.maximum((((count + jnp.maximum(tiles_per, 1) - 1) // jnp.maximum(tiles_per, 1)) + 7) // 8 * 8, 8)
    tile_end = jnp.cumsum(tiles_per)
    tile_start = tile_end - tiles_per
    pos = (tile_start[e_flat] + rank // rows_per[e_flat]) * tm + rank % rows_per[e_flat]
    token = jnp.arange(n_pairs, dtype=jnp.int32) % rows
    src_row = jnp.zeros((n_tiles * tm,), jnp.int32).at[pos].set(token, unique_indices=True, mode='promise_in_bounds')
    tile_id = jnp.arange(n_tiles, dtype=jnp.int32)
    n_used = tile_end[-1]
    tile_expert = jnp.minimum(jnp.sum(tile_id[:, None] >= tile_end[None, :], axis=1), N_EXPERTS - 1).astype(jnp.int32)
    tile_rows = jnp.clip(count[tile_expert] - (tile_id - tile_start[tile_expert]) * rows_per[tile_expert],
                         0, rows_per[tile_expert])
    tile_rows = jnp.where(tile_id < n_used, tile_rows, 0).astype(jnp.int32)
    last_expert = tile_expert[jnp.maximum(n_used - 1, 0)]
    tile_expert = jnp.where(tile_rows > 0, tile_expert, last_expert)
    return tile_expert, tile_rows, src_row, pos.astype(jnp.int32)


def _moe_kernel(te_ref, nr_ref, src_ref, h_hbm, w1_ref, w3_ref, w2_ref, o_ref, rows_scr, h_scr, acc_scr, sem,
                *, tm, sub, n_f):
    del te_ref
    m, f = pl.program_id(0), pl.program_id(1)
    half = D_MODEL // 2
    group = rows_scr.shape[1]

    def n_groups(tile):
        return (nr_ref[tile] + group - 1) // group

    def start_rows(tile):
        def body(i, c):
            for j in range(group):
                row = src_ref[tile * tm + i * group + j]
                pltpu.make_async_copy(h_hbm.at[pl.ds(row, 1)], rows_scr.at[i, pl.ds(j, 1)], sem).start()
            return c
        lax.fori_loop(0, n_groups(tile), body, 0)

    def wait_rows(tile):
        def body(i, c):
            pltpu.make_async_copy(h_hbm.at[pl.ds(0, group)], rows_scr.at[i], sem).wait()
            return c
        lax.fori_loop(0, n_groups(tile), body, 0)

    @pl.when(f == 0)
    def _():
        acc_scr[...] = jnp.zeros_like(acc_scr)

        @pl.when(m == 0)
        def _():
            rows_scr[...] = jnp.zeros_like(rows_scr)
            start_rows(0)

        wait_rows(m)
        first, second = _unpack_halves(rows_scr[...].reshape(tm, half))
        h_scr[:, :half] = first.astype(BF16)
        h_scr[:, half:] = second.astype(BF16)

        @pl.when(m + 1 < pl.num_programs(0))
        def _():
            start_rows(m + 1)

    n_sub = (nr_ref[m] + sub - 1) // sub
    for n in range(1, tm // sub + 1):
        @pl.when(n_sub == n)
        def _(n=n):
            h = h_scr[0:n * sub, :]
            a = jnp.dot(h, _mat(w1_ref).astype(BF16), preferred_element_type=F32)
            b = jnp.dot(h, _mat(w3_ref).astype(BF16), preferred_element_type=F32)
            acc_scr[0:n * sub, :] += jnp.dot((_silu(a) * b).astype(BF16), _mat(w2_ref).astype(BF16),
                                             preferred_element_type=F32)

    @pl.when(f == n_f - 1)
    def _():
        o_ref[...] = _pack_halves(acc_scr[...])


def _moe_sparse(layer, h, plan, w1, w3, w2, tm):
    j = layer // 2
    tile_expert, tile_rows, src_row, _ = plan
    n_tiles = tile_expert.shape[0]
    d_ff = w1.shape[3]
    assert d_ff % TF == 0 and tm % SUB_MOE == 0
    n_f = d_ff // TF

    def f_sel(m, f, nr):
        return jnp.where(nr[m] > 0, f, n_f - 1)

    return pl.pallas_call(
        functools.partial(_moe_kernel, tm=tm, sub=SUB_MOE, n_f=n_f),
        grid_spec=pltpu.PrefetchScalarGridSpec(
            num_scalar_prefetch=3,
            grid=(n_tiles, n_f),
            in_specs=[pl.BlockSpec(memory_space=pl.ANY),
                      pl.BlockSpec((1, 1, D_MODEL, TF), lambda m, f, te, nr, sr: (j, te[m], 0, f_sel(m, f, nr))),
                      pl.BlockSpec((1, 1, D_MODEL, TF), lambda m, f, te, nr, sr: (j, te[m], 0, f_sel(m, f, nr))),
                      pl.BlockSpec((1, 1, TF, D_MODEL), lambda m, f, te, nr, sr: (j, te[m], f_sel(m, f, nr), 0))],
            out_specs=pl.BlockSpec((tm, D_MODEL // 2), lambda m, f, te, nr, sr: (m, 0)),
            scratch_shapes=[pltpu.VMEM((tm // SUBLANES, SUBLANES, D_MODEL // 2), jnp.uint32),
                            pltpu.VMEM((tm, D_MODEL), BF16),
                            pltpu.VMEM((tm, D_MODEL), F32), pltpu.SemaphoreType.DMA(())]),
        out_shape=jax.ShapeDtypeStruct((n_tiles * tm, D_MODEL // 2), jnp.uint32),
        compiler_params=_params("arbitrary", "arbitrary"),
        name="moe",
    )(tile_expert, tile_rows, src_row, h, w1, w3, w2)


def _combine_kernel(pos_ref, y_hbm, x_ref, gt_ref, w_ref, *rest, tm, row0, rows_all, with_final):
    gfin_ref = rest[0] if with_final else None
    o_ref, y_scr, sems = rest[-3:]
    i, n = pl.program_id(0), pl.num_programs(0)

    half = D_MODEL // 2
    group = y_scr.shape[3]

    def start_rows(tile, slot):
        def body(g, c):
            for k in range(2):
                for j in range(group):
                    slot_of_pair = pos_ref[k * rows_all + row0 + tile * tm + g * group + j]
                    pltpu.make_async_copy(y_hbm.at[pl.ds(slot_of_pair, 1)], y_scr.at[slot, k, g, pl.ds(j, 1)],
                                          sems.at[slot]).start()
            return c
        lax.fori_loop(0, tm // group, body, 0)

    @pl.when(i == 0)
    def _():
        start_rows(0, 0)

    @pl.when(i + 1 < n)
    def _():
        start_rows(i + 1, (i + 1) % 2)

    slot = i % 2
    def wait_group(g, c):
        for k in range(2):
            pltpu.make_async_copy(y_hbm.at[pl.ds(0, group)], y_scr.at[slot, k, g], sems.at[slot]).wait()
        return c

    lax.fori_loop(0, tm // group, wait_group, 0)
    w = w_ref[...]
    first0, second0 = _unpack_halves(y_scr[slot, 0].reshape(tm, half))
    first1, second1 = _unpack_halves(y_scr[slot, 1].reshape(tm, half))
    f = jnp.concatenate([w[:, 0:1] * first0 + w[:, 1:2] * first1,
                         w[:, 0:1] * second0 + w[:, 1:2] * second1], axis=1)
    y = x_ref[...] + (1.0 + gt_ref[0]) * f
    o_ref[...] = _final_norm(y, gfin_ref) if with_final else y


def _combine(tok, layer, y_sorted, pos, row0, x, wsel, g_final):
    tok_c = tok.with_tm(TM_COMBINE)
    tm = tok_c.tm
    mod3 = tok_c.mod
    in_specs = [pl.BlockSpec(memory_space=pl.ANY),
                pl.BlockSpec((tm, D_MODEL), lambda m, pos: (m, 0)),
                tok_c.mod_spec(layer, 5),
                pl.BlockSpec((tm, N_EXPERTS), lambda m, pos: (m, 0))]
    args = [pos, y_sorted, x, mod3, wsel]
    if g_final is not None:
        in_specs.append(pl.BlockSpec((1, D_MODEL), lambda m, pos: (0, 0)))
        args.append(g_final)
    return pl.pallas_call(
        functools.partial(_combine_kernel, tm=tm, row0=row0, rows_all=pos.shape[0] // 2,
                          with_final=g_final is not None),
        grid_spec=pltpu.PrefetchScalarGridSpec(
            num_scalar_prefetch=1,
            grid=(tok_c.n_tiles,),
            in_specs=in_specs,
            out_specs=pl.BlockSpec((tm, D_MODEL), lambda m, pos: (m, 0)),
            scratch_shapes=[pltpu.VMEM((2, 2, tm // SUBLANES, SUBLANES, D_MODEL // 2), jnp.uint32),
                            pltpu.SemaphoreType.DMA((2,))]),
        out_shape=jax.ShapeDtypeStruct((tok.rows, D_MODEL), F32),
        compiler_params=_params("arbitrary"),
        name="combine",
    )(*args)


class _Stream:
    def __init__(self, tok, x, ret_state, pool_state):
        self.tok, self.x, self.ret_state, self.pool_state = tok, x, ret_state, pool_state
        self.rope_tabs = tok.rope_tables()
        self.new_state = None
        self.u_layers = []


def _mixer_layer(s, i, p):
    tok = s.tok
    is_prompt = s.ret_state is None
    proj, u = _inproj(tok, i, s.x, p['g_mix'], p['w_in'], s.rope_tabs, BF16 if is_prompt else F32)
    s.u_layers.append(u)
    if is_prompt:
        yret, s.new_state = _ret_prompt(tok, i, proj, p['g_ret'], s.new_state)
        mixed = _pool_prompt(tok, i, u, p['w_pool_map'], p['pool_scale'])
    else:
        yret, s.new_state = _ret_sample(tok, i, proj, p['g_ret'], s.ret_state, s.new_state)
        mixed = _pool_sample(tok, i, u, s.pool_state[i].transpose(1, 0, 2), p['w_pool_map'], p['pool_scale'])
    merged = _merge(tok, i, yret, mixed, proj, p['w_ret_out'], p['w_pool_out'])
    s.x = _wo(tok, i, merged, s.x, p['w_o'])


def _moe_layer(streams, i, p, g_final):
    rows_all = sum(s.tok.rows for s in streams)
    h_all, row0, choices, weights, starts = None, 0, [], [], []
    for s in streams:
        esel, wsel, h_all = _router(s.tok, i, s.x, p['g_ffn'], p['w_router'], p['b_router'],
                                    h_all=h_all, row0=row0, rows_all=rows_all)
        choices.append(esel[:, :2])
        weights.append(wsel)
        starts.append(row0)
        row0 += s.tok.rows
    plan = _route_plan(jnp.concatenate(choices, axis=0), TM_MOE)
    y_sorted = _moe_sparse(i, h_all, plan, p['w1_moe'], p['w3_moe'], p['w2_moe'], TM_MOE)
    for s, wsel, start in zip(streams, weights, starts):
        s.x = _combine(s.tok, i, y_sorted, plan[3], start, s.x, wsel, g_final)


def kernel(x_prompt, x_sample, c_prompt, c_sample, state_ret, state_pool, w_ada, b_ada, g_mix, w_in, g_ret,
           w_ret_out, w_pool_map, pool_scale, w_pool_out, w_o, g_ffn, w1_dense, w3_dense, w2_dense, w_router,
           b_router, w1_moe, w3_moe, w2_moe, g_final):
    row3 = lambda a: a.reshape(a.shape[0], 1, a.shape[1])
    p = dict(g_mix=row3(g_mix), w_in=w_in, g_ret=row3(g_ret), w_ret_out=w_ret_out, w_pool_map=w_pool_map,
             pool_scale=row3(pool_scale), w_pool_out=w_pool_out, w_o=w_o, g_ffn=row3(g_ffn),
             w1_dense=w1_dense, w3_dense=w3_dense, w2_dense=w2_dense, w_router=w_router, b_router=row3(b_router),
             w1_moe=w1_moe, w3_moe=w3_moe, w2_moe=w2_moe, g_final=g_final.reshape(1, D_MODEL))
    n_bp, seq_p, _ = x_prompt.shape
    n_bs, seq_s, _ = x_sample.shape

    mod_p, mod_s = _ada(c_prompt, jnp.repeat(c_sample, seq_s, axis=0), w_ada, b_ada)

    tok_p = _Tokens(n_bp, seq_p, 0, TM_PROMPT, per_row_mod=False, mod=mod_p)
    tok_s = _Tokens(n_bs, seq_s, PAST_LEN, TM_SAMPLE, per_row_mod=True, mod=mod_s)
    prompt = _Stream(tok_p, x_prompt.reshape(tok_p.rows, D_MODEL), None, None)
    sample = _Stream(tok_s, x_sample.reshape(tok_s.rows, D_MODEL), state_ret, state_pool)
    streams = [prompt, sample]

    for i in range(DEPTH):
        g_fin = p['g_final'] if i == DEPTH - 1 else None
        for s in streams:
            _mixer_layer(s, i, p)
        if i % 2 == 0:
            for s in streams:
                s.x = _ffn(s.tok, i, s.x, p['g_ffn'], p['w1_dense'], p['w3_dense'], p['w2_dense'], g_final=g_fin)
        else:
            _moe_layer(streams, i, p, g_fin)

    buf_prompt = jnp.stack([u.reshape(n_bp, seq_p, D_POOL)[:, seq_p - POOL_BUF:] for u in prompt.u_layers])
    buf_sample = jnp.stack([
        jnp.concatenate([state_pool[i], sample.u_layers[i].reshape(n_bs, seq_s, D_POOL)], axis=1)[:, -POOL_BUF:]
        for i in range(DEPTH)])
    return (prompt.x.reshape(x_prompt.shape), sample.x.reshape(x_sample.shape), prompt.new_state, buf_prompt,
            sample.new_state, buf_sample)
```

```python
import functools

import jax
import jax.numpy as jnp
from jax import lax
from jax.experimental import pallas as pl
from jax.experimental.pallas import tpu as pltpu

F32 = jnp.float32
BF16 = jnp.bfloat16

D_MODEL = 2048
DEPTH = 2
PAST_LEN = 16384
RET_HEADS = 8
RET_DK = 128
RET_DV = 256
RET_CHUNK = 128
ROPE_BASE = 10000.0
QK_W = RET_HEADS * RET_DK
V_W = RET_HEADS * RET_DV
POOL_WINDOWS = (2, 4, 8, 16)
D_POOL = 1024
POOL_GW = 256
POOL_BUF = 15
POOL_HALO = 16
N_IN = 2 * QK_W + 2 * V_W + D_POOL + 2 * D_MODEL
N_EXPERTS = 8
EPS = 1e-6
N_MOD = 6

O_Q, O_K, O_V, O_G, O_U = 0, QK_W, 2 * QK_W, 2 * QK_W + V_W, 2 * QK_W + 2 * V_W
O_M1 = O_U + D_POOL
O_M2 = O_M1 + D_MODEL

VMEM_LIMIT_BYTES = 56 * 1024 * 1024
SUBLANES = 8
TN = 512
TN_IN = 1024
TN_MERGE = 1024
TF = 256
TM_PROMPT = 1024
TM_SAMPLE = 512
TM_FFN = 1024
TM_OUT = 2048
ROW_CHUNK = 256
NORM_ROWS = 16
NORM_UNROLL = 16
TL_POOL = 512
TM_MOE = 1024
SUB_MOE = 256
TM_COMBINE = 256
RET_PROMPT_BATCH = 4


def _params(*sem):
    return pltpu.CompilerParams(dimension_semantics=sem, vmem_limit_bytes=VMEM_LIMIT_BYTES)


class _Tokens:
    def __init__(self, n_batch, seq, n_past, tm, per_row_mod, mod=None):
        self.n_batch, self.seq, self.n_past = n_batch, seq, n_past
        self.rows = n_batch * seq
        self.tm = min(tm, self.rows)
        self.per_row_mod = per_row_mod
        if not per_row_mod:
            self.tm = min(self.tm, seq)
            assert seq % self.tm == 0
        assert self.rows % self.tm == 0
        self.n_tiles = self.rows // self.tm
        if mod is not None and not per_row_mod and mod.ndim == 3 and mod.shape[1] != 1:
            mod = mod.reshape(DEPTH * n_batch, 1, N_MOD * D_MODEL)
        self.mod = mod

    def with_tm(self, tm):
        return _Tokens(self.n_batch, self.seq, self.n_past, tm, self.per_row_mod, self.mod)

    def mod_spec(self, layer, chunk, tn=D_MODEL, col_axis=False):
        per_chunk = D_MODEL // tn
        col = (lambda n: chunk * per_chunk + n) if col_axis else (lambda n: chunk * per_chunk)
        if self.per_row_mod:
            if col_axis:
                return pl.BlockSpec((1, self.tm, tn), lambda m, n, *_: (layer, m, col(n)))
            return pl.BlockSpec((1, self.tm, tn), lambda m, *_: (layer, m, col(0)))
        tiles_per_batch = self.seq // self.tm
        base = layer * self.n_batch
        if col_axis:
            return pl.BlockSpec((1, 1, tn), lambda m, n, *_: (base + m // tiles_per_batch, 0, col(n)))
        return pl.BlockSpec((1, 1, tn), lambda m, *_: (base + m // tiles_per_batch, 0, col(0)))

    def rope_tables(self):
        half = RET_DK // 2
        pos = jnp.arange(self.seq, dtype=F32) + float(self.n_past)
        inv = ROPE_BASE ** (-jnp.arange(half, dtype=F32) / half)
        ang = pos[:, None] * inv[None, :]
        cos, sin = jnp.cos(ang), jnp.sin(ang)
        cos_full = jnp.concatenate([cos, cos], axis=1)
        sin_signed = jnp.concatenate([-sin, sin], axis=1)
        if self.per_row_mod:
            cos_full = jnp.tile(cos_full, (self.n_batch, 1))
            sin_signed = jnp.tile(sin_signed, (self.n_batch, 1))
        return cos_full, sin_signed

    def rope_spec(self):
        n_blocks = (self.rows if self.per_row_mod else self.seq) // self.tm
        return pl.BlockSpec((self.tm, RET_DK), lambda m, *_: (m % n_blocks, 0))


def _retention_tables(chunk):
    log_gamma = jnp.log1p(-jnp.exp2(-5.0 - jnp.arange(RET_HEADS, dtype=F32)))
    idx = jnp.arange(chunk, dtype=F32)
    rel = idx[:, None] - idx[None, :]
    dmask = jnp.where(rel[None] >= 0, jnp.exp(log_gamma[:, None, None] * jnp.maximum(rel, 0.0)[None]), 0.0)
    q_dec = jnp.exp(log_gamma[:, None] * (idx + 1.0)[None, :])
    k_dec = jnp.exp(log_gamma[:, None] * (chunk - 1.0 - idx)[None, :])
    chunk_dec = jnp.exp(log_gamma * chunk)
    q_dec = jnp.broadcast_to(q_dec[:, :, None], (RET_HEADS, chunk, RET_DV))
    k_dec = jnp.broadcast_to(k_dec[:, :, None], (RET_HEADS, chunk, RET_DK))
    chunk_dec = jnp.broadcast_to(chunk_dec[:, None, None], (RET_HEADS, 1, RET_DV))
    return dmask, q_dec, k_dec, chunk_dec


def _norm_rows(x_ref, g_ref, sc_ref, sh_ref, h_ref):
    tm = x_ref.shape[0]
    step = min(NORM_ROWS, tm)
    trips = tm // step
    per_row = sc_ref.shape[1] != 1
    if not per_row:
        gain, shift = g_ref[0] * (1.0 + sc_ref[0]), sh_ref[0]

    def body(c, carry):
        rows = pl.ds(pl.multiple_of(c * step, step), step)
        x = x_ref[rows, :]
        y = x * lax.rsqrt(jnp.mean(x * x, axis=-1, keepdims=True) + EPS)
        if per_row:
            y = y * (g_ref[0] * (1.0 + sc_ref[0, rows, :])) + sh_ref[0, rows, :]
        else:
            y = y * gain + shift
        h_ref[rows, :] = y.astype(h_ref.dtype)
        return carry

    lax.fori_loop(0, trips, body, 0, unroll=NORM_UNROLL if trips % NORM_UNROLL == 0 else 1)


def _silu(x):
    return x * jax.nn.sigmoid(x)


def _pack_halves(y):
    half = y.shape[1] // 2
    hi = lax.bitcast_convert_type(y[:, :half].astype(jnp.bfloat16).astype(F32), jnp.uint32)
    lo = lax.bitcast_convert_type(y[:, half:].astype(jnp.bfloat16).astype(F32), jnp.uint32)
    return hi | (lo >> 16)


def _unpack_halves(u):
    return (lax.bitcast_convert_type(u & jnp.uint32(0xFFFF0000), F32),
            lax.bitcast_convert_type(u << 16, F32))


def _ada_kernel(cp_ref, cs_ref, w_ref, b_ref, op_ref, os_ref):
    w = w_ref[0].astype(BF16)
    for c_ref, o_ref in ((cp_ref, op_ref), (cs_ref, os_ref)):
        o_ref[0] = jnp.dot(_silu(c_ref[...]).astype(BF16), w, preferred_element_type=F32) + b_ref[0]


def _ada(c_p, c_s, w_ada, b_ada):
    tn = 1024
    width = N_MOD * D_MODEL
    row_spec = lambda c: pl.BlockSpec(c.shape, lambda l, n: (0, 0))
    out_spec = lambda c: pl.BlockSpec((1, c.shape[0], tn), lambda l, n: (l, 0, n))
    return pl.pallas_call(
        _ada_kernel,
        grid=(DEPTH, width // tn),
        in_specs=[row_spec(c_p), row_spec(c_s),
                  pl.BlockSpec((1, D_MODEL, tn), lambda l, n: (l, 0, n)),
                  pl.BlockSpec((1, 1, tn), lambda l, n: (l, 0, n))],
        out_specs=[out_spec(c_p), out_spec(c_s)],
        out_shape=[jax.ShapeDtypeStruct((DEPTH, c.shape[0], width), F32) for c in (c_p, c_s)],
        compiler_params=_params("arbitrary", "arbitrary"),
        name="ada",
    )(c_p, c_s, w_ada, b_ada.reshape(DEPTH, 1, width))


def _inproj_kernel(x_ref, sc_ref, sh_ref, g_ref, w_ref, cos_ref, sin_ref, o_ref, u_ref, h_scr):
    n = pl.program_id(1)
    tm = x_ref.shape[0]
    rc = min(ROW_CHUNK, tm)

    @pl.when(n == 0)
    def _():
        _norm_rows(x_ref, g_ref, sc_ref, sh_ref, h_scr)

    def project(epilogue):
        w = w_ref[0].astype(BF16)
        for r in range(0, tm, rc):
            rows = slice(r, r + rc)
            epilogue(jnp.dot(h_scr[rows, :], w, preferred_element_type=F32), rows)

    def store(acc, rows):
        o_ref[rows, :] = acc.astype(o_ref.dtype)

    def rope(scale):
        def epilogue(acc, rows):
            for j in range(TN_IN // RET_DK):
                cols = slice(j * RET_DK, (j + 1) * RET_DK)
                xh = acc[:, cols]
                r = xh * cos_ref[rows, :] + pltpu.roll(xh, RET_DK // 2, 1) * sin_ref[rows, :]
                if scale != 1.0:
                    r = r * scale
                o_ref[rows, cols] = r.astype(o_ref.dtype)
        project(epilogue)

    @pl.when(n < O_K // TN_IN)
    def _():
        rope(1.0)

    @pl.when(jnp.logical_and(n >= O_K // TN_IN, n < O_V // TN_IN))
    def _():
        rope(RET_DK ** -0.5)

    @pl.when(jnp.logical_and(n >= O_V // TN_IN, n < O_G // TN_IN))
    def _():
        project(store)

    @pl.when(jnp.logical_and(n >= O_G // TN_IN, n < O_U // TN_IN))
    def _():
        project(lambda acc, rows: store(_silu(acc), rows))

    @pl.when(jnp.logical_and(n >= O_U // TN_IN, n < O_M1 // TN_IN))
    def _():
        def epilogue(acc, rows):
            store(acc, rows)
            u_ref[rows, :] = acc
        project(epilogue)

    @pl.when(n >= O_M1 // TN_IN)
    def _():
        project(lambda acc, rows: store(jax.nn.sigmoid(acc), rows))


def _inproj(tok, layer, x, g_mix, w_in, rope_tabs, out_dtype):
    tm = tok.tm
    tn = TN_IN
    assert all(o % tn == 0 for o in (O_K, O_V, O_G, O_U, O_M1, N_IN))
    n_u = D_POOL // tn
    mod3 = tok.mod
    proj, u = pl.pallas_call(
        _inproj_kernel,
        grid=(tok.n_tiles, N_IN // tn),
        in_specs=[pl.BlockSpec((tm, D_MODEL), lambda m, n: (m, 0)),
                  tok.mod_spec(layer, 1), tok.mod_spec(layer, 0),
                  pl.BlockSpec((1, 1, D_MODEL), lambda m, n: (layer, 0, 0)),
                  pl.BlockSpec((1, D_MODEL, tn), lambda m, n: (layer, 0, n)),
                  tok.rope_spec(), tok.rope_spec()],
        out_specs=[pl.BlockSpec((tm, tn), lambda m, n: (m, n)),
                   pl.BlockSpec((tm, tn), lambda m, n: (m, jnp.clip(n - O_U // tn, 0, n_u - 1)),
                                pipeline_mode=pl.Buffered(1))],
        out_shape=[jax.ShapeDtypeStruct((tok.rows, N_IN), out_dtype),
                   jax.ShapeDtypeStruct((tok.rows, D_POOL), F32)],
        scratch_shapes=[pltpu.VMEM((tm, D_MODEL), BF16)],
        compiler_params=_params("arbitrary", "arbitrary"),
        name="inproj",
    )(x, mod3, mod3, g_mix, w_in, *rope_tabs)
    return proj, u


def _head_norm_gate(o, gret, gate):
    mu = jnp.mean(o, axis=-1, keepdims=True)
    d = o - mu
    var = jnp.mean(d * d, axis=-1, keepdims=True)
    return gate.astype(F32) * (d * lax.rsqrt(var + EPS) * gret)


def _ret_prompt_kernel(q_ref, k_ref, v_ref, g_ref, gret_ref, dmask_ref, qdec_ref, kdec_ref, cdec_ref,
                       *rest):
    y_ref, s_ref = rest[-2:]

    @pl.when(pl.program_id(1) == 0)
    def _():
        s_ref[...] = jnp.zeros_like(s_ref)

    for b in range(q_ref.shape[0]):
        for h in range(RET_HEADS):
            ks, vs = slice(h * RET_DK, (h + 1) * RET_DK), slice(h * RET_DV, (h + 1) * RET_DV)
            q, k, v = q_ref[b, :, ks], k_ref[b, :, ks], v_ref[b, :, vs]
            state = s_ref[0, b, h]
            scores = lax.dot_general(q, k, (((1,), (1,)), ((), ())), preferred_element_type=F32) * dmask_ref[h]
            inner = jnp.dot(scores.astype(BF16), v, preferred_element_type=F32)
            cross = jnp.dot(q, state.astype(BF16), preferred_element_type=F32) * qdec_ref[h]
            k_decayed = (k.astype(F32) * kdec_ref[h]).astype(BF16)
            s_ref[0, b, h] = cdec_ref[h] * state + lax.dot_general(
                k_decayed, v, (((0,), (0,)), ((), ())), preferred_element_type=F32)
            y_ref[b, :, vs] = _head_norm_gate(inner + cross, gret_ref[0, :, vs], g_ref[b, :, vs]).astype(y_ref.dtype)


def _table_specs(tables):
    return [pl.BlockSpec(t.shape, lambda *_: (0, 0, 0)) for t in tables]


def _ret_prompt(tok, layer, proj, g_ret, s_all):
    n_b, seq = tok.n_batch, tok.seq
    c = RET_CHUNK
    assert seq % c == 0
    nc = seq // c
    tables = _retention_tables(c)
    bb = RET_PROMPT_BATCH if n_b % RET_PROMPT_BATCH == 0 else 1
    proj3 = proj.reshape(n_b, seq, N_IN)
    in_specs = [pl.BlockSpec((bb, c, QK_W), lambda b, i: (b, i, O_Q // QK_W)),
                pl.BlockSpec((bb, c, QK_W), lambda b, i: (b, i, O_K // QK_W)),
                pl.BlockSpec((bb, c, V_W), lambda b, i: (b, i, O_V // V_W)),
                pl.BlockSpec((bb, c, V_W), lambda b, i: (b, i, O_G // V_W)),
                pl.BlockSpec((1, 1, V_W), lambda b, i: (layer, 0, 0))] + _table_specs(tables)
    args = [proj3, proj3, proj3, proj3, g_ret, *tables]
    aliases = {}
    if s_all is not None:
        aliases = {len(args): 1}
        in_specs.append(pl.BlockSpec(memory_space=pl.ANY))
        args.append(s_all)
    y, s_all = pl.pallas_call(
        _ret_prompt_kernel,
        grid=(n_b // bb, nc),
        in_specs=in_specs,
        out_specs=[pl.BlockSpec((bb, c, V_W), lambda b, i: (b, i, 0)),
                   pl.BlockSpec((1, bb, RET_HEADS, RET_DK, RET_DV), lambda b, i: (layer, b, 0, 0, 0))],
        out_shape=[jax.ShapeDtypeStruct((n_b, seq, V_W), BF16),
                   jax.ShapeDtypeStruct((DEPTH, n_b, RET_HEADS, RET_DK, RET_DV), F32)],
        input_output_aliases=aliases,
        compiler_params=_params("arbitrary", "arbitrary"),
        name="ret_prompt",
    )(*args)
    return y.reshape(tok.rows, V_W), s_all


def _ret_sample_kernel(q_ref, k_ref, v_ref, g_ref, kt_ref, gret_ref, s0_ref, dm_ref, qdec_ref, kdect_ref, cdec_ref,
                       *rest, bb, seq):
    y_ref, s_ref = rest[-2:]
    hs = RET_HEADS * seq
    reps = lambda a: jnp.concatenate([a] * RET_HEADS, axis=0)
    mask_q = (lax.broadcasted_iota(jnp.int32, (hs, QK_W), 0) // seq
              == lax.broadcasted_iota(jnp.int32, (hs, QK_W), 1) // RET_DK)
    mask_v = (lax.broadcasted_iota(jnp.int32, (hs, V_W), 0) // seq
              == lax.broadcasted_iota(jnp.int32, (hs, V_W), 1) // RET_DV)
    nt = (((1,), (1,)), ((), ()))
    for b in range(bb):
        rows_b = slice(b * seq, (b + 1) * seq)
        q, k, v = q_ref[rows_b, :], k_ref[rows_b, :], v_ref[rows_b, :]
        q_blk = jnp.where(mask_q, reps(q), 0.0).astype(BF16)
        scores = lax.dot_general(q_blk, reps(k).astype(BF16), nt, preferred_element_type=F32) * dm_ref[...]
        v_stack = jnp.concatenate([v[:, h * RET_DV:(h + 1) * RET_DV] for h in range(RET_HEADS)], axis=0)
        inner = jnp.dot(scores.astype(BF16), v_stack.astype(BF16), preferred_element_type=F32)
        state = s0_ref[0, b]
        cross = jnp.dot(q_blk, state.reshape(QK_W, RET_DV).astype(BF16), preferred_element_type=F32)
        o = inner + cross * qdec_ref[...]
        mu = jnp.mean(o, axis=-1, keepdims=True)
        d = o - mu
        normed = d * lax.rsqrt(jnp.mean(d * d, axis=-1, keepdims=True) + EPS)
        k_dec_t = (kt_ref[b] * kdect_ref[...]).astype(BF16)
        v_blk = jnp.where(mask_v, reps(v), 0.0).astype(BF16)
        upd = jnp.dot(k_dec_t, v_blk, preferred_element_type=F32)
        for h in range(RET_HEADS):
            vs = slice(h * RET_DV, (h + 1) * RET_DV)
            y_ref[rows_b, vs] = g_ref[rows_b, vs] * (normed[h * seq:(h + 1) * seq] * gret_ref[0, :, vs])
            s_ref[0, b, h] = cdec_ref[h] * state[h] + upd[:, vs]


def _ret_sample(tok, layer, proj, g_ret, state_ret, s_all):
    n_b, seq = tok.n_batch, tok.seq
    bb = 4
    assert n_b % bb == 0
    hs = RET_HEADS * seq
    dmask, q_dec, k_dec, chunk_dec = _retention_tables(seq)
    dm_blk = (dmask[:, :, None, :] * jnp.eye(RET_HEADS, dtype=F32)[:, None, :, None]).reshape(hs, hs)
    tables = [dm_blk, q_dec.reshape(hs, RET_DV), k_dec[:, :, 0].reshape(1, hs), chunk_dec]
    k_t = proj[:, O_K:O_K + QK_W].reshape(n_b, seq, RET_HEADS, RET_DK).transpose(0, 3, 2, 1).reshape(n_b, RET_DK, hs)
    state_spec = pl.BlockSpec((1, bb, RET_HEADS, RET_DK, RET_DV), lambda i: (layer, i, 0, 0, 0))
    in_specs = [pl.BlockSpec((bb * seq, QK_W), lambda i: (i, O_Q // QK_W)),
                pl.BlockSpec((bb * seq, QK_W), lambda i: (i, O_K // QK_W)),
                pl.BlockSpec((bb * seq, V_W), lambda i: (i, O_V // V_W)),
                pl.BlockSpec((bb * seq, V_W), lambda i: (i, O_G // V_W)),
                pl.BlockSpec((bb, RET_DK, hs), lambda i: (i, 0, 0)),
                pl.BlockSpec((1, 1, V_W), lambda i: (layer, 0, 0)),
                state_spec] + [pl.BlockSpec(t.shape, lambda i, nd=t.ndim: (0,) * nd) for t in tables]
    args = [proj, proj, proj, proj, k_t, g_ret, state_ret, *tables]
    aliases = {}
    if s_all is not None:
        aliases = {len(args): 1}
        in_specs.append(pl.BlockSpec(memory_space=pl.ANY))
        args.append(s_all)
    y, s_all = pl.pallas_call(
        functools.partial(_ret_sample_kernel, bb=bb, seq=seq),
        grid=(n_b // bb,),
        in_specs=in_specs,
        out_specs=[pl.BlockSpec((bb * seq, V_W), lambda i: (i, 0)), state_spec],
        out_shape=[jax.ShapeDtypeStruct((tok.rows, V_W), F32),
                   jax.ShapeDtypeStruct(state_ret.shape, F32)],
        input_output_aliases=aliases,
        compiler_params=_params("arbitrary"),
        name="ret_sample",
    )(*args)
    return y, s_all


def _pool_map(pooled, wmap_ref, scale_ref, gi):
    cols = slice(gi * POOL_GW, (gi + 1) * POOL_GW)
    mixed = jnp.dot(pooled.astype(BF16), wmap_ref[0, gi].astype(BF16), preferred_element_type=F32)
    return mixed * scale_ref[0, :, cols]


def _pool_prompt_kernel(u_ref, halo_ref, wmap_ref, scale_ref, o_ref, uf_scr, a_scr, b_scr, *, tl):
    pad = SUBLANES
    top = pad + POOL_HALO
    first = pl.program_id(1) == 0
    uf_scr[0:pad, :] = jnp.zeros((pad, D_POOL), F32)
    a_scr[0:pad, :] = jnp.zeros((pad, POOL_GW), F32)
    b_scr[0:pad, :] = jnp.zeros((pad, POOL_GW), F32)
    uf_scr[pad:top, :] = jnp.where(first, 0.0, halo_ref[...])
    uf_scr[top:, :] = u_ref[...]
    n_rows = POOL_HALO + tl
    row = lax.broadcasted_iota(jnp.int32, (tl, POOL_GW), 0) + pl.program_id(1) * tl
    for gi, w in enumerate(POOL_WINDOWS):
        cols = slice(gi * POOL_GW, (gi + 1) * POOL_GW)
        src = lambda lo, cols=cols: uf_scr[lo:lo + n_rows, cols]
        dst_scr, other = a_scr, b_scr
        shift = 1
        while shift < w:
            dst_scr[pad:, :] = src(pad) + src(pad - shift)
            src = lambda lo, s=dst_scr: s[lo:lo + n_rows, :]
            dst_scr, other = other, dst_scr
            shift *= 2
        acc = src(pad)[POOL_HALO:, :]
        u = uf_scr[top:, cols]
        cnt = jnp.minimum(w, row + 1).astype(F32)
        o_ref[:, cols] = _pool_map(acc / cnt - u, wmap_ref, scale_ref, gi).astype(o_ref.dtype)


def _pool_prompt(tok, layer, u, w_pool_map, pool_scale):
    n_b, seq = tok.n_batch, tok.seq
    tl = min(TL_POOL, seq)
    assert seq % tl == 0 and tl % POOL_HALO == 0
    nl = seq // tl
    per = tl // POOL_HALO
    return pl.pallas_call(
        functools.partial(_pool_prompt_kernel, tl=tl),
        grid=(n_b, nl),
        in_specs=[pl.BlockSpec((tl, D_POOL), lambda b, l: (b * nl + l, 0)),
                  pl.BlockSpec((POOL_HALO, D_POOL), lambda b, l: (jnp.maximum((b * nl + l) * per - 1, 0), 0)),
                  pl.BlockSpec((1,) + w_pool_map.shape[1:], lambda b, l: (layer, 0, 0, 0)),
                  pl.BlockSpec((1, 1, D_POOL), lambda b, l: (layer, 0, 0))],
        out_specs=pl.BlockSpec((tl, D_POOL), lambda b, l: (b * nl + l, 0)),
        out_shape=jax.ShapeDtypeStruct((tok.rows, D_POOL), BF16),
        scratch_shapes=[pltpu.VMEM((SUBLANES + POOL_HALO + tl, D_POOL), F32),
                        pltpu.VMEM((SUBLANES + POOL_HALO + tl, POOL_GW), F32),
                        pltpu.VMEM((SUBLANES + POOL_HALO + tl, POOL_GW), F32)],
        compiler_params=_params("arbitrary", "arbitrary"),
        name="pool_prompt",
    )(u, u, w_pool_map, pool_scale)


def _pool_sample_kernel(buf_ref, u_ref, wmap_ref, scale_ref, o_ref, *, seq, n_b):
    def row(j, cols):
        return buf_ref[j, :, cols] if j < POOL_BUF else u_ref[j - POOL_BUF, :, cols]

    for gi, w in enumerate(POOL_WINDOWS):
        cols = slice(gi * POOL_GW, (gi + 1) * POOL_GW)
        for t in range(seq):
            acc = row(POOL_BUF + t, cols)
            for j in range(1, w):
                acc = acc + row(POOL_BUF + t - j, cols)
            pooled = acc / float(w) - row(POOL_BUF + t, cols)
            o_ref[t, :, cols] = _pool_map(pooled, wmap_ref, scale_ref, gi)


def _pool_sample(tok, layer, u, buf_tm, w_pool_map, pool_scale):
    n_b, seq = tok.n_batch, tok.seq
    assert tok.n_past >= max(POOL_WINDOWS)
    u_tm = u.reshape(n_b, seq, D_POOL).transpose(1, 0, 2)
    mixed_tm = pl.pallas_call(
        functools.partial(_pool_sample_kernel, seq=seq, n_b=n_b),
        grid=(1,),
        in_specs=[pl.BlockSpec(buf_tm.shape, lambda i: (0, 0, 0)),
                  pl.BlockSpec(u_tm.shape, lambda i: (0, 0, 0)),
                  pl.BlockSpec((1,) + w_pool_map.shape[1:], lambda i: (layer, 0, 0, 0)),
                  pl.BlockSpec((1, 1, D_POOL), lambda i: (layer, 0, 0))],
        out_specs=pl.BlockSpec(u_tm.shape, lambda i: (0, 0, 0)),
        out_shape=jax.ShapeDtypeStruct(u_tm.shape, F32),
        compiler_params=_params("arbitrary"),
        name="pool_sample",
    )(buf_tm, u_tm, w_pool_map, pool_scale)
    return mixed_tm.transpose(1, 0, 2).reshape(tok.rows, D_POOL)


def _merge_kernel(yret_ref, mixed_ref, wr_ref, wp_ref, g1_ref, g2_ref, o_ref, wr_scr, wp_scr):
    @pl.when(pl.program_id(1) == 0)
    def _():
        wr_scr[...] = wr_ref[0].astype(BF16)
        wp_scr[...] = wp_ref[0].astype(BF16)

    ret = jnp.dot(yret_ref[...].astype(BF16), wr_scr[...], preferred_element_type=F32)
    pool = jnp.dot(mixed_ref[...].astype(BF16), wp_scr[...], preferred_element_type=F32)
    o_ref[...] = (g1_ref[...].astype(F32) * ret + g2_ref[...].astype(F32) * pool).astype(o_ref.dtype)


def _merge(tok, layer, yret, mixed, proj, w_ret_out, w_pool_out):
    tm, tn = tok.tm, TN_MERGE
    once = pl.Buffered(1)
    return pl.pallas_call(
        _merge_kernel,
        grid=(D_MODEL // tn, tok.n_tiles),
        in_specs=[pl.BlockSpec((tm, V_W), lambda n, m: (m, 0)),
                  pl.BlockSpec((tm, D_POOL), lambda n, m: (m, 0)),
                  pl.BlockSpec((1, V_W, tn), lambda n, m: (layer, 0, n), pipeline_mode=once),
                  pl.BlockSpec((1, D_POOL, tn), lambda n, m: (layer, 0, n), pipeline_mode=once),
                  pl.BlockSpec((tm, tn), lambda n, m: (m, O_M1 // tn + n)),
                  pl.BlockSpec((tm, tn), lambda n, m: (m, O_M2 // tn + n))],
        out_specs=pl.BlockSpec((tm, tn), lambda n, m: (m, n)),
        out_shape=jax.ShapeDtypeStruct((tok.rows, D_MODEL), BF16),
        scratch_shapes=[pltpu.VMEM((V_W, tn), BF16), pltpu.VMEM((D_POOL, tn), BF16)],
        compiler_params=_params("arbitrary", "arbitrary"),
        name="merge",
    )(yret, mixed, w_ret_out, w_pool_out, proj, proj)


def _wo_kernel(merged_ref, w_ref, x_ref, gt_ref, o_ref):
    mix = jnp.dot(merged_ref[...], w_ref[0].astype(BF16), preferred_element_type=F32)
    o_ref[...] = x_ref[...] + (1.0 + gt_ref[0]) * mix


def _wo(tok, layer, merged, x, w_o):
    tok = tok.with_tm(TM_OUT)
    tm = tok.tm
    mod3 = tok.mod
    return pl.pallas_call(
        _wo_kernel,
        grid=(tok.n_tiles, D_MODEL // TN),
        in_specs=[pl.BlockSpec((tm, D_MODEL), lambda m, n: (m, 0)),
                  pl.BlockSpec((1, D_MODEL, TN), lambda m, n: (layer, 0, n)),
                  pl.BlockSpec((tm, TN), lambda m, n: (m, n)),
                  tok.mod_spec(layer, 2, tn=TN, col_axis=True)],
        out_specs=pl.BlockSpec((tm, TN), lambda m, n: (m, n)),
        out_shape=jax.ShapeDtypeStruct((tok.rows, D_MODEL), F32),
        compiler_params=_params("arbitrary", "arbitrary"),
        name="wo",
    )(merged, w_o, x, mod3)


def _mat(w_ref):
    return w_ref[(0,) * (len(w_ref.shape) - 2)]


def _final_norm(y, gfin_ref):
    return y * lax.rsqrt(jnp.mean(y * y, axis=-1, keepdims=True) + EPS) * gfin_ref[...]


def _ffn_kernel(x_ref, sc_ref, sh_ref, gt_ref, g_ref, w1_ref, w3_ref, w2_ref, *rest, n_f, with_final):
    gfin_ref = rest[0] if with_final else None
    o_ref, h_scr = rest[-2:]
    f = pl.program_id(1)

    @pl.when(f == 0)
    def _():
        _norm_rows(x_ref, g_ref, sc_ref, sh_ref, h_scr)
        o_ref[...] = jnp.zeros_like(o_ref)

    h = h_scr[...]
    a = jnp.dot(h, _mat(w1_ref).astype(BF16), preferred_element_type=F32)
    b = jnp.dot(h, _mat(w3_ref).astype(BF16), preferred_element_type=F32)
    o_ref[...] += jnp.dot((_silu(a) * b).astype(BF16), _mat(w2_ref).astype(BF16), preferred_element_type=F32)

    @pl.when(f == n_f - 1)
    def _():
        y = x_ref[...] + (1.0 + gt_ref[0]) * o_ref[...]
        o_ref[...] = _final_norm(y, gfin_ref) if with_final else y


def _ffn(tok, layer, x, g_ffn, w1, w3, w2, g_final=None):
    j = layer // 2
    d_ff = w1.shape[-1]
    tok_f = tok.with_tm(TM_FFN)
    tm, n_tiles = tok_f.tm, tok_f.n_tiles
    mod3 = tok_f.mod
    assert d_ff % TF == 0
    n_f = d_ff // TF
    in_specs = [pl.BlockSpec((tm, D_MODEL), lambda m, f: (m, 0)),
                tok_f.mod_spec(layer, 4), tok_f.mod_spec(layer, 3), tok_f.mod_spec(layer, 5),
                pl.BlockSpec((1, 1, D_MODEL), lambda m, f: (layer, 0, 0)),
                pl.BlockSpec((1, D_MODEL, TF), lambda m, f: (j, 0, f)),
                pl.BlockSpec((1, D_MODEL, TF), lambda m, f: (j, 0, f)),
                pl.BlockSpec((1, TF, D_MODEL), lambda m, f: (j, f, 0))]
    args = [x, mod3, mod3, mod3, g_ffn, w1, w3, w2]
    if g_final is not None:
        in_specs.append(pl.BlockSpec((1, D_MODEL), lambda m, f: (0, 0)))
        args.append(g_final)
    return pl.pallas_call(
        functools.partial(_ffn_kernel, n_f=n_f, with_final=g_final is not None),
        grid=(n_tiles, n_f),
        in_specs=in_specs,
        out_specs=pl.BlockSpec((tm, D_MODEL), lambda m, f: (m, 0)),
        out_shape=jax.ShapeDtypeStruct((tok.rows, D_MODEL), F32),
        scratch_shapes=[pltpu.VMEM((tm, D_MODEL), BF16)],
        compiler_params=_params("arbitrary", "arbitrary"),
        name="ffn",
    )(*args)


def _router_kernel(x_ref, sc_ref, sh_ref, g_ref, w_ref, b_ref, *rest):
    esel_ref, wsel_ref, h_ref, h_scr = rest[-4:]
    _norm_rows(x_ref, g_ref, sc_ref, sh_ref, h_scr)
    h = h_scr[...]
    logits = jnp.dot(h, w_ref[0].astype(BF16), preferred_element_type=F32) + b_ref[0]
    idx = lax.broadcasted_iota(jnp.int32, logits.shape, 1)
    top1 = jnp.max(logits, axis=-1, keepdims=True)
    i1 = jnp.min(jnp.where(logits == top1, idx, N_EXPERTS), axis=-1, keepdims=True)
    others = jnp.where(idx == i1, -jnp.inf, logits)
    top2 = jnp.max(others, axis=-1, keepdims=True)
    i2 = jnp.min(jnp.where(others == top2, idx, N_EXPERTS), axis=-1, keepdims=True)
    e2 = jnp.exp(top2 - top1)
    denom = 1.0 + e2
    w1, w2 = 1.0 / denom, e2 / denom
    esel_ref[...] = jnp.where(idx == 0, i1, i2)
    wsel_ref[...] = jnp.where(idx == 0, w1, w2)
    h_ref[...] = _pack_halves(h.astype(F32))


def _router(tok, layer, x, g_ffn, w_router, b_router, h_all=None, row0=0, rows_all=None):
    j = layer // 2
    tm = tok.tm
    rows_all = tok.rows if rows_all is None else rows_all
    assert row0 % tm == 0
    small = pl.BlockSpec((tm, N_EXPERTS), lambda m: (m, 0))
    mod3 = tok.mod
    in_specs = [pl.BlockSpec((tm, D_MODEL), lambda m: (m, 0)),
                tok.mod_spec(layer, 4), tok.mod_spec(layer, 3),
                pl.BlockSpec((1, 1, D_MODEL), lambda m: (layer, 0, 0)),
                pl.BlockSpec((1, D_MODEL, N_EXPERTS), lambda m: (j, 0, 0)),
                pl.BlockSpec((1, 1, N_EXPERTS), lambda m: (j, 0, 0))]
    args = [x, mod3, mod3, g_ffn, w_router, b_router]
    aliases = {}
    if h_all is not None:
        aliases = {len(args): 2}
        in_specs.append(pl.BlockSpec(memory_space=pl.ANY))
        args.append(h_all)
    return pl.pallas_call(
        _router_kernel,
        grid=(tok.n_tiles,),
        in_specs=in_specs,
        out_specs=[small, small, pl.BlockSpec((tm, D_MODEL // 2), lambda m: (row0 // tm + m, 0))],
        out_shape=[jax.ShapeDtypeStruct((tok.rows, N_EXPERTS), jnp.int32),
                   jax.ShapeDtypeStruct((tok.rows, N_EXPERTS), F32),
                   jax.ShapeDtypeStruct((rows_all, D_MODEL // 2), jnp.uint32)],
        input_output_aliases=aliases,
        scratch_shapes=[pltpu.VMEM((tm, D_MODEL), BF16)],
        compiler_params=_params("arbitrary"),
        name="router",
    )(*args)


def _route_plan(choice, tm):
    rows = choice.shape[0]
    n_pairs = 2 * rows
    n_tiles = n_pairs // tm + N_EXPERTS
    e_flat = choice.T.reshape(n_pairs)
    onehot = (e_flat[:, None] == jnp.arange(N_EXPERTS, dtype=jnp.int32)[None, :]).astype(jnp.int32)
    csum = jnp.cumsum(onehot, axis=0)
    rank = jnp.sum(csum * onehot, axis=1) - 1
    count = csum[-1]
    tiles_per = (count + tm - 1) // tm
    rows_per = jnp.maximum((((count + jnp.maximum(tiles_per, 1) - 1) // jnp.maximum(tiles_per, 1)) + 7) // 8 * 8, 8)
    tile_end = jnp.cumsum(tiles_per)
    tile_start = tile_end - tiles_per
    pos = (tile_start[e_flat] + rank // rows_per[e_flat]) * tm + rank % rows_per[e_flat]
    token = jnp.arange(n_pairs, dtype=jnp.int32) % rows
    src_row = jnp.zeros((n_tiles * tm,), jnp.int32).at[pos].set(token, unique_indices=True, mode='promise_in_bounds')
    tile_id = jnp.arange(n_tiles, dtype=jnp.int32)
    n_used = tile_end[-1]
    tile_expert = jnp.minimum(jnp.sum(tile_id[:, None] >= tile_end[None, :], axis=1), N_EXPERTS - 1).astype(jnp.int32)
    tile_rows = jnp.clip(count[tile_expert] - (tile_id - tile_start[tile_expert]) * rows_per[tile_expert],
                         0, rows_per[tile_expert])
    tile_rows = jnp.where(tile_id < n_used, tile_rows, 0).astype(jnp.int32)
    last_expert = tile_expert[jnp.maximum(n_used - 1, 0)]
    tile_expert = jnp.where(tile_rows > 0, tile_expert, last_expert)
    return tile_expert, tile_rows, src_row, pos.astype(jnp.int32)


def _moe_kernel(te_ref, nr_ref, src_ref, h_hbm, w1_ref, w3_ref, w2_ref, o_ref, rows_scr, h_scr, acc_scr, sem,
                *, tm, sub, n_f):
    del te_ref
    m, f = pl.program_id(0), pl.program_id(1)
    half = D_MODEL // 2
    group = rows_scr.shape[1]

    def n_groups(tile):
        return (nr_ref[tile] + group - 1) // group

    def start_rows(tile):
        def body(i, c):
            for j in range(group):
                row = src_ref[tile * tm + i * group + j]
                pltpu.make_async_copy(h_hbm.at[pl.ds(row, 1)], rows_scr.at[i, pl.ds(j, 1)], sem).start()
            return c
        lax.fori_loop(0, n_groups(tile), body, 0)

    def wait_rows(tile):
        def body(i, c):
            pltpu.make_async_copy(h_hbm.at[pl.ds(0, group)], rows_scr.at[i], sem).wait()
            return c
        lax.fori_loop(0, n_groups(tile), body, 0)

    @pl.when(f == 0)
    def _():
        acc_scr[...] = jnp.zeros_like(acc_scr)

        @pl.when(m == 0)
        def _():
            rows_scr[...] = jnp.zeros_like(rows_scr)
            start_rows(0)

        wait_rows(m)
        first, second = _unpack_halves(rows_scr[...].reshape(tm, half))
        h_scr[:, :half] = first.astype(BF16)
        h_scr[:, half:] = second.astype(BF16)

        @pl.when(m + 1 < pl.num_programs(0))
        def _():
            start_rows(m + 1)

    n_sub = (nr_ref[m] + sub - 1) // sub
    for n in range(1, tm // sub + 1):
        @pl.when(n_sub == n)
        def _(n=n):
            h = h_scr[0:n * sub, :]
            a = jnp.dot(h, _mat(w1_ref).astype(BF16), preferred_element_type=F32)
            b = jnp.dot(h, _mat(w3_ref).astype(BF16), preferred_element_type=F32)
            acc_scr[0:n * sub, :] += jnp.dot((_silu(a) * b).astype(BF16), _mat(w2_ref).astype(BF16),
                                             preferred_element_type=F32)

    @pl.when(f == n_f - 1)
    def _():
        o_ref[...] = _pack_halves(acc_scr[...])


def _moe_sparse(layer, h, plan, w1, w3, w2, tm):
    j = layer // 2
    tile_expert, tile_rows, src_row, _ = plan
    n_tiles = tile_expert.shape[0]
    d_ff = w1.shape[3]
    assert d_ff % TF == 0 and tm % SUB_MOE == 0
    n_f = d_ff // TF

    def f_sel(m, f, nr):
        return jnp.where(nr[m] > 0, f, n_f - 1)

    return pl.pallas_call(
        functools.partial(_moe_kernel, tm=tm, sub=SUB_MOE, n_f=n_f),
        grid_spec=pltpu.PrefetchScalarGridSpec(
            num_scalar_prefetch=3,
            grid=(n_tiles, n_f),
            in_specs=[pl.BlockSpec(memory_space=pl.ANY),
                      pl.BlockSpec((1, 1, D_MODEL, TF), lambda m, f, te, nr, sr: (j, te[m], 0, f_sel(m, f, nr))),
                      pl.BlockSpec((1, 1, D_MODEL, TF), lambda m, f, te, nr, sr: (j, te[m], 0, f_sel(m, f, nr))),
                      pl.BlockSpec((1, 1, TF, D_MODEL), lambda m, f, te, nr, sr: (j, te[m], f_sel(m, f, nr), 0))],
            out_specs=pl.BlockSpec((tm, D_MODEL // 2), lambda m, f, te, nr, sr: (m, 0)),
            scratch_shapes=[pltpu.VMEM((tm // SUBLANES, SUBLANES, D_MODEL // 2), jnp.uint32),
                            pltpu.VMEM((tm, D_MODEL), BF16),
                            pltpu.VMEM((tm, D_MODEL), F32), pltpu.SemaphoreType.DMA(())]),
        out_shape=jax.ShapeDtypeStruct((n_tiles * tm, D_MODEL // 2), jnp.uint32),
        compiler_params=_params("arbitrary", "arbitrary"),
        name="moe",
    )(tile_expert, tile_rows, src_row, h, w1, w3, w2)


def _combine_kernel(pos_ref, y_hbm, x_ref, gt_ref, w_ref, *rest, tm, row0, rows_all, with_final):
    gfin_ref = rest[0] if with_final else None
    o_ref, y_scr, sems = rest[-3:]
    i, n = pl.program_id(0), pl.num_programs(0)

    half = D_MODEL // 2
    group = y_scr.shape[3]

    def start_rows(tile, slot):
        def body(g, c):
            for k in range(2):
                for j in range(group):
                    slot_of_pair = pos_ref[k * rows_all + row0 + tile * tm + g * group + j]
                    pltpu.make_async_copy(y_hbm.at[pl.ds(slot_of_pair, 1)], y_scr.at[slot, k, g, pl.ds(j, 1)],
                                          sems.at[slot]).start()
            return c
        lax.fori_loop(0, tm // group, body, 0)

    @pl.when(i == 0)
    def _():
        start_rows(0, 0)

    @pl.when(i + 1 < n)
    def _():
        start_rows(i + 1, (i + 1) % 2)

    slot = i % 2
    def wait_group(g, c):
        for k in range(2):
            pltpu.make_async_copy(y_hbm.at[pl.ds(0, group)], y_scr.at[slot, k, g], sems.at[slot]).wait()
        return c

    lax.fori_loop(0, tm // group, wait_group, 0)
    w = w_ref[...]
    first0, second0 = _unpack_halves(y_scr[slot, 0].reshape(tm, half))
    first1, second1 = _unpack_halves(y_scr[slot, 1].reshape(tm, half))
    f = jnp.concatenate([w[:, 0:1] * first0 + w[:, 1:2] * first1,
                         w[:, 0:1] * second0 + w[:, 1:2] * second1], axis=1)
    y = x_ref[...] + (1.0 + gt_ref[0]) * f
    o_ref[...] = _final_norm(y, gfin_ref) if with_final else y


def _combine(tok, layer, y_sorted, pos, row0, x, wsel, g_final):
    tok_c = tok.with_tm(TM_COMBINE)
    tm = tok_c.tm
    mod3 = tok_c.mod
    in_specs = [pl.BlockSpec(memory_space=pl.ANY),
                pl.BlockSpec((tm, D_MODEL), lambda m, pos: (m, 0)),
                tok_c.mod_spec(layer, 5),
                pl.BlockSpec((tm, N_EXPERTS), lambda m, pos: (m, 0))]
    args = [pos, y_sorted, x, mod3, wsel]
    if g_final is not None:
        in_specs.append(pl.BlockSpec((1, D_MODEL), lambda m, pos: (0, 0)))
        args.append(g_final)
    return pl.pallas_call(
        functools.partial(_combine_kernel, tm=tm, row0=row0, rows_all=pos.shape[0] // 2,
                          with_final=g_final is not None),
        grid_spec=pltpu.PrefetchScalarGridSpec(
            num_scalar_prefetch=1,
            grid=(tok_c.n_tiles,),
            in_specs=in_specs,
            out_specs=pl.BlockSpec((tm, D_MODEL), lambda m, pos: (m, 0)),
            scratch_shapes=[pltpu.VMEM((2, 2, tm // SUBLANES, SUBLANES, D_MODEL // 2), jnp.uint32),
                            pltpu.SemaphoreType.DMA((2,))]),
        out_shape=jax.ShapeDtypeStruct((tok.rows, D_MODEL), F32),
        compiler_params=_params("arbitrary"),
        name="combine",
    )(*args)


class _Stream:
    def __init__(self, tok, x, ret_state, pool_state):
        self.tok, self.x, self.ret_state, self.pool_state = tok, x, ret_state, pool_state
        self.rope_tabs = tok.rope_tables()
        self.new_state = None
        self.u_layers = []


def _mixer_layer(s, i, p):
    tok = s.tok
    is_prompt = s.ret_state is None
    proj, u = _inproj(tok, i, s.x, p['g_mix'], p['w_in'], s.rope_tabs, BF16 if is_prompt else F32)
    s.u_layers.append(u)
    if is_prompt:
        yret, s.new_state = _ret_prompt(tok, i, proj, p['g_ret'], s.new_state)
        mixed = _pool_prompt(tok, i, u, p['w_pool_map'], p['pool_scale'])
    else:
        yret, s.new_state = _ret_sample(tok, i, proj, p['g_ret'], s.ret_state, s.new_state)
        mixed = _pool_sample(tok, i, u, s.pool_state[i].transpose(1, 0, 2), p['w_pool_map'], p['pool_scale'])
    merged = _merge(tok, i, yret, mixed, proj, p['w_ret_out'], p['w_pool_out'])
    s.x = _wo(tok, i, merged, s.x, p['w_o'])


def _moe_layer(streams, i, p, g_final):
    rows_all = sum(s.tok.rows for s in streams)
    h_all, row0, choices, weights, starts = None, 0, [], [], []
    for s in streams:
        esel, wsel, h_all = _router(s.tok, i, s.x, p['g_ffn'], p['w_router'], p['b_router'],
                                    h_all=h_all, row0=row0, rows_all=rows_all)
        choices.append(esel[:, :2])
        weights.append(wsel)
        starts.append(row0)
        row0 += s.tok.rows
    plan = _route_plan(jnp.concatenate(choices, axis=0), TM_MOE)
    y_sorted = _moe_sparse(i, h_all, plan, p['w1_moe'], p['w3_moe'], p['w2_moe'], TM_MOE)
    for s, wsel, start in zip(streams, weights, starts):
        s.x = _combine(s.tok, i, y_sorted, plan[3], start, s.x, wsel, g_final)


def kernel(x_prompt, x_sample, c_prompt, c_sample, state_ret, state_pool, w_ada, b_ada, g_mix, w_in, g_ret,
           w_ret_out, w_pool_map, pool_scale, w_pool_out, w_o, g_ffn, w1_dense, w3_dense, w2_dense, w_router,
           b_router, w1_moe, w3_moe, w2_moe, g_final):
    row3 = lambda a: a.reshape(a.shape[0], 1, a.shape[1])
    p = dict(g_mix=row3(g_mix), w_in=w_in, g_ret=row3(g_ret), w_ret_out=w_ret_out, w_pool_map=w_pool_map,
             pool_scale=row3(pool_scale), w_pool_out=w_pool_out, w_o=w_o, g_ffn=row3(g_ffn),
             w1_dense=w1_dense, w3_dense=w3_dense, w2_dense=w2_dense, w_router=w_router, b_router=row3(b_router),
             w1_moe=w1_moe, w3_moe=w3_moe, w2_moe=w2_moe, g_final=g_final.reshape(1, D_MODEL))
    n_bp, seq_p, _ = x_prompt.shape
    n_bs, seq_s, _ = x_sample.shape

    mod_p, mod_s = _ada(c_prompt, jnp.repeat(c_sample, seq_s, axis=0), w_ada, b_ada)

    tok_p = _Tokens(n_bp, seq_p, 0, TM_PROMPT, per_row_mod=False, mod=mod_p)
    tok_s = _Tokens(n_bs, seq_s, PAST_LEN, TM_SAMPLE, per_row_mod=True, mod=mod_s)
    prompt = _Stream(tok_p, x_prompt.reshape(tok_p.rows, D_MODEL), None, None)
    sample = _Stream(tok_s, x_sample.reshape(tok_s.rows, D_MODEL), state_ret, state_pool)
    streams = [prompt, sample]

    for i in range(DEPTH):
        g_fin = p['g_final'] if i == DEPTH - 1 else None
        for s in streams:
            _mixer_layer(s, i, p)
        if i % 2 == 0:
            for s in streams:
                s.x = _ffn(s.tok, i, s.x, p['g_ffn'], p['w1_dense'], p['w3_dense'], p['w2_dense'], g_final=g_fin)
        else:
            _moe_layer(streams, i, p, g_fin)

    buf_prompt = jnp.stack([u.reshape(n_bp, seq_p, D_POOL)[:, seq_p - POOL_BUF:] for u in prompt.u_layers])
    buf_sample = jnp.stack([
        jnp.concatenate([state_pool[i], sample.u_layers[i].reshape(n_bs, seq_s, D_POOL)], axis=1)[:, -POOL_BUF:]
        for i in range(DEPTH)])
    return (prompt.x.reshape(x_prompt.shape), sample.x.reshape(x_sample.shape), prompt.new_state, buf_prompt,
            sample.new_state, buf_sample)
```

```python
import functools

import jax
import jax.numpy as jnp
from jax import lax
from jax.experimental import pallas as pl
from jax.experimental.pallas import tpu as pltpu

F32 = jnp.float32
BF16 = jnp.bfloat16

D_MODEL = 2048
DEPTH = 2
PAST_LEN = 16384
RET_HEADS = 8
RET_DK = 128
RET_DV = 256
RET_CHUNK = 128
ROPE_BASE = 10000.0
QK_W = RET_HEADS * RET_DK
V_W = RET_HEADS * RET_DV
POOL_WINDOWS = (2, 4, 8, 16)
D_POOL = 1024
POOL_GW = 256
POOL_BUF = 15
POOL_HALO = 16
N_IN = 2 * QK_W + 2 * V_W + D_POOL + 2 * D_MODEL
N_EXPERTS = 8
EPS = 1e-6
N_MOD = 6

O_Q, O_K, O_V, O_G, O_U = 0, QK_W, 2 * QK_W, 2 * QK_W + V_W, 2 * QK_W + 2 * V_W
O_M1 = O_U + D_POOL
O_M2 = O_M1 + D_MODEL

VMEM_LIMIT_BYTES = 56 * 1024 * 1024
SUBLANES = 8
TN = 512
TN_IN = 1024
TN_MERGE = 1024
TF = 256
TM_PROMPT = 1024
TM_SAMPLE = 512
TM_FFN = 1024
TM_OUT = 2048
ROW_CHUNK = 256
NORM_ROWS = 16
NORM_UNROLL = 16
TL_POOL = 512
TM_MOE = 1280
SUB_MOE = 128
TM_COMBINE = 256
RET_PROMPT_BATCH = 4


def _params(*sem):
    return pltpu.CompilerParams(dimension_semantics=sem, vmem_limit_bytes=VMEM_LIMIT_BYTES)


class _Tokens:
    def __init__(self, n_batch, seq, n_past, tm, per_row_mod, mod=None):
        self.n_batch, self.seq, self.n_past = n_batch, seq, n_past
        self.rows = n_batch * seq
        self.tm = min(tm, self.rows)
        self.per_row_mod = per_row_mod
        if not per_row_mod:
            self.tm = min(self.tm, seq)
            assert seq % self.tm == 0
        assert self.rows % self.tm == 0
        self.n_tiles = self.rows // self.tm
        if mod is not None and not per_row_mod and mod.ndim == 3 and mod.shape[1] != 1:
            mod = mod.reshape(DEPTH * n_batch, 1, N_MOD * D_MODEL)
        self.mod = mod

    def with_tm(self, tm):
        return _Tokens(self.n_batch, self.seq, self.n_past, tm, self.per_row_mod, self.mod)

    def mod_spec(self, layer, chunk, tn=D_MODEL, col_axis=False):
        per_chunk = D_MODEL // tn
        col = (lambda n: chunk * per_chunk + n) if col_axis else (lambda n: chunk * per_chunk)
        if self.per_row_mod:
            if col_axis:
                return pl.BlockSpec((1, self.tm, tn), lambda m, n, *_: (layer, m, col(n)))
            return pl.BlockSpec((1, self.tm, tn), lambda m, *_: (layer, m, col(0)))
        tiles_per_batch = self.seq // self.tm
        base = layer * self.n_batch
        if col_axis:
            return pl.BlockSpec((1, 1, tn), lambda m, n, *_: (base + m // tiles_per_batch, 0, col(n)))
        return pl.BlockSpec((1, 1, tn), lambda m, *_: (base + m // tiles_per_batch, 0, col(0)))

    def rope_tables(self):
        half = RET_DK // 2
        pos = jnp.arange(self.seq, dtype=F32) + float(self.n_past)
        inv = ROPE_BASE ** (-jnp.arange(half, dtype=F32) / half)
        ang = pos[:, None] * inv[None, :]
        cos, sin = jnp.cos(ang), jnp.sin(ang)
        cos_full = jnp.concatenate([cos, cos], axis=1)
        sin_signed = jnp.concatenate([-sin, sin], axis=1)
        if self.per_row_mod:
            cos_full = jnp.tile(cos_full, (self.n_batch, 1))
            sin_signed = jnp.tile(sin_signed, (self.n_batch, 1))
        return cos_full, sin_signed

    def rope_spec(self):
        n_blocks = (self.rows if self.per_row_mod else self.seq) // self.tm
        return pl.BlockSpec((self.tm, RET_DK), lambda m, *_: (m % n_blocks, 0))


def _retention_tables(chunk):
    log_gamma = jnp.log1p(-jnp.exp2(-5.0 - jnp.arange(RET_HEADS, dtype=F32)))
    idx = jnp.arange(chunk, dtype=F32)
    rel = idx[:, None] - idx[None, :]
    dmask = jnp.where(rel[None] >= 0, jnp.exp(log_gamma[:, None, None] * jnp.maximum(rel, 0.0)[None]), 0.0)
    q_dec = jnp.exp(log_gamma[:, None] * (idx + 1.0)[None, :])
    k_dec = jnp.exp(log_gamma[:, None] * (chunk - 1.0 - idx)[None, :])
    chunk_dec = jnp.exp(log_gamma * chunk)
    q_dec = jnp.broadcast_to(q_dec[:, :, None], (RET_HEADS, chunk, RET_DV))
    k_dec = jnp.broadcast_to(k_dec[:, :, None], (RET_HEADS, chunk, RET_DK))
    chunk_dec = jnp.broadcast_to(chunk_dec[:, None, None], (RET_HEADS, 1, RET_DV))
    return dmask, q_dec, k_dec, chunk_dec


def _norm_rows(x_ref, g_ref, sc_ref, sh_ref, h_ref):
    tm = x_ref.shape[0]
    step = min(NORM_ROWS, tm)
    trips = tm // step
    per_row = sc_ref.shape[1] != 1
    if not per_row:
        gain, shift = g_ref[0] * (1.0 + sc_ref[0]), sh_ref[0]

    def body(c, carry):
        rows = pl.ds(pl.multiple_of(c * step, step), step)
        x = x_ref[rows, :]
        y = x * lax.rsqrt(jnp.mean(x * x, axis=-1, keepdims=True) + EPS)
        if per_row:
            y = y * (g_ref[0] * (1.0 + sc_ref[0, rows, :])) + sh_ref[0, rows, :]
        else:
            y = y * gain + shift
        h_ref[rows, :] = y.astype(h_ref.dtype)
        return carry

    lax.fori_loop(0, trips, body, 0, unroll=NORM_UNROLL if trips % NORM_UNROLL == 0 else 1)


def _silu(x):
    return x * jax.nn.sigmoid(x)


def _pack_halves(y):
    half = y.shape[1] // 2
    hi = lax.bitcast_convert_type(y[:, :half].astype(jnp.bfloat16).astype(F32), jnp.uint32)
    lo = lax.bitcast_convert_type(y[:, half:].astype(jnp.bfloat16).astype(F32), jnp.uint32)
    return hi | (lo >> 16)


def _unpack_halves(u):
    return (lax.bitcast_convert_type(u & jnp.uint32(0xFFFF0000), F32),
            lax.bitcast_convert_type(u << 16, F32))


def _ada_kernel(cp_ref, cs_ref, w_ref, b_ref, op_ref, os_ref):
    w = w_ref[0].astype(BF16)
    for c_ref, o_ref in ((cp_ref, op_ref), (cs_ref, os_ref)):
        o_ref[0] = jnp.dot(_silu(c_ref[...]).astype(BF16), w, preferred_element_type=F32) + b_ref[0]


def _ada(c_p, c_s, w_ada, b_ada):
    tn = 1024
    width = N_MOD * D_MODEL
    row_spec = lambda c: pl.BlockSpec(c.shape, lambda l, n: (0, 0))
    out_spec = lambda c: pl.BlockSpec((1, c.shape[0], tn), lambda l, n: (l, 0, n))
    return pl.pallas_call(
        _ada_kernel,
        grid=(DEPTH, width // tn),
        in_specs=[row_spec(c_p), row_spec(c_s),
                  pl.BlockSpec((1, D_MODEL, tn), lambda l, n: (l, 0, n)),
                  pl.BlockSpec((1, 1, tn), lambda l, n: (l, 0, n))],
        out_specs=[out_spec(c_p), out_spec(c_s)],
        out_shape=[jax.ShapeDtypeStruct((DEPTH, c.shape[0], width), F32) for c in (c_p, c_s)],
        compiler_params=_params("arbitrary", "arbitrary"),
        name="ada",
    )(c_p, c_s, w_ada, b_ada.reshape(DEPTH, 1, width))


def _inproj_kernel(x_ref, sc_ref, sh_ref, g_ref, w_ref, cos_ref, sin_ref, o_ref, u_ref, h_scr):
    n = pl.program_id(1)
    tm = x_ref.shape[0]
    rc = min(ROW_CHUNK, tm)

    @pl.when(n == 0)
    def _():
        _norm_rows(x_ref, g_ref, sc_ref, sh_ref, h_scr)

    def project(epilogue):
        w = w_ref[0].astype(BF16)
        for r in range(0, tm, rc):
            rows = slice(r, r + rc)
            epilogue(jnp.dot(h_scr[rows, :], w, preferred_element_type=F32), rows)

    def store(acc, rows):
        o_ref[rows, :] = acc.astype(o_ref.dtype)

    def rope(scale):
        def epilogue(acc, rows):
            for j in range(TN_IN // RET_DK):
                cols = slice(j * RET_DK, (j + 1) * RET_DK)
                xh = acc[:, cols]
                r = xh * cos_ref[rows, :] + pltpu.roll(xh, RET_DK // 2, 1) * sin_ref[rows, :]
                if scale != 1.0:
                    r = r * scale
                o_ref[rows, cols] = r.astype(o_ref.dtype)
        project(epilogue)

    @pl.when(n < O_K // TN_IN)
    def _():
        rope(1.0)

    @pl.when(jnp.logical_and(n >= O_K // TN_IN, n < O_V // TN_IN))
    def _():
        rope(RET_DK ** -0.5)

    @pl.when(jnp.logical_and(n >= O_V // TN_IN, n < O_G // TN_IN))
    def _():
        project(store)

    @pl.when(jnp.logical_and(n >= O_G // TN_IN, n < O_U // TN_IN))
    def _():
        project(lambda acc, rows: store(_silu(acc), rows))

    @pl.when(jnp.logical_and(n >= O_U // TN_IN, n < O_M1 // TN_IN))
    def _():
        def epilogue(acc, rows):
            store(acc, rows)
            u_ref[rows, :] = acc
        project(epilogue)

    @pl.when(n >= O_M1 // TN_IN)
    def _():
        project(lambda acc, rows: store(jax.nn.sigmoid(acc), rows))


def _inproj(tok, layer, x, g_mix, w_in, rope_tabs, out_dtype):
    tm = tok.tm
    tn = TN_IN
    assert all(o % tn == 0 for o in (O_K, O_V, O_G, O_U, O_M1, N_IN))
    n_u = D_POOL // tn
    mod3 = tok.mod
    proj, u = pl.pallas_call(
        _inproj_kernel,
        grid=(tok.n_tiles, N_IN // tn),
        in_specs=[pl.BlockSpec((tm, D_MODEL), lambda m, n: (m, 0)),
                  tok.mod_spec(layer, 1), tok.mod_spec(layer, 0),
                  pl.BlockSpec((1, 1, D_MODEL), lambda m, n: (layer, 0, 0)),
                  pl.BlockSpec((1, D_MODEL, tn), lambda m, n: (layer, 0, n)),
                  tok.rope_spec(), tok.rope_spec()],
        out_specs=[pl.BlockSpec((tm, tn), lambda m, n: (m, n)),
                   pl.BlockSpec((tm, tn), lambda m, n: (m, jnp.clip(n - O_U // tn, 0, n_u - 1)),
                                pipeline_mode=pl.Buffered(1))],
        out_shape=[jax.ShapeDtypeStruct((tok.rows, N_IN), out_dtype),
                   jax.ShapeDtypeStruct((tok.rows, D_POOL), F32)],
        scratch_shapes=[pltpu.VMEM((tm, D_MODEL), BF16)],
        compiler_params=_params("arbitrary", "arbitrary"),
        name="inproj",
    )(x, mod3, mod3, g_mix, w_in, *rope_tabs)
    return proj, u


def _head_norm_gate(o, gret, gate):
    mu = jnp.mean(o, axis=-1, keepdims=True)
    d = o - mu
    var = jnp.mean(d * d, axis=-1, keepdims=True)
    return gate.astype(F32) * (d * lax.rsqrt(var + EPS) * gret)


def _ret_prompt_kernel(q_ref, k_ref, v_ref, g_ref, gret_ref, dmask_ref, qdec_ref, kdec_ref, cdec_ref,
                       *rest):
    y_ref, s_ref = rest[-2:]

    @pl.when(pl.program_id(1) == 0)
    def _():
        s_ref[...] = jnp.zeros_like(s_ref)

    for b in range(q_ref.shape[0]):
        for h in range(RET_HEADS):
            ks, vs = slice(h * RET_DK, (h + 1) * RET_DK), slice(h * RET_DV, (h + 1) * RET_DV)
            q, k, v = q_ref[b, :, ks], k_ref[b, :, ks], v_ref[b, :, vs]
            state = s_ref[0, b, h]
            scores = lax.dot_general(q, k, (((1,), (1,)), ((), ())), preferred_element_type=F32) * dmask_ref[h]
            inner = jnp.dot(scores.astype(BF16), v, preferred_element_type=F32)
            cross = jnp.dot(q, state.astype(BF16), preferred_element_type=F32) * qdec_ref[h]
            k_decayed = (k.astype(F32) * kdec_ref[h]).astype(BF16)
            s_ref[0, b, h] = cdec_ref[h] * state + lax.dot_general(
                k_decayed, v, (((0,), (0,)), ((), ())), preferred_element_type=F32)
            y_ref[b, :, vs] = _head_norm_gate(inner + cross, gret_ref[0, :, vs], g_ref[b, :, vs]).astype(y_ref.dtype)


def _table_specs(tables):
    return [pl.BlockSpec(t.shape, lambda *_: (0, 0, 0)) for t in tables]


def _ret_prompt(tok, layer, proj, g_ret, s_all):
    n_b, seq = tok.n_batch, tok.seq
    c = RET_CHUNK
    assert seq % c == 0
    nc = seq // c
    tables = _retention_tables(c)
    bb = RET_PROMPT_BATCH if n_b % RET_PROMPT_BATCH == 0 else 1
    proj3 = proj.reshape(n_b, seq, N_IN)
    in_specs = [pl.BlockSpec((bb, c, QK_W), lambda b, i: (b, i, O_Q // QK_W)),
                pl.BlockSpec((bb, c, QK_W), lambda b, i: (b, i, O_K // QK_W)),
                pl.BlockSpec((bb, c, V_W), lambda b, i: (b, i, O_V // V_W)),
                pl.BlockSpec((bb, c, V_W), lambda b, i: (b, i, O_G // V_W)),
                pl.BlockSpec((1, 1, V_W), lambda b, i: (layer, 0, 0))] + _table_specs(tables)
    args = [proj3, proj3, proj3, proj3, g_ret, *tables]
    aliases = {}
    if s_all is not None:
        aliases = {len(args): 1}
        in_specs.append(pl.BlockSpec(memory_space=pl.ANY))
        args.append(s_all)
    y, s_all = pl.pallas_call(
        _ret_prompt_kernel,
        grid=(n_b // bb, nc),
        in_specs=in_specs,
        out_specs=[pl.BlockSpec((bb, c, V_W), lambda b, i: (b, i, 0)),
                   pl.BlockSpec((1, bb, RET_HEADS, RET_DK, RET_DV), lambda b, i: (layer, b, 0, 0, 0))],
        out_shape=[jax.ShapeDtypeStruct((n_b, seq, V_W), BF16),
                   jax.ShapeDtypeStruct((DEPTH, n_b, RET_HEADS, RET_DK, RET_DV), F32)],
        input_output_aliases=aliases,
        compiler_params=_params("arbitrary", "arbitrary"),
        name="ret_prompt",
    )(*args)
    return y.reshape(tok.rows, V_W), s_all


def _ret_sample_kernel(q_ref, k_ref, v_ref, g_ref, kt_ref, gret_ref, s0_ref, dm_ref, qdec_ref, kdect_ref, cdec_ref,
                       *rest, bb, seq):
    y_ref, s_ref = rest[-2:]
    hs = RET_HEADS * seq
    reps = lambda a: jnp.concatenate([a] * RET_HEADS, axis=0)
    mask_q = (lax.broadcasted_iota(jnp.int32, (hs, QK_W), 0) // seq
              == lax.broadcasted_iota(jnp.int32, (hs, QK_W), 1) // RET_DK)
    mask_v = (lax.broadcasted_iota(jnp.int32, (hs, V_W), 0) // seq
              == lax.broadcasted_iota(jnp.int32, (hs, V_W), 1) // RET_DV)
    nt = (((1,), (1,)), ((), ()))
    for b in range(bb):
        rows_b = slice(b * seq, (b + 1) * seq)
        q, k, v = q_ref[rows_b, :], k_ref[rows_b, :], v_ref[rows_b, :]
        q_blk = jnp.where(mask_q, reps(q), 0.0).astype(BF16)
        scores = lax.dot_general(q_blk, reps(k).astype(BF16), nt, preferred_element_type=F32) * dm_ref[...]
        v_stack = jnp.concatenate([v[:, h * RET_DV:(h + 1) * RET_DV] for h in range(RET_HEADS)], axis=0)
        inner = jnp.dot(scores.astype(BF16), v_stack.astype(BF16), preferred_element_type=F32)
        state = s0_ref[0, b]
        cross = jnp.dot(q_blk, state.reshape(QK_W, RET_DV).astype(BF16), preferred_element_type=F32)
        o = inner + cross * qdec_ref[...]
        mu = jnp.mean(o, axis=-1, keepdims=True)
        d = o - mu
        normed = d * lax.rsqrt(jnp.mean(d * d, axis=-1, keepdims=True) + EPS)
        k_dec_t = (kt_ref[b] * kdect_ref[...]).astype(BF16)
        v_blk = jnp.where(mask_v, reps(v), 0.0).astype(BF16)
        upd = jnp.dot(k_dec_t, v_blk, preferred_element_type=F32)
        for h in range(RET_HEADS):
            vs = slice(h * RET_DV, (h + 1) * RET_DV)
            y_ref[rows_b, vs] = g_ref[rows_b, vs] * (normed[h * seq:(h + 1) * seq] * gret_ref[0, :, vs])
            s_ref[0, b, h] = cdec_ref[h] * state[h] + upd[:, vs]


def _ret_sample(tok, layer, proj, g_ret, state_ret, s_all):
    n_b, seq = tok.n_batch, tok.seq
    bb = 4
    assert n_b % bb == 0
    hs = RET_HEADS * seq
    dmask, q_dec, k_dec, chunk_dec = _retention_tables(seq)
    dm_blk = (dmask[:, :, None, :] * jnp.eye(RET_HEADS, dtype=F32)[:, None, :, None]).reshape(hs, hs)
    tables = [dm_blk, q_dec.reshape(hs, RET_DV), k_dec[:, :, 0].reshape(1, hs), chunk_dec]
    k_t = proj[:, O_K:O_K + QK_W].reshape(n_b, seq, RET_HEADS, RET_DK).transpose(0, 3, 2, 1).reshape(n_b, RET_DK, hs)
    state_spec = pl.BlockSpec((1, bb, RET_HEADS, RET_DK, RET_DV), lambda i: (layer, i, 0, 0, 0))
    in_specs = [pl.BlockSpec((bb * seq, QK_W), lambda i: (i, O_Q // QK_W)),
                pl.BlockSpec((bb * seq, QK_W), lambda i: (i, O_K // QK_W)),
                pl.BlockSpec((bb * seq, V_W), lambda i: (i, O_V // V_W)),
                pl.BlockSpec((bb * seq, V_W), lambda i: (i, O_G // V_W)),
                pl.BlockSpec((bb, RET_DK, hs), lambda i: (i, 0, 0)),
                pl.BlockSpec((1, 1, V_W), lambda i: (layer, 0, 0)),
                state_spec] + [pl.BlockSpec(t.shape, lambda i, nd=t.ndim: (0,) * nd) for t in tables]
    args = [proj, proj, proj, proj, k_t, g_ret, state_ret, *tables]
    aliases = {}
    if s_all is not None:
        aliases = {len(args): 1}
        in_specs.append(pl.BlockSpec(memory_space=pl.ANY))
        args.append(s_all)
    y, s_all = pl.pallas_call(
        functools.partial(_ret_sample_kernel, bb=bb, seq=seq),
        grid=(n_b // bb,),
        in_specs=in_specs,
        out_specs=[pl.BlockSpec((bb * seq, V_W), lambda i: (i, 0)), state_spec],
        out_shape=[jax.ShapeDtypeStruct((tok.rows, V_W), F32),
                   jax.ShapeDtypeStruct(state_ret.shape, F32)],
        input_output_aliases=aliases,
        compiler_params=_params("arbitrary"),
        name="ret_sample",
    )(*args)
    return y, s_all


def _pool_map(pooled, wmap_ref, scale_ref, gi):
    cols = slice(gi * POOL_GW, (gi + 1) * POOL_GW)
    mixed = jnp.dot(pooled.astype(BF16), wmap_ref[0, gi].astype(BF16), preferred_element_type=F32)
    return mixed * scale_ref[0, :, cols]


def _pool_prompt_kernel(u_ref, halo_ref, wmap_ref, scale_ref, o_ref, uf_scr, a_scr, b_scr, *, tl):
    pad = SUBLANES
    top = pad + POOL_HALO
    first = pl.program_id(1) == 0
    uf_scr[0:pad, :] = jnp.zeros((pad, D_POOL), F32)
    a_scr[0:pad, :] = jnp.zeros((pad, POOL_GW), F32)
    b_scr[0:pad, :] = jnp.zeros((pad, POOL_GW), F32)
    uf_scr[pad:top, :] = jnp.where(first, 0.0, halo_ref[...])
    uf_scr[top:, :] = u_ref[...]
    n_rows = POOL_HALO + tl
    row = lax.broadcasted_iota(jnp.int32, (tl, POOL_GW), 0) + pl.program_id(1) * tl
    for gi, w in enumerate(POOL_WINDOWS):
        cols = slice(gi * POOL_GW, (gi + 1) * POOL_GW)
        src = lambda lo, cols=cols: uf_scr[lo:lo + n_rows, cols]
        dst_scr, other = a_scr, b_scr
        shift = 1
        while shift < w:
            dst_scr[pad:, :] = src(pad) + src(pad - shift)
            src = lambda lo, s=dst_scr: s[lo:lo + n_rows, :]
            dst_scr, other = other, dst_scr
            shift *= 2
        acc = src(pad)[POOL_HALO:, :]
        u = uf_scr[top:, cols]
        cnt = jnp.minimum(w, row + 1).astype(F32)
        o_ref[:, cols] = _pool_map(acc / cnt - u, wmap_ref, scale_ref, gi).astype(o_ref.dtype)


def _pool_prompt(tok, layer, u, w_pool_map, pool_scale):
    n_b, seq = tok.n_batch, tok.seq
    tl = min(TL_POOL, seq)
    assert seq % tl == 0 and tl % POOL_HALO == 0
    nl = seq // tl
    per = tl // POOL_HALO
    return pl.pallas_call(
        functools.partial(_pool_prompt_kernel, tl=tl),
        grid=(n_b, nl),
        in_specs=[pl.BlockSpec((tl, D_POOL), lambda b, l: (b * nl + l, 0)),
                  pl.BlockSpec((POOL_HALO, D_POOL), lambda b, l: (jnp.maximum((b * nl + l) * per - 1, 0), 0)),
                  pl.BlockSpec((1,) + w_pool_map.shape[1:], lambda b, l: (layer, 0, 0, 0)),
                  pl.BlockSpec((1, 1, D_POOL), lambda b, l: (layer, 0, 0))],
        out_specs=pl.BlockSpec((tl, D_POOL), lambda b, l: (b * nl + l, 0)),
        out_shape=jax.ShapeDtypeStruct((tok.rows, D_POOL), BF16),
        scratch_shapes=[pltpu.VMEM((SUBLANES + POOL_HALO + tl, D_POOL), F32),
                        pltpu.VMEM((SUBLANES + POOL_HALO + tl, POOL_GW), F32),
                        pltpu.VMEM((SUBLANES + POOL_HALO + tl, POOL_GW), F32)],
        compiler_params=_params("arbitrary", "arbitrary"),
        name="pool_prompt",
    )(u, u, w_pool_map, pool_scale)


def _pool_sample_kernel(buf_ref, u_ref, wmap_ref, scale_ref, o_ref, *, seq, n_b):
    def row(j, cols):
        return buf_ref[j, :, cols] if j < POOL_BUF else u_ref[j - POOL_BUF, :, cols]

    for gi, w in enumerate(POOL_WINDOWS):
        cols = slice(gi * POOL_GW, (gi + 1) * POOL_GW)
        for t in range(seq):
            acc = row(POOL_BUF + t, cols)
            for j in range(1, w):
                acc = acc + row(POOL_BUF + t - j, cols)
            pooled = acc / float(w) - row(POOL_BUF + t, cols)
            o_ref[t, :, cols] = _pool_map(pooled, wmap_ref, scale_ref, gi)


def _pool_sample(tok, layer, u, buf_tm, w_pool_map, pool_scale):
    n_b, seq = tok.n_batch, tok.seq
    assert tok.n_past >= max(POOL_WINDOWS)
    u_tm = u.reshape(n_b, seq, D_POOL).transpose(1, 0, 2)
    mixed_tm = pl.pallas_call(
        functools.partial(_pool_sample_kernel, seq=seq, n_b=n_b),
        grid=(1,),
        in_specs=[pl.BlockSpec(buf_tm.shape, lambda i: (0, 0, 0)),
                  pl.BlockSpec(u_tm.shape, lambda i: (0, 0, 0)),
                  pl.BlockSpec((1,) + w_pool_map.shape[1:], lambda i: (layer, 0, 0, 0)),
                  pl.BlockSpec((1, 1, D_POOL), lambda i: (layer, 0, 0))],
        out_specs=pl.BlockSpec(u_tm.shape, lambda i: (0, 0, 0)),
        out_shape=jax.ShapeDtypeStruct(u_tm.shape, F32),
        compiler_params=_params("arbitrary"),
        name="pool_sample",
    )(buf_tm, u_tm, w_pool_map, pool_scale)
    return mixed_tm.transpose(1, 0, 2).reshape(tok.rows, D_POOL)


def _merge_kernel(yret_ref, mixed_ref, wr_ref, wp_ref, g1_ref, g2_ref, o_ref, wr_scr, wp_scr):
    @pl.when(pl.program_id(1) == 0)
    def _():
        wr_scr[...] = wr_ref[0].astype(BF16)
        wp_scr[...] = wp_ref[0].astype(BF16)

    ret = jnp.dot(yret_ref[...].astype(BF16), wr_scr[...], preferred_element_type=F32)
    pool = jnp.dot(mixed_ref[...].astype(BF16), wp_scr[...], preferred_element_type=F32)
    o_ref[...] = (g1_ref[...].astype(F32) * ret + g2_ref[...].astype(F32) * pool).astype(o_ref.dtype)


def _merge(tok, layer, yret, mixed, proj, w_ret_out, w_pool_out):
    tm, tn = tok.tm, TN_MERGE
    once = pl.Buffered(1)
    return pl.pallas_call(
        _merge_kernel,
        grid=(D_MODEL // tn, tok.n_tiles),
        in_specs=[pl.BlockSpec((tm, V_W), lambda n, m: (m, 0)),
                  pl.BlockSpec((tm, D_POOL), lambda n, m: (m, 0)),
                  pl.BlockSpec((1, V_W, tn), lambda n, m: (layer, 0, n), pipeline_mode=once),
                  pl.BlockSpec((1, D_POOL, tn), lambda n, m: (layer, 0, n), pipeline_mode=once),
                  pl.BlockSpec((tm, tn), lambda n, m: (m, O_M1 // tn + n)),
                  pl.BlockSpec((tm, tn), lambda n, m: (m, O_M2 // tn + n))],
        out_specs=pl.BlockSpec((tm, tn), lambda n, m: (m, n)),
        out_shape=jax.ShapeDtypeStruct((tok.rows, D_MODEL), BF16),
        scratch_shapes=[pltpu.VMEM((V_W, tn), BF16), pltpu.VMEM((D_POOL, tn), BF16)],
        compiler_params=_params("arbitrary", "arbitrary"),
        name="merge",
    )(yret, mixed, w_ret_out, w_pool_out, proj, proj)


def _wo_kernel(merged_ref, w_ref, x_ref, gt_ref, o_ref):
    mix = jnp.dot(merged_ref[...], w_ref[0].astype(BF16), preferred_element_type=F32)
    o_ref[...] = x_ref[...] + (1.0 + gt_ref[0]) * mix


def _wo(tok, layer, merged, x, w_o):
    tok = tok.with_tm(TM_OUT)
    tm = tok.tm
    mod3 = tok.mod
    return pl.pallas_call(
        _wo_kernel,
        grid=(tok.n_tiles, D_MODEL // TN),
        in_specs=[pl.BlockSpec((tm, D_MODEL), lambda m, n: (m, 0)),
                  pl.BlockSpec((1, D_MODEL, TN), lambda m, n: (layer, 0, n)),
                  pl.BlockSpec((tm, TN), lambda m, n: (m, n)),
                  tok.mod_spec(layer, 2, tn=TN, col_axis=True)],
        out_specs=pl.BlockSpec((tm, TN), lambda m, n: (m, n)),
        out_shape=jax.ShapeDtypeStruct((tok.rows, D_MODEL), F32),
        compiler_params=_params("arbitrary", "arbitrary"),
        name="wo",
    )(merged, w_o, x, mod3)


def _mat(w_ref):
    return w_ref[(0,) * (len(w_ref.shape) - 2)]


def _final_norm(y, gfin_ref):
    return y * lax.rsqrt(jnp.mean(y * y, axis=-1, keepdims=True) + EPS) * gfin_ref[...]


def _ffn_kernel(x_ref, sc_ref, sh_ref, gt_ref, g_ref, w1_ref, w3_ref, w2_ref, *rest, n_f, with_final):
    gfin_ref = rest[0] if with_final else None
    o_ref, h_scr = rest[-2:]
    f = pl.program_id(1)

    @pl.when(f == 0)
    def _():
        _norm_rows(x_ref, g_ref, sc_ref, sh_ref, h_scr)
        o_ref[...] = jnp.zeros_like(o_ref)

    h = h_scr[...]
    a = jnp.dot(h, _mat(w1_ref).astype(BF16), preferred_element_type=F32)
    b = jnp.dot(h, _mat(w3_ref).astype(BF16), preferred_element_type=F32)
    o_ref[...] += jnp.dot((_silu(a) * b).astype(BF16), _mat(w2_ref).astype(BF16), preferred_element_type=F32)

    @pl.when(f == n_f - 1)
    def _():
        y = x_ref[...] + (1.0 + gt_ref[0]) * o_ref[...]
        o_ref[...] = _final_norm(y, gfin_ref) if with_final else y


def _ffn(tok, layer, x, g_ffn, w1, w3, w2, g_final=None):
    j = layer // 2
    d_ff = w1.shape[-1]
    tok_f = tok.with_tm(TM_FFN)
    tm, n_tiles = tok_f.tm, tok_f.n_tiles
    mod3 = tok_f.mod
    assert d_ff % TF == 0
    n_f = d_ff // TF
    in_specs = [pl.BlockSpec((tm, D_MODEL), lambda m, f: (m, 0)),
                tok_f.mod_spec(layer, 4), tok_f.mod_spec(layer, 3), tok_f.mod_spec(layer, 5),
                pl.BlockSpec((1, 1, D_MODEL), lambda m, f: (layer, 0, 0)),
                pl.BlockSpec((1, D_MODEL, TF), lambda m, f: (j, 0, f)),
                pl.BlockSpec((1, D_MODEL, TF), lambda m, f: (j, 0, f)),
                pl.BlockSpec((1, TF, D_MODEL), lambda m, f: (j, f, 0))]
    args = [x, mod3, mod3, mod3, g_ffn, w1, w3, w2]
    if g_final is not None:
        in_specs.append(pl.BlockSpec((1, D_MODEL), lambda m, f: (0, 0)))
        args.append(g_final)
    return pl.pallas_call(
        functools.partial(_ffn_kernel, n_f=n_f, with_final=g_final is not None),
        grid=(n_tiles, n_f),
        in_specs=in_specs,
        out_specs=pl.BlockSpec((tm, D_MODEL), lambda m, f: (m, 0)),
        out_shape=jax.ShapeDtypeStruct((tok.rows, D_MODEL), F32),
        scratch_shapes=[pltpu.VMEM((tm, D_MODEL), BF16)],
        compiler_params=_params("arbitrary", "arbitrary"),
        name="ffn",
    )(*args)


def _router_kernel(x_ref, sc_ref, sh_ref, g_ref, w_ref, b_ref, *rest):
    esel_ref, wsel_ref, h_ref, h_scr = rest[-4:]
    _norm_rows(x_ref, g_ref, sc_ref, sh_ref, h_scr)
    h = h_scr[...]
    logits = jnp.dot(h, w_ref[0].astype(BF16), preferred_element_type=F32) + b_ref[0]
    idx = lax.broadcasted_iota(jnp.int32, logits.shape, 1)
    top1 = jnp.max(logits, axis=-1, keepdims=True)
    i1 = jnp.min(jnp.where(logits == top1, idx, N_EXPERTS), axis=-1, keepdims=True)
    others = jnp.where(idx == i1, -jnp.inf, logits)
    top2 = jnp.max(others, axis=-1, keepdims=True)
    i2 = jnp.min(jnp.where(others == top2, idx, N_EXPERTS), axis=-1, keepdims=True)
    e2 = jnp.exp(top2 - top1)
    denom = 1.0 + e2
    w1, w2 = 1.0 / denom, e2 / denom
    esel_ref[...] = jnp.where(idx == 0, i1, i2)
    wsel_ref[...] = jnp.where(idx == 0, w1, w2)
    h_ref[...] = _pack_halves(h.astype(F32))


def _router(tok, layer, x, g_ffn, w_router, b_router, h_all=None, row0=0, rows_all=None):
    j = layer // 2
    tm = tok.tm
    rows_all = tok.rows if rows_all is None else rows_all
    assert row0 % tm == 0
    small = pl.BlockSpec((tm, N_EXPERTS), lambda m: (m, 0))
    mod3 = tok.mod
    in_specs = [pl.BlockSpec((tm, D_MODEL), lambda m: (m, 0)),
                tok.mod_spec(layer, 4), tok.mod_spec(layer, 3),
                pl.BlockSpec((1, 1, D_MODEL), lambda m: (layer, 0, 0)),
                pl.BlockSpec((1, D_MODEL, N_EXPERTS), lambda m: (j, 0, 0)),
                pl.BlockSpec((1, 1, N_EXPERTS), lambda m: (j, 0, 0))]
    args = [x, mod3, mod3, g_ffn, w_router, b_router]
    aliases = {}
    if h_all is not None:
        aliases = {len(args): 2}
        in_specs.append(pl.BlockSpec(memory_space=pl.ANY))
        args.append(h_all)
    return pl.pallas_call(
        _router_kernel,
        grid=(tok.n_tiles,),
        in_specs=in_specs,
        out_specs=[small, small, pl.BlockSpec((tm, D_MODEL // 2), lambda m: (row0 // tm + m, 0))],
        out_shape=[jax.ShapeDtypeStruct((tok.rows, N_EXPERTS), jnp.int32),
                   jax.ShapeDtypeStruct((tok.rows, N_EXPERTS), F32),
                   jax.ShapeDtypeStruct((rows_all, D_MODEL // 2), jnp.uint32)],
        input_output_aliases=aliases,
        scratch_shapes=[pltpu.VMEM((tm, D_MODEL), BF16)],
        compiler_params=_params("arbitrary"),
        name="router",
    )(*args)


def _route_plan(choice, tm):
    rows = choice.shape[0]
    n_pairs = 2 * rows
    n_tiles = n_pairs // tm + N_EXPERTS
    e_flat = choice.T.reshape(n_pairs)
    onehot = (e_flat[:, None] == jnp.arange(N_EXPERTS, dtype=jnp.int32)[None, :]).astype(jnp.int32)
    csum = jnp.cumsum(onehot, axis=0)
    rank = jnp.sum(csum * onehot, axis=1) - 1
    count = csum[-1]
    tiles_per = (count + tm - 1) // tm
    rows_per = jnp.maximum((((count + jnp.maximum(tiles_per, 1) - 1) // jnp.maximum(tiles_per, 1)) + 7) // 8 * 8, 8)
    tile_end = jnp.cumsum(tiles_per)
    tile_start = tile_end - tiles_per
    pos = (tile_start[e_flat] + rank // rows_per[e_flat]) * tm + rank % rows_per[e_flat]
    token = jnp.arange(n_pairs, dtype=jnp.int32) % rows
    src_row = jnp.zeros((n_tiles * tm,), jnp.int32).at[pos].set(token, unique_indices=True, mode='promise_in_bounds')
    tile_id = jnp.arange(n_tiles, dtype=jnp.int32)
    n_used = tile_end[-1]
    tile_expert = jnp.minimum(jnp.sum(tile_id[:, None] >= tile_end[None, :], axis=1), N_EXPERTS - 1).astype(jnp.int32)
    tile_rows = jnp.clip(count[tile_expert] - (tile_id - tile_start[tile_expert]) * rows_per[tile_expert],
                         0, rows_per[tile_expert])
    tile_rows = jnp.where(tile_id < n_used, tile_rows, 0).astype(jnp.int32)
    last_expert = tile_expert[jnp.maximum(n_used - 1, 0)]
    tile_expert = jnp.where(tile_rows > 0, tile_expert, last_expert)
    return tile_expert, tile_rows, src_row, pos.astype(jnp.int32)


def _moe_kernel(te_ref, nr_ref, src_ref, h_hbm, w1_ref, w3_ref, w2_ref, o_ref, rows_scr, h_scr, acc_scr, sem,
                *, tm, sub, n_f):
    del te_ref
    m, f = pl.program_id(0), pl.program_id(1)
    half = D_MODEL // 2
    group = rows_scr.shape[1]

    def n_groups(tile):
        return (nr_ref[tile] + group - 1) // group

    def start_rows(tile):
        def body(i, c):
            for j in range(group):
                row = src_ref[tile * tm + i * group + j]
                pltpu.make_async_copy(h_hbm.at[pl.ds(row, 1)], rows_scr.at[i, pl.ds(j, 1)], sem).start()
            return c
        lax.fori_loop(0, n_groups(tile), body, 0)

    def wait_rows(tile):
        def body(i, c):
            pltpu.make_async_copy(h_hbm.at[pl.ds(0, group)], rows_scr.at[i], sem).wait()
            return c
        lax.fori_loop(0, n_groups(tile), body, 0)

    @pl.when(f == 0)
    def _():
        acc_scr[...] = jnp.zeros_like(acc_scr)

        @pl.when(m == 0)
        def _():
            rows_scr[...] = jnp.zeros_like(rows_scr)
            start_rows(0)

        wait_rows(m)
        first, second = _unpack_halves(rows_scr[...].reshape(tm, half))
        h_scr[:, :half] = first.astype(BF16)
        h_scr[:, half:] = second.astype(BF16)

        @pl.when(m + 1 < pl.num_programs(0))
        def _():
            start_rows(m + 1)

    n_sub = (nr_ref[m] + sub - 1) // sub
    for n in range(1, tm // sub + 1):
        @pl.when(n_sub == n)
        def _(n=n):
            h = h_scr[0:n * sub, :]
            a = jnp.dot(h, _mat(w1_ref).astype(BF16), preferred_element_type=F32)
            b = jnp.dot(h, _mat(w3_ref).astype(BF16), preferred_element_type=F32)
            acc_scr[0:n * sub, :] += jnp.dot((_silu(a) * b).astype(BF16), _mat(w2_ref).astype(BF16),
                                             preferred_element_type=F32)

    @pl.when(f == n_f - 1)
    def _():
        o_ref[...] = _pack_halves(acc_scr[...])


def _moe_sparse(layer, h, plan, w1, w3, w2, tm):
    j = layer // 2
    tile_expert, tile_rows, src_row, _ = plan
    n_tiles = tile_expert.shape[0]
    d_ff = w1.shape[3]
    assert d_ff % TF == 0 and tm % SUB_MOE == 0
    n_f = d_ff // TF

    def f_sel(m, f, nr):
        return jnp.where(nr[m] > 0, f, n_f - 1)

    return pl.pallas_call(
        functools.partial(_moe_kernel, tm=tm, sub=SUB_MOE, n_f=n_f),
        grid_spec=pltpu.PrefetchScalarGridSpec(
            num_scalar_prefetch=3,
            grid=(n_tiles, n_f),
            in_specs=[pl.BlockSpec(memory_space=pl.ANY),
                      pl.BlockSpec((1, 1, D_MODEL, TF), lambda m, f, te, nr, sr: (j, te[m], 0, f_sel(m, f, nr))),
                      pl.BlockSpec((1, 1, D_MODEL, TF), lambda m, f, te, nr, sr: (j, te[m], 0, f_sel(m, f, nr))),
                      pl.BlockSpec((1, 1, TF, D_MODEL), lambda m, f, te, nr, sr: (j, te[m], f_sel(m, f, nr), 0))],
            out_specs=pl.BlockSpec((tm, D_MODEL // 2), lambda m, f, te, nr, sr: (m, 0)),
            scratch_shapes=[pltpu.VMEM((tm // SUBLANES, SUBLANES, D_MODEL // 2), jnp.uint32),
                            pltpu.VMEM((tm, D_MODEL), BF16),
                            pltpu.VMEM((tm, D_MODEL), F32), pltpu.SemaphoreType.DMA(())]),
        out_shape=jax.ShapeDtypeStruct((n_tiles * tm, D_MODEL // 2), jnp.uint32),
        compiler_params=_params("arbitrary", "arbitrary"),
        name="moe",
    )(tile_expert, tile_rows, src_row, h, w1, w3, w2)


def _combine_kernel(pos_ref, y_hbm, x_ref, gt_ref, w_ref, *rest, tm, row0, rows_all, with_final):
    gfin_ref = rest[0] if with_final else None
    o_ref, y_scr, sems = rest[-3:]
    i, n = pl.program_id(0), pl.num_programs(0)

    half = D_MODEL // 2
    group = y_scr.shape[3]

    def start_rows(tile, slot):
        def body(g, c):
            for k in range(2):
                for j in range(group):
                    slot_of_pair = pos_ref[k * rows_all + row0 + tile * tm + g * group + j]
                    pltpu.make_async_copy(y_hbm.at[pl.ds(slot_of_pair, 1)], y_scr.at[slot, k, g, pl.ds(j, 1)],
                                          sems.at[slot]).start()
            return c
        lax.fori_loop(0, tm // group, body, 0)

    @pl.when(i == 0)
    def _():
        start_rows(0, 0)

    @pl.when(i + 1 < n)
    def _():
        start_rows(i + 1, (i + 1) % 2)

    slot = i % 2
    def wait_group(g, c):
        for k in range(2):
            pltpu.make_async_copy(y_hbm.at[pl.ds(0, group)], y_scr.at[slot, k, g], sems.at[slot]).wait()
        return c

    lax.fori_loop(0, tm // group, wait_group, 0)
    w = w_ref[...]
    first0, second0 = _unpack_halves(y_scr[slot, 0].reshape(tm, half))
    first1, second1 = _unpack_halves(y_scr[slot, 1].reshape(tm, half))
    f = jnp.concatenate([w[:, 0:1] * first0 + w[:, 1:2] * first1,
                         w[:, 0:1] * second0 + w[:, 1:2] * second1], axis=1)
    y = x_ref[...] + (1.0 + gt_ref[0]) * f
    o_ref[...] = _final_norm(y, gfin_ref) if with_final else y


def _combine(tok, layer, y_sorted, pos, row0, x, wsel, g_final):
    tok_c = tok.with_tm(TM_COMBINE)
    tm = tok_c.tm
    mod3 = tok_c.mod
    in_specs = [pl.BlockSpec(memory_space=pl.ANY),
                pl.BlockSpec((tm, D_MODEL), lambda m, pos: (m, 0)),
                tok_c.mod_spec(layer, 5),
                pl.BlockSpec((tm, N_EXPERTS), lambda m, pos: (m, 0))]
    args = [pos, y_sorted, x, mod3, wsel]
    if g_final is not None:
        in_specs.append(pl.BlockSpec((1, D_MODEL), lambda m, pos: (0, 0)))
        args.append(g_final)
    return pl.pallas_call(
        functools.partial(_combine_kernel, tm=tm, row0=row0, rows_all=pos.shape[0] // 2,
                          with_final=g_final is not None),
        grid_spec=pltpu.PrefetchScalarGridSpec(
            num_scalar_prefetch=1,
            grid=(tok_c.n_tiles,),
            in_specs=in_specs,
            out_specs=pl.BlockSpec((tm, D_MODEL), lambda m, pos: (m, 0)),
            scratch_shapes=[pltpu.VMEM((2, 2, tm // SUBLANES, SUBLANES, D_MODEL // 2), jnp.uint32),
                            pltpu.SemaphoreType.DMA((2,))]),
        out_shape=jax.ShapeDtypeStruct((tok.rows, D_MODEL), F32),
        compiler_params=_params("arbitrary"),
        name="combine",
    )(*args)


class _Stream:
    def __init__(self, tok, x, ret_state, pool_state):
        self.tok, self.x, self.ret_state, self.pool_state = tok, x, ret_state, pool_state
        self.rope_tabs = tok.rope_tables()
        self.new_state = None
        self.u_layers = []


def _mixer_layer(s, i, p):
    tok = s.tok
    is_prompt = s.ret_state is None
    proj, u = _inproj(tok, i, s.x, p['g_mix'], p['w_in'], s.rope_tabs, BF16 if is_prompt else F32)
    s.u_layers.append(u)
    if is_prompt:
        yret, s.new_state = _ret_prompt(tok, i, proj, p['g_ret'], s.new_state)
        mixed = _pool_prompt(tok, i, u, p['w_pool_map'], p['pool_scale'])
    else:
        yret, s.new_state = _ret_sample(tok, i, proj, p['g_ret'], s.ret_state, s.new_state)
        mixed = _pool_sample(tok, i, u, s.pool_state[i].transpose(1, 0, 2), p['w_pool_map'], p['pool_scale'])
    merged = _merge(tok, i, yret, mixed, proj, p['w_ret_out'], p['w_pool_out'])
    s.x = _wo(tok, i, merged, s.x, p['w_o'])


def _moe_layer(streams, i, p, g_final):
    rows_all = sum(s.tok.rows for s in streams)
    h_all, row0, choices, weights, starts = None, 0, [], [], []
    for s in streams:
        esel, wsel, h_all = _router(s.tok, i, s.x, p['g_ffn'], p['w_router'], p['b_router'],
                                    h_all=h_all, row0=row0, rows_all=rows_all)
        choices.append(esel[:, :2])
        weights.append(wsel)
        starts.append(row0)
        row0 += s.tok.rows
    plan = _route_plan(jnp.concatenate(choices, axis=0), TM_MOE)
    y_sorted = _moe_sparse(i, h_all, plan, p['w1_moe'], p['w3_moe'], p['w2_moe'], TM_MOE)
    for s, wsel, start in zip(streams, weights, starts):
        s.x = _combine(s.tok, i, y_sorted, plan[3], start, s.x, wsel, g_final)


def kernel(x_prompt, x_sample, c_prompt, c_sample, state_ret, state_pool, w_ada, b_ada, g_mix, w_in, g_ret,
           w_ret_out, w_pool_map, pool_scale, w_pool_out, w_o, g_ffn, w1_dense, w3_dense, w2_dense, w_router,
           b_router, w1_moe, w3_moe, w2_moe, g_final):
    row3 = lambda a: a.reshape(a.shape[0], 1, a.shape[1])
    p = dict(g_mix=row3(g_mix), w_in=w_in, g_ret=row3(g_ret), w_ret_out=w_ret_out, w_pool_map=w_pool_map,
             pool_scale=row3(pool_scale), w_pool_out=w_pool_out, w_o=w_o, g_ffn=row3(g_ffn),
             w1_dense=w1_dense, w3_dense=w3_dense, w2_dense=w2_dense, w_router=w_router, b_router=row3(b_router),
             w1_moe=w1_moe, w3_moe=w3_moe, w2_moe=w2_moe, g_final=g_final.reshape(1, D_MODEL))
    n_bp, seq_p, _ = x_prompt.shape
    n_bs, seq_s, _ = x_sample.shape

    mod_p, mod_s = _ada(c_prompt, jnp.repeat(c_sample, seq_s, axis=0), w_ada, b_ada)

    tok_p = _Tokens(n_bp, seq_p, 0, TM_PROMPT, per_row_mod=False, mod=mod_p)
    tok_s = _Tokens(n_bs, seq_s, PAST_LEN, TM_SAMPLE, per_row_mod=True, mod=mod_s)
    prompt = _Stream(tok_p, x_prompt.reshape(tok_p.rows, D_MODEL), None, None)
    sample = _Stream(tok_s, x_sample.reshape(tok_s.rows, D_MODEL), state_ret, state_pool)
    streams = [prompt, sample]

    for i in range(DEPTH):
        g_fin = p['g_final'] if i == DEPTH - 1 else None
        for s in streams:
            _mixer_layer(s, i, p)
        if i % 2 == 0:
            for s in streams:
                s.x = _ffn(s.tok, i, s.x, p['g_ffn'], p['w1_dense'], p['w3_dense'], p['w2_dense'], g_final=g_fin)
        else:
            _moe_layer(streams, i, p, g_fin)

    buf_prompt = jnp.stack([u.reshape(n_bp, seq_p, D_POOL)[:, seq_p - POOL_BUF:] for u in prompt.u_layers])
    buf_sample = jnp.stack([
        jnp.concatenate([state_pool[i], sample.u_layers[i].reshape(n_bs, seq_s, D_POOL)], axis=1)[:, -POOL_BUF:]
        for i in range(DEPTH)])
    return (prompt.x.reshape(x_prompt.shape), sample.x.reshape(x_sample.shape), prompt.new_state, buf_prompt,
            sample.new_state, buf_sample)
```

```python
import functools

import jax
import jax.numpy as jnp
from jax import lax
from jax.experimental import pallas as pl
from jax.experimental.pallas import tpu as pltpu

F32 = jnp.float32
BF16 = jnp.bfloat16

D_MODEL = 2048
DEPTH = 2
PAST_LEN = 16384
RET_HEADS = 8
RET_DK = 128
RET_DV = 256
RET_CHUNK = 128
ROPE_BASE = 10000.0
QK_W = RET_HEADS * RET_DK
V_W = RET_HEADS * RET_DV
POOL_WINDOWS = (2, 4, 8, 16)
D_POOL = 1024
POOL_GW = 256
POOL_BUF = 15
POOL_HALO = 16
N_IN = 2 * QK_W + 2 * V_W + D_POOL + 2 * D_MODEL
N_EXPERTS = 8
EPS = 1e-6
N_MOD = 6

O_Q, O_K, O_V, O_G, O_U = 0, QK_W, 2 * QK_W, 2 * QK_W + V_W, 2 * QK_W + 2 * V_W
O_M1 = O_U + D_POOL
O_M2 = O_M1 + D_MODEL

VMEM_LIMIT_BYTES = 56 * 1024 * 1024
SUBLANES = 8
TN = 512
TN_IN = 1024
TN_MERGE = 1024
TF = 256
TM_PROMPT = 1024
TM_SAMPLE = 512
TM_FFN = 1024
TM_OUT = 2048
ROW_CHUNK = 128
RET_SAMPLE_BATCH = 8
NORM_ROWS = 16
NORM_UNROLL = 16
TL_POOL = 512
TM_MOE = 1280
SUB_MOE = 128
TM_COMBINE = 256
RET_PROMPT_BATCH = 4


def _params(*sem):
    return pltpu.CompilerParams(dimension_semantics=sem, vmem_limit_bytes=VMEM_LIMIT_BYTES)


class _Tokens:
    def __init__(self, n_batch, seq, n_past, tm, per_row_mod, mod=None):
        self.n_batch, self.seq, self.n_past = n_batch, seq, n_past
        self.rows = n_batch * seq
        self.tm = min(tm, self.rows)
        self.per_row_mod = per_row_mod
        if not per_row_mod:
            self.tm = min(self.tm, seq)
            assert seq % self.tm == 0
        assert self.rows % self.tm == 0
        self.n_tiles = self.rows // self.tm
        if mod is not None and not per_row_mod and mod.ndim == 3 and mod.shape[1] != 1:
            mod = mod.reshape(DEPTH * n_batch, 1, N_MOD * D_MODEL)
        self.mod = mod

    def with_tm(self, tm):
        return _Tokens(self.n_batch, self.seq, self.n_past, tm, self.per_row_mod, self.mod)

    def mod_spec(self, layer, chunk, tn=D_MODEL, col_axis=False):
        per_chunk = D_MODEL // tn
        col = (lambda n: chunk * per_chunk + n) if col_axis else (lambda n: chunk * per_chunk)
        if self.per_row_mod:
            if col_axis:
                return pl.BlockSpec((1, self.tm, tn), lambda m, n, *_: (layer, m, col(n)))
            return pl.BlockSpec((1, self.tm, tn), lambda m, *_: (layer, m, col(0)))
        tiles_per_batch = self.seq // self.tm
        base = layer * self.n_batch
        if col_axis:
            return pl.BlockSpec((1, 1, tn), lambda m, n, *_: (base + m // tiles_per_batch, 0, col(n)))
        return pl.BlockSpec((1, 1, tn), lambda m, *_: (base + m // tiles_per_batch, 0, col(0)))

    def rope_tables(self):
        half = RET_DK // 2
        pos = jnp.arange(self.seq, dtype=F32) + float(self.n_past)
        inv = ROPE_BASE ** (-jnp.arange(half, dtype=F32) / half)
        ang = pos[:, None] * inv[None, :]
        cos, sin = jnp.cos(ang), jnp.sin(ang)
        cos_full = jnp.concatenate([cos, cos], axis=1)
        sin_signed = jnp.concatenate([-sin, sin], axis=1)
        if self.per_row_mod:
            cos_full = jnp.tile(cos_full, (self.n_batch, 1))
            sin_signed = jnp.tile(sin_signed, (self.n_batch, 1))
        return cos_full, sin_signed

    def rope_spec(self):
        n_blocks = (self.rows if self.per_row_mod else self.seq) // self.tm
        return pl.BlockSpec((self.tm, RET_DK), lambda m, *_: (m % n_blocks, 0))


def _retention_tables(chunk):
    log_gamma = jnp.log1p(-jnp.exp2(-5.0 - jnp.arange(RET_HEADS, dtype=F32)))
    idx = jnp.arange(chunk, dtype=F32)
    rel = idx[:, None] - idx[None, :]
    dmask = jnp.where(rel[None] >= 0, jnp.exp(log_gamma[:, None, None] * jnp.maximum(rel, 0.0)[None]), 0.0)
    q_dec = jnp.exp(log_gamma[:, None] * (idx + 1.0)[None, :])
    k_dec = jnp.exp(log_gamma[:, None] * (chunk - 1.0 - idx)[None, :])
    chunk_dec = jnp.exp(log_gamma * chunk)
    q_dec = jnp.broadcast_to(q_dec[:, :, None], (RET_HEADS, chunk, RET_DV))
    k_dec = jnp.broadcast_to(k_dec[:, :, None], (RET_HEADS, chunk, RET_DK))
    chunk_dec = jnp.broadcast_to(chunk_dec[:, None, None], (RET_HEADS, 1, RET_DV))
    return dmask, q_dec, k_dec, chunk_dec


def _norm_rows(x_ref, g_ref, sc_ref, sh_ref, h_ref):
    tm = x_ref.shape[0]
    step = min(NORM_ROWS, tm)
    trips = tm // step
    per_row = sc_ref.shape[1] != 1
    if not per_row:
        gain, shift = g_ref[0] * (1.0 + sc_ref[0]), sh_ref[0]

    def body(c, carry):
        rows = pl.ds(pl.multiple_of(c * step, step), step)
        x = x_ref[rows, :]
        y = x * lax.rsqrt(jnp.mean(x * x, axis=-1, keepdims=True) + EPS)
        if per_row:
            y = y * (g_ref[0] * (1.0 + sc_ref[0, rows, :])) + sh_ref[0, rows, :]
        else:
            y = y * gain + shift
        h_ref[rows, :] = y.astype(h_ref.dtype)
        return carry

    lax.fori_loop(0, trips, body, 0, unroll=NORM_UNROLL if trips % NORM_UNROLL == 0 else 1)


def _silu(x):
    return x * jax.nn.sigmoid(x)


def _pack_halves(y):
    half = y.shape[1] // 2
    hi = lax.bitcast_convert_type(y[:, :half].astype(jnp.bfloat16).astype(F32), jnp.uint32)
    lo = lax.bitcast_convert_type(y[:, half:].astype(jnp.bfloat16).astype(F32), jnp.uint32)
    return hi | (lo >> 16)


def _unpack_halves(u):
    return (lax.bitcast_convert_type(u & jnp.uint32(0xFFFF0000), F32),
            lax.bitcast_convert_type(u << 16, F32))


def _ada_kernel(cp_ref, cs_ref, w_ref, b_ref, op_ref, os_ref):
    w = w_ref[0].astype(BF16)
    for c_ref, o_ref in ((cp_ref, op_ref), (cs_ref, os_ref)):
        o_ref[0] = jnp.dot(_silu(c_ref[...]).astype(BF16), w, preferred_element_type=F32) + b_ref[0]


def _ada(c_p, c_s, w_ada, b_ada):
    tn = 1024
    width = N_MOD * D_MODEL
    row_spec = lambda c: pl.BlockSpec(c.shape, lambda l, n: (0, 0))
    out_spec = lambda c: pl.BlockSpec((1, c.shape[0], tn), lambda l, n: (l, 0, n))
    return pl.pallas_call(
        _ada_kernel,
        grid=(DEPTH, width // tn),
        in_specs=[row_spec(c_p), row_spec(c_s),
                  pl.BlockSpec((1, D_MODEL, tn), lambda l, n: (l, 0, n)),
                  pl.BlockSpec((1, 1, tn), lambda l, n: (l, 0, n))],
        out_specs=[out_spec(c_p), out_spec(c_s)],
        out_shape=[jax.ShapeDtypeStruct((DEPTH, c.shape[0], width), F32) for c in (c_p, c_s)],
        compiler_params=_params("arbitrary", "arbitrary"),
        name="ada",
    )(c_p, c_s, w_ada, b_ada.reshape(DEPTH, 1, width))


def _inproj_kernel(x_ref, sc_ref, sh_ref, g_ref, w_ref, cos_ref, sin_ref, o_ref, u_ref, h_scr):
    n = pl.program_id(1)
    tm = x_ref.shape[0]
    rc = min(ROW_CHUNK, tm)

    @pl.when(n == 0)
    def _():
        _norm_rows(x_ref, g_ref, sc_ref, sh_ref, h_scr)

    def project(epilogue):
        w = w_ref[0].astype(BF16)
        for r in range(0, tm, rc):
            rows = slice(r, r + rc)
            epilogue(jnp.dot(h_scr[rows, :], w, preferred_element_type=F32), rows)

    def store(acc, rows):
        o_ref[rows, :] = acc.astype(o_ref.dtype)

    def rope(scale):
        def epilogue(acc, rows):
            for j in range(TN_IN // RET_DK):
                cols = slice(j * RET_DK, (j + 1) * RET_DK)
                xh = acc[:, cols]
                r = xh * cos_ref[rows, :] + pltpu.roll(xh, RET_DK // 2, 1) * sin_ref[rows, :]
                if scale != 1.0:
                    r = r * scale
                o_ref[rows, cols] = r.astype(o_ref.dtype)
        project(epilogue)

    @pl.when(n < O_K // TN_IN)
    def _():
        rope(1.0)

    @pl.when(jnp.logical_and(n >= O_K // TN_IN, n < O_V // TN_IN))
    def _():
        rope(RET_DK ** -0.5)

    @pl.when(jnp.logical_and(n >= O_V // TN_IN, n < O_G // TN_IN))
    def _():
        project(store)

    @pl.when(jnp.logical_and(n >= O_G // TN_IN, n < O_U // TN_IN))
    def _():
        project(lambda acc, rows: store(_silu(acc), rows))

    @pl.when(jnp.logical_and(n >= O_U // TN_IN, n < O_M1 // TN_IN))
    def _():
        def epilogue(acc, rows):
            store(acc, rows)
            u_ref[rows, :] = acc
        project(epilogue)

    @pl.when(n >= O_M1 // TN_IN)
    def _():
        project(lambda acc, rows: store(jax.nn.sigmoid(acc), rows))


def _inproj(tok, layer, x, g_mix, w_in, rope_tabs, out_dtype):
    tm = tok.tm
    tn = TN_IN
    assert all(o % tn == 0 for o in (O_K, O_V, O_G, O_U, O_M1, N_IN))
    n_u = D_POOL // tn
    mod3 = tok.mod
    proj, u = pl.pallas_call(
        _inproj_kernel,
        grid=(tok.n_tiles, N_IN // tn),
        in_specs=[pl.BlockSpec((tm, D_MODEL), lambda m, n: (m, 0)),
                  tok.mod_spec(layer, 1), tok.mod_spec(layer, 0),
                  pl.BlockSpec((1, 1, D_MODEL), lambda m, n: (layer, 0, 0)),
                  pl.BlockSpec((1, D_MODEL, tn), lambda m, n: (layer, 0, n)),
                  tok.rope_spec(), tok.rope_spec()],
        out_specs=[pl.BlockSpec((tm, tn), lambda m, n: (m, n)),
                   pl.BlockSpec((tm, tn), lambda m, n: (m, jnp.clip(n - O_U // tn, 0, n_u - 1)),
                                pipeline_mode=pl.Buffered(1))],
        out_shape=[jax.ShapeDtypeStruct((tok.rows, N_IN), out_dtype),
                   jax.ShapeDtypeStruct((tok.rows, D_POOL), F32)],
        scratch_shapes=[pltpu.VMEM((tm, D_MODEL), BF16)],
        compiler_params=_params("arbitrary", "arbitrary"),
        name="inproj",
    )(x, mod3, mod3, g_mix, w_in, *rope_tabs)
    return proj, u


def _head_norm_gate(o, gret, gate):
    mu = jnp.mean(o, axis=-1, keepdims=True)
    d = o - mu
    var = jnp.mean(d * d, axis=-1, keepdims=True)
    return gate.astype(F32) * (d * lax.rsqrt(var + EPS) * gret)


def _ret_prompt_kernel(q_ref, k_ref, v_ref, g_ref, gret_ref, dmask_ref, qdec_ref, kdec_ref, cdec_ref,
                       *rest):
    y_ref, s_ref = rest[-2:]

    @pl.when(pl.program_id(1) == 0)
    def _():
        s_ref[...] = jnp.zeros_like(s_ref)

    for b in range(q_ref.shape[0]):
        for h in range(RET_HEADS):
            ks, vs = slice(h * RET_DK, (h + 1) * RET_DK), slice(h * RET_DV, (h + 1) * RET_DV)
            q, k, v = q_ref[b, :, ks], k_ref[b, :, ks], v_ref[b, :, vs]
            state = s_ref[0, b, h]
            scores = lax.dot_general(q, k, (((1,), (1,)), ((), ())), preferred_element_type=F32) * dmask_ref[h]
            inner = jnp.dot(scores.astype(BF16), v, preferred_element_type=F32)
            cross = jnp.dot(q, state.astype(BF16), preferred_element_type=F32) * qdec_ref[h]
            k_decayed = (k.astype(F32) * kdec_ref[h]).astype(BF16)
            s_ref[0, b, h] = cdec_ref[h] * state + lax.dot_general(
                k_decayed, v, (((0,), (0,)), ((), ())), preferred_element_type=F32)
            y_ref[b, :, vs] = _head_norm_gate(inner + cross, gret_ref[0, :, vs], g_ref[b, :, vs]).astype(y_ref.dtype)


def _table_specs(tables):
    return [pl.BlockSpec(t.shape, lambda *_: (0, 0, 0)) for t in tables]


def _ret_prompt(tok, layer, proj, g_ret, s_all):
    n_b, seq = tok.n_batch, tok.seq
    c = RET_CHUNK
    assert seq % c == 0
    nc = seq // c
    tables = _retention_tables(c)
    bb = RET_PROMPT_BATCH if n_b % RET_PROMPT_BATCH == 0 else 1
    proj3 = proj.reshape(n_b, seq, N_IN)
    in_specs = [pl.BlockSpec((bb, c, QK_W), lambda b, i: (b, i, O_Q // QK_W)),
                pl.BlockSpec((bb, c, QK_W), lambda b, i: (b, i, O_K // QK_W)),
                pl.BlockSpec((bb, c, V_W), lambda b, i: (b, i, O_V // V_W)),
                pl.BlockSpec((bb, c, V_W), lambda b, i: (b, i, O_G // V_W)),
                pl.BlockSpec((1, 1, V_W), lambda b, i: (layer, 0, 0))] + _table_specs(tables)
    args = [proj3, proj3, proj3, proj3, g_ret, *tables]
    aliases = {}
    if s_all is not None:
        aliases = {len(args): 1}
        in_specs.append(pl.BlockSpec(memory_space=pl.ANY))
        args.append(s_all)
    y, s_all = pl.pallas_call(
        _ret_prompt_kernel,
        grid=(n_b // bb, nc),
        in_specs=in_specs,
        out_specs=[pl.BlockSpec((bb, c, V_W), lambda b, i: (b, i, 0)),
                   pl.BlockSpec((1, bb, RET_HEADS, RET_DK, RET_DV), lambda b, i: (layer, b, 0, 0, 0))],
        out_shape=[jax.ShapeDtypeStruct((n_b, seq, V_W), BF16),
                   jax.ShapeDtypeStruct((DEPTH, n_b, RET_HEADS, RET_DK, RET_DV), F32)],
        input_output_aliases=aliases,
        compiler_params=_params("arbitrary", "arbitrary"),
        name="ret_prompt",
    )(*args)
    return y.reshape(tok.rows, V_W), s_all


def _ret_sample_kernel(q_ref, k_ref, v_ref, g_ref, kt_ref, gret_ref, s0_ref, dm_ref, qdec_ref, kdect_ref, cdec_ref,
                       *rest, bb, seq):
    y_ref, s_ref = rest[-2:]
    hs = RET_HEADS * seq
    reps = lambda a: jnp.concatenate([a] * RET_HEADS, axis=0)
    mask_q = (lax.broadcasted_iota(jnp.int32, (hs, QK_W), 0) // seq
              == lax.broadcasted_iota(jnp.int32, (hs, QK_W), 1) // RET_DK)
    mask_v = (lax.broadcasted_iota(jnp.int32, (hs, V_W), 0) // seq
              == lax.broadcasted_iota(jnp.int32, (hs, V_W), 1) // RET_DV)
    nt = (((1,), (1,)), ((), ()))
    for b in range(bb):
        rows_b = slice(b * seq, (b + 1) * seq)
        q, k, v = q_ref[rows_b, :], k_ref[rows_b, :], v_ref[rows_b, :]
        q_blk = jnp.where(mask_q, reps(q), 0.0).astype(BF16)
        scores = lax.dot_general(q_blk, reps(k).astype(BF16), nt, preferred_element_type=F32) * dm_ref[...]
        v_stack = jnp.concatenate([v[:, h * RET_DV:(h + 1) * RET_DV] for h in range(RET_HEADS)], axis=0)
        inner = jnp.dot(scores.astype(BF16), v_stack.astype(BF16), preferred_element_type=F32)
        state = s0_ref[0, b]
        cross = jnp.dot(q_blk, state.reshape(QK_W, RET_DV).astype(BF16), preferred_element_type=F32)
        o = inner + cross * qdec_ref[...]
        mu = jnp.mean(o, axis=-1, keepdims=True)
        d = o - mu
        normed = d * lax.rsqrt(jnp.mean(d * d, axis=-1, keepdims=True) + EPS)
        k_dec_t = (kt_ref[b] * kdect_ref[...]).astype(BF16)
        v_blk = jnp.where(mask_v, reps(v), 0.0).astype(BF16)
        upd = jnp.dot(k_dec_t, v_blk, preferred_element_type=F32)
        for h in range(RET_HEADS):
            vs = slice(h * RET_DV, (h + 1) * RET_DV)
            y_ref[rows_b, vs] = g_ref[rows_b, vs] * (normed[h * seq:(h + 1) * seq] * gret_ref[0, :, vs])
            s_ref[0, b, h] = cdec_ref[h] * state[h] + upd[:, vs]


def _ret_sample(tok, layer, proj, g_ret, state_ret, s_all):
    n_b, seq = tok.n_batch, tok.seq
    bb = RET_SAMPLE_BATCH if n_b % RET_SAMPLE_BATCH == 0 else 4
    assert n_b % bb == 0
    hs = RET_HEADS * seq
    dmask, q_dec, k_dec, chunk_dec = _retention_tables(seq)
    dm_blk = (dmask[:, :, None, :] * jnp.eye(RET_HEADS, dtype=F32)[:, None, :, None]).reshape(hs, hs)
    tables = [dm_blk, q_dec.reshape(hs, RET_DV), k_dec[:, :, 0].reshape(1, hs), chunk_dec]
    k_t = proj[:, O_K:O_K + QK_W].reshape(n_b, seq, RET_HEADS, RET_DK).transpose(0, 3, 2, 1).reshape(n_b, RET_DK, hs)
    state_spec = pl.BlockSpec((1, bb, RET_HEADS, RET_DK, RET_DV), lambda i: (layer, i, 0, 0, 0))
    in_specs = [pl.BlockSpec((bb * seq, QK_W), lambda i: (i, O_Q // QK_W)),
                pl.BlockSpec((bb * seq, QK_W), lambda i: (i, O_K // QK_W)),
                pl.BlockSpec((bb * seq, V_W), lambda i: (i, O_V // V_W)),
                pl.BlockSpec((bb * seq, V_W), lambda i: (i, O_G // V_W)),
                pl.BlockSpec((bb, RET_DK, hs), lambda i: (i, 0, 0)),
                pl.BlockSpec((1, 1, V_W), lambda i: (layer, 0, 0)),
                state_spec] + [pl.BlockSpec(t.shape, lambda i, nd=t.ndim: (0,) * nd) for t in tables]
    args = [proj, proj, proj, proj, k_t, g_ret, state_ret, *tables]
    aliases = {}
    if s_all is not None:
        aliases = {len(args): 1}
        in_specs.append(pl.BlockSpec(memory_space=pl.ANY))
        args.append(s_all)
    y, s_all = pl.pallas_call(
        functools.partial(_ret_sample_kernel, bb=bb, seq=seq),
        grid=(n_b // bb,),
        in_specs=in_specs,
        out_specs=[pl.BlockSpec((bb * seq, V_W), lambda i: (i, 0)), state_spec],
        out_shape=[jax.ShapeDtypeStruct((tok.rows, V_W), F32),
                   jax.ShapeDtypeStruct(state_ret.shape, F32)],
        input_output_aliases=aliases,
        compiler_params=_params("arbitrary"),
        name="ret_sample",
    )(*args)
    return y, s_all


def _pool_map(pooled, wmap_ref, scale_ref, gi):
    cols = slice(gi * POOL_GW, (gi + 1) * POOL_GW)
    mixed = jnp.dot(pooled.astype(BF16), wmap_ref[0, gi].astype(BF16), preferred_element_type=F32)
    return mixed * scale_ref[0, :, cols]


def _pool_prompt_kernel(u_ref, halo_ref, wmap_ref, scale_ref, o_ref, uf_scr, a_scr, b_scr, *, tl):
    pad = SUBLANES
    top = pad + POOL_HALO
    first = pl.program_id(1) == 0
    uf_scr[0:pad, :] = jnp.zeros((pad, D_POOL), F32)
    a_scr[0:pad, :] = jnp.zeros((pad, POOL_GW), F32)
    b_scr[0:pad, :] = jnp.zeros((pad, POOL_GW), F32)
    uf_scr[pad:top, :] = jnp.where(first, 0.0, halo_ref[...])
    uf_scr[top:, :] = u_ref[...]
    n_rows = POOL_HALO + tl
    row = lax.broadcasted_iota(jnp.int32, (tl, POOL_GW), 0) + pl.program_id(1) * tl
    for gi, w in enumerate(POOL_WINDOWS):
        cols = slice(gi * POOL_GW, (gi + 1) * POOL_GW)
        src = lambda lo, cols=cols: uf_scr[lo:lo + n_rows, cols]
        dst_scr, other = a_scr, b_scr
        shift = 1
        while shift < w:
            dst_scr[pad:, :] = src(pad) + src(pad - shift)
            src = lambda lo, s=dst_scr: s[lo:lo + n_rows, :]
            dst_scr, other = other, dst_scr
            shift *= 2
        acc = src(pad)[POOL_HALO:, :]
        u = uf_scr[top:, cols]
        cnt = jnp.minimum(w, row + 1).astype(F32)
        o_ref[:, cols] = _pool_map(acc / cnt - u, wmap_ref, scale_ref, gi).astype(o_ref.dtype)


def _pool_prompt(tok, layer, u, w_pool_map, pool_scale):
    n_b, seq = tok.n_batch, tok.seq
    tl = min(TL_POOL, seq)
    assert seq % tl == 0 and tl % POOL_HALO == 0
    nl = seq // tl
    per = tl // POOL_HALO
    return pl.pallas_call(
        functools.partial(_pool_prompt_kernel, tl=tl),
        grid=(n_b, nl),
        in_specs=[pl.BlockSpec((tl, D_POOL), lambda b, l: (b * nl + l, 0)),
                  pl.BlockSpec((POOL_HALO, D_POOL), lambda b, l: (jnp.maximum((b * nl + l) * per - 1, 0), 0)),
                  pl.BlockSpec((1,) + w_pool_map.shape[1:], lambda b, l: (layer, 0, 0, 0)),
                  pl.BlockSpec((1, 1, D_POOL), lambda b, l: (layer, 0, 0))],
        out_specs=pl.BlockSpec((tl, D_POOL), lambda b, l: (b * nl + l, 0)),
        out_shape=jax.ShapeDtypeStruct((tok.rows, D_POOL), BF16),
        scratch_shapes=[pltpu.VMEM((SUBLANES + POOL_HALO + tl, D_POOL), F32),
                        pltpu.VMEM((SUBLANES + POOL_HALO + tl, POOL_GW), F32),
                        pltpu.VMEM((SUBLANES + POOL_HALO + tl, POOL_GW), F32)],
        compiler_params=_params("arbitrary", "arbitrary"),
        name="pool_prompt",
    )(u, u, w_pool_map, pool_scale)


def _pool_sample_kernel(buf_ref, u_ref, wmap_ref, scale_ref, o_ref, *, seq, n_b):
    def row(j, cols):
        return buf_ref[j, :, cols] if j < POOL_BUF else u_ref[j - POOL_BUF, :, cols]

    for gi, w in enumerate(POOL_WINDOWS):
        cols = slice(gi * POOL_GW, (gi + 1) * POOL_GW)
        for t in range(seq):
            acc = row(POOL_BUF + t, cols)
            for j in range(1, w):
                acc = acc + row(POOL_BUF + t - j, cols)
            pooled = acc / float(w) - row(POOL_BUF + t, cols)
            o_ref[t, :, cols] = _pool_map(pooled, wmap_ref, scale_ref, gi)


def _pool_sample(tok, layer, u, buf_tm, w_pool_map, pool_scale):
    n_b, seq = tok.n_batch, tok.seq
    assert tok.n_past >= max(POOL_WINDOWS)
    u_tm = u.reshape(n_b, seq, D_POOL).transpose(1, 0, 2)
    mixed_tm = pl.pallas_call(
        functools.partial(_pool_sample_kernel, seq=seq, n_b=n_b),
        grid=(1,),
        in_specs=[pl.BlockSpec(buf_tm.shape, lambda i: (0, 0, 0)),
                  pl.BlockSpec(u_tm.shape, lambda i: (0, 0, 0)),
                  pl.BlockSpec((1,) + w_pool_map.shape[1:], lambda i: (layer, 0, 0, 0)),
                  pl.BlockSpec((1, 1, D_POOL), lambda i: (layer, 0, 0))],
        out_specs=pl.BlockSpec(u_tm.shape, lambda i: (0, 0, 0)),
        out_shape=jax.ShapeDtypeStruct(u_tm.shape, F32),
        compiler_params=_params("arbitrary"),
        name="pool_sample",
    )(buf_tm, u_tm, w_pool_map, pool_scale)
    return mixed_tm.transpose(1, 0, 2).reshape(tok.rows, D_POOL)


def _merge_kernel(yret_ref, mixed_ref, wr_ref, wp_ref, g1_ref, g2_ref, o_ref, wr_scr, wp_scr):
    @pl.when(pl.program_id(1) == 0)
    def _():
        wr_scr[...] = wr_ref[0].astype(BF16)
        wp_scr[...] = wp_ref[0].astype(BF16)

    ret = jnp.dot(yret_ref[...].astype(BF16), wr_scr[...], preferred_element_type=F32)
    pool = jnp.dot(mixed_ref[...].astype(BF16), wp_scr[...], preferred_element_type=F32)
    o_ref[...] = (g1_ref[...].astype(F32) * ret + g2_ref[...].astype(F32) * pool).astype(o_ref.dtype)


def _merge(tok, layer, yret, mixed, proj, w_ret_out, w_pool_out):
    tm, tn = tok.tm, TN_MERGE
    once = pl.Buffered(1)
    return pl.pallas_call(
        _merge_kernel,
        grid=(D_MODEL // tn, tok.n_tiles),
        in_specs=[pl.BlockSpec((tm, V_W), lambda n, m: (m, 0)),
                  pl.BlockSpec((tm, D_POOL), lambda n, m: (m, 0)),
                  pl.BlockSpec((1, V_W, tn), lambda n, m: (layer, 0, n), pipeline_mode=once),
                  pl.BlockSpec((1, D_POOL, tn), lambda n, m: (layer, 0, n), pipeline_mode=once),
                  pl.BlockSpec((tm, tn), lambda n, m: (m, O_M1 // tn + n)),
                  pl.BlockSpec((tm, tn), lambda n, m: (m, O_M2 // tn + n))],
        out_specs=pl.BlockSpec((tm, tn), lambda n, m: (m, n)),
        out_shape=jax.ShapeDtypeStruct((tok.rows, D_MODEL), BF16),
        scratch_shapes=[pltpu.VMEM((V_W, tn), BF16), pltpu.VMEM((D_POOL, tn), BF16)],
        compiler_params=_params("arbitrary", "arbitrary"),
        name="merge",
    )(yret, mixed, w_ret_out, w_pool_out, proj, proj)


def _wo_kernel(merged_ref, w_ref, x_ref, gt_ref, o_ref):
    mix = jnp.dot(merged_ref[...], w_ref[0].astype(BF16), preferred_element_type=F32)
    o_ref[...] = x_ref[...] + (1.0 + gt_ref[0]) * mix


def _wo(tok, layer, merged, x, w_o):
    tok = tok.with_tm(TM_OUT)
    tm = tok.tm
    mod3 = tok.mod
    return pl.pallas_call(
        _wo_kernel,
        grid=(tok.n_tiles, D_MODEL // TN),
        in_specs=[pl.BlockSpec((tm, D_MODEL), lambda m, n: (m, 0)),
                  pl.BlockSpec((1, D_MODEL, TN), lambda m, n: (layer, 0, n)),
                  pl.BlockSpec((tm, TN), lambda m, n: (m, n)),
                  tok.mod_spec(layer, 2, tn=TN, col_axis=True)],
        out_specs=pl.BlockSpec((tm, TN), lambda m, n: (m, n)),
        out_shape=jax.ShapeDtypeStruct((tok.rows, D_MODEL), F32),
        compiler_params=_params("arbitrary", "arbitrary"),
        name="wo",
    )(merged, w_o, x, mod3)


def _mat(w_ref):
    return w_ref[(0,) * (len(w_ref.shape) - 2)]


def _final_norm(y, gfin_ref):
    return y * lax.rsqrt(jnp.mean(y * y, axis=-1, keepdims=True) + EPS) * gfin_ref[...]


def _ffn_kernel(x_ref, sc_ref, sh_ref, gt_ref, g_ref, w1_ref, w3_ref, w2_ref, *rest, n_f, with_final):
    gfin_ref = rest[0] if with_final else None
    o_ref, h_scr = rest[-2:]
    f = pl.program_id(1)

    @pl.when(f == 0)
    def _():
        _norm_rows(x_ref, g_ref, sc_ref, sh_ref, h_scr)
        o_ref[...] = jnp.zeros_like(o_ref)

    h = h_scr[...]
    a = jnp.dot(h, _mat(w1_ref).astype(BF16), preferred_element_type=F32)
    b = jnp.dot(h, _mat(w3_ref).astype(BF16), preferred_element_type=F32)
    o_ref[...] += jnp.dot((_silu(a) * b).astype(BF16), _mat(w2_ref).astype(BF16), preferred_element_type=F32)

    @pl.when(f == n_f - 1)
    def _():
        y = x_ref[...] + (1.0 + gt_ref[0]) * o_ref[...]
        o_ref[...] = _final_norm(y, gfin_ref) if with_final else y


def _ffn(tok, layer, x, g_ffn, w1, w3, w2, g_final=None):
    j = layer // 2
    d_ff = w1.shape[-1]
    tok_f = tok.with_tm(TM_FFN)
    tm, n_tiles = tok_f.tm, tok_f.n_tiles
    mod3 = tok_f.mod
    assert d_ff % TF == 0
    n_f = d_ff // TF
    in_specs = [pl.BlockSpec((tm, D_MODEL), lambda m, f: (m, 0)),
                tok_f.mod_spec(layer, 4), tok_f.mod_spec(layer, 3), tok_f.mod_spec(layer, 5),
                pl.BlockSpec((1, 1, D_MODEL), lambda m, f: (layer, 0, 0)),
                pl.BlockSpec((1, D_MODEL, TF), lambda m, f: (j, 0, f)),
                pl.BlockSpec((1, D_MODEL, TF), lambda m, f: (j, 0, f)),
                pl.BlockSpec((1, TF, D_MODEL), lambda m, f: (j, f, 0))]
    args = [x, mod3, mod3, mod3, g_ffn, w1, w3, w2]
    if g_final is not None:
        in_specs.append(pl.BlockSpec((1, D_MODEL), lambda m, f: (0, 0)))
        args.append(g_final)
    return pl.pallas_call(
        functools.partial(_ffn_kernel, n_f=n_f, with_final=g_final is not None),
        grid=(n_tiles, n_f),
        in_specs=in_specs,
        out_specs=pl.BlockSpec((tm, D_MODEL), lambda m, f: (m, 0)),
        out_shape=jax.ShapeDtypeStruct((tok.rows, D_MODEL), F32),
        scratch_shapes=[pltpu.VMEM((tm, D_MODEL), BF16)],
        compiler_params=_params("arbitrary", "arbitrary"),
        name="ffn",
    )(*args)


def _router_kernel(x_ref, sc_ref, sh_ref, g_ref, w_ref, b_ref, *rest):
    esel_ref, wsel_ref, h_ref, h_scr = rest[-4:]
    _norm_rows(x_ref, g_ref, sc_ref, sh_ref, h_scr)
    h = h_scr[...]
    logits = jnp.dot(h, w_ref[0].astype(BF16), preferred_element_type=F32) + b_ref[0]
    idx = lax.broadcasted_iota(jnp.int32, logits.shape, 1)
    top1 = jnp.max(logits, axis=-1, keepdims=True)
    i1 = jnp.min(jnp.where(logits == top1, idx, N_EXPERTS), axis=-1, keepdims=True)
    others = jnp.where(idx == i1, -jnp.inf, logits)
    top2 = jnp.max(others, axis=-1, keepdims=True)
    i2 = jnp.min(jnp.where(others == top2, idx, N_EXPERTS), axis=-1, keepdims=True)
    e2 = jnp.exp(top2 - top1)
    denom = 1.0 + e2
    w1, w2 = 1.0 / denom, e2 / denom
    esel_ref[...] = jnp.where(idx == 0, i1, i2)
    wsel_ref[...] = jnp.where(idx == 0, w1, w2)
    h_ref[...] = _pack_halves(h.astype(F32))


def _router(tok, layer, x, g_ffn, w_router, b_router, h_all=None, row0=0, rows_all=None):
    j = layer // 2
    tm = tok.tm
    rows_all = tok.rows if rows_all is None else rows_all
    assert row0 % tm == 0
    small = pl.BlockSpec((tm, N_EXPERTS), lambda m: (m, 0))
    mod3 = tok.mod
    in_specs = [pl.BlockSpec((tm, D_MODEL), lambda m: (m, 0)),
                tok.mod_spec(layer, 4), tok.mod_spec(layer, 3),
                pl.BlockSpec((1, 1, D_MODEL), lambda m: (layer, 0, 0)),
                pl.BlockSpec((1, D_MODEL, N_EXPERTS), lambda m: (j, 0, 0)),
                pl.BlockSpec((1, 1, N_EXPERTS), lambda m: (j, 0, 0))]
    args = [x, mod3, mod3, g_ffn, w_router, b_router]
    aliases = {}
    if h_all is not None:
        aliases = {len(args): 2}
        in_specs.append(pl.BlockSpec(memory_space=pl.ANY))
        args.append(h_all)
    return pl.pallas_call(
        _router_kernel,
        grid=(tok.n_tiles,),
        in_specs=in_specs,
        out_specs=[small, small, pl.BlockSpec((tm, D_MODEL // 2), lambda m: (row0 // tm + m, 0))],
        out_shape=[jax.ShapeDtypeStruct((tok.rows, N_EXPERTS), jnp.int32),
                   jax.ShapeDtypeStruct((tok.rows, N_EXPERTS), F32),
                   jax.ShapeDtypeStruct((rows_all, D_MODEL // 2), jnp.uint32)],
        input_output_aliases=aliases,
        scratch_shapes=[pltpu.VMEM((tm, D_MODEL), BF16)],
        compiler_params=_params("arbitrary"),
        name="router",
    )(*args)


def _route_plan(choice, tm):
    rows = choice.shape[0]
    n_pairs = 2 * rows
    n_tiles = n_pairs // tm + N_EXPERTS
    e_flat = choice.T.reshape(n_pairs)
    onehot = (e_flat[:, None] == jnp.arange(N_EXPERTS, dtype=jnp.int32)[None, :]).astype(jnp.int32)
    csum = jnp.cumsum(onehot, axis=0)
    rank = jnp.sum(csum * onehot, axis=1) - 1
    count = csum[-1]
    tiles_per = (count + tm - 1) // tm
    rows_per = jnp.maximum((((count + jnp.maximum(tiles_per, 1) - 1) // jnp.maximum(tiles_per, 1)) + 7) // 8 * 8, 8)
    tile_end = jnp.cumsum(tiles_per)
    tile_start = tile_end - tiles_per
    pos = (tile_start[e_flat] + rank // rows_per[e_flat]) * tm + rank % rows_per[e_flat]
    token = jnp.arange(n_pairs, dtype=jnp.int32) % rows
    src_row = jnp.zeros((n_tiles * tm,), jnp.int32).at[pos].set(token, unique_indices=True, mode='promise_in_bounds')
    tile_id = jnp.arange(n_tiles, dtype=jnp.int32)
    n_used = tile_end[-1]
    tile_expert = jnp.minimum(jnp.sum(tile_id[:, None] >= tile_end[None, :], axis=1), N_EXPERTS - 1).astype(jnp.int32)
    tile_rows = jnp.clip(count[tile_expert] - (tile_id - tile_start[tile_expert]) * rows_per[tile_expert],
                         0, rows_per[tile_expert])
    tile_rows = jnp.where(tile_id < n_used, tile_rows, 0).astype(jnp.int32)
    last_expert = tile_expert[jnp.maximum(n_used - 1, 0)]
    tile_expert = jnp.where(tile_rows > 0, tile_expert, last_expert)
    return tile_expert, tile_rows, src_row, pos.astype(jnp.int32)


def _moe_kernel(te_ref, nr_ref, src_ref, h_hbm, w1_ref, w3_ref, w2_ref, o_ref, rows_scr, h_scr, acc_scr, sem,
                *, tm, sub, n_f):
    del te_ref
    m, f = pl.program_id(0), pl.program_id(1)
    half = D_MODEL // 2
    group = rows_scr.shape[1]

    def n_groups(tile):
        return (nr_ref[tile] + group - 1) // group

    def start_rows(tile):
        def body(i, c):
            for j in range(group):
                row = src_ref[tile * tm + i * group + j]
                pltpu.make_async_copy(h_hbm.at[pl.ds(row, 1)], rows_scr.at[i, pl.ds(j, 1)], sem).start()
            return c
        lax.fori_loop(0, n_groups(tile), body, 0)

    def wait_rows(tile):
        def body(i, c):
            pltpu.make_async_copy(h_hbm.at[pl.ds(0, group)], rows_scr.at[i], sem).wait()
            return c
        lax.fori_loop(0, n_groups(tile), body, 0)

    @pl.when(f == 0)
    def _():
        acc_scr[...] = jnp.zeros_like(acc_scr)

        @pl.when(m == 0)
        def _():
            rows_scr[...] = jnp.zeros_like(rows_scr)
            start_rows(0)

        wait_rows(m)
        first, second = _unpack_halves(rows_scr[...].reshape(tm, half))
        h_scr[:, :half] = first.astype(BF16)
        h_scr[:, half:] = second.astype(BF16)

        @pl.when(m + 1 < pl.num_programs(0))
        def _():
            start_rows(m + 1)

    n_sub = (nr_ref[m] + sub - 1) // sub
    for n in range(1, tm // sub + 1):
        @pl.when(n_sub == n)
        def _(n=n):
            h = h_scr[0:n * sub, :]
            a = jnp.dot(h, _mat(w1_ref).astype(BF16), preferred_element_type=F32)
            b = jnp.dot(h, _mat(w3_ref).astype(BF16), preferred_element_type=F32)
            acc_scr[0:n * sub, :] += jnp.dot((_silu(a) * b).astype(BF16), _mat(w2_ref).astype(BF16),
                                             preferred_element_type=F32)

    @pl.when(f == n_f - 1)
    def _():
        o_ref[...] = _pack_halves(acc_scr[...])


def _moe_sparse(layer, h, plan, w1, w3, w2, tm):
    j = layer // 2
    tile_expert, tile_rows, src_row, _ = plan
    n_tiles = tile_expert.shape[0]
    d_ff = w1.shape[3]
    assert d_ff % TF == 0 and tm % SUB_MOE == 0
    n_f = d_ff // TF

    def f_sel(m, f, nr):
        return jnp.where(nr[m] > 0, f, n_f - 1)

    return pl.pallas_call(
        functools.partial(_moe_kernel, tm=tm, sub=SUB_MOE, n_f=n_f),
        grid_spec=pltpu.PrefetchScalarGridSpec(
            num_scalar_prefetch=3,
            grid=(n_tiles, n_f),
            in_specs=[pl.BlockSpec(memory_space=pl.ANY),
                      pl.BlockSpec((1, 1, D_MODEL, TF), lambda m, f, te, nr, sr: (j, te[m], 0, f_sel(m, f, nr))),
                      pl.BlockSpec((1, 1, D_MODEL, TF), lambda m, f, te, nr, sr: (j, te[m], 0, f_sel(m, f, nr))),
                      pl.BlockSpec((1, 1, TF, D_MODEL), lambda m, f, te, nr, sr: (j, te[m], f_sel(m, f, nr), 0))],
            out_specs=pl.BlockSpec((tm, D_MODEL // 2), lambda m, f, te, nr, sr: (m, 0)),
            scratch_shapes=[pltpu.VMEM((tm // SUBLANES, SUBLANES, D_MODEL // 2), jnp.uint32),
                            pltpu.VMEM((tm, D_MODEL), BF16),
                            pltpu.VMEM((tm, D_MODEL), F32), pltpu.SemaphoreType.DMA(())]),
        out_shape=jax.ShapeDtypeStruct((n_tiles * tm, D_MODEL // 2), jnp.uint32),
        compiler_params=_params("arbitrary", "arbitrary"),
        name="moe",
    )(tile_expert, tile_rows, src_row, h, w1, w3, w2)


def _combine_kernel(pos_ref, y_hbm, x_ref, gt_ref, w_ref, *rest, tm, row0, rows_all, with_final):
    gfin_ref = rest[0] if with_final else None
    o_ref, y_scr, sems = rest[-3:]
    i, n = pl.program_id(0), pl.num_programs(0)

    half = D_MODEL // 2
    group = y_scr.shape[3]

    def start_rows(tile, slot):
        def body(g, c):
            for k in range(2):
                for j in range(group):
                    slot_of_pair = pos_ref[k * rows_all + row0 + tile * tm + g * group + j]
                    pltpu.make_async_copy(y_hbm.at[pl.ds(slot_of_pair, 1)], y_scr.at[slot, k, g, pl.ds(j, 1)],
                                          sems.at[slot]).start()
            return c
        lax.fori_loop(0, tm // group, body, 0)

    @pl.when(i == 0)
    def _():
        start_rows(0, 0)

    @pl.when(i + 1 < n)
    def _():
        start_rows(i + 1, (i + 1) % 2)

    slot = i % 2
    def wait_group(g, c):
        for k in range(2):
            pltpu.make_async_copy(y_hbm.at[pl.ds(0, group)], y_scr.at[slot, k, g], sems.at[slot]).wait()
        return c

    lax.fori_loop(0, tm // group, wait_group, 0)
    w = w_ref[...]
    first0, second0 = _unpack_halves(y_scr[slot, 0].reshape(tm, half))
    first1, second1 = _unpack_halves(y_scr[slot, 1].reshape(tm, half))
    f = jnp.concatenate([w[:, 0:1] * first0 + w[:, 1:2] * first1,
                         w[:, 0:1] * second0 + w[:, 1:2] * second1], axis=1)
    y = x_ref[...] + (1.0 + gt_ref[0]) * f
    o_ref[...] = _final_norm(y, gfin_ref) if with_final else y


def _combine(tok, layer, y_sorted, pos, row0, x, wsel, g_final):
    tok_c = tok.with_tm(TM_COMBINE)
    tm = tok_c.tm
    mod3 = tok_c.mod
    in_specs = [pl.BlockSpec(memory_space=pl.ANY),
                pl.BlockSpec((tm, D_MODEL), lambda m, pos: (m, 0)),
                tok_c.mod_spec(layer, 5),
                pl.BlockSpec((tm, N_EXPERTS), lambda m, pos: (m, 0))]
    args = [pos, y_sorted, x, mod3, wsel]
    if g_final is not None:
        in_specs.append(pl.BlockSpec((1, D_MODEL), lambda m, pos: (0, 0)))
        args.append(g_final)
    return pl.pallas_call(
        functools.partial(_combine_kernel, tm=tm, row0=row0, rows_all=pos.shape[0] // 2,
                          with_final=g_final is not None),
        grid_spec=pltpu.PrefetchScalarGridSpec(
            num_scalar_prefetch=1,
            grid=(tok_c.n_tiles,),
            in_specs=in_specs,
            out_specs=pl.BlockSpec((tm, D_MODEL), lambda m, pos: (m, 0)),
            scratch_shapes=[pltpu.VMEM((2, 2, tm // SUBLANES, SUBLANES, D_MODEL // 2), jnp.uint32),
                            pltpu.SemaphoreType.DMA((2,))]),
        out_shape=jax.ShapeDtypeStruct((tok.rows, D_MODEL), F32),
        compiler_params=_params("arbitrary"),
        name="combine",
    )(*args)


class _Stream:
    def __init__(self, tok, x, ret_state, pool_state):
        self.tok, self.x, self.ret_state, self.pool_state = tok, x, ret_state, pool_state
        self.rope_tabs = tok.rope_tables()
        self.new_state = None
        self.u_layers = []


def _mixer_layer(s, i, p):
    tok = s.tok
    is_prompt = s.ret_state is None
    proj, u = _inproj(tok, i, s.x, p['g_mix'], p['w_in'], s.rope_tabs, BF16 if is_prompt else F32)
    s.u_layers.append(u)
    if is_prompt:
        yret, s.new_state = _ret_prompt(tok, i, proj, p['g_ret'], s.new_state)
        mixed = _pool_prompt(tok, i, u, p['w_pool_map'], p['pool_scale'])
    else:
        yret, s.new_state = _ret_sample(tok, i, proj, p['g_ret'], s.ret_state, s.new_state)
        mixed = _pool_sample(tok, i, u, s.pool_state[i].transpose(1, 0, 2), p['w_pool_map'], p['pool_scale'])
    merged = _merge(tok, i, yret, mixed, proj, p['w_ret_out'], p['w_pool_out'])
    s.x = _wo(tok, i, merged, s.x, p['w_o'])


def _moe_layer(streams, i, p, g_final):
    rows_all = sum(s.tok.rows for s in streams)
    h_all, row0, choices, weights, starts = None, 0, [], [], []
    for s in streams:
        esel, wsel, h_all = _router(s.tok, i, s.x, p['g_ffn'], p['w_router'], p['b_router'],
                                    h_all=h_all, row0=row0, rows_all=rows_all)
        choices.append(esel[:, :2])
        weights.append(wsel)
        starts.append(row0)
        row0 += s.tok.rows
    plan = _route_plan(jnp.concatenate(choices, axis=0), TM_MOE)
    y_sorted = _moe_sparse(i, h_all, plan, p['w1_moe'], p['w3_moe'], p['w2_moe'], TM_MOE)
    for s, wsel, start in zip(streams, weights, starts):
        s.x = _combine(s.tok, i, y_sorted, plan[3], start, s.x, wsel, g_final)


def kernel(x_prompt, x_sample, c_prompt, c_sample, state_ret, state_pool, w_ada, b_ada, g_mix, w_in, g_ret,
           w_ret_out, w_pool_map, pool_scale, w_pool_out, w_o, g_ffn, w1_dense, w3_dense, w2_dense, w_router,
           b_router, w1_moe, w3_moe, w2_moe, g_final):
    row3 = lambda a: a.reshape(a.shape[0], 1, a.shape[1])
    p = dict(g_mix=row3(g_mix), w_in=w_in, g_ret=row3(g_ret), w_ret_out=w_ret_out, w_pool_map=w_pool_map,
             pool_scale=row3(pool_scale), w_pool_out=w_pool_out, w_o=w_o, g_ffn=row3(g_ffn),
             w1_dense=w1_dense, w3_dense=w3_dense, w2_dense=w2_dense, w_router=w_router, b_router=row3(b_router),
             w1_moe=w1_moe, w3_moe=w3_moe, w2_moe=w2_moe, g_final=g_final.reshape(1, D_MODEL))
    n_bp, seq_p, _ = x_prompt.shape
    n_bs, seq_s, _ = x_sample.shape

    mod_p, mod_s = _ada(c_prompt, jnp.repeat(c_sample, seq_s, axis=0), w_ada, b_ada)

    tok_p = _Tokens(n_bp, seq_p, 0, TM_PROMPT, per_row_mod=False, mod=mod_p)
    tok_s = _Tokens(n_bs, seq_s, PAST_LEN, TM_SAMPLE, per_row_mod=True, mod=mod_s)
    prompt = _Stream(tok_p, x_prompt.reshape(tok_p.rows, D_MODEL), None, None)
    sample = _Stream(tok_s, x_sample.reshape(tok_s.rows, D_MODEL), state_ret, state_pool)
    streams = [prompt, sample]

    for i in range(DEPTH):
        g_fin = p['g_final'] if i == DEPTH - 1 else None
        for s in streams:
            _mixer_layer(s, i, p)
        if i % 2 == 0:
            for s in streams:
                s.x = _ffn(s.tok, i, s.x, p['g_ffn'], p['w1_dense'], p['w3_dense'], p['w2_dense'], g_final=g_fin)
        else:
            _moe_layer(streams, i, p, g_fin)

    buf_prompt = jnp.stack([u.reshape(n_bp, seq_p, D_POOL)[:, seq_p - POOL_BUF:] for u in prompt.u_layers])
    buf_sample = jnp.stack([
        jnp.concatenate([state_pool[i], sample.u_layers[i].reshape(n_bs, seq_s, D_POOL)], axis=1)[:, -POOL_BUF:]
        for i in range(DEPTH)])
    return (prompt.x.reshape(x_prompt.shape), sample.x.reshape(x_sample.shape), prompt.new_state, buf_prompt,
            sample.new_state, buf_sample)
```

```python
import functools

import jax
import jax.numpy as jnp
from jax import lax
from jax.experimental import pallas as pl
from jax.experimental.pallas import tpu as pltpu

F32 = jnp.float32
BF16 = jnp.bfloat16

D_MODEL = 2048
DEPTH = 2
PAST_LEN = 16384
RET_HEADS = 8
RET_DK = 128
RET_DV = 256
RET_CHUNK = 128
ROPE_BASE = 10000.0
QK_W = RET_HEADS * RET_DK
V_W = RET_HEADS * RET_DV
POOL_WINDOWS = (2, 4, 8, 16)
D_POOL = 1024
POOL_GW = 256
POOL_BUF = 15
POOL_HALO = 16
N_IN = 2 * QK_W + 2 * V_W + D_POOL + 2 * D_MODEL
N_EXPERTS = 8
EPS = 1e-6
N_MOD = 6

O_Q, O_K, O_V, O_G, O_U = 0, QK_W, 2 * QK_W, 2 * QK_W + V_W, 2 * QK_W + 2 * V_W
O_M1 = O_U + D_POOL
O_M2 = O_M1 + D_MODEL

VMEM_LIMIT_BYTES = 56 * 1024 * 1024
SUBLANES = 8
TN = 512
TN_IN = 1024
TN_MERGE = 1024
TF = 256
TM_PROMPT = 1024
TM_SAMPLE = 512
TM_FFN = 1024
TM_OUT = 2048
ROW_CHUNK = 256
RET_SAMPLE_BATCH = 8
NORM_ROWS = 16
NORM_UNROLL = 16
TL_POOL = 512
TM_MOE = 1280
SUB_MOE = 128
TM_COMBINE = 256
RET_PROMPT_BATCH = 4


def _params(*sem):
    return pltpu.CompilerParams(dimension_semantics=sem, vmem_limit_bytes=VMEM_LIMIT_BYTES)


class _Tokens:
    def __init__(self, n_batch, seq, n_past, tm, per_row_mod, mod=None):
        self.n_batch, self.seq, self.n_past = n_batch, seq, n_past
        self.rows = n_batch * seq
        self.tm = min(tm, self.rows)
        self.per_row_mod = per_row_mod
        if not per_row_mod:
            self.tm = min(self.tm, seq)
            assert seq % self.tm == 0
        assert self.rows % self.tm == 0
        self.n_tiles = self.rows // self.tm
        if mod is not None and not per_row_mod and mod.ndim == 3 and mod.shape[1] != 1:
            mod = mod.reshape(DEPTH * n_batch, 1, N_MOD * D_MODEL)
        self.mod = mod

    def with_tm(self, tm):
        return _Tokens(self.n_batch, self.seq, self.n_past, tm, self.per_row_mod, self.mod)

    def mod_spec(self, layer, chunk, tn=D_MODEL, col_axis=False):
        per_chunk = D_MODEL // tn
        col = (lambda n: chunk * per_chunk + n) if col_axis else (lambda n: chunk * per_chunk)
        if self.per_row_mod:
            if col_axis:
                return pl.BlockSpec((1, self.tm, tn), lambda m, n, *_: (layer, m, col(n)))
            return pl.BlockSpec((1, self.tm, tn), lambda m, *_: (layer, m, col(0)))
        tiles_per_batch = self.seq // self.tm
        base = layer * self.n_batch
        if col_axis:
            return pl.BlockSpec((1, 1, tn), lambda m, n, *_: (base + m // tiles_per_batch, 0, col(n)))
        return pl.BlockSpec((1, 1, tn), lambda m, *_: (base + m // tiles_per_batch, 0, col(0)))

    def rope_tables(self):
        half = RET_DK // 2
        pos = jnp.arange(self.seq, dtype=F32) + float(self.n_past)
        inv = ROPE_BASE ** (-jnp.arange(half, dtype=F32) / half)
        ang = pos[:, None] * inv[None, :]
        cos, sin = jnp.cos(ang), jnp.sin(ang)
        cos_full = jnp.concatenate([cos, cos], axis=1)
        sin_signed = jnp.concatenate([-sin, sin], axis=1)
        if self.per_row_mod:
            cos_full = jnp.tile(cos_full, (self.n_batch, 1))
            sin_signed = jnp.tile(sin_signed, (self.n_batch, 1))
        return cos_full, sin_signed

    def rope_spec(self):
        n_blocks = (self.rows if self.per_row_mod else self.seq) // self.tm
        return pl.BlockSpec((self.tm, RET_DK), lambda m, *_: (m % n_blocks, 0))


def _retention_tables(chunk):
    log_gamma = jnp.log1p(-jnp.exp2(-5.0 - jnp.arange(RET_HEADS, dtype=F32)))
    idx = jnp.arange(chunk, dtype=F32)
    rel = idx[:, None] - idx[None, :]
    dmask = jnp.where(rel[None] >= 0, jnp.exp(log_gamma[:, None, None] * jnp.maximum(rel, 0.0)[None]), 0.0)
    q_dec = jnp.exp(log_gamma[:, None] * (idx + 1.0)[None, :])
    k_dec = jnp.exp(log_gamma[:, None] * (chunk - 1.0 - idx)[None, :])
    chunk_dec = jnp.exp(log_gamma * chunk)
    q_dec = jnp.broadcast_to(q_dec[:, :, None], (RET_HEADS, chunk, RET_DV))
    k_dec = jnp.broadcast_to(k_dec[:, :, None], (RET_HEADS, chunk, RET_DK))
    chunk_dec = jnp.broadcast_to(chunk_dec[:, None, None], (RET_HEADS, 1, RET_DV))
    return dmask, q_dec, k_dec, chunk_dec


def _norm_rows(x_ref, g_ref, sc_ref, sh_ref, h_ref):
    tm = x_ref.shape[0]
    step = min(NORM_ROWS, tm)
    trips = tm // step
    per_row = sc_ref.shape[1] != 1
    if not per_row:
        gain, shift = g_ref[0] * (1.0 + sc_ref[0]), sh_ref[0]

    def body(c, carry):
        rows = pl.ds(pl.multiple_of(c * step, step), step)
        x = x_ref[rows, :]
        y = x * lax.rsqrt(jnp.mean(x * x, axis=-1, keepdims=True) + EPS)
        if per_row:
            y = y * (g_ref[0] * (1.0 + sc_ref[0, rows, :])) + sh_ref[0, rows, :]
        else:
            y = y * gain + shift
        h_ref[rows, :] = y.astype(h_ref.dtype)
        return carry

    lax.fori_loop(0, trips, body, 0, unroll=NORM_UNROLL if trips % NORM_UNROLL == 0 else 1)


def _silu(x):
    return x * jax.nn.sigmoid(x)


def _pack_halves(y):
    half = y.shape[1] // 2
    hi = lax.bitcast_convert_type(y[:, :half].astype(jnp.bfloat16).astype(F32), jnp.uint32)
    lo = lax.bitcast_convert_type(y[:, half:].astype(jnp.bfloat16).astype(F32), jnp.uint32)
    return hi | (lo >> 16)


def _unpack_halves(u):
    return (lax.bitcast_convert_type(u & jnp.uint32(0xFFFF0000), F32),
            lax.bitcast_convert_type(u << 16, F32))


def _ada_kernel(cp_ref, cs_ref, w_ref, b_ref, op_ref, os_ref):
    w = w_ref[0].astype(BF16)
    for c_ref, o_ref in ((cp_ref, op_ref), (cs_ref, os_ref)):
        o_ref[0] = jnp.dot(_silu(c_ref[...]).astype(BF16), w, preferred_element_type=F32) + b_ref[0]


def _ada(c_p, c_s, w_ada, b_ada):
    tn = 1024
    width = N_MOD * D_MODEL
    row_spec = lambda c: pl.BlockSpec(c.shape, lambda l, n: (0, 0))
    out_spec = lambda c: pl.BlockSpec((1, c.shape[0], tn), lambda l, n: (l, 0, n))
    return pl.pallas_call(
        _ada_kernel,
        grid=(DEPTH, width // tn),
        in_specs=[row_spec(c_p), row_spec(c_s),
                  pl.BlockSpec((1, D_MODEL, tn), lambda l, n: (l, 0, n)),
                  pl.BlockSpec((1, 1, tn), lambda l, n: (l, 0, n))],
        out_specs=[out_spec(c_p), out_spec(c_s)],
        out_shape=[jax.ShapeDtypeStruct((DEPTH, c.shape[0], width), F32) for c in (c_p, c_s)],
        compiler_params=_params("arbitrary", "arbitrary"),
        name="ada",
    )(c_p, c_s, w_ada, b_ada.reshape(DEPTH, 1, width))


def _inproj_kernel(x_ref, sc_ref, sh_ref, g_ref, w_ref, cos_ref, sin_ref, o_ref, u_ref, h_scr):
    n = pl.program_id(1)
    tm = x_ref.shape[0]
    rc = min(ROW_CHUNK, tm)

    @pl.when(n == 0)
    def _():
        _norm_rows(x_ref, g_ref, sc_ref, sh_ref, h_scr)

    def project(epilogue):
        w = w_ref[0].astype(BF16)
        for r in range(0, tm, rc):
            rows = slice(r, r + rc)
            epilogue(jnp.dot(h_scr[rows, :], w, preferred_element_type=F32), rows)

    def store(acc, rows):
        o_ref[rows, :] = acc.astype(o_ref.dtype)

    def rope(scale):
        def epilogue(acc, rows):
            for j in range(TN_IN // RET_DK):
                cols = slice(j * RET_DK, (j + 1) * RET_DK)
                xh = acc[:, cols]
                r = xh * cos_ref[rows, :] + pltpu.roll(xh, RET_DK // 2, 1) * sin_ref[rows, :]
                if scale != 1.0:
                    r = r * scale
                o_ref[rows, cols] = r.astype(o_ref.dtype)
        project(epilogue)

    @pl.when(n < O_K // TN_IN)
    def _():
        rope(1.0)

    @pl.when(jnp.logical_and(n >= O_K // TN_IN, n < O_V // TN_IN))
    def _():
        rope(RET_DK ** -0.5)

    @pl.when(jnp.logical_and(n >= O_V // TN_IN, n < O_G // TN_IN))
    def _():
        project(store)

    @pl.when(jnp.logical_and(n >= O_G // TN_IN, n < O_U // TN_IN))
    def _():
        project(lambda acc, rows: store(_silu(acc), rows))

    @pl.when(jnp.logical_and(n >= O_U // TN_IN, n < O_M1 // TN_IN))
    def _():
        def epilogue(acc, rows):
            store(acc, rows)
            u_ref[rows, :] = acc
        project(epilogue)

    @pl.when(n >= O_M1 // TN_IN)
    def _():
        project(lambda acc, rows: store(jax.nn.sigmoid(acc), rows))


def _inproj(tok, layer, x, g_mix, w_in, rope_tabs, out_dtype):
    tm = tok.tm
    tn = TN_IN
    assert all(o % tn == 0 for o in (O_K, O_V, O_G, O_U, O_M1, N_IN))
    n_u = D_POOL // tn
    mod3 = tok.mod
    proj, u = pl.pallas_call(
        _inproj_kernel,
        grid=(tok.n_tiles, N_IN // tn),
        in_specs=[pl.BlockSpec((tm, D_MODEL), lambda m, n: (m, 0)),
                  tok.mod_spec(layer, 1), tok.mod_spec(layer, 0),
                  pl.BlockSpec((1, 1, D_MODEL), lambda m, n: (layer, 0, 0)),
                  pl.BlockSpec((1, D_MODEL, tn), lambda m, n: (layer, 0, n)),
                  tok.rope_spec(), tok.rope_spec()],
        out_specs=[pl.BlockSpec((tm, tn), lambda m, n: (m, n)),
                   pl.BlockSpec((tm, tn), lambda m, n: (m, jnp.clip(n - O_U // tn, 0, n_u - 1)),
                                pipeline_mode=pl.Buffered(1))],
        out_shape=[jax.ShapeDtypeStruct((tok.rows, N_IN), out_dtype),
                   jax.ShapeDtypeStruct((tok.rows, D_POOL), F32)],
        scratch_shapes=[pltpu.VMEM((tm, D_MODEL), BF16)],
        compiler_params=_params("arbitrary", "arbitrary"),
        name="inproj",
    )(x, mod3, mod3, g_mix, w_in, *rope_tabs)
    return proj, u


def _head_norm_gate(o, gret, gate):
    mu = jnp.mean(o, axis=-1, keepdims=True)
    d = o - mu
    var = jnp.mean(d * d, axis=-1, keepdims=True)
    return gate.astype(F32) * (d * lax.rsqrt(var + EPS) * gret)


def _ret_prompt_kernel(q_ref, k_ref, v_ref, g_ref, gret_ref, dmask_ref, qdec_ref, kdec_ref, cdec_ref,
                       *rest):
    y_ref, s_ref = rest[-2:]

    @pl.when(pl.program_id(1) == 0)
    def _():
        s_ref[...] = jnp.zeros_like(s_ref)

    for b in range(q_ref.shape[0]):
        for h in range(RET_HEADS):
            ks, vs = slice(h * RET_DK, (h + 1) * RET_DK), slice(h * RET_DV, (h + 1) * RET_DV)
            q, k, v = q_ref[b, :, ks], k_ref[b, :, ks], v_ref[b, :, vs]
            state = s_ref[0, b, h]
            scores = lax.dot_general(q, k, (((1,), (1,)), ((), ())), preferred_element_type=F32) * dmask_ref[h]
            inner = jnp.dot(scores.astype(BF16), v, preferred_element_type=F32)
            cross = jnp.dot(q, state.astype(BF16), preferred_element_type=F32) * qdec_ref[h]
            k_decayed = (k.astype(F32) * kdec_ref[h]).astype(BF16)
            s_ref[0, b, h] = cdec_ref[h] * state + lax.dot_general(
                k_decayed, v, (((0,), (0,)), ((), ())), preferred_element_type=F32)
            y_ref[b, :, vs] = _head_norm_gate(inner + cross, gret_ref[0, :, vs], g_ref[b, :, vs]).astype(y_ref.dtype)


def _table_specs(tables):
    return [pl.BlockSpec(t.shape, lambda *_: (0, 0, 0)) for t in tables]


def _ret_prompt(tok, layer, proj, g_ret, s_all):
    n_b, seq = tok.n_batch, tok.seq
    c = RET_CHUNK
    assert seq % c == 0
    nc = seq // c
    tables = _retention_tables(c)
    bb = RET_PROMPT_BATCH if n_b % RET_PROMPT_BATCH == 0 else 1
    proj3 = proj.reshape(n_b, seq, N_IN)
    in_specs = [pl.BlockSpec((bb, c, QK_W), lambda b, i: (b, i, O_Q // QK_W)),
                pl.BlockSpec((bb, c, QK_W), lambda b, i: (b, i, O_K // QK_W)),
                pl.BlockSpec((bb, c, V_W), lambda b, i: (b, i, O_V // V_W)),
                pl.BlockSpec((bb, c, V_W), lambda b, i: (b, i, O_G // V_W)),
                pl.BlockSpec((1, 1, V_W), lambda b, i: (layer, 0, 0))] + _table_specs(tables)
    args = [proj3, proj3, proj3, proj3, g_ret, *tables]
    aliases = {}
    if s_all is not None:
        aliases = {len(args): 1}
        in_specs.append(pl.BlockSpec(memory_space=pl.ANY))
        args.append(s_all)
    y, s_all = pl.pallas_call(
        _ret_prompt_kernel,
        grid=(n_b // bb, nc),
        in_specs=in_specs,
        out_specs=[pl.BlockSpec((bb, c, V_W), lambda b, i: (b, i, 0)),
                   pl.BlockSpec((1, bb, RET_HEADS, RET_DK, RET_DV), lambda b, i: (layer, b, 0, 0, 0))],
        out_shape=[jax.ShapeDtypeStruct((n_b, seq, V_W), BF16),
                   jax.ShapeDtypeStruct((DEPTH, n_b, RET_HEADS, RET_DK, RET_DV), F32)],
        input_output_aliases=aliases,
        compiler_params=_params("arbitrary", "arbitrary"),
        name="ret_prompt",
    )(*args)
    return y.reshape(tok.rows, V_W), s_all


def _ret_sample_kernel(q_ref, k_ref, v_ref, g_ref, kt_ref, gret_ref, s0_ref, dm_ref, qdec_ref, kdect_ref, cdec_ref,
                       *rest, bb, seq):
    y_ref, s_ref = rest[-2:]
    hs = RET_HEADS * seq
    reps = lambda a: jnp.concatenate([a] * RET_HEADS, axis=0)
    mask_q = (lax.broadcasted_iota(jnp.int32, (hs, QK_W), 0) // seq
              == lax.broadcasted_iota(jnp.int32, (hs, QK_W), 1) // RET_DK)
    mask_v = (lax.broadcasted_iota(jnp.int32, (hs, V_W), 0) // seq
              == lax.broadcasted_iota(jnp.int32, (hs, V_W), 1) // RET_DV)
    nt = (((1,), (1,)), ((), ()))
    for b in range(bb):
        rows_b = slice(b * seq, (b + 1) * seq)
        q, k, v = q_ref[rows_b, :], k_ref[rows_b, :], v_ref[rows_b, :]
        q_blk = jnp.where(mask_q, reps(q), 0.0).astype(BF16)
        scores = lax.dot_general(q_blk, reps(k).astype(BF16), nt, preferred_element_type=F32) * dm_ref[...]
        v_stack = jnp.concatenate([v[:, h * RET_DV:(h + 1) * RET_DV] for h in range(RET_HEADS)], axis=0)
        inner = jnp.dot(scores.astype(BF16), v_stack.astype(BF16), preferred_element_type=F32)
        state = s0_ref[0, b]
        cross = jnp.dot(q_blk, state.reshape(QK_W, RET_DV).astype(BF16), preferred_element_type=F32)
        o = inner + cross * qdec_ref[...]
        mu = jnp.mean(o, axis=-1, keepdims=True)
        d = o - mu
        normed = d * lax.rsqrt(jnp.mean(d * d, axis=-1, keepdims=True) + EPS)
        k_dec_t = (kt_ref[b] * kdect_ref[...]).astype(BF16)
        v_blk = jnp.where(mask_v, reps(v), 0.0).astype(BF16)
        upd = jnp.dot(k_dec_t, v_blk, preferred_element_type=F32)
        for h in range(RET_HEADS):
            vs = slice(h * RET_DV, (h + 1) * RET_DV)
            y_ref[rows_b, vs] = g_ref[rows_b, vs] * (normed[h * seq:(h + 1) * seq] * gret_ref[0, :, vs])
            s_ref[0, b, h] = cdec_ref[h] * state[h] + upd[:, vs]


def _ret_sample(tok, layer, proj, g_ret, state_ret, s_all):
    n_b, seq = tok.n_batch, tok.seq
    bb = RET_SAMPLE_BATCH if n_b % RET_SAMPLE_BATCH == 0 else 4
    assert n_b % bb == 0
    hs = RET_HEADS * seq
    dmask, q_dec, k_dec, chunk_dec = _retention_tables(seq)
    dm_blk = (dmask[:, :, None, :] * jnp.eye(RET_HEADS, dtype=F32)[:, None, :, None]).reshape(hs, hs)
    tables = [dm_blk, q_dec.reshape(hs, RET_DV), k_dec[:, :, 0].reshape(1, hs), chunk_dec]
    k_t = proj[:, O_K:O_K + QK_W].reshape(n_b, seq, RET_HEADS, RET_DK).transpose(0, 3, 2, 1).reshape(n_b, RET_DK, hs)
    state_spec = pl.BlockSpec((1, bb, RET_HEADS, RET_DK, RET_DV), lambda i: (layer, i, 0, 0, 0))
    in_specs = [pl.BlockSpec((bb * seq, QK_W), lambda i: (i, O_Q // QK_W)),
                pl.BlockSpec((bb * seq, QK_W), lambda i: (i, O_K // QK_W)),
                pl.BlockSpec((bb * seq, V_W), lambda i: (i, O_V // V_W)),
                pl.BlockSpec((bb * seq, V_W), lambda i: (i, O_G // V_W)),
                pl.BlockSpec((bb, RET_DK, hs), lambda i: (i, 0, 0)),
                pl.BlockSpec((1, 1, V_W), lambda i: (layer, 0, 0)),
                state_spec] + [pl.BlockSpec(t.shape, lambda i, nd=t.ndim: (0,) * nd) for t in tables]
    args = [proj, proj, proj, proj, k_t, g_ret, state_ret, *tables]
    aliases = {}
    if s_all is not None:
        aliases = {len(args): 1}
        in_specs.append(pl.BlockSpec(memory_space=pl.ANY))
        args.append(s_all)
    y, s_all = pl.pallas_call(
        functools.partial(_ret_sample_kernel, bb=bb, seq=seq),
        grid=(n_b // bb,),
        in_specs=in_specs,
        out_specs=[pl.BlockSpec((bb * seq, V_W), lambda i: (i, 0)), state_spec],
        out_shape=[jax.ShapeDtypeStruct((tok.rows, V_W), F32),
                   jax.ShapeDtypeStruct(state_ret.shape, F32)],
        input_output_aliases=aliases,
        compiler_params=_params("arbitrary"),
        name="ret_sample",
    )(*args)
    return y, s_all


def _pool_map(pooled, wmap_ref, scale_ref, gi):
    cols = slice(gi * POOL_GW, (gi + 1) * POOL_GW)
    mixed = jnp.dot(pooled.astype(BF16), wmap_ref[0, gi].astype(BF16), preferred_element_type=F32)
    return mixed * scale_ref[0, :, cols]


def _pool_prompt_kernel(u_ref, halo_ref, wmap_ref, scale_ref, o_ref, uf_scr, a_scr, b_scr, *, tl):
    pad = SUBLANES
    top = pad + POOL_HALO
    first = pl.program_id(1) == 0
    uf_scr[0:pad, :] = jnp.zeros((pad, D_POOL), F32)
    a_scr[0:pad, :] = jnp.zeros((pad, POOL_GW), F32)
    b_scr[0:pad, :] = jnp.zeros((pad, POOL_GW), F32)
    uf_scr[pad:top, :] = jnp.where(first, 0.0, halo_ref[...])
    uf_scr[top:, :] = u_ref[...]
    n_rows = POOL_HALO + tl
    row = lax.broadcasted_iota(jnp.int32, (tl, POOL_GW), 0) + pl.program_id(1) * tl
    for gi, w in enumerate(POOL_WINDOWS):
        cols = slice(gi * POOL_GW, (gi + 1) * POOL_GW)
        src = lambda lo, cols=cols: uf_scr[lo:lo + n_rows, cols]
        dst_scr, other = a_scr, b_scr
        shift = 1
        while shift < w:
            dst_scr[pad:, :] = src(pad) + src(pad - shift)
            src = lambda lo, s=dst_scr: s[lo:lo + n_rows, :]
            dst_scr, other = other, dst_scr
            shift *= 2
        acc = src(pad)[POOL_HALO:, :]
        u = uf_scr[top:, cols]
        cnt = jnp.minimum(w, row + 1).astype(F32)
        o_ref[:, cols] = _pool_map(acc / cnt - u, wmap_ref, scale_ref, gi).astype(o_ref.dtype)


def _pool_prompt(tok, layer, u, w_pool_map, pool_scale):
    n_b, seq = tok.n_batch, tok.seq
    tl = min(TL_POOL, seq)
    assert seq % tl == 0 and tl % POOL_HALO == 0
    nl = seq // tl
    per = tl // POOL_HALO
    return pl.pallas_call(
        functools.partial(_pool_prompt_kernel, tl=tl),
        grid=(n_b, nl),
        in_specs=[pl.BlockSpec((tl, D_POOL), lambda b, l: (b * nl + l, 0)),
                  pl.BlockSpec((POOL_HALO, D_POOL), lambda b, l: (jnp.maximum((b * nl + l) * per - 1, 0), 0)),
                  pl.BlockSpec((1,) + w_pool_map.shape[1:], lambda b, l: (layer, 0, 0, 0)),
                  pl.BlockSpec((1, 1, D_POOL), lambda b, l: (layer, 0, 0))],
        out_specs=pl.BlockSpec((tl, D_POOL), lambda b, l: (b * nl + l, 0)),
        out_shape=jax.ShapeDtypeStruct((tok.rows, D_POOL), BF16),
        scratch_shapes=[pltpu.VMEM((SUBLANES + POOL_HALO + tl, D_POOL), F32),
                        pltpu.VMEM((SUBLANES + POOL_HALO + tl, POOL_GW), F32),
                        pltpu.VMEM((SUBLANES + POOL_HALO + tl, POOL_GW), F32)],
        compiler_params=_params("arbitrary", "arbitrary"),
        name="pool_prompt",
    )(u, u, w_pool_map, pool_scale)


def _pool_sample_kernel(buf_ref, u_ref, wmap_ref, scale_ref, o_ref, *, seq, n_b):
    def row(j, cols):
        return buf_ref[j, :, cols] if j < POOL_BUF else u_ref[j - POOL_BUF, :, cols]

    for gi, w in enumerate(POOL_WINDOWS):
        cols = slice(gi * POOL_GW, (gi + 1) * POOL_GW)
        for t in range(seq):
            acc = row(POOL_BUF + t, cols)
            for j in range(1, w):
                acc = acc + row(POOL_BUF + t - j, cols)
            pooled = acc / float(w) - row(POOL_BUF + t, cols)
            o_ref[t, :, cols] = _pool_map(pooled, wmap_ref, scale_ref, gi)


def _pool_sample(tok, layer, u, buf_tm, w_pool_map, pool_scale):
    n_b, seq = tok.n_batch, tok.seq
    assert tok.n_past >= max(POOL_WINDOWS)
    u_tm = u.reshape(n_b, seq, D_POOL).transpose(1, 0, 2)
    mixed_tm = pl.pallas_call(
        functools.partial(_pool_sample_kernel, seq=seq, n_b=n_b),
        grid=(1,),
        in_specs=[pl.BlockSpec(buf_tm.shape, lambda i: (0, 0, 0)),
                  pl.BlockSpec(u_tm.shape, lambda i: (0, 0, 0)),
                  pl.BlockSpec((1,) + w_pool_map.shape[1:], lambda i: (layer, 0, 0, 0)),
                  pl.BlockSpec((1, 1, D_POOL), lambda i: (layer, 0, 0))],
        out_specs=pl.BlockSpec(u_tm.shape, lambda i: (0, 0, 0)),
        out_shape=jax.ShapeDtypeStruct(u_tm.shape, F32),
        compiler_params=_params("arbitrary"),
        name="pool_sample",
    )(buf_tm, u_tm, w_pool_map, pool_scale)
    return mixed_tm.transpose(1, 0, 2).reshape(tok.rows, D_POOL)


def _merge_kernel(yret_ref, mixed_ref, wr_ref, wp_ref, g1_ref, g2_ref, o_ref, wr_scr, wp_scr):
    @pl.when(pl.program_id(1) == 0)
    def _():
        wr_scr[...] = wr_ref[0].astype(BF16)
        wp_scr[...] = wp_ref[0].astype(BF16)

    ret = jnp.dot(yret_ref[...].astype(BF16), wr_scr[...], preferred_element_type=F32)
    pool = jnp.dot(mixed_ref[...].astype(BF16), wp_scr[...], preferred_element_type=F32)
    o_ref[...] = (g1_ref[...].astype(F32) * ret + g2_ref[...].astype(F32) * pool).astype(o_ref.dtype)


def _merge(tok, layer, yret, mixed, proj, w_ret_out, w_pool_out):
    tm, tn = tok.tm, TN_MERGE
    once = pl.Buffered(1)
    return pl.pallas_call(
        _merge_kernel,
        grid=(D_MODEL // tn, tok.n_tiles),
        in_specs=[pl.BlockSpec((tm, V_W), lambda n, m: (m, 0)),
                  pl.BlockSpec((tm, D_POOL), lambda n, m: (m, 0)),
                  pl.BlockSpec((1, V_W, tn), lambda n, m: (layer, 0, n), pipeline_mode=once),
                  pl.BlockSpec((1, D_POOL, tn), lambda n, m: (layer, 0, n), pipeline_mode=once),
                  pl.BlockSpec((tm, tn), lambda n, m: (m, O_M1 // tn + n)),
                  pl.BlockSpec((tm, tn), lambda n, m: (m, O_M2 // tn + n))],
        out_specs=pl.BlockSpec((tm, tn), lambda n, m: (m, n)),
        out_shape=jax.ShapeDtypeStruct((tok.rows, D_MODEL), BF16),
        scratch_shapes=[pltpu.VMEM((V_W, tn), BF16), pltpu.VMEM((D_POOL, tn), BF16)],
        compiler_params=_params("arbitrary", "arbitrary"),
        name="merge",
    )(yret, mixed, w_ret_out, w_pool_out, proj, proj)


def _wo_kernel(merged_ref, w_ref, x_ref, gt_ref, o_ref):
    mix = jnp.dot(merged_ref[...], w_ref[0].astype(BF16), preferred_element_type=F32)
    o_ref[...] = x_ref[...] + (1.0 + gt_ref[0]) * mix


def _wo(tok, layer, merged, x, w_o):
    tok = tok.with_tm(TM_OUT)
    tm = tok.tm
    mod3 = tok.mod
    return pl.pallas_call(
        _wo_kernel,
        grid=(tok.n_tiles, D_MODEL // TN),
        in_specs=[pl.BlockSpec((tm, D_MODEL), lambda m, n: (m, 0)),
                  pl.BlockSpec((1, D_MODEL, TN), lambda m, n: (layer, 0, n)),
                  pl.BlockSpec((tm, TN), lambda m, n: (m, n)),
                  tok.mod_spec(layer, 2, tn=TN, col_axis=True)],
        out_specs=pl.BlockSpec((tm, TN), lambda m, n: (m, n)),
        out_shape=jax.ShapeDtypeStruct((tok.rows, D_MODEL), F32),
        compiler_params=_params("arbitrary", "arbitrary"),
        name="wo",
    )(merged, w_o, x, mod3)


def _mat(w_ref):
    return w_ref[(0,) * (len(w_ref.shape) - 2)]


def _final_norm(y, gfin_ref):
    return y * lax.rsqrt(jnp.mean(y * y, axis=-1, keepdims=True) + EPS) * gfin_ref[...]


def _ffn_kernel(x_ref, sc_ref, sh_ref, gt_ref, g_ref, w1_ref, w3_ref, w2_ref, *rest, n_f, with_final):
    gfin_ref = rest[0] if with_final else None
    o_ref, h_scr = rest[-2:]
    f = pl.program_id(1)

    @pl.when(f == 0)
    def _():
        _norm_rows(x_ref, g_ref, sc_ref, sh_ref, h_scr)
        o_ref[...] = jnp.zeros_like(o_ref)

    h = h_scr[...]
    a = jnp.dot(h, _mat(w1_ref).astype(BF16), preferred_element_type=F32)
    b = jnp.dot(h, _mat(w3_ref).astype(BF16), preferred_element_type=F32)
    o_ref[...] += jnp.dot((_silu(a) * b).astype(BF16), _mat(w2_ref).astype(BF16), preferred_element_type=F32)

    @pl.when(f == n_f - 1)
    def _():
        y = x_ref[...] + (1.0 + gt_ref[0]) * o_ref[...]
        o_ref[...] = _final_norm(y, gfin_ref) if with_final else y


def _ffn(tok, layer, x, g_ffn, w1, w3, w2, g_final=None):
    j = layer // 2
    d_ff = w1.shape[-1]
    tok_f = tok.with_tm(TM_FFN)
    tm, n_tiles = tok_f.tm, tok_f.n_tiles
    mod3 = tok_f.mod
    assert d_ff % TF == 0
    n_f = d_ff // TF
    in_specs = [pl.BlockSpec((tm, D_MODEL), lambda m, f: (m, 0)),
                tok_f.mod_spec(layer, 4), tok_f.mod_spec(layer, 3), tok_f.mod_spec(layer, 5),
                pl.BlockSpec((1, 1, D_MODEL), lambda m, f: (layer, 0, 0)),
                pl.BlockSpec((1, D_MODEL, TF), lambda m, f: (j, 0, f)),
                pl.BlockSpec((1, D_MODEL, TF), lambda m, f: (j, 0, f)),
                pl.BlockSpec((1, TF, D_MODEL), lambda m, f: (j, f, 0))]
    args = [x, mod3, mod3, mod3, g_ffn, w1, w3, w2]
    if g_final is not None:
        in_specs.append(pl.BlockSpec((1, D_MODEL), lambda m, f: (0, 0)))
        args.append(g_final)
    return pl.pallas_call(
        functools.partial(_ffn_kernel, n_f=n_f, with_final=g_final is not None),
        grid=(n_tiles, n_f),
        in_specs=in_specs,
        out_specs=pl.BlockSpec((tm, D_MODEL), lambda m, f: (m, 0)),
        out_shape=jax.ShapeDtypeStruct((tok.rows, D_MODEL), F32),
        scratch_shapes=[pltpu.VMEM((tm, D_MODEL), BF16)],
        compiler_params=_params("arbitrary", "arbitrary"),
        name="ffn",
    )(*args)


def _router_kernel(x_ref, sc_ref, sh_ref, g_ref, w_ref, b_ref, *rest):
    esel_ref, wsel_ref, h_ref, h_scr = rest[-4:]
    _norm_rows(x_ref, g_ref, sc_ref, sh_ref, h_scr)
    h = h_scr[...]
    logits = jnp.dot(h, w_ref[0].astype(BF16), preferred_element_type=F32) + b_ref[0]
    idx = lax.broadcasted_iota(jnp.int32, logits.shape, 1)
    top1 = jnp.max(logits, axis=-1, keepdims=True)
    i1 = jnp.min(jnp.where(logits == top1, idx, N_EXPERTS), axis=-1, keepdims=True)
    others = jnp.where(idx == i1, -jnp.inf, logits)
    top2 = jnp.max(others, axis=-1, keepdims=True)
    i2 = jnp.min(jnp.where(others == top2, idx, N_EXPERTS), axis=-1, keepdims=True)
    e2 = jnp.exp(top2 - top1)
    denom = 1.0 + e2
    w1, w2 = 1.0 / denom, e2 / denom
    esel_ref[...] = jnp.where(idx == 0, i1, i2)
    wsel_ref[...] = jnp.where(idx == 0, w1, w2)
    h_ref[...] = _pack_halves(h.astype(F32))


def _router(tok, layer, x, g_ffn, w_router, b_router, h_all=None, row0=0, rows_all=None):
    j = layer // 2
    tm = tok.tm
    rows_all = tok.rows if rows_all is None else rows_all
    assert row0 % tm == 0
    small = pl.BlockSpec((tm, N_EXPERTS), lambda m: (m, 0))
    mod3 = tok.mod
    in_specs = [pl.BlockSpec((tm, D_MODEL), lambda m: (m, 0)),
                tok.mod_spec(layer, 4), tok.mod_spec(layer, 3),
                pl.BlockSpec((1, 1, D_MODEL), lambda m: (layer, 0, 0)),
                pl.BlockSpec((1, D_MODEL, N_EXPERTS), lambda m: (j, 0, 0)),
                pl.BlockSpec((1, 1, N_EXPERTS), lambda m: (j, 0, 0))]
    args = [x, mod3, mod3, g_ffn, w_router, b_router]
    aliases = {}
    if h_all is not None:
        aliases = {len(args): 2}
        in_specs.append(pl.BlockSpec(memory_space=pl.ANY))
        args.append(h_all)
    return pl.pallas_call(
        _router_kernel,
        grid=(tok.n_tiles,),
        in_specs=in_specs,
        out_specs=[small, small, pl.BlockSpec((tm, D_MODEL // 2), lambda m: (row0 // tm + m, 0))],
        out_shape=[jax.ShapeDtypeStruct((tok.rows, N_EXPERTS), jnp.int32),
                   jax.ShapeDtypeStruct((tok.rows, N_EXPERTS), F32),
                   jax.ShapeDtypeStruct((rows_all, D_MODEL // 2), jnp.uint32)],
        input_output_aliases=aliases,
        scratch_shapes=[pltpu.VMEM((tm, D_MODEL), BF16)],
        compiler_params=_params("arbitrary"),
        name="router",
    )(*args)


def _route_plan(choice, tm):
    rows = choice.shape[0]
    n_pairs = 2 * rows
    n_tiles = n_pairs // tm + N_EXPERTS
    e_flat = choice.T.reshape(n_pairs)
    onehot = (e_flat[:, None] == jnp.arange(N_EXPERTS, dtype=jnp.int32)[None, :]).astype(jnp.int32)
    csum = jnp.cumsum(onehot, axis=0)
    rank = jnp.sum(csum * onehot, axis=1) - 1
    count = csum[-1]
    tiles_per = (count + tm - 1) // tm
    rows_per = jnp.maximum((((count + jnp.maximum(tiles_per, 1) - 1) // jnp.maximum(tiles_per, 1)) + 7) // 8 * 8, 8)
    tile_end = jnp.cumsum(tiles_per)
    tile_start = tile_end - tiles_per
    pos = (tile_start[e_flat] + rank // rows_per[e_flat]) * tm + rank % rows_per[e_flat]
    token = jnp.arange(n_pairs, dtype=jnp.int32) % rows
    src_row = jnp.zeros((n_tiles * tm,), jnp.int32).at[pos].set(token, unique_indices=True, mode='promise_in_bounds')
    tile_id = jnp.arange(n_tiles, dtype=jnp.int32)
    n_used = tile_end[-1]
    tile_expert = jnp.minimum(jnp.sum(tile_id[:, None] >= tile_end[None, :], axis=1), N_EXPERTS - 1).astype(jnp.int32)
    tile_rows = jnp.clip(count[tile_expert] - (tile_id - tile_start[tile_expert]) * rows_per[tile_expert],
                         0, rows_per[tile_expert])
    tile_rows = jnp.where(tile_id < n_used, tile_rows, 0).astype(jnp.int32)
    last_expert = tile_expert[jnp.maximum(n_used - 1, 0)]
    tile_expert = jnp.where(tile_rows > 0, tile_expert, last_expert)
    return tile_expert, tile_rows, src_row, pos.astype(jnp.int32)


def _moe_kernel(te_ref, nr_ref, src_ref, h_hbm, w1_ref, w3_ref, w2_ref, o_ref, rows_scr, h_scr, acc_scr, sem,
                *, tm, sub, n_f):
    del te_ref
    m, f = pl.program_id(0), pl.program_id(1)
    half = D_MODEL // 2
    group = rows_scr.shape[1]

    def n_groups(tile):
        return (nr_ref[tile] + group - 1) // group

    def start_rows(tile):
        def body(i, c):
            for j in range(group):
                row = src_ref[tile * tm + i * group + j]
                pltpu.make_async_copy(h_hbm.at[pl.ds(row, 1)], rows_scr.at[i, pl.ds(j, 1)], sem).start()
            return c
        lax.fori_loop(0, n_groups(tile), body, 0)

    def wait_rows(tile):
        def body(i, c):
            pltpu.make_async_copy(h_hbm.at[pl.ds(0, group)], rows_scr.at[i], sem).wait()
            return c
        lax.fori_loop(0, n_groups(tile), body, 0)

    @pl.when(f == 0)
    def _():
        acc_scr[...] = jnp.zeros_like(acc_scr)

        @pl.when(m == 0)
        def _():
            rows_scr[...] = jnp.zeros_like(rows_scr)
            start_rows(0)

        wait_rows(m)
        first, second = _unpack_halves(rows_scr[...].reshape(tm, half))
        h_scr[:, :half] = first.astype(BF16)
        h_scr[:, half:] = second.astype(BF16)

        @pl.when(m + 1 < pl.num_programs(0))
        def _():
            start_rows(m + 1)

    n_sub = (nr_ref[m] + sub - 1) // sub
    for n in range(1, tm // sub + 1):
        @pl.when(n_sub == n)
        def _(n=n):
            h = h_scr[0:n * sub, :]
            a = jnp.dot(h, _mat(w1_ref).astype(BF16), preferred_element_type=F32)
            b = jnp.dot(h, _mat(w3_ref).astype(BF16), preferred_element_type=F32)
            acc_scr[0:n * sub, :] += jnp.dot((_silu(a) * b).astype(BF16), _mat(w2_ref).astype(BF16),
                                             preferred_element_type=F32)

    @pl.when(f == n_f - 1)
    def _():
        o_ref[...] = _pack_halves(acc_scr[...])


def _moe_sparse(layer, h, plan, w1, w3, w2, tm):
    j = layer // 2
    tile_expert, tile_rows, src_row, _ = plan
    n_tiles = tile_expert.shape[0]
    d_ff = w1.shape[3]
    assert d_ff % TF == 0 and tm % SUB_MOE == 0
    n_f = d_ff // TF

    def f_sel(m, f, nr):
        return jnp.where(nr[m] > 0, f, n_f - 1)

    return pl.pallas_call(
        functools.partial(_moe_kernel, tm=tm, sub=SUB_MOE, n_f=n_f),
        grid_spec=pltpu.PrefetchScalarGridSpec(
            num_scalar_prefetch=3,
            grid=(n_tiles, n_f),
            in_specs=[pl.BlockSpec(memory_space=pl.ANY),
                      pl.BlockSpec((1, 1, D_MODEL, TF), lambda m, f, te, nr, sr: (j, te[m], 0, f_sel(m, f, nr))),
                      pl.BlockSpec((1, 1, D_MODEL, TF), lambda m, f, te, nr, sr: (j, te[m], 0, f_sel(m, f, nr))),
                      pl.BlockSpec((1, 1, TF, D_MODEL), lambda m, f, te, nr, sr: (j, te[m], f_sel(m, f, nr), 0))],
            out_specs=pl.BlockSpec((tm, D_MODEL // 2), lambda m, f, te, nr, sr: (m, 0)),
            scratch_shapes=[pltpu.VMEM((tm // SUBLANES, SUBLANES, D_MODEL // 2), jnp.uint32),
                            pltpu.VMEM((tm, D_MODEL), BF16),
                            pltpu.VMEM((tm, D_MODEL), F32), pltpu.SemaphoreType.DMA(())]),
        out_shape=jax.ShapeDtypeStruct((n_tiles * tm, D_MODEL // 2), jnp.uint32),
        compiler_params=_params("arbitrary", "arbitrary"),
        name="moe",
    )(tile_expert, tile_rows, src_row, h, w1, w3, w2)


def _combine_kernel(pos_ref, y_hbm, x_ref, gt_ref, w_ref, *rest, tm, row0, rows_all, with_final):
    gfin_ref = rest[0] if with_final else None
    o_ref, y_scr, sems = rest[-3:]
    i, n = pl.program_id(0), pl.num_programs(0)

    half = D_MODEL // 2
    group = y_scr.shape[3]

    def start_rows(tile, slot):
        def body(g, c):
            for k in range(2):
                for j in range(group):
                    slot_of_pair = pos_ref[k * rows_all + row0 + tile * tm + g * group + j]
                    pltpu.make_async_copy(y_hbm.at[pl.ds(slot_of_pair, 1)], y_scr.at[slot, k, g, pl.ds(j, 1)],
                                          sems.at[slot]).start()
            return c
        lax.fori_loop(0, tm // group, body, 0)

    @pl.when(i == 0)
    def _():
        start_rows(0, 0)

    @pl.when(i + 1 < n)
    def _():
        start_rows(i + 1, (i + 1) % 2)

    slot = i % 2
    def wait_group(g, c):
        for k in range(2):
            pltpu.make_async_copy(y_hbm.at[pl.ds(0, group)], y_scr.at[slot, k, g], sems.at[slot]).wait()
        return c

    lax.fori_loop(0, tm // group, wait_group, 0)
    w = w_ref[...]
    first0, second0 = _unpack_halves(y_scr[slot, 0].reshape(tm, half))
    first1, second1 = _unpack_halves(y_scr[slot, 1].reshape(tm, half))
    f = jnp.concatenate([w[:, 0:1] * first0 + w[:, 1:2] * first1,
                         w[:, 0:1] * second0 + w[:, 1:2] * second1], axis=1)
    y = x_ref[...] + (1.0 + gt_ref[0]) * f
    o_ref[...] = _final_norm(y, gfin_ref) if with_final else y


def _combine(tok, layer, y_sorted, pos, row0, x, wsel, g_final):
    tok_c = tok.with_tm(TM_COMBINE)
    tm = tok_c.tm
    mod3 = tok_c.mod
    in_specs = [pl.BlockSpec(memory_space=pl.ANY),
                pl.BlockSpec((tm, D_MODEL), lambda m, pos: (m, 0)),
                tok_c.mod_spec(layer, 5),
                pl.BlockSpec((tm, N_EXPERTS), lambda m, pos: (m, 0))]
    args = [pos, y_sorted, x, mod3, wsel]
    if g_final is not None:
        in_specs.append(pl.BlockSpec((1, D_MODEL), lambda m, pos: (0, 0)))
        args.append(g_final)
    return pl.pallas_call(
        functools.partial(_combine_kernel, tm=tm, row0=row0, rows_all=pos.shape[0] // 2,
                          with_final=g_final is not None),
        grid_spec=pltpu.PrefetchScalarGridSpec(
            num_scalar_prefetch=1,
            grid=(tok_c.n_tiles,),
            in_specs=in_specs,
            out_specs=pl.BlockSpec((tm, D_MODEL), lambda m, pos: (m, 0)),
            scratch_shapes=[pltpu.VMEM((2, 2, tm // SUBLANES, SUBLANES, D_MODEL // 2), jnp.uint32),
                            pltpu.SemaphoreType.DMA((2,))]),
        out_shape=jax.ShapeDtypeStruct((tok.rows, D_MODEL), F32),
        compiler_params=_params("arbitrary"),
        name="combine",
    )(*args)


class _Stream:
    def __init__(self, tok, x, ret_state, pool_state):
        self.tok, self.x, self.ret_state, self.pool_state = tok, x, ret_state, pool_state
        self.rope_tabs = tok.rope_tables()
        self.new_state = None
        self.u_layers = []


def _mixer_layer(s, i, p):
    tok = s.tok
    is_prompt = s.ret_state is None
    proj, u = _inproj(tok, i, s.x, p['g_mix'], p['w_in'], s.rope_tabs, BF16 if is_prompt else F32)
    s.u_layers.append(u)
    if is_prompt:
        yret, s.new_state = _ret_prompt(tok, i, proj, p['g_ret'], s.new_state)
        mixed = _pool_prompt(tok, i, u, p['w_pool_map'], p['pool_scale'])
    else:
        yret, s.new_state = _ret_sample(tok, i, proj, p['g_ret'], s.ret_state, s.new_state)
        mixed = _pool_sample(tok, i, u, s.pool_state[i].transpose(1, 0, 2), p['w_pool_map'], p['pool_scale'])
    merged = _merge(tok, i, yret, mixed, proj, p['w_ret_out'], p['w_pool_out'])
    s.x = _wo(tok, i, merged, s.x, p['w_o'])


def _moe_layer(streams, i, p, g_final):
    rows_all = sum(s.tok.rows for s in streams)
    h_all, row0, choices, weights, starts = None, 0, [], [], []
    for s in streams:
        esel, wsel, h_all = _router(s.tok, i, s.x, p['g_ffn'], p['w_router'], p['b_router'],
                                    h_all=h_all, row0=row0, rows_all=rows_all)
        choices.append(esel[:, :2])
        weights.append(wsel)
        starts.append(row0)
        row0 += s.tok.rows
    plan = _route_plan(jnp.concatenate(choices, axis=0), TM_MOE)
    y_sorted = _moe_sparse(i, h_all, plan, p['w1_moe'], p['w3_moe'], p['w2_moe'], TM_MOE)
    for s, wsel, start in zip(streams, weights, starts):
        s.x = _combine(s.tok, i, y_sorted, plan[3], start, s.x, wsel, g_final)


def kernel(x_prompt, x_sample, c_prompt, c_sample, state_ret, state_pool, w_ada, b_ada, g_mix, w_in, g_ret,
           w_ret_out, w_pool_map, pool_scale, w_pool_out, w_o, g_ffn, w1_dense, w3_dense, w2_dense, w_router,
           b_router, w1_moe, w3_moe, w2_moe, g_final):
    row3 = lambda a: a.reshape(a.shape[0], 1, a.shape[1])
    p = dict(g_mix=row3(g_mix), w_in=w_in, g_ret=row3(g_ret), w_ret_out=w_ret_out, w_pool_map=w_pool_map,
             pool_scale=row3(pool_scale), w_pool_out=w_pool_out, w_o=w_o, g_ffn=row3(g_ffn),
             w1_dense=w1_dense, w3_dense=w3_dense, w2_dense=w2_dense, w_router=w_router, b_router=row3(b_router),
             w1_moe=w1_moe, w3_moe=w3_moe, w2_moe=w2_moe, g_final=g_final.reshape(1, D_MODEL))
    n_bp, seq_p, _ = x_prompt.shape
    n_bs, seq_s, _ = x_sample.shape

    mod_p, mod_s = _ada(c_prompt, jnp.repeat(c_sample, seq_s, axis=0), w_ada, b_ada)

    tok_p = _Tokens(n_bp, seq_p, 0, TM_PROMPT, per_row_mod=False, mod=mod_p)
    tok_s = _Tokens(n_bs, seq_s, PAST_LEN, TM_SAMPLE, per_row_mod=True, mod=mod_s)
    prompt = _Stream(tok_p, x_prompt.reshape(tok_p.rows, D_MODEL), None, None)
    sample = _Stream(tok_s, x_sample.reshape(tok_s.rows, D_MODEL), state_ret, state_pool)
    streams = [prompt, sample]

    for i in range(DEPTH):
        g_fin = p['g_final'] if i == DEPTH - 1 else None
        for s in streams:
            _mixer_layer(s, i, p)
        if i % 2 == 0:
            for s in streams:
                s.x = _ffn(s.tok, i, s.x, p['g_ffn'], p['w1_dense'], p['w3_dense'], p['w2_dense'], g_final=g_fin)
        else:
            _moe_layer(streams, i, p, g_fin)

    buf_prompt = jnp.stack([u.reshape(n_bp, seq_p, D_POOL)[:, seq_p - POOL_BUF:] for u in prompt.u_layers])
    buf_sample = jnp.stack([
        jnp.concatenate([state_pool[i], sample.u_layers[i].reshape(n_bs, seq_s, D_POOL)], axis=1)[:, -POOL_BUF:]
        for i in range(DEPTH)])
    return (prompt.x.reshape(x_prompt.shape), sample.x.reshape(x_sample.shape), prompt.new_state, buf_prompt,
            sample.new_state, buf_sample)
```

```python
import functools

import jax
import jax.numpy as jnp
from jax import lax
from jax.experimental import pallas as pl
from jax.experimental.pallas import tpu as pltpu

F32 = jnp.float32
BF16 = jnp.bfloat16

D_MODEL = 2048
DEPTH = 2
PAST_LEN = 16384
RET_HEADS = 8
RET_DK = 128
RET_DV = 256
RET_CHUNK = 128
ROPE_BASE = 10000.0
QK_W = RET_HEADS * RET_DK
V_W = RET_HEADS * RET_DV
POOL_WINDOWS = (2, 4, 8, 16)
D_POOL = 1024
POOL_GW = 256
POOL_BUF = 15
POOL_HALO = 16
N_IN = 2 * QK_W + 2 * V_W + D_POOL + 2 * D_MODEL
N_EXPERTS = 8
EPS = 1e-6
N_MOD = 6

O_Q, O_K, O_V, O_G, O_U = 0, QK_W, 2 * QK_W, 2 * QK_W + V_W, 2 * QK_W + 2 * V_W
O_M1 = O_U + D_POOL
O_M2 = O_M1 + D_MODEL

VMEM_LIMIT_BYTES = 56 * 1024 * 1024
SUBLANES = 8
TN = 512
TN_IN = 1024
TN_MERGE = 1024
TF = 256
TM_PROMPT = 1024
TM_SAMPLE = 512
TM_FFN = 1024
TM_OUT = 2048
ROW_CHUNK = 256
RET_SAMPLE_BATCH = 8
NORM_ROWS = 16
NORM_UNROLL = 16
TL_POOL = 512
TM_MOE = 1280
SUB_MOE = 128
TM_COMBINE = 256
RET_PROMPT_BATCH = 4


def _params(*sem):
    return pltpu.CompilerParams(dimension_semantics=sem, vmem_limit_bytes=VMEM_LIMIT_BYTES)


class _Tokens:
    def __init__(self, n_batch, seq, n_past, tm, per_row_mod, mod=None):
        self.n_batch, self.seq, self.n_past = n_batch, seq, n_past
        self.rows = n_batch * seq
        self.tm = min(tm, self.rows)
        self.per_row_mod = per_row_mod
        if not per_row_mod:
            self.tm = min(self.tm, seq)
            assert seq % self.tm == 0
        assert self.rows % self.tm == 0
        self.n_tiles = self.rows // self.tm
        if mod is not None and not per_row_mod and mod.ndim == 3 and mod.shape[1] != 1:
            mod = mod.reshape(DEPTH * n_batch, 1, N_MOD * D_MODEL)
        self.mod = mod

    def with_tm(self, tm):
        return _Tokens(self.n_batch, self.seq, self.n_past, tm, self.per_row_mod, self.mod)

    def mod_spec(self, layer, chunk, tn=D_MODEL, col_axis=False):
        per_chunk = D_MODEL // tn
        col = (lambda n: chunk * per_chunk + n) if col_axis else (lambda n: chunk * per_chunk)
        if self.per_row_mod:
            if col_axis:
                return pl.BlockSpec((1, self.tm, tn), lambda m, n, *_: (layer, m, col(n)))
            return pl.BlockSpec((1, self.tm, tn), lambda m, *_: (layer, m, col(0)))
        tiles_per_batch = self.seq // self.tm
        base = layer * self.n_batch
        if col_axis:
            return pl.BlockSpec((1, 1, tn), lambda m, n, *_: (base + m // tiles_per_batch, 0, col(n)))
        return pl.BlockSpec((1, 1, tn), lambda m, *_: (base + m // tiles_per_batch, 0, col(0)))

    def rope_tables(self):
        half = RET_DK // 2
        pos = jnp.arange(self.seq, dtype=F32) + float(self.n_past)
        inv = ROPE_BASE ** (-jnp.arange(half, dtype=F32) / half)
        ang = pos[:, None] * inv[None, :]
        cos, sin = jnp.cos(ang), jnp.sin(ang)
        cos_full = jnp.concatenate([cos, cos], axis=1)
        sin_signed = jnp.concatenate([-sin, sin], axis=1)
        if self.per_row_mod:
            cos_full = jnp.tile(cos_full, (self.n_batch, 1))
            sin_signed = jnp.tile(sin_signed, (self.n_batch, 1))
        return cos_full, sin_signed

    def rope_spec(self):
        n_blocks = (self.rows if self.per_row_mod else self.seq) // self.tm
        return pl.BlockSpec((self.tm, RET_DK), lambda m, *_: (m % n_blocks, 0))


def _retention_tables(chunk):
    log_gamma = jnp.log1p(-jnp.exp2(-5.0 - jnp.arange(RET_HEADS, dtype=F32)))
    idx = jnp.arange(chunk, dtype=F32)
    rel = idx[:, None] - idx[None, :]
    dmask = jnp.where(rel[None] >= 0, jnp.exp(log_gamma[:, None, None] * jnp.maximum(rel, 0.0)[None]), 0.0)
    q_dec = jnp.exp(log_gamma[:, None] * (idx + 1.0)[None, :])
    k_dec = jnp.exp(log_gamma[:, None] * (chunk - 1.0 - idx)[None, :])
    chunk_dec = jnp.exp(log_gamma * chunk)
    q_dec = jnp.broadcast_to(q_dec[:, :, None], (RET_HEADS, chunk, RET_DV))
    k_dec = jnp.broadcast_to(k_dec[:, :, None], (RET_HEADS, chunk, RET_DK))
    chunk_dec = jnp.broadcast_to(chunk_dec[:, None, None], (RET_HEADS, 1, RET_DV))
    return dmask, q_dec, k_dec, chunk_dec


def _norm_rows(x_ref, g_ref, sc_ref, sh_ref, h_ref):
    tm = x_ref.shape[0]
    step = min(NORM_ROWS, tm)
    trips = tm // step
    per_row = sc_ref.shape[1] != 1
    if not per_row:
        gain, shift = g_ref[0] * (1.0 + sc_ref[0]), sh_ref[0]

    def body(c, carry):
        rows = pl.ds(pl.multiple_of(c * step, step), step)
        x = x_ref[rows, :]
        y = x * lax.rsqrt(jnp.mean(x * x, axis=-1, keepdims=True) + EPS)
        if per_row:
            y = y * (g_ref[0] * (1.0 + sc_ref[0, rows, :])) + sh_ref[0, rows, :]
        else:
            y = y * gain + shift
        h_ref[rows, :] = y.astype(h_ref.dtype)
        return carry

    lax.fori_loop(0, trips, body, 0, unroll=NORM_UNROLL if trips % NORM_UNROLL == 0 else 1)


def _silu(x):
    return x * jax.nn.sigmoid(x)


def _pack_halves(y):
    half = y.shape[1] // 2
    hi = lax.bitcast_convert_type(y[:, :half].astype(jnp.bfloat16).astype(F32), jnp.uint32)
    lo = lax.bitcast_convert_type(y[:, half:].astype(jnp.bfloat16).astype(F32), jnp.uint32)
    return hi | (lo >> 16)


def _unpack_halves(u):
    return (lax.bitcast_convert_type(u & jnp.uint32(0xFFFF0000), F32),
            lax.bitcast_convert_type(u << 16, F32))


def _ada_kernel(cp_ref, cs_ref, w_ref, b_ref, op_ref, os_ref):
    w = w_ref[0].astype(BF16)
    for c_ref, o_ref in ((cp_ref, op_ref), (cs_ref, os_ref)):
        o_ref[0] = jnp.dot(_silu(c_ref[...]).astype(BF16), w, preferred_element_type=F32) + b_ref[0]


def _ada(c_p, c_s, w_ada, b_ada):
    tn = 1024
    width = N_MOD * D_MODEL
    row_spec = lambda c: pl.BlockSpec(c.shape, lambda l, n: (0, 0))
    out_spec = lambda c: pl.BlockSpec((1, c.shape[0], tn), lambda l, n: (l, 0, n))
    return pl.pallas_call(
        _ada_kernel,
        grid=(DEPTH, width // tn),
        in_specs=[row_spec(c_p), row_spec(c_s),
                  pl.BlockSpec((1, D_MODEL, tn), lambda l, n: (l, 0, n)),
                  pl.BlockSpec((1, 1, tn), lambda l, n: (l, 0, n))],
        out_specs=[out_spec(c_p), out_spec(c_s)],
        out_shape=[jax.ShapeDtypeStruct((DEPTH, c.shape[0], width), F32) for c in (c_p, c_s)],
        compiler_params=_params("arbitrary", "arbitrary"),
        name="ada",
    )(c_p, c_s, w_ada, b_ada.reshape(DEPTH, 1, width))


def _inproj_kernel(x_ref, sc_ref, sh_ref, g_ref, w_ref, cos_ref, sin_ref, o_ref, u_ref, h_scr):
    n = pl.program_id(1)
    tm = x_ref.shape[0]
    rc = min(ROW_CHUNK, tm)

    @pl.when(n == 0)
    def _():
        _norm_rows(x_ref, g_ref, sc_ref, sh_ref, h_scr)

    def project(epilogue):
        w = w_ref[0].astype(BF16)
        for r in range(0, tm, rc):
            rows = slice(r, r + rc)
            epilogue(jnp.dot(h_scr[rows, :], w, preferred_element_type=F32), rows)

    def store(acc, rows):
        o_ref[rows, :] = acc.astype(o_ref.dtype)

    def rope(scale):
        def epilogue(acc, rows):
            for j in range(TN_IN // RET_DK):
                cols = slice(j * RET_DK, (j + 1) * RET_DK)
                xh = acc[:, cols]
                r = xh * cos_ref[rows, :] + pltpu.roll(xh, RET_DK // 2, 1) * sin_ref[rows, :]
                if scale != 1.0:
                    r = r * scale
                o_ref[rows, cols] = r.astype(o_ref.dtype)
        project(epilogue)

    @pl.when(n < O_K // TN_IN)
    def _():
        rope(1.0)

    @pl.when(jnp.logical_and(n >= O_K // TN_IN, n < O_V // TN_IN))
    def _():
        rope(RET_DK ** -0.5)

    @pl.when(jnp.logical_and(n >= O_V // TN_IN, n < O_G // TN_IN))
    def _():
        project(store)

    @pl.when(jnp.logical_and(n >= O_G // TN_IN, n < O_U // TN_IN))
    def _():
        project(lambda acc, rows: store(_silu(acc), rows))

    @pl.when(jnp.logical_and(n >= O_U // TN_IN, n < O_M1 // TN_IN))
    def _():
        def epilogue(acc, rows):
            store(acc, rows)
            u_ref[rows, :] = acc
        project(epilogue)

    @pl.when(n >= O_M1 // TN_IN)
    def _():
        project(lambda acc, rows: store(jax.nn.sigmoid(acc), rows))


def _inproj(tok, layer, x, g_mix, w_in, rope_tabs, out_dtype):
    tm = tok.tm
    tn = TN_IN
    assert all(o % tn == 0 for o in (O_K, O_V, O_G, O_U, O_M1, N_IN))
    n_u = D_POOL // tn
    mod3 = tok.mod
    proj, u = pl.pallas_call(
        _inproj_kernel,
        grid=(tok.n_tiles, N_IN // tn),
        in_specs=[pl.BlockSpec((tm, D_MODEL), lambda m, n: (m, 0)),
                  tok.mod_spec(layer, 1), tok.mod_spec(layer, 0),
                  pl.BlockSpec((1, 1, D_MODEL), lambda m, n: (layer, 0, 0)),
                  pl.BlockSpec((1, D_MODEL, tn), lambda m, n: (layer, 0, n)),
                  tok.rope_spec(), tok.rope_spec()],
        out_specs=[pl.BlockSpec((tm, tn), lambda m, n: (m, n)),
                   pl.BlockSpec((tm, tn), lambda m, n: (m, jnp.clip(n - O_U // tn, 0, n_u - 1)),
                                pipeline_mode=pl.Buffered(1))],
        out_shape=[jax.ShapeDtypeStruct((tok.rows, N_IN), out_dtype),
                   jax.ShapeDtypeStruct((tok.rows, D_POOL), F32)],
        scratch_shapes=[pltpu.VMEM((tm, D_MODEL), BF16)],
        compiler_params=_params("arbitrary", "arbitrary"),
        name="inproj",
    )(x, mod3, mod3, g_mix, w_in, *rope_tabs)
    return proj, u


def _head_norm_gate(o, gret, gate):
    mu = jnp.mean(o, axis=-1, keepdims=True)
    d = o - mu
    var = jnp.mean(d * d, axis=-1, keepdims=True)
    return gate.astype(F32) * (d * lax.rsqrt(var + EPS) * gret)


def _ret_prompt_kernel(q_ref, k_ref, v_ref, g_ref, gret_ref, dmask_ref, qdec_ref, kdec_ref, cdec_ref,
                       *rest):
    y_ref, s_ref = rest[-2:]

    @pl.when(pl.program_id(1) == 0)
    def _():
        s_ref[...] = jnp.zeros_like(s_ref)

    for b in range(q_ref.shape[0]):
        for h in range(RET_HEADS):
            ks, vs = slice(h * RET_DK, (h + 1) * RET_DK), slice(h * RET_DV, (h + 1) * RET_DV)
            q, k, v = q_ref[b, :, ks], k_ref[b, :, ks], v_ref[b, :, vs]
            state = s_ref[0, b, h]
            scores = lax.dot_general(q, k, (((1,), (1,)), ((), ())), preferred_element_type=F32) * dmask_ref[h]
            inner = jnp.dot(scores.astype(BF16), v, preferred_element_type=F32)
            cross = jnp.dot(q, state.astype(BF16), preferred_element_type=F32) * qdec_ref[h]
            k_decayed = (k.astype(F32) * kdec_ref[h]).astype(BF16)
            s_ref[0, b, h] = cdec_ref[h] * state + lax.dot_general(
                k_decayed, v, (((0,), (0,)), ((), ())), preferred_element_type=F32)
            y_ref[b, :, vs] = _head_norm_gate(inner + cross, gret_ref[0, :, vs], g_ref[b, :, vs]).astype(y_ref.dtype)


def _table_specs(tables):
    return [pl.BlockSpec(t.shape, lambda *_: (0, 0, 0)) for t in tables]


def _ret_prompt(tok, layer, proj, g_ret, s_all):
    n_b, seq = tok.n_batch, tok.seq
    c = RET_CHUNK
    assert seq % c == 0
    nc = seq // c
    tables = _retention_tables(c)
    bb = RET_PROMPT_BATCH if n_b % RET_PROMPT_BATCH == 0 else 1
    proj3 = proj.reshape(n_b, seq, N_IN)
    in_specs = [pl.BlockSpec((bb, c, QK_W), lambda b, i: (b, i, O_Q // QK_W)),
                pl.BlockSpec((bb, c, QK_W), lambda b, i: (b, i, O_K // QK_W)),
                pl.BlockSpec((bb, c, V_W), lambda b, i: (b, i, O_V // V_W)),
                pl.BlockSpec((bb, c, V_W), lambda b, i: (b, i, O_G // V_W)),
                pl.BlockSpec((1, 1, V_W), lambda b, i: (layer, 0, 0))] + _table_specs(tables)
    args = [proj3, proj3, proj3, proj3, g_ret, *tables]
    aliases = {}
    if s_all is not None:
        aliases = {len(args): 1}
        in_specs.append(pl.BlockSpec(memory_space=pl.ANY))
        args.append(s_all)
    y, s_all = pl.pallas_call(
        _ret_prompt_kernel,
        grid=(n_b // bb, nc),
        in_specs=in_specs,
        out_specs=[pl.BlockSpec((bb, c, V_W), lambda b, i: (b, i, 0)),
                   pl.BlockSpec((1, bb, RET_HEADS, RET_DK, RET_DV), lambda b, i: (layer, b, 0, 0, 0))],
        out_shape=[jax.ShapeDtypeStruct((n_b, seq, V_W), BF16),
                   jax.ShapeDtypeStruct((DEPTH, n_b, RET_HEADS, RET_DK, RET_DV), F32)],
        input_output_aliases=aliases,
        compiler_params=_params("arbitrary", "arbitrary"),
        name="ret_prompt",
    )(*args)
    return y.reshape(tok.rows, V_W), s_all


def _ret_sample_kernel(q_ref, k_ref, v_ref, g_ref, kt_ref, gret_ref, s0_ref, dm_ref, qdec_ref, kdect_ref, cdec_ref,
                       *rest, bb, seq):
    y_ref, s_ref = rest[-2:]
    hs = RET_HEADS * seq
    reps = lambda a: jnp.concatenate([a] * RET_HEADS, axis=0)
    mask_q = (lax.broadcasted_iota(jnp.int32, (hs, QK_W), 0) // seq
              == lax.broadcasted_iota(jnp.int32, (hs, QK_W), 1) // RET_DK)
    mask_v = (lax.broadcasted_iota(jnp.int32, (hs, V_W), 0) // seq
              == lax.broadcasted_iota(jnp.int32, (hs, V_W), 1) // RET_DV)
    nt = (((1,), (1,)), ((), ()))
    for b in range(bb):
        rows_b = slice(b * seq, (b + 1) * seq)
        q, k, v = q_ref[rows_b, :], k_ref[rows_b, :], v_ref[rows_b, :]
        q_blk = jnp.where(mask_q, reps(q), 0.0).astype(BF16)
        scores = lax.dot_general(q_blk, reps(k).astype(BF16), nt, preferred_element_type=F32) * dm_ref[...]
        v_stack = jnp.concatenate([v[:, h * RET_DV:(h + 1) * RET_DV] for h in range(RET_HEADS)], axis=0)
        inner = jnp.dot(scores.astype(BF16), v_stack.astype(BF16), preferred_element_type=F32)
        state = s0_ref[0, b]
        cross = jnp.dot(q_blk, state.reshape(QK_W, RET_DV).astype(BF16), preferred_element_type=F32)
        o = inner + cross * qdec_ref[...]
        mu = jnp.mean(o, axis=-1, keepdims=True)
        d = o - mu
        normed = d * lax.rsqrt(jnp.mean(d * d, axis=-1, keepdims=True) + EPS)
        k_dec_t = (kt_ref[b] * kdect_ref[...]).astype(BF16)
        v_blk = jnp.where(mask_v, reps(v), 0.0).astype(BF16)
        upd = jnp.dot(k_dec_t, v_blk, preferred_element_type=F32)
        for h in range(RET_HEADS):
            vs = slice(h * RET_DV, (h + 1) * RET_DV)
            y_ref[rows_b, vs] = g_ref[rows_b, vs] * (normed[h * seq:(h + 1) * seq] * gret_ref[0, :, vs])
            s_ref[0, b, h] = cdec_ref[h] * state[h] + upd[:, vs]


def _ret_sample(tok, layer, proj, g_ret, state_ret, s_all):
    n_b, seq = tok.n_batch, tok.seq
    bb = RET_SAMPLE_BATCH if n_b % RET_SAMPLE_BATCH == 0 else 4
    assert n_b % bb == 0
    hs = RET_HEADS * seq
    dmask, q_dec, k_dec, chunk_dec = _retention_tables(seq)
    dm_blk = (dmask[:, :, None, :] * jnp.eye(RET_HEADS, dtype=F32)[:, None, :, None]).reshape(hs, hs)
    tables = [dm_blk, q_dec.reshape(hs, RET_DV), k_dec[:, :, 0].reshape(1, hs), chunk_dec]
    k_t = proj[:, O_K:O_K + QK_W].reshape(n_b, seq, RET_HEADS, RET_DK).transpose(0, 3, 2, 1).reshape(n_b, RET_DK, hs)
    state_spec = pl.BlockSpec((1, bb, RET_HEADS, RET_DK, RET_DV), lambda i: (layer, i, 0, 0, 0))
    in_specs = [pl.BlockSpec((bb * seq, QK_W), lambda i: (i, O_Q // QK_W)),
                pl.BlockSpec((bb * seq, QK_W), lambda i: (i, O_K // QK_W)),
                pl.BlockSpec((bb * seq, V_W), lambda i: (i, O_V // V_W)),
                pl.BlockSpec((bb * seq, V_W), lambda i: (i, O_G // V_W)),
                pl.BlockSpec((bb, RET_DK, hs), lambda i: (i, 0, 0)),
                pl.BlockSpec((1, 1, V_W), lambda i: (layer, 0, 0)),
                state_spec] + [pl.BlockSpec(t.shape, lambda i, nd=t.ndim: (0,) * nd) for t in tables]
    args = [proj, proj, proj, proj, k_t, g_ret, state_ret, *tables]
    aliases = {}
    if s_all is not None:
        aliases = {len(args): 1}
        in_specs.append(pl.BlockSpec(memory_space=pl.ANY))
        args.append(s_all)
    y, s_all = pl.pallas_call(
        functools.partial(_ret_sample_kernel, bb=bb, seq=seq),
        grid=(n_b // bb,),
        in_specs=in_specs,
        out_specs=[pl.BlockSpec((bb * seq, V_W), lambda i: (i, 0)), state_spec],
        out_shape=[jax.ShapeDtypeStruct((tok.rows, V_W), F32),
                   jax.ShapeDtypeStruct(state_ret.shape, F32)],
        input_output_aliases=aliases,
        compiler_params=_params("arbitrary"),
        name="ret_sample",
    )(*args)
    return y, s_all


def _pool_map(pooled, wmap_ref, scale_ref, gi):
    cols = slice(gi * POOL_GW, (gi + 1) * POOL_GW)
    mixed = jnp.dot(pooled.astype(BF16), wmap_ref[0, gi].astype(BF16), preferred_element_type=F32)
    return mixed * scale_ref[0, :, cols]


def _pool_prompt_kernel(u_ref, halo_ref, wmap_ref, scale_ref, o_ref, uf_scr, a_scr, b_scr, *, tl):
    pad = SUBLANES
    top = pad + POOL_HALO
    first = pl.program_id(1) == 0
    uf_scr[0:pad, :] = jnp.zeros((pad, D_POOL), F32)
    a_scr[0:pad, :] = jnp.zeros((pad, POOL_GW), F32)
    b_scr[0:pad, :] = jnp.zeros((pad, POOL_GW), F32)
    uf_scr[pad:top, :] = jnp.where(first, 0.0, halo_ref[...])
    uf_scr[top:, :] = u_ref[...]
    n_rows = POOL_HALO + tl
    row = lax.broadcasted_iota(jnp.int32, (tl, POOL_GW), 0) + pl.program_id(1) * tl
    for gi, w in enumerate(POOL_WINDOWS):
        cols = slice(gi * POOL_GW, (gi + 1) * POOL_GW)
        src = lambda lo, cols=cols: uf_scr[lo:lo + n_rows, cols]
        dst_scr, other = a_scr, b_scr
        shift = 1
        while shift < w:
            dst_scr[pad:, :] = src(pad) + src(pad - shift)
            src = lambda lo, s=dst_scr: s[lo:lo + n_rows, :]
            dst_scr, other = other, dst_scr
            shift *= 2
        acc = src(pad)[POOL_HALO:, :]
        u = uf_scr[top:, cols]
        cnt = jnp.minimum(w, row + 1).astype(F32)
        o_ref[:, cols] = _pool_map(acc / cnt - u, wmap_ref, scale_ref, gi).astype(o_ref.dtype)


def _pool_prompt(tok, layer, u, w_pool_map, pool_scale):
    n_b, seq = tok.n_batch, tok.seq
    tl = min(TL_POOL, seq)
    assert seq % tl == 0 and tl % POOL_HALO == 0
    nl = seq // tl
    per = tl // POOL_HALO
    return pl.pallas_call(
        functools.partial(_pool_prompt_kernel, tl=tl),
        grid=(n_b, nl),
        in_specs=[pl.BlockSpec((tl, D_POOL), lambda b, l: (b * nl + l, 0)),
                  pl.BlockSpec((POOL_HALO, D_POOL), lambda b, l: (jnp.maximum((b * nl + l) * per - 1, 0), 0)),
                  pl.BlockSpec((1,) + w_pool_map.shape[1:], lambda b, l: (layer, 0, 0, 0)),
                  pl.BlockSpec((1, 1, D_POOL), lambda b, l: (layer, 0, 0))],
        out_specs=pl.BlockSpec((tl, D_POOL), lambda b, l: (b * nl + l, 0)),
        out_shape=jax.ShapeDtypeStruct((tok.rows, D_POOL), BF16),
        scratch_shapes=[pltpu.VMEM((SUBLANES + POOL_HALO + tl, D_POOL), F32),
                        pltpu.VMEM((SUBLANES + POOL_HALO + tl, POOL_GW), F32),
                        pltpu.VMEM((SUBLANES + POOL_HALO + tl, POOL_GW), F32)],
        compiler_params=_params("arbitrary", "arbitrary"),
        name="pool_prompt",
    )(u, u, w_pool_map, pool_scale)


def _pool_sample_kernel(buf_ref, u_ref, wmap_ref, scale_ref, o_ref, *, seq, n_b):
    def row(j, cols):
        return buf_ref[j, :, cols] if j < POOL_BUF else u_ref[j - POOL_BUF, :, cols]

    for gi, w in enumerate(POOL_WINDOWS):
        cols = slice(gi * POOL_GW, (gi + 1) * POOL_GW)
        for t in range(seq):
            acc = row(POOL_BUF + t, cols)
            for j in range(1, w):
                acc = acc + row(POOL_BUF + t - j, cols)
            pooled = acc / float(w) - row(POOL_BUF + t, cols)
            o_ref[t, :, cols] = _pool_map(pooled, wmap_ref, scale_ref, gi)


def _pool_sample(tok, layer, u, buf_tm, w_pool_map, pool_scale):
    n_b, seq = tok.n_batch, tok.seq
    assert tok.n_past >= max(POOL_WINDOWS)
    u_tm = u.reshape(n_b, seq, D_POOL).transpose(1, 0, 2)
    mixed_tm = pl.pallas_call(
        functools.partial(_pool_sample_kernel, seq=seq, n_b=n_b),
        grid=(1,),
        in_specs=[pl.BlockSpec(buf_tm.shape, lambda i: (0, 0, 0)),
                  pl.BlockSpec(u_tm.shape, lambda i: (0, 0, 0)),
                  pl.BlockSpec((1,) + w_pool_map.shape[1:], lambda i: (layer, 0, 0, 0)),
                  pl.BlockSpec((1, 1, D_POOL), lambda i: (layer, 0, 0))],
        out_specs=pl.BlockSpec(u_tm.shape, lambda i: (0, 0, 0)),
        out_shape=jax.ShapeDtypeStruct(u_tm.shape, F32),
        compiler_params=_params("arbitrary"),
        name="pool_sample",
    )(buf_tm, u_tm, w_pool_map, pool_scale)
    return mixed_tm.transpose(1, 0, 2).reshape(tok.rows, D_POOL)


def _merge_kernel(yret_ref, mixed_ref, wr_ref, wp_ref, g1_ref, g2_ref, o_ref, wr_scr, wp_scr):
    @pl.when(pl.program_id(1) == 0)
    def _():
        wr_scr[...] = wr_ref[0].astype(BF16)
        wp_scr[...] = wp_ref[0].astype(BF16)

    ret = jnp.dot(yret_ref[...].astype(BF16), wr_scr[...], preferred_element_type=F32)
    pool = jnp.dot(mixed_ref[...].astype(BF16), wp_scr[...], preferred_element_type=F32)
    o_ref[...] = (g1_ref[...].astype(F32) * ret + g2_ref[...].astype(F32) * pool).astype(o_ref.dtype)


def _merge(tok, layer, yret, mixed, proj, w_ret_out, w_pool_out):
    tm, tn = tok.tm, TN_MERGE
    once = pl.Buffered(1)
    return pl.pallas_call(
        _merge_kernel,
        grid=(D_MODEL // tn, tok.n_tiles),
        in_specs=[pl.BlockSpec((tm, V_W), lambda n, m: (m, 0)),
                  pl.BlockSpec((tm, D_POOL), lambda n, m: (m, 0)),
                  pl.BlockSpec((1, V_W, tn), lambda n, m: (layer, 0, n), pipeline_mode=once),
                  pl.BlockSpec((1, D_POOL, tn), lambda n, m: (layer, 0, n), pipeline_mode=once),
                  pl.BlockSpec((tm, tn), lambda n, m: (m, O_M1 // tn + n)),
                  pl.BlockSpec((tm, tn), lambda n, m: (m, O_M2 // tn + n))],
        out_specs=pl.BlockSpec((tm, tn), lambda n, m: (m, n)),
        out_shape=jax.ShapeDtypeStruct((tok.rows, D_MODEL), BF16),
        scratch_shapes=[pltpu.VMEM((V_W, tn), BF16), pltpu.VMEM((D_POOL, tn), BF16)],
        compiler_params=_params("arbitrary", "arbitrary"),
        name="merge",
    )(yret, mixed, w_ret_out, w_pool_out, proj, proj)


def _wo_kernel(merged_ref, w_ref, x_ref, gt_ref, o_ref, w_scr):
    @pl.when(pl.program_id(1) == 0)
    def _():
        w_scr[...] = w_ref[0].astype(BF16)

    mix = jnp.dot(merged_ref[...], w_scr[...], preferred_element_type=F32)
    o_ref[...] = x_ref[...] + (1.0 + gt_ref[0]) * mix


def _wo(tok, layer, merged, x, w_o):
    tm, tn = tok.tm, TN_MERGE
    mod3 = tok.mod
    gate = tok.mod_spec(layer, 2, tn=tn, col_axis=True)
    return pl.pallas_call(
        _wo_kernel,
        grid=(D_MODEL // tn, tok.n_tiles),
        in_specs=[pl.BlockSpec((tm, D_MODEL), lambda n, m: (m, 0)),
                  pl.BlockSpec((1, D_MODEL, tn), lambda n, m: (layer, 0, n), pipeline_mode=pl.Buffered(1)),
                  pl.BlockSpec((tm, tn), lambda n, m: (m, n)),
                  pl.BlockSpec(gate.block_shape, lambda n, m: gate.index_map(m, n))],
        out_specs=pl.BlockSpec((tm, tn), lambda n, m: (m, n)),
        out_shape=jax.ShapeDtypeStruct((tok.rows, D_MODEL), F32),
        scratch_shapes=[pltpu.VMEM((D_MODEL, tn), BF16)],
        compiler_params=_params("arbitrary", "arbitrary"),
        name="wo",
    )(merged, w_o, x, mod3)


def _mat(w_ref):
    return w_ref[(0,) * (len(w_ref.shape) - 2)]


def _final_norm(y, gfin_ref):
    return y * lax.rsqrt(jnp.mean(y * y, axis=-1, keepdims=True) + EPS) * gfin_ref[...]


def _ffn_kernel(x_ref, sc_ref, sh_ref, gt_ref, g_ref, w1_ref, w3_ref, w2_ref, *rest, n_f, with_final):
    gfin_ref = rest[0] if with_final else None
    o_ref, h_scr = rest[-2:]
    f = pl.program_id(1)

    @pl.when(f == 0)
    def _():
        _norm_rows(x_ref, g_ref, sc_ref, sh_ref, h_scr)
        o_ref[...] = jnp.zeros_like(o_ref)

    h = h_scr[...]
    a = jnp.dot(h, _mat(w1_ref).astype(BF16), preferred_element_type=F32)
    b = jnp.dot(h, _mat(w3_ref).astype(BF16), preferred_element_type=F32)
    o_ref[...] += jnp.dot((_silu(a) * b).astype(BF16), _mat(w2_ref).astype(BF16), preferred_element_type=F32)

    @pl.when(f == n_f - 1)
    def _():
        y = x_ref[...] + (1.0 + gt_ref[0]) * o_ref[...]
        o_ref[...] = _final_norm(y, gfin_ref) if with_final else y


def _ffn(tok, layer, x, g_ffn, w1, w3, w2, g_final=None):
    j = layer // 2
    d_ff = w1.shape[-1]
    tok_f = tok.with_tm(TM_FFN)
    tm, n_tiles = tok_f.tm, tok_f.n_tiles
    mod3 = tok_f.mod
    assert d_ff % TF == 0
    n_f = d_ff // TF
    in_specs = [pl.BlockSpec((tm, D_MODEL), lambda m, f: (m, 0)),
                tok_f.mod_spec(layer, 4), tok_f.mod_spec(layer, 3), tok_f.mod_spec(layer, 5),
                pl.BlockSpec((1, 1, D_MODEL), lambda m, f: (layer, 0, 0)),
                pl.BlockSpec((1, D_MODEL, TF), lambda m, f: (j, 0, f)),
                pl.BlockSpec((1, D_MODEL, TF), lambda m, f: (j, 0, f)),
                pl.BlockSpec((1, TF, D_MODEL), lambda m, f: (j, f, 0))]
    args = [x, mod3, mod3, mod3, g_ffn, w1, w3, w2]
    if g_final is not None:
        in_specs.append(pl.BlockSpec((1, D_MODEL), lambda m, f: (0, 0)))
        args.append(g_final)
    return pl.pallas_call(
        functools.partial(_ffn_kernel, n_f=n_f, with_final=g_final is not None),
        grid=(n_tiles, n_f),
        in_specs=in_specs,
        out_specs=pl.BlockSpec((tm, D_MODEL), lambda m, f: (m, 0)),
        out_shape=jax.ShapeDtypeStruct((tok.rows, D_MODEL), F32),
        scratch_shapes=[pltpu.VMEM((tm, D_MODEL), BF16)],
        compiler_params=_params("arbitrary", "arbitrary"),
        name="ffn",
    )(*args)


def _router_kernel(x_ref, sc_ref, sh_ref, g_ref, w_ref, b_ref, *rest):
    esel_ref, wsel_ref, h_ref, h_scr = rest[-4:]
    _norm_rows(x_ref, g_ref, sc_ref, sh_ref, h_scr)
    h = h_scr[...]
    logits = jnp.dot(h, w_ref[0].astype(BF16), preferred_element_type=F32) + b_ref[0]
    idx = lax.broadcasted_iota(jnp.int32, logits.shape, 1)
    top1 = jnp.max(logits, axis=-1, keepdims=True)
    i1 = jnp.min(jnp.where(logits == top1, idx, N_EXPERTS), axis=-1, keepdims=True)
    others = jnp.where(idx == i1, -jnp.inf, logits)
    top2 = jnp.max(others, axis=-1, keepdims=True)
    i2 = jnp.min(jnp.where(others == top2, idx, N_EXPERTS), axis=-1, keepdims=True)
    e2 = jnp.exp(top2 - top1)
    denom = 1.0 + e2
    w1, w2 = 1.0 / denom, e2 / denom
    esel_ref[...] = jnp.where(idx == 0, i1, i2)
    wsel_ref[...] = jnp.where(idx == 0, w1, w2)
    h_ref[...] = _pack_halves(h.astype(F32))


def _router(tok, layer, x, g_ffn, w_router, b_router, h_all=None, row0=0, rows_all=None):
    j = layer // 2
    tm = tok.tm
    rows_all = tok.rows if rows_all is None else rows_all
    assert row0 % tm == 0
    small = pl.BlockSpec((tm, N_EXPERTS), lambda m: (m, 0))
    mod3 = tok.mod
    in_specs = [pl.BlockSpec((tm, D_MODEL), lambda m: (m, 0)),
                tok.mod_spec(layer, 4), tok.mod_spec(layer, 3),
                pl.BlockSpec((1, 1, D_MODEL), lambda m: (layer, 0, 0)),
                pl.BlockSpec((1, D_MODEL, N_EXPERTS), lambda m: (j, 0, 0)),
                pl.BlockSpec((1, 1, N_EXPERTS), lambda m: (j, 0, 0))]
    args = [x, mod3, mod3, g_ffn, w_router, b_router]
    aliases = {}
    if h_all is not None:
        aliases = {len(args): 2}
        in_specs.append(pl.BlockSpec(memory_space=pl.ANY))
        args.append(h_all)
    return pl.pallas_call(
        _router_kernel,
        grid=(tok.n_tiles,),
        in_specs=in_specs,
        out_specs=[small, small, pl.BlockSpec((tm, D_MODEL // 2), lambda m: (row0 // tm + m, 0))],
        out_shape=[jax.ShapeDtypeStruct((tok.rows, N_EXPERTS), jnp.int32),
                   jax.ShapeDtypeStruct((tok.rows, N_EXPERTS), F32),
                   jax.ShapeDtypeStruct((rows_all, D_MODEL // 2), jnp.uint32)],
        input_output_aliases=aliases,
        scratch_shapes=[pltpu.VMEM((tm, D_MODEL), BF16)],
        compiler_params=_params("arbitrary"),
        name="router",
    )(*args)


def _route_plan(choice, tm):
    rows = choice.shape[0]
    n_pairs = 2 * rows
    n_tiles = n_pairs // tm + N_EXPERTS
    e_flat = choice.T.reshape(n_pairs)
    onehot = (e_flat[:, None] == jnp.arange(N_EXPERTS, dtype=jnp.int32)[None, :]).astype(jnp.int32)
    csum = jnp.cumsum(onehot, axis=0)
    rank = jnp.sum(csum * onehot, axis=1) - 1
    count = csum[-1]
    tiles_per = (count + tm - 1) // tm
    rows_per = jnp.maximum((((count + jnp.maximum(tiles_per, 1) - 1) // jnp.maximum(tiles_per, 1)) + 7) // 8 * 8, 8)
    tile_end = jnp.cumsum(tiles_per)
    tile_start = tile_end - tiles_per
    pos = (tile_start[e_flat] + rank // rows_per[e_flat]) * tm + rank % rows_per[e_flat]
    token = jnp.arange(n_pairs, dtype=jnp.int32) % rows
    src_row = jnp.zeros((n_tiles * tm,), jnp.int32).at[pos].set(token, unique_indices=True, mode='promise_in_bounds')
    tile_id = jnp.arange(n_tiles, dtype=jnp.int32)
    n_used = tile_end[-1]
    tile_expert = jnp.minimum(jnp.sum(tile_id[:, None] >= tile_end[None, :], axis=1), N_EXPERTS - 1).astype(jnp.int32)
    tile_rows = jnp.clip(count[tile_expert] - (tile_id - tile_start[tile_expert]) * rows_per[tile_expert],
                         0, rows_per[tile_expert])
    tile_rows = jnp.where(tile_id < n_used, tile_rows, 0).astype(jnp.int32)
    last_expert = tile_expert[jnp.maximum(n_used - 1, 0)]
    tile_expert = jnp.where(tile_rows > 0, tile_expert, last_expert)
    return tile_expert, tile_rows, src_row, pos.astype(jnp.int32)


def _moe_kernel(te_ref, nr_ref, src_ref, h_hbm, w1_ref, w3_ref, w2_ref, o_ref, rows_scr, h_scr, acc_scr, sem,
                *, tm, sub, n_f):
    del te_ref
    m, f = pl.program_id(0), pl.program_id(1)
    half = D_MODEL // 2
    group = rows_scr.shape[1]

    def n_groups(tile):
        return (nr_ref[tile] + group - 1) // group

    def start_rows(tile):
        def body(i, c):
            for j in range(group):
                row = src_ref[tile * tm + i * group + j]
                pltpu.make_async_copy(h_hbm.at[pl.ds(row, 1)], rows_scr.at[i, pl.ds(j, 1)], sem).start()
            return c
        lax.fori_loop(0, n_groups(tile), body, 0)

    def wait_rows(tile):
        def body(i, c):
            pltpu.make_async_copy(h_hbm.at[pl.ds(0, group)], rows_scr.at[i], sem).wait()
            return c
        lax.fori_loop(0, n_groups(tile), body, 0)

    @pl.when(f == 0)
    def _():
        acc_scr[...] = jnp.zeros_like(acc_scr)

        @pl.when(m == 0)
        def _():
            rows_scr[...] = jnp.zeros_like(rows_scr)
            start_rows(0)

        wait_rows(m)
        first, second = _unpack_halves(rows_scr[...].reshape(tm, half))
        h_scr[:, :half] = first.astype(BF16)
        h_scr[:, half:] = second.astype(BF16)

        @pl.when(m + 1 < pl.num_programs(0))
        def _():
            start_rows(m + 1)

    n_sub = (nr_ref[m] + sub - 1) // sub
    for n in range(1, tm // sub + 1):
        @pl.when(n_sub == n)
        def _(n=n):
            h = h_scr[0:n * sub, :]
            a = jnp.dot(h, _mat(w1_ref).astype(BF16), preferred_element_type=F32)
            b = jnp.dot(h, _mat(w3_ref).astype(BF16), preferred_element_type=F32)
            acc_scr[0:n * sub, :] += jnp.dot((_silu(a) * b).astype(BF16), _mat(w2_ref).astype(BF16),
                                             preferred_element_type=F32)

    @pl.when(f == n_f - 1)
    def _():
        o_ref[...] = _pack_halves(acc_scr[...])


def _moe_sparse(layer, h, plan, w1, w3, w2, tm):
    j = layer // 2
    tile_expert, tile_rows, src_row, _ = plan
    n_tiles = tile_expert.shape[0]
    d_ff = w1.shape[3]
    assert d_ff % TF == 0 and tm % SUB_MOE == 0
    n_f = d_ff // TF

    def f_sel(m, f, nr):
        return jnp.where(nr[m] > 0, f, n_f - 1)

    return pl.pallas_call(
        functools.partial(_moe_kernel, tm=tm, sub=SUB_MOE, n_f=n_f),
        grid_spec=pltpu.PrefetchScalarGridSpec(
            num_scalar_prefetch=3,
            grid=(n_tiles, n_f),
            in_specs=[pl.BlockSpec(memory_space=pl.ANY),
                      pl.BlockSpec((1, 1, D_MODEL, TF), lambda m, f, te, nr, sr: (j, te[m], 0, f_sel(m, f, nr))),
                      pl.BlockSpec((1, 1, D_MODEL, TF), lambda m, f, te, nr, sr: (j, te[m], 0, f_sel(m, f, nr))),
                      pl.BlockSpec((1, 1, TF, D_MODEL), lambda m, f, te, nr, sr: (j, te[m], f_sel(m, f, nr), 0))],
            out_specs=pl.BlockSpec((tm, D_MODEL // 2), lambda m, f, te, nr, sr: (m, 0)),
            scratch_shapes=[pltpu.VMEM((tm // SUBLANES, SUBLANES, D_MODEL // 2), jnp.uint32),
                            pltpu.VMEM((tm, D_MODEL), BF16),
                            pltpu.VMEM((tm, D_MODEL), F32), pltpu.SemaphoreType.DMA(())]),
        out_shape=jax.ShapeDtypeStruct((n_tiles * tm, D_MODEL // 2), jnp.uint32),
        compiler_params=_params("arbitrary", "arbitrary"),
        name="moe",
    )(tile_expert, tile_rows, src_row, h, w1, w3, w2)


def _combine_kernel(pos_ref, y_hbm, x_ref, gt_ref, w_ref, *rest, tm, row0, rows_all, with_final):
    gfin_ref = rest[0] if with_final else None
    o_ref, y_scr, sems = rest[-3:]
    i, n = pl.program_id(0), pl.num_programs(0)

    half = D_MODEL // 2
    group = y_scr.shape[3]

    def start_rows(tile, slot):
        def body(g, c):
            for k in range(2):
                for j in range(group):
                    slot_of_pair = pos_ref[k * rows_all + row0 + tile * tm + g * group + j]
                    pltpu.make_async_copy(y_hbm.at[pl.ds(slot_of_pair, 1)], y_scr.at[slot, k, g, pl.ds(j, 1)],
                                          sems.at[slot]).start()
            return c
        lax.fori_loop(0, tm // group, body, 0)

    @pl.when(i == 0)
    def _():
        start_rows(0, 0)

    @pl.when(i + 1 < n)
    def _():
        start_rows(i + 1, (i + 1) % 2)

    slot = i % 2
    def wait_group(g, c):
        for k in range(2):
            pltpu.make_async_copy(y_hbm.at[pl.ds(0, group)], y_scr.at[slot, k, g], sems.at[slot]).wait()
        return c

    lax.fori_loop(0, tm // group, wait_group, 0)
    w = w_ref[...]
    first0, second0 = _unpack_halves(y_scr[slot, 0].reshape(tm, half))
    first1, second1 = _unpack_halves(y_scr[slot, 1].reshape(tm, half))
    f = jnp.concatenate([w[:, 0:1] * first0 + w[:, 1:2] * first1,
                         w[:, 0:1] * second0 + w[:, 1:2] * second1], axis=1)
    y = x_ref[...] + (1.0 + gt_ref[0]) * f
    o_ref[...] = _final_norm(y, gfin_ref) if with_final else y


def _combine(tok, layer, y_sorted, pos, row0, x, wsel, g_final):
    tok_c = tok.with_tm(TM_COMBINE)
    tm = tok_c.tm
    mod3 = tok_c.mod
    in_specs = [pl.BlockSpec(memory_space=pl.ANY),
                pl.BlockSpec((tm, D_MODEL), lambda m, pos: (m, 0)),
                tok_c.mod_spec(layer, 5),
                pl.BlockSpec((tm, N_EXPERTS), lambda m, pos: (m, 0))]
    args = [pos, y_sorted, x, mod3, wsel]
    if g_final is not None:
        in_specs.append(pl.BlockSpec((1, D_MODEL), lambda m, pos: (0, 0)))
        args.append(g_final)
    return pl.pallas_call(
        functools.partial(_combine_kernel, tm=tm, row0=row0, rows_all=pos.shape[0] // 2,
                          with_final=g_final is not None),
        grid_spec=pltpu.PrefetchScalarGridSpec(
            num_scalar_prefetch=1,
            grid=(tok_c.n_tiles,),
            in_specs=in_specs,
            out_specs=pl.BlockSpec((tm, D_MODEL), lambda m, pos: (m, 0)),
            scratch_shapes=[pltpu.VMEM((2, 2, tm // SUBLANES, SUBLANES, D_MODEL // 2), jnp.uint32),
                            pltpu.SemaphoreType.DMA((2,))]),
        out_shape=jax.ShapeDtypeStruct((tok.rows, D_MODEL), F32),
        compiler_params=_params("arbitrary"),
        name="combine",
    )(*args)


class _Stream:
    def __init__(self, tok, x, ret_state, pool_state):
        self.tok, self.x, self.ret_state, self.pool_state = tok, x, ret_state, pool_state
        self.rope_tabs = tok.rope_tables()
        self.new_state = None
        self.u_layers = []


def _mixer_layer(s, i, p):
    tok = s.tok
    is_prompt = s.ret_state is None
    proj, u = _inproj(tok, i, s.x, p['g_mix'], p['w_in'], s.rope_tabs, BF16 if is_prompt else F32)
    s.u_layers.append(u)
    if is_prompt:
        yret, s.new_state = _ret_prompt(tok, i, proj, p['g_ret'], s.new_state)
        mixed = _pool_prompt(tok, i, u, p['w_pool_map'], p['pool_scale'])
    else:
        yret, s.new_state = _ret_sample(tok, i, proj, p['g_ret'], s.ret_state, s.new_state)
        mixed = _pool_sample(tok, i, u, s.pool_state[i].transpose(1, 0, 2), p['w_pool_map'], p['pool_scale'])
    merged = _merge(tok, i, yret, mixed, proj, p['w_ret_out'], p['w_pool_out'])
    s.x = _wo(tok, i, merged, s.x, p['w_o'])


def _moe_layer(streams, i, p, g_final):
    rows_all = sum(s.tok.rows for s in streams)
    h_all, row0, choices, weights, starts = None, 0, [], [], []
    for s in streams:
        esel, wsel, h_all = _router(s.tok, i, s.x, p['g_ffn'], p['w_router'], p['b_router'],
                                    h_all=h_all, row0=row0, rows_all=rows_all)
        choices.append(esel[:, :2])
        weights.append(wsel)
        starts.append(row0)
        row0 += s.tok.rows
    plan = _route_plan(jnp.concatenate(choices, axis=0), TM_MOE)
    y_sorted = _moe_sparse(i, h_all, plan, p['w1_moe'], p['w3_moe'], p['w2_moe'], TM_MOE)
    for s, wsel, start in zip(streams, weights, starts):
        s.x = _combine(s.tok, i, y_sorted, plan[3], start, s.x, wsel, g_final)


def kernel(x_prompt, x_sample, c_prompt, c_sample, state_ret, state_pool, w_ada, b_ada, g_mix, w_in, g_ret,
           w_ret_out, w_pool_map, pool_scale, w_pool_out, w_o, g_ffn, w1_dense, w3_dense, w2_dense, w_router,
           b_router, w1_moe, w3_moe, w2_moe, g_final):
    row3 = lambda a: a.reshape(a.shape[0], 1, a.shape[1])
    p = dict(g_mix=row3(g_mix), w_in=w_in, g_ret=row3(g_ret), w_ret_out=w_ret_out, w_pool_map=w_pool_map,
             pool_scale=row3(pool_scale), w_pool_out=w_pool_out, w_o=w_o, g_ffn=row3(g_ffn),
             w1_dense=w1_dense, w3_dense=w3_dense, w2_dense=w2_dense, w_router=w_router, b_router=row3(b_router),
             w1_moe=w1_moe, w3_moe=w3_moe, w2_moe=w2_moe, g_final=g_final.reshape(1, D_MODEL))
    n_bp, seq_p, _ = x_prompt.shape
    n_bs, seq_s, _ = x_sample.shape

    mod_p, mod_s = _ada(c_prompt, jnp.repeat(c_sample, seq_s, axis=0), w_ada, b_ada)

    tok_p = _Tokens(n_bp, seq_p, 0, TM_PROMPT, per_row_mod=False, mod=mod_p)
    tok_s = _Tokens(n_bs, seq_s, PAST_LEN, TM_SAMPLE, per_row_mod=True, mod=mod_s)
    prompt = _Stream(tok_p, x_prompt.reshape(tok_p.rows, D_MODEL), None, None)
    sample = _Stream(tok_s, x_sample.reshape(tok_s.rows, D_MODEL), state_ret, state_pool)
    streams = [prompt, sample]

    for i in range(DEPTH):
        g_fin = p['g_final'] if i == DEPTH - 1 else None
        for s in streams:
            _mixer_layer(s, i, p)
        if i % 2 == 0:
            for s in streams:
                s.x = _ffn(s.tok, i, s.x, p['g_ffn'], p['w1_dense'], p['w3_dense'], p['w2_dense'], g_final=g_fin)
        else:
            _moe_layer(streams, i, p, g_fin)

    buf_prompt = jnp.stack([u.reshape(n_bp, seq_p, D_POOL)[:, seq_p - POOL_BUF:] for u in prompt.u_layers])
    buf_sample = jnp.stack([
        jnp.concatenate([state_pool[i], sample.u_layers[i].reshape(n_bs, seq_s, D_POOL)], axis=1)[:, -POOL_BUF:]
        for i in range(DEPTH)])
    return (prompt.x.reshape(x_prompt.shape), sample.x.reshape(x_sample.shape), prompt.new_state, buf_prompt,
            sample.new_state, buf_sample)
```
